```python
import math
import jax, jax.numpy as jnp
from jax import lax
import numpy as np

D_MODEL = 2048
BATCH = 2
SEQ = 16384
DEPTH = 2

CTX_LEN = 256
GRID_W = 64
N_MIXERS = 4
GROUP_WIDTH = D_MODEL // N_MIXERS
HEAD_DIM = 128
NA_HEADS = GROUP_WIDTH // HEAD_DIM
NA_WIN_R = 8
NA_WIN_C = 16
DIFF_HEADS = GROUP_WIDTH // HEAD_DIM
DIFF_DIM = HEAD_DIM // 2
DIFF_Q_BLOCK = 128
RET_HEADS = GROUP_WIDTH // HEAD_DIM
RET_CHUNK = 128
SWA_Q_HEADS = GROUP_WIDTH // HEAD_DIM
SWA_KV_HEADS = SWA_Q_HEADS // 2
SWA_KV_WIDTH = SWA_KV_HEADS * HEAD_DIM
SWA_WINDOW = 128
SWA_BLOCK = 128
N_EXPERTS = 16
EXPERT_FF = D_MODEL // 2
EC_CAPACITY = 2
ROPE_BASE = 10000.0
NORM_EPS = 1e-6
GN_EPS = 1e-5
NEG_INF = -1e30
IN_SPLITS = (GROUP_WIDTH,) * 3 + (GROUP_WIDTH,) * 3 + (GROUP_WIDTH,) * 4 + (GROUP_WIDTH, SWA_KV_WIDTH, SWA_KV_WIDTH)
IN_WIDTH = sum(IN_SPLITS)
SPLIT_POINTS = tuple(int(v) for v in np.cumsum(IN_SPLITS)[:-1])

kernel_name = 'hybrid_diffusion_parallel_heads_ec_moe'


def rmsnorm(x, g):
    xf = x.astype(jnp.float32)
    y = xf * lax.rsqrt(jnp.mean(xf * xf, axis=-1, keepdims=True) + NORM_EPS)
    return (y * g.astype(jnp.float32)).astype(x.dtype)


def axial_rope_tables(n_tok, dim, dtype):
    t = jnp.arange(n_tok)
    row = (t // GRID_W).astype(jnp.float32)
    col = (t % GRID_W).astype(jnp.float32)
    nf = dim // 4
    inv = ROPE_BASE ** (-jnp.arange(nf, dtype=jnp.float32) / nf)
    ar = row[:, None, None] * inv
    ac = col[:, None, None] * inv
    return tuple(a.astype(dtype) for a in (jnp.cos(ar), jnp.sin(ar), jnp.cos(ac), jnp.sin(ac)))


def _rot_half(u, cos, sin):
    u1, u2 = jnp.split(u, 2, axis=-1)
    return jnp.concatenate([u1 * cos - u2 * sin, u2 * cos + u1 * sin], axis=-1)


def apply_axial_rope(x, tables):
    cos_r, sin_r, cos_c, sin_c = tables
    x_row, x_col = jnp.split(x, 2, axis=-1)
    return jnp.concatenate([_rot_half(x_row, cos_r, sin_r), _rot_half(x_col, cos_c, sin_c)], axis=-1)


def split_heads(t, n_heads):
    return t.reshape(t.shape[0], t.shape[1], n_heads, t.shape[-1] // n_heads)


def context_attention(q, k, v, sink=None):
    b, l, hq, d = q.shape
    hkv = k.shape[2]
    grp = hq // hkv
    qg = q.reshape(b, l, hkv, grp, d)
    s = jnp.einsum('bqkgd,bskd->bkgqs', qg, k).astype(jnp.float32) * (d ** -0.5)
    if sink is not None:
        s_sink = jnp.broadcast_to(sink.astype(jnp.float32).reshape(1, hkv, grp, 1, 1), s.shape[:-1] + (1,))
        p = jax.nn.softmax(jnp.concatenate([s, s_sink], axis=-1), axis=-1)[..., :-1]
    else:
        p = jax.nn.softmax(s, axis=-1)
    o = jnp.einsum('bkgqs,bskd->bqkgd', p.astype(v.dtype), v)
    return o.reshape(b, l, hq * d)


def neighbourhood_attention(q, k, v, kc, vc, rpb):
    b, n, h, d = q.shape
    rows = n // GRID_W
    kr = min(NA_WIN_R, rows)
    r = jnp.arange(rows)
    row_idx = jnp.clip(r - kr // 2, 0, rows - kr)[:, None] + jnp.arange(kr)[None, :]
    cols = jnp.arange(GRID_W)
    col_start = jnp.clip(cols - NA_WIN_C // 2, 0, GRID_W - NA_WIN_C)
    col_ok = (cols[None, :] >= col_start[:, None]) & (cols[None, :] < col_start[:, None] + NA_WIN_C)
    row_off = row_idx - r[:, None] + NA_WIN_R - 1
    col_off = jnp.clip(cols[None, :] - cols[:, None] + NA_WIN_C - 1, 0, 2 * NA_WIN_C - 2)
    bias = rpb.astype(jnp.float32)[:, row_off][:, :, :, col_off]
    bias = jnp.where(col_ok[None, None, None], bias, NEG_INF).transpose(0, 1, 3, 2, 4)
    qg = q.reshape(b, rows, GRID_W, h, d)
    kg = k.reshape(b, rows, GRID_W, h, d)[:, row_idx]
    vg = v.reshape(b, rows, GRID_W, h, d)[:, row_idx]
    scale = d ** -0.5
    s_loc = jnp.einsum('brqhd,brkchd->bhrqkc', qg, kg).astype(jnp.float32) * scale + bias[None]
    s_ctx = jnp.einsum('brqhd,blhd->bhrql', qg, kc).astype(jnp.float32) * scale
    nloc = kr * GRID_W
    logits = jnp.concatenate([s_loc.reshape(b, h, rows, GRID_W, nloc), s_ctx], axis=-1)
    p = jax.nn.softmax(logits, axis=-1).astype(v.dtype)
    p_loc = p[..., :nloc].reshape(b, h, rows, GRID_W, kr, GRID_W)
    o = jnp.einsum('bhrqkc,brkchd->brqhd', p_loc, vg) + jnp.einsum('bhrql,blhd->brqhd', p[..., nloc:], vc)
    return o.reshape(b, n, h * d)


def _diff_apply(q, k, v, lam):
    d = q.shape[-1]
    s = jnp.einsum('bqhmd,bshmd->bhmqs', q, k).astype(jnp.float32) * (d ** -0.5)
    p = jax.nn.softmax(s, axis=-1)
    w = p[:, :, 0] - lam * p[:, :, 1]
    return jnp.einsum('bhqs,bshe->bqhe', w.astype(v.dtype), v)


def _diff_post(o, g, lam_init):
    b, t, h, e = o.shape
    return (rmsnorm(o, g) * (1.0 - lam_init)).reshape(b, t, h * e)


def differential_attention(q, k, v, kc, vc, lam, lam_init, norm_g):
    b, n, h, _, d = q.shape
    k_all = jnp.concatenate([k, kc], axis=1)
    v_all = jnp.concatenate([v, vc], axis=1)
    nb = n // DIFF_Q_BLOCK
    q_blocks = jnp.swapaxes(q.reshape(b, nb, DIFF_Q_BLOCK, h, 2, d), 0, 1)
    o = lax.map(lambda qb: _diff_apply(qb, k_all, v_all, lam), q_blocks)
    o = jnp.swapaxes(o, 0, 1).reshape(b, n, h, 2 * d)
    return _diff_post(o, norm_g, lam_init)


def _retention_chunks(q, k, v, log_gamma, s0):
    b, t, h, d = q.shape
    nc = t // RET_CHUNK
    pos = jnp.arange(RET_CHUNK, dtype=jnp.float32)
    rel = pos[:, None] - pos[None, :]
    intra = jnp.where(rel >= 0, jnp.exp(jnp.maximum(rel, 0.0) * log_gamma[:, None, None]), 0.0)
    q_dec = jnp.exp((pos + 1.0) * log_gamma[:, None])[..., None]
    k_dec = jnp.exp((RET_CHUNK - 1.0 - pos) * log_gamma[:, None])[..., None]
    c_dec = jnp.exp(RET_CHUNK * log_gamma)[:, None, None]

    def to_chunks(a):
        return a.astype(jnp.float32).reshape(b, nc, RET_CHUNK, h, d).transpose(1, 0, 3, 2, 4)

    def step(s, qkv):
        qi, ki, vi = qkv
        a = jnp.einsum('bhnd,bhmd->bhnm', qi, ki) * intra
        o = jnp.einsum('bhnm,bhme->bhne', a, vi) + jnp.einsum('bhnd,bhde->bhne', qi * q_dec, s)
        s = c_dec * s + jnp.einsum('bhmd,bhme->bhde', ki * k_dec, vi)
        return s, o

    s_fin, o = lax.scan(step, s0, (to_chunks(q), to_chunks(k), to_chunks(v)))
    return o.transpose(1, 0, 3, 2, 4).reshape(b, t, h, d), s_fin


def _retention_post(o, gate, gn_g, gn_b):
    b, t, h, d = o.shape
    mu = jnp.mean(o, axis=-1, keepdims=True)
    var = jnp.mean(jnp.square(o - mu), axis=-1, keepdims=True)
    y = ((o - mu) * lax.rsqrt(var + GN_EPS)).reshape(b, t, h * d)
    y = y * gn_g.astype(jnp.float32) + gn_b.astype(jnp.float32)
    return (jax.nn.silu(gate.astype(jnp.float32)) * y).astype(gate.dtype)


def bidirectional_retention(q, k, v, gate, qc, kc, vc, decay_f, decay_b, gn_g, gn_b):
    b, _, h, d = q.shape
    lg_f = jax.nn.log_sigmoid(decay_f.astype(jnp.float32))
    lg_b = jax.nn.log_sigmoid(decay_b.astype(jnp.float32))
    s0 = jnp.zeros((b, h, d, d), jnp.float32)
    rev = lambda a: jnp.flip(a, axis=1)
    oc_f, s_f = _retention_chunks(qc, kc, vc, lg_f, s0)
    oc_b, s_b = _retention_chunks(rev(qc), rev(kc), rev(vc), lg_b, s0)
    o_f, _ = _retention_chunks(q, k, v, lg_f, s_f)
    o_b, _ = _retention_chunks(rev(q), rev(k), rev(v), lg_b, s_b)
    return (_retention_post(o_f + rev(o_b), gate, gn_g, gn_b), oc_f + rev(oc_b))


def window_gqa_sink(q, k, v, kc, vc, sink):
    b, n, hq, d = q.shape
    hkv = k.shape[2]
    grp = hq // hkv
    nb = n // SWA_BLOCK

    def band(a):
        ab = jnp.pad(a.reshape(b, nb, SWA_BLOCK, hkv, d), ((0, 0), (1, 1), (0, 0), (0, 0), (0, 0)))
        return jnp.concatenate([ab[:, :-2], ab[:, 1:-1], ab[:, 2:]], axis=2)

    kw, vw = band(k), band(v)
    qb = q.reshape(b, nb, SWA_BLOCK, hkv, grp, d)
    i = jnp.arange(SWA_BLOCK)[:, None]
    j = jnp.arange(3 * SWA_BLOCK)[None, :]
    kpos = (jnp.arange(nb)[:, None, None] - 1) * SWA_BLOCK + j[None]
    ok = (jnp.abs((j - SWA_BLOCK) - i)[None] <= SWA_WINDOW) & (kpos >= 0) & (kpos < n)
    scale = d ** -0.5
    s_loc = jnp.einsum('bnqkgd,bnskd->bnkgqs', qb, kw).astype(jnp.float32) * scale
    s_loc = jnp.where(ok[None, :, None, None], s_loc, NEG_INF)
    s_ctx = jnp.einsum('bnqkgd,blkd->bnkgql', qb, kc).astype(jnp.float32) * scale
    s_sink = jnp.broadcast_to(sink.astype(jnp.float32).reshape(1, 1, hkv, grp, 1, 1), s_ctx.shape[:-1] + (1,))
    nloc = 3 * SWA_BLOCK
    p = jax.nn.softmax(jnp.concatenate([s_loc, s_ctx, s_sink], axis=-1), axis=-1).astype(v.dtype)
    o = jnp.einsum('bnkgqs,bnskd->bnqkgd', p[..., :nloc], vw) + jnp.einsum('bnkgql,blkd->bnqkgd', p[..., nloc:-1], vc)
    return o.reshape(b, n, hq * d)


def expert_choice_ffn(h, w_router, w_gate, w_up, w_down):
    b, t, dm = h.shape
    cap = EC_CAPACITY * t // N_EXPERTS
    aff = jax.nn.softmax((h @ w_router).astype(jnp.float32), axis=-1)
    gate, idx = lax.top_k(jnp.swapaxes(aff, 1, 2), cap)
    xin = jax.vmap(lambda hb, ib: hb[ib])(h, idx)
    a = jnp.einsum('becd,edf->becf', xin, w_gate)
    u = jnp.einsum('becd,edf->becf', xin, w_up)
    y = jnp.einsum('becf,efd->becd', jax.nn.silu(a) * u, w_down) * gate[..., None].astype(h.dtype)
    return jax.vmap(lambda yb, ib: jnp.zeros((t, dm), yb.dtype).at[ib.reshape(-1)].add(yb.reshape(-1, dm)))(y, idx)


def hybrid_layer(x, xc, c, c_ctx, rope_d, rope_h, layer_idx, with_ctx,
                 w_mod, b_mod, norm1_g, w_in, na_rpb, diff_lambda, diff_norm_g,
                 ret_decay_fwd, ret_decay_bwd, ret_gn_g, ret_gn_b, swa_sink, w_out,
                 norm2_g, w_router, w_gate, w_up, w_down):
    b, n, _ = x.shape
    lc = xc.shape[1]
    mod = jax.nn.silu(c) @ w_mod + b_mod
    mod_c = jax.nn.silu(c_ctx) @ w_mod + b_mod
    sh1, sc1, g1, sh2, sc2, g2 = jnp.split(mod[:, None, :], 6, axis=-1)
    sh1c, sc1c, g1c, sh2c, sc2c, g2c = jnp.split(mod_c, 6, axis=-1)
    h = rmsnorm(x, norm1_g) * (1.0 + sc1) + sh1
    hc = rmsnorm(xc, norm1_g) * (1.0 + sc1c) + sh1c
    pl = jnp.split(h @ w_in, SPLIT_POINTS, axis=-1)
    pc = jnp.split(hc @ w_in, SPLIT_POINTS, axis=-1)

    qa, ka, va = [split_heads(t, NA_HEADS) for t in pl[0:3]]
    qac, kac, vac = [split_heads(t, NA_HEADS) for t in pc[0:3]]
    o_a = neighbourhood_attention(qa, ka, va, kac, vac, na_rpb)

    qb = apply_axial_rope(split_heads(pl[3], 2 * DIFF_HEADS), rope_d).reshape(b, n, DIFF_HEADS, 2, DIFF_DIM)
    kb = apply_axial_rope(split_heads(pl[4], 2 * DIFF_HEADS), rope_d).reshape(b, n, DIFF_HEADS, 2, DIFF_DIM)
    vb = split_heads(pl[5], DIFF_HEADS)
    qbc = pc[3].reshape(b, lc, DIFF_HEADS, 2, DIFF_DIM)
    kbc = pc[4].reshape(b, lc, DIFF_HEADS, 2, DIFF_DIM)
    vbc = split_heads(pc[5], DIFF_HEADS)
    lam_init = 0.8 - 0.6 * math.exp(-0.3 * layer_idx)
    lq1, lk1, lq2, lk2 = diff_lambda.astype(jnp.float32)
    lam = jnp.exp(jnp.sum(lq1 * lk1)) - jnp.exp(jnp.sum(lq2 * lk2)) + lam_init
    o_b = differential_attention(qb, kb, vb, kbc, vbc, lam, lam_init, diff_norm_g)

    ret_scale = HEAD_DIM ** -0.5
    q_ret = apply_axial_rope(split_heads(pl[6], RET_HEADS), rope_h)
    k_ret = apply_axial_rope(split_heads(pl[7], RET_HEADS), rope_h) * ret_scale
    v_ret = split_heads(pl[8], RET_HEADS)
    q_retc = split_heads(pc[6], RET_HEADS)
    k_retc = split_heads(pc[7], RET_HEADS) * ret_scale
    v_retc = split_heads(pc[8], RET_HEADS)
    o_c, ret_ctx = bidirectional_retention(q_ret, k_ret, v_ret, pl[9], q_retc, k_retc, v_retc,
                                           ret_decay_fwd, ret_decay_bwd, ret_gn_g, ret_gn_b)

    qd = apply_axial_rope(split_heads(pl[10], SWA_Q_HEADS), rope_h)
    kd = apply_axial_rope(split_heads(pl[11], SWA_KV_HEADS), rope_h)
    vd = split_heads(pl[12], SWA_KV_HEADS)
    qdc = split_heads(pc[10], SWA_Q_HEADS)
    kdc = split_heads(pc[11], SWA_KV_HEADS)
    vdc = split_heads(pc[12], SWA_KV_HEADS)
    o_d = window_gqa_sink(qd, kd, vd, kdc, vdc, swa_sink)

    x = x + g1 * (jnp.concatenate([o_a, o_b, o_c, o_d], axis=-1) @ w_out)
    h2 = rmsnorm(x, norm2_g) * (1.0 + sc2) + sh2
    x = x + g2 * expert_choice_ffn(h2, w_router, w_gate, w_up, w_down)

    if with_ctx:
        oc = jnp.concatenate([
            context_attention(qac, kac, vac),
            _diff_post(_diff_apply(qbc, kbc, vbc, lam), diff_norm_g, lam_init),
            _retention_post(ret_ctx, pc[9], ret_gn_g, ret_gn_b),
            context_attention(qdc, kdc, vdc, swa_sink)], axis=-1)
        xc = xc + g1c * (oc @ w_out)
        h2c = rmsnorm(xc, norm2_g) * (1.0 + sc2c) + sh2c
        xc = xc + g2c * expert_choice_ffn(h2c, w_router, w_gate, w_up, w_down)
    return x, xc


def setup_inputs(seed: int = 0) -> dict:
    key = jax.random.key(seed)
    ks = jax.random.split(key, 23)
    D = D_MODEL

    def nrm(k, shape, scale):
        return jax.random.normal(k, shape, jnp.float32) * scale

    ret_base = jnp.log(2.0 ** (5.0 + jnp.arange(RET_HEADS, dtype=jnp.float32)) - 1.0)
    return {
        'x': nrm(ks[0], (BATCH, SEQ, D), 1.0),
        'c': nrm(ks[1], (BATCH, D), 1.0),
        'ctx': nrm(ks[2], (BATCH, CTX_LEN, D), 1.0),
        'c_ctx': nrm(ks[3], (D,), 1.0),
        'w_mod': nrm(ks[4], (DEPTH, D, 6 * D), 0.5 * D ** -0.5),
        'b_mod': nrm(ks[5], (DEPTH, 6 * D), 0.02),
        'norm1_g': 1.0 + nrm(ks[6], (DEPTH, D), 0.05),
        'w_in': nrm(ks[7], (DEPTH, D, IN_WIDTH), D ** -0.5),
        'na_rpb': nrm(ks[8], (DEPTH, NA_HEADS, 2 * NA_WIN_R - 1, 2 * NA_WIN_C - 1), 0.1),
        'diff_lambda': nrm(ks[9], (DEPTH, 4, DIFF_DIM), 0.1),
        'diff_norm_g': 1.0 + nrm(ks[10], (DEPTH, 2 * DIFF_DIM), 0.05),
        'ret_decay_fwd': ret_base + nrm(ks[11], (DEPTH, RET_HEADS), 0.1),
        'ret_decay_bwd': ret_base + nrm(ks[12], (DEPTH, RET_HEADS), 0.1),
        'ret_gn_g': 1.0 + nrm(ks[13], (DEPTH, GROUP_WIDTH), 0.05),
        'ret_gn_b': nrm(ks[14], (DEPTH, GROUP_WIDTH), 0.02),
        'swa_sink': nrm(ks[15], (DEPTH, SWA_Q_HEADS), 0.5),
        'w_out': nrm(ks[16], (DEPTH, D, D), D ** -0.5),
        'norm2_g': 1.0 + nrm(ks[17], (DEPTH, D), 0.05),
        'w_router': nrm(ks[18], (DEPTH, D, N_EXPERTS), D ** -0.5),
        'w_gate': nrm(ks[19], (DEPTH, N_EXPERTS, D, EXPERT_FF), D ** -0.5),
        'w_up': nrm(ks[20], (DEPTH, N_EXPERTS, D, EXPERT_FF), D ** -0.5),
        'w_down': nrm(ks[21], (DEPTH, N_EXPERTS, EXPERT_FF, D), EXPERT_FF ** -0.5),
        'final_norm_g': 1.0 + nrm(ks[22], (D,), 0.05),
    }


def reference(x, c, ctx, c_ctx, w_mod, b_mod, norm1_g, w_in, na_rpb, diff_lambda, diff_norm_g,
              ret_decay_fwd, ret_decay_bwd, ret_gn_g, ret_gn_b, swa_sink, w_out, norm2_g,
              w_router, w_gate, w_up, w_down, final_norm_g):
    n = x.shape[1]
    rope_d = axial_rope_tables(n, DIFF_DIM, x.dtype)
    rope_h = axial_rope_tables(n, HEAD_DIM, x.dtype)
    xc = ctx
    for li in range(DEPTH):
        x, xc = hybrid_layer(x, xc, c, c_ctx, rope_d, rope_h, li, li < DEPTH - 1,
                             w_mod[li], b_mod[li], norm1_g[li], w_in[li], na_rpb[li], diff_lambda[li],
                             diff_norm_g[li], ret_decay_fwd[li], ret_decay_bwd[li], ret_gn_g[li], ret_gn_b[li],
                             swa_sink[li], w_out[li], norm2_g[li], w_router[li], w_gate[li], w_up[li], w_down[li])
    return rmsnorm(x, final_norm_g)
```

```python
import functools
import math

import jax
import jax.numpy as jnp
import numpy as np
from jax import lax
from jax.experimental import pallas as pl
from jax.experimental.pallas import tpu as pltpu

GRID_W = 64
HEAD_DIM = 128
DIFF_DIM = HEAD_DIM // 2
NA_WIN_R = 8
NA_WIN_C = 16
RET_CHUNK = 128
SWA_WINDOW = 128
SWA_BLOCK = 128
N_EXPERTS = 16
EC_CAPACITY = 2
ROPE_BASE = 10000.0
NORM_EPS = 1e-6
GN_EPS = 1e-5
NEG_INF = -1e30

BF16 = jnp.bfloat16
F32 = jnp.float32

V7X_VMEM_BYTES = 64 * 1024 * 1024
VMEM_LIMIT_BYTES = 48 * 1024 * 1024


def _diff_attn_kernel(lam_ref, qt_ref, k_ref, vt_ref, kc_ref, vct_ref, g_ref, o_ref,
                      qbd_ref, m_ref, l_ref, acc_ref, *, tq, tk, post_scale):
    qt = qt_ref[...]
    row = lax.broadcasted_iota(jnp.int32, qt.shape, 0)
    zero = jnp.zeros_like(qt)
    qbd_ref[:, :tq] = jnp.where(row < DIFF_DIM, qt, zero).astype(BF16)
    qbd_ref[:, tq:] = jnp.where(row >= DIFF_DIM, qt, zero).astype(BF16)
    m_ref[...] = jnp.full(m_ref.shape, -jnp.inf, F32)
    l_ref[...] = jnp.zeros(l_ref.shape, F32)
    acc_ref[...] = jnp.zeros(acc_ref.shape, F32)

    def step(k_tile, vt_tile):
        s = jnp.dot(k_tile, qbd_ref[...], preferred_element_type=F32)
        m_old = m_ref[...]
        m_new = jnp.maximum(m_old, jnp.max(s, axis=0, keepdims=True))
        alpha = jnp.exp(m_old - m_new)
        p = jnp.exp(s - m_new)
        l_ref[...] = alpha * l_ref[...] + jnp.sum(p, axis=0, keepdims=True)
        pv = jnp.dot(vt_tile, p.astype(BF16), preferred_element_type=F32)
        acc_ref[...] = alpha * acc_ref[...] + pv
        m_ref[...] = m_new

    n_kt = k_ref.shape[0] // tk

    def body(kt, carry):
        off = pl.multiple_of(kt * tk, tk)
        step(k_ref[pl.ds(off, tk), :], vt_ref[:, pl.ds(off, tk)])
        return carry

    lax.fori_loop(0, n_kt, body, 0)
    step(kc_ref[...], vct_ref[...])

    lam = lam_ref[0]
    inv = 1.0 / l_ref[...]
    acc = acc_ref[...]
    o_t = acc[:, :tq] * inv[:, :tq] - lam * (acc[:, tq:] * inv[:, tq:])
    o = o_t.T
    ms = jnp.mean(o * o, axis=-1, keepdims=True)
    y = o * lax.rsqrt(ms + NORM_EPS) * g_ref[...]
    o_ref[...] = (y * post_scale).astype(o_ref.dtype)


def diff_attention(qt, k, vt, kc, vct, lam, norm_g, post_scale, *, tq=256, tk=512):
    b, n, width = k.shape
    h = width // HEAD_DIM
    lc = kc.shape[1]
    tq = min(tq, n)
    tk = min(tk, n)
    assert n % tq == 0 and n % tk == 0
    kern = functools.partial(_diff_attn_kernel, tq=tq, tk=tk, post_scale=post_scale)
    return pl.pallas_call(
        kern,
        grid=(b, h, n // tq),
        in_specs=[
            pl.BlockSpec(memory_space=pltpu.SMEM),
            pl.BlockSpec((None, HEAD_DIM, tq), lambda bi, hi, qi: (bi, hi, qi)),
            pl.BlockSpec((None, n, HEAD_DIM), lambda bi, hi, qi: (bi, 0, hi)),
            pl.BlockSpec((None, HEAD_DIM, n), lambda bi, hi, qi: (bi, hi, 0)),
            pl.BlockSpec((None, lc, HEAD_DIM), lambda bi, hi, qi: (bi, 0, hi)),
            pl.BlockSpec((None, HEAD_DIM, lc), lambda bi, hi, qi: (bi, hi, 0)),
            pl.BlockSpec((1, HEAD_DIM), lambda bi, hi, qi: (0, 0)),
        ],
        out_specs=pl.BlockSpec((None, tq, HEAD_DIM), lambda bi, hi, qi: (bi, qi, hi)),
        out_shape=jax.ShapeDtypeStruct((b, n, width), F32),
        scratch_shapes=[
            pltpu.VMEM((HEAD_DIM, 2 * tq), BF16),
            pltpu.VMEM((1, 2 * tq), F32),
            pltpu.VMEM((1, 2 * tq), F32),
            pltpu.VMEM((HEAD_DIM, 2 * tq), F32),
        ],
        compiler_params=pltpu.CompilerParams(
            dimension_semantics=("parallel", "parallel", "arbitrary"),
            vmem_limit_bytes=VMEM_LIMIT_BYTES),
        name="diff_attention",
    )(lam, qt, k, vt, kc, vct, norm_g)


def _rmsnorm(x, g):
    xf = x.astype(F32)
    y = xf * lax.rsqrt(jnp.mean(xf * xf, axis=-1, keepdims=True) + NORM_EPS)
    return (y * g.astype(F32)).astype(x.dtype)


def _axial_rope_tables(n_tok, dim, dtype):
    t = jnp.arange(n_tok)
    row = (t // GRID_W).astype(F32)
    col = (t % GRID_W).astype(F32)
    nf = dim // 4
    inv = ROPE_BASE ** (-jnp.arange(nf, dtype=F32) / nf)
    ar = row[:, None, None] * inv
    ac = col[:, None, None] * inv
    return tuple(a.astype(dtype) for a in (jnp.cos(ar), jnp.sin(ar), jnp.cos(ac), jnp.sin(ac)))


def _rot_half(u, cos, sin):
    u1, u2 = jnp.split(u, 2, axis=-1)
    return jnp.concatenate([u1 * cos - u2 * sin, u2 * cos + u1 * sin], axis=-1)


def _apply_axial_rope(x, tables):
    cos_r, sin_r, cos_c, sin_c = tables
    x_row, x_col = jnp.split(x, 2, axis=-1)
    return jnp.concatenate([_rot_half(x_row, cos_r, sin_r), _rot_half(x_col, cos_c, sin_c)], axis=-1)


def _split_heads(t, n_heads):
    return t.reshape(t.shape[0], t.shape[1], n_heads, t.shape[-1] // n_heads)


def _context_attention(q, k, v, sink=None):
    b, l, hq, d = q.shape
    hkv = k.shape[2]
    grp = hq // hkv
    qg = q.reshape(b, l, hkv, grp, d)
    s = jnp.einsum('bqkgd,bskd->bkgqs', qg, k).astype(F32) * (d ** -0.5)
    if sink is not None:
        s_sink = jnp.broadcast_to(sink.astype(F32).reshape(1, hkv, grp, 1, 1), s.shape[:-1] + (1,))
        p = jax.nn.softmax(jnp.concatenate([s, s_sink], axis=-1), axis=-1)[..., :-1]
    else:
        p = jax.nn.softmax(s, axis=-1)
    o = jnp.einsum('bkgqs,bskd->bqkgd', p.astype(v.dtype), v)
    return o.reshape(b, l, hq * d)


def _neighbourhood_attention(q, k, v, kc, vc, rpb):
    b, n, h, d = q.shape
    rows = n // GRID_W
    kr = min(NA_WIN_R, rows)
    r = jnp.arange(rows)
    row_idx = jnp.clip(r - kr // 2, 0, rows - kr)[:, None] + jnp.arange(kr)[None, :]
    cols = jnp.arange(GRID_W)
    col_start = jnp.clip(cols - NA_WIN_C // 2, 0, GRID_W - NA_WIN_C)
    col_ok = (cols[None, :] >= col_start[:, None]) & (cols[None, :] < col_start[:, None] + NA_WIN_C)
    row_off = row_idx - r[:, None] + NA_WIN_R - 1
    col_off = jnp.clip(cols[None, :] - cols[:, None] + NA_WIN_C - 1, 0, 2 * NA_WIN_C - 2)
    bias = rpb.astype(F32)[:, row_off][:, :, :, col_off]
    bias = jnp.where(col_ok[None, None, None], bias, NEG_INF).transpose(0, 1, 3, 2, 4)
    qg = q.reshape(b, rows, GRID_W, h, d)
    kg = k.reshape(b, rows, GRID_W, h, d)[:, row_idx]
    vg = v.reshape(b, rows, GRID_W, h, d)[:, row_idx]
    scale = d ** -0.5
    s_loc = jnp.einsum('brqhd,brkchd->bhrqkc', qg, kg).astype(F32) * scale + bias[None]
    s_ctx = jnp.einsum('brqhd,blhd->bhrql', qg, kc).astype(F32) * scale
    nloc = kr * GRID_W
    logits = jnp.concatenate([s_loc.reshape(b, h, rows, GRID_W, nloc), s_ctx], axis=-1)
    p = jax.nn.softmax(logits, axis=-1).astype(v.dtype)
    p_loc = p[..., :nloc].reshape(b, h, rows, GRID_W, kr, GRID_W)
    o = jnp.einsum('bhrqkc,brkchd->brqhd', p_loc, vg) + jnp.einsum('bhrql,blhd->brqhd', p[..., nloc:], vc)
    return o.reshape(b, n, h * d)


def _diff_apply(q, k, v, lam):
    d = q.shape[-1]
    s = jnp.einsum('bqhmd,bshmd->bhmqs', q, k).astype(F32) * (d ** -0.5)
    p = jax.nn.softmax(s, axis=-1)
    w = p[:, :, 0] - lam * p[:, :, 1]
    return jnp.einsum('bhqs,bshe->bqhe', w.astype(v.dtype), v)


def _diff_post(o, g, lam_init):
    b, t, h, e = o.shape
    return (_rmsnorm(o, g) * (1.0 - lam_init)).reshape(b, t, h * e)


def _retention_chunks(q, k, v, log_gamma, s0):
    b, t, h, d = q.shape
    nc = t // RET_CHUNK
    pos = jnp.arange(RET_CHUNK, dtype=F32)
    rel = pos[:, None] - pos[None, :]
    intra = jnp.where(rel >= 0, jnp.exp(jnp.maximum(rel, 0.0) * log_gamma[:, None, None]), 0.0)
    q_dec = jnp.exp((pos + 1.0) * log_gamma[:, None])[..., None]
    k_dec = jnp.exp((RET_CHUNK - 1.0 - pos) * log_gamma[:, None])[..., None]
    c_dec = jnp.exp(RET_CHUNK * log_gamma)[:, None, None]

    def to_chunks(a):
        return a.astype(F32).reshape(b, nc, RET_CHUNK, h, d).transpose(1, 0, 3, 2, 4)

    def step(s, qkv):
        qi, ki, vi = qkv
        a = jnp.einsum('bhnd,bhmd->bhnm', qi, ki) * intra
        o = jnp.einsum('bhnm,bhme->bhne', a, vi) + jnp.einsum('bhnd,bhde->bhne', qi * q_dec, s)
        s = c_dec * s + jnp.einsum('bhmd,bhme->bhde', ki * k_dec, vi)
        return s, o

    s_fin, o = lax.scan(step, s0, (to_chunks(q), to_chunks(k), to_chunks(v)))
    return o.transpose(1, 0, 3, 2, 4).reshape(b, t, h, d), s_fin


def _retention_post(o, gate, gn_g, gn_b):
    b, t, h, d = o.shape
    mu = jnp.mean(o, axis=-1, keepdims=True)
    var = jnp.mean(jnp.square(o - mu), axis=-1, keepdims=True)
    y = ((o - mu) * lax.rsqrt(var + GN_EPS)).reshape(b, t, h * d)
    y = y * gn_g.astype(F32) + gn_b.astype(F32)
    return (jax.nn.silu(gate.astype(F32)) * y).astype(gate.dtype)


def _bidirectional_retention(q, k, v, gate, qc, kc, vc, decay_f, decay_b, gn_g, gn_b):
    b, _, h, d = q.shape
    lg_f = jax.nn.log_sigmoid(decay_f.astype(F32))
    lg_b = jax.nn.log_sigmoid(decay_b.astype(F32))
    s0 = jnp.zeros((b, h, d, d), F32)
    rev = lambda a: jnp.flip(a, axis=1)
    oc_f, s_f = _retention_chunks(qc, kc, vc, lg_f, s0)
    oc_b, s_b = _retention_chunks(rev(qc), rev(kc), rev(vc), lg_b, s0)
    o_f, _ = _retention_chunks(q, k, v, lg_f, s_f)
    o_b, _ = _retention_chunks(rev(q), rev(k), rev(v), lg_b, s_b)
    return (_retention_post(o_f + rev(o_b), gate, gn_g, gn_b), oc_f + rev(oc_b))


def _window_gqa_sink(q, k, v, kc, vc, sink):
    b, n, hq, d = q.shape
    hkv = k.shape[2]
    grp = hq // hkv
    nb = n // SWA_BLOCK

    def band(a):
        ab = jnp.pad(a.reshape(b, nb, SWA_BLOCK, hkv, d), ((0, 0), (1, 1), (0, 0), (0, 0), (0, 0)))
        return jnp.concatenate([ab[:, :-2], ab[:, 1:-1], ab[:, 2:]], axis=2)

    kw, vw = band(k), band(v)
    qb = q.reshape(b, nb, SWA_BLOCK, hkv, grp, d)
    i = jnp.arange(SWA_BLOCK)[:, None]
    j = jnp.arange(3 * SWA_BLOCK)[None, :]
    kpos = (jnp.arange(nb)[:, None, None] - 1) * SWA_BLOCK + j[None]
    ok = (jnp.abs((j - SWA_BLOCK) - i)[None] <= SWA_WINDOW) & (kpos >= 0) & (kpos < n)
    scale = d ** -0.5
    s_loc = jnp.einsum('bnqkgd,bnskd->bnkgqs', qb, kw).astype(F32) * scale
    s_loc = jnp.where(ok[None, :, None, None], s_loc, NEG_INF)
    s_ctx = jnp.einsum('bnqkgd,blkd->bnkgql', qb, kc).astype(F32) * scale
    s_sink = jnp.broadcast_to(sink.astype(F32).reshape(1, 1, hkv, grp, 1, 1), s_ctx.shape[:-1] + (1,))
    nloc = 3 * SWA_BLOCK
    p = jax.nn.softmax(jnp.concatenate([s_loc, s_ctx, s_sink], axis=-1), axis=-1).astype(v.dtype)
    o = jnp.einsum('bnkgqs,bnskd->bnqkgd', p[..., :nloc], vw) + jnp.einsum('bnkgql,blkd->bnqkgd', p[..., nloc:-1], vc)
    return o.reshape(b, n, hq * d)


def _expert_choice_ffn(h, w_router, w_gate, w_up, w_down):
    b, t, dm = h.shape
    cap = EC_CAPACITY * t // N_EXPERTS
    aff = jax.nn.softmax((h @ w_router).astype(F32), axis=-1)
    gate, idx = lax.top_k(jnp.swapaxes(aff, 1, 2), cap)
    xin = jax.vmap(lambda hb, ib: hb[ib])(h, idx)
    a = jnp.einsum('becd,edf->becf', xin, w_gate)
    u = jnp.einsum('becd,edf->becf', xin, w_up)
    y = jnp.einsum('becf,efd->becd', jax.nn.silu(a) * u, w_down) * gate[..., None].astype(h.dtype)
    return jax.vmap(lambda yb, ib: jnp.zeros((t, dm), yb.dtype).at[ib.reshape(-1)].add(yb.reshape(-1, dm)))(y, idx)


def _hybrid_layer(x, xc, c, c_ctx, rope_d, rope_h, layer_idx, with_ctx,
                  w_mod, b_mod, norm1_g, w_in, na_rpb, diff_lambda, diff_norm_g,
                  ret_decay_fwd, ret_decay_bwd, ret_gn_g, ret_gn_b, swa_sink, w_out,
                  norm2_g, w_router, w_gate, w_up, w_down):
    b, n, d_model = x.shape
    lc = xc.shape[1]
    gw = d_model // 4
    n_heads = gw // HEAD_DIM
    kv_heads = n_heads // 2
    kv_w = kv_heads * HEAD_DIM
    splits = (gw,) * 10 + (gw, kv_w, kv_w)
    split_points = tuple(int(v) for v in np.cumsum(splits)[:-1])

    mod = jax.nn.silu(c) @ w_mod + b_mod
    mod_c = jax.nn.silu(c_ctx) @ w_mod + b_mod
    sh1, sc1, g1, sh2, sc2, g2 = jnp.split(mod[:, None, :], 6, axis=-1)
    sh1c, sc1c, g1c, sh2c, sc2c, g2c = jnp.split(mod_c, 6, axis=-1)
    h = _rmsnorm(x, norm1_g) * (1.0 + sc1) + sh1
    hc = _rmsnorm(xc, norm1_g) * (1.0 + sc1c) + sh1c
    pl_ = jnp.split(h @ w_in, split_points, axis=-1)
    pc = jnp.split(hc @ w_in, split_points, axis=-1)

    qa, ka, va = [_split_heads(t, n_heads) for t in pl_[0:3]]
    qac, kac, vac = [_split_heads(t, n_heads) for t in pc[0:3]]
    o_a = _neighbourhood_attention(qa, ka, va, kac, vac, na_rpb)

    qb = _apply_axial_rope(_split_heads(pl_[3], 2 * n_heads), rope_d).reshape(b, n, gw)
    kb = _apply_axial_rope(_split_heads(pl_[4], 2 * n_heads), rope_d).reshape(b, n, gw)
    lam_init = 0.8 - 0.6 * math.exp(-0.3 * layer_idx)
    lq1, lk1, lq2, lk2 = diff_lambda.astype(F32)
    lam = jnp.exp(jnp.sum(lq1 * lk1)) - jnp.exp(jnp.sum(lq2 * lk2)) + lam_init
    qt = jnp.swapaxes(qb * (DIFF_DIM ** -0.5), 1, 2)
    o_b = diff_attention(qt, kb.astype(BF16), jnp.swapaxes(pl_[5], 1, 2).astype(BF16),
                         pc[4].astype(BF16), jnp.swapaxes(pc[5], 1, 2).astype(BF16),
                         lam.reshape(1), diff_norm_g.reshape(1, HEAD_DIM).astype(F32), 1.0 - lam_init)

    ret_scale = HEAD_DIM ** -0.5
    q_ret = _apply_axial_rope(_split_heads(pl_[6], n_heads), rope_h)
    k_ret = _apply_axial_rope(_split_heads(pl_[7], n_heads), rope_h) * ret_scale
    v_ret = _split_heads(pl_[8], n_heads)
    q_retc = _split_heads(pc[6], n_heads)
    k_retc = _split_heads(pc[7], n_heads) * ret_scale
    v_retc = _split_heads(pc[8], n_heads)
    o_c, ret_ctx = _bidirectional_retention(q_ret, k_ret, v_ret, pl_[9], q_retc, k_retc, v_retc,
                                            ret_decay_fwd, ret_decay_bwd, ret_gn_g, ret_gn_b)

    qd = _apply_axial_rope(_split_heads(pl_[10], n_heads), rope_h)
    kd = _apply_axial_rope(_split_heads(pl_[11], kv_heads), rope_h)
    vd = _split_heads(pl_[12], kv_heads)
    qdc = _split_heads(pc[10], n_heads)
    kdc = _split_heads(pc[11], kv_heads)
    vdc = _split_heads(pc[12], kv_heads)
    o_d = _window_gqa_sink(qd, kd, vd, kdc, vdc, swa_sink)

    x = x + g1 * (jnp.concatenate([o_a, o_b, o_c, o_d], axis=-1) @ w_out)
    h2 = _rmsnorm(x, norm2_g) * (1.0 + sc2) + sh2
    x = x + g2 * _expert_choice_ffn(h2, w_router, w_gate, w_up, w_down)

    if with_ctx:
        qbc = pc[3].reshape(b, lc, n_heads, 2, DIFF_DIM)
        kbc = pc[4].reshape(b, lc, n_heads, 2, DIFF_DIM)
        vbc = _split_heads(pc[5], n_heads)
        oc = jnp.concatenate([
            _context_attention(qac, kac, vac),
            _diff_post(_diff_apply(qbc, kbc, vbc, lam), diff_norm_g, lam_init),
            _retention_post(ret_ctx, pc[9], ret_gn_g, ret_gn_b),
            _context_attention(qdc, kdc, vdc, swa_sink)], axis=-1)
        xc = xc + g1c * (oc @ w_out)
        h2c = _rmsnorm(xc, norm2_g) * (1.0 + sc2c) + sh2c
        xc = xc + g2c * _expert_choice_ffn(h2c, w_router, w_gate, w_up, w_down)
    return x, xc


def kernel(x, c, ctx, c_ctx, w_mod, b_mod, norm1_g, w_in, na_rpb, diff_lambda, diff_norm_g,
           ret_decay_fwd, ret_decay_bwd, ret_gn_g, ret_gn_b, swa_sink, w_out, norm2_g,
           w_router, w_gate, w_up, w_down, final_norm_g):
    n = x.shape[1]
    depth = w_in.shape[0]
    rope_d = _axial_rope_tables(n, DIFF_DIM, x.dtype)
    rope_h = _axial_rope_tables(n, HEAD_DIM, x.dtype)
    xc = ctx
    for li in range(depth):
        x, xc = _hybrid_layer(x, xc, c, c_ctx, rope_d, rope_h, li, li < depth - 1,
                              w_mod[li], b_mod[li], norm1_g[li], w_in[li], na_rpb[li], diff_lambda[li],
                              diff_norm_g[li], ret_decay_fwd[li], ret_decay_bwd[li], ret_gn_g[li],
                              ret_gn_b[li], swa_sink[li], w_out[li], norm2_g[li], w_router[li],
                              w_gate[li], w_up[li], w_down[li])
    return _rmsnorm(x, final_norm_g)
```

```python
import functools
import math

import jax
import jax.numpy as jnp
from jax import lax
from jax.experimental import pallas as pl
from jax.experimental.pallas import tpu as pltpu

GRID_W = 64
HEAD_DIM = 128
DIFF_DIM = HEAD_DIM // 2
NA_WIN_R = 8
NA_WIN_C = 16
RET_CHUNK = 128
SWA_WINDOW = 128
N_EXPERTS = 16
EC_CAPACITY = 2
ROPE_BASE = 10000.0
NORM_EPS = 1e-6
GN_EPS = 1e-5
NEG_INF = -1e30

BF16 = jnp.bfloat16
F32 = jnp.float32

VMEM_LIMIT_BYTES = 48 * 1024 * 1024

GROUP_BLOCKS = 4
COL_A_Q, COL_A_K, COL_A_V = 0, 4, 8
COL_B_K = 16
COL_C_Q, COL_C_K, COL_C_V, COL_C_G = 24, 28, 32, 36
COL_D_Q, COL_D_K, COL_D_V = 40, 44, 46
PROJ_TN = GROUP_BLOCKS * HEAD_DIM


def _cparams(*sem):
    return pltpu.CompilerParams(dimension_semantics=sem, vmem_limit_bytes=VMEM_LIMIT_BYTES)


def _mod_kernel(c_ref, w_ref, b_ref, o_ref):
    c = c_ref[...]
    s = (c * (1.0 / (1.0 + jnp.exp(-c)))).astype(BF16)
    o_ref[...] = jnp.dot(s, w_ref[...].astype(BF16), preferred_element_type=F32) + b_ref[...]


def modulation(cvec, w_mod, b_mod, *, tn=1024):
    depth, d, n6 = w_mod.shape
    return pl.pallas_call(
        _mod_kernel,
        grid=(depth, n6 // tn),
        in_specs=[
            pl.BlockSpec((8, d), lambda l, j: (0, 0)),
            pl.BlockSpec((None, d, tn), lambda l, j: (l, 0, j)),
            pl.BlockSpec((None, 1, tn), lambda l, j: (l, 0, j)),
        ],
        out_specs=pl.BlockSpec((None, 8, tn), lambda l, j: (l, 0, j)),
        out_shape=jax.ShapeDtypeStruct((depth, 8, n6), F32),
        compiler_params=_cparams("parallel", "parallel"),
        name="modulation",
    )(cvec, w_mod, b_mod)


def _rope_tables(n_tok, dim):
    t = jnp.arange(n_tok)
    row = (t // GRID_W).astype(F32)[:, None]
    col = (t % GRID_W).astype(F32)[:, None]
    nf = dim // 4
    lane = jnp.arange(HEAD_DIM)
    quarter = (lane % dim) // nf
    inv = ROPE_BASE ** (-jnp.arange(nf, dtype=F32) / nf)
    ang = jnp.where(quarter[None, :] < 2, row, col) * inv[lane % nf][None, :]
    cos, sin = jnp.cos(ang), jnp.sin(ang)
    even = (quarter % 2 == 0)[None, :]
    return cos, jnp.where(even, -sin, 0.0), jnp.where(even, 0.0, sin)


def _identity_rope_tables(n_tok):
    z = jnp.zeros((n_tok, HEAD_DIM), F32)
    return jnp.ones((n_tok, HEAD_DIM), F32), z, z


def _proj_modes():
    att = HEAD_DIM ** -0.5
    plain = [(None, 1.0)] * GROUP_BLOCKS
    return [
        ([(None, att)] * 4, None), (plain, None), (plain, None),
        ([("d", DIFF_DIM ** -0.5)] * 4, 0), ([("d", 1.0)] * 4, None), (plain, 1),
        ([("h", 1.0)] * 4, None), ([("h", att)] * 4, None), (plain, None), (plain, None),
        ([("h", att)] * 4, None), ([("h", 1.0)] * 2 + [(None, 1.0)] * 2, None),
    ]


def _proj_kernel(x_ref, g_ref, sc_ref, sh_ref, w_ref, ch_ref, sah_ref, sbh_ref, cd_ref, sad_ref, sbd_ref,
                 o_ref, t_ref, h_ref):
    j = pl.program_id(2)

    @pl.when(j == 0)
    def _():
        x = x_ref[...]
        y = x * lax.rsqrt(jnp.mean(x * x, axis=-1, keepdims=True) + NORM_EPS) * g_ref[...]
        h_ref[...] = (y * (1.0 + sc_ref[...]) + sh_ref[...]).astype(BF16)

    acc = jnp.dot(h_ref[...], w_ref[...], preferred_element_type=F32)

    def rope(a, kind):
        if kind is None:
            return a
        c, sa, sb, sh = ((ch_ref, sah_ref, sbh_ref, HEAD_DIM // 4) if kind == "h"
                         else (cd_ref, sad_ref, sbd_ref, DIFF_DIM // 4))
        return (a * c[...] + pltpu.roll(a, HEAD_DIM - sh, 1) * sa[...] + pltpu.roll(a, sh, 1) * sb[...])

    for jj, (blocks, t_slot) in enumerate(_proj_modes()):
        @pl.when(j == jj)
        def _(blocks=blocks, t_slot=t_slot):
            outs = []
            for hb, (kind, scale) in enumerate(blocks):
                a = rope(acc[:, hb * HEAD_DIM:(hb + 1) * HEAD_DIM], kind)
                outs.append(a if scale == 1.0 else a * scale)
            full = jnp.concatenate(outs, axis=1)
            o_ref[...] = full.astype(o_ref.dtype)
            if t_slot is not None:
                t_ref[t_slot] = full.T.astype(t_ref.dtype)


def norm_project(x, norm_g, scale, shift, w_in, rope_h, rope_d, *, tm=1024):
    b, n, d = x.shape
    width = w_in.shape[1]
    tm = min(tm, n)
    n_j = width // PROJ_TN
    tok = lambda bi, i, j: (i, 0)
    return pl.pallas_call(
        _proj_kernel,
        grid=(b, n // tm, n_j),
        in_specs=[
            pl.BlockSpec((None, tm, d), lambda bi, i, j: (bi, i, 0)),
            pl.BlockSpec((1, d), lambda bi, i, j: (0, 0)),
            pl.BlockSpec((None, 1, d), lambda bi, i, j: (bi, 0, 0)),
            pl.BlockSpec((None, 1, d), lambda bi, i, j: (bi, 0, 0)),
            pl.BlockSpec((d, PROJ_TN), lambda bi, i, j: (0, j)),
        ] + [pl.BlockSpec((tm, HEAD_DIM), tok)] * 6,
        out_specs=[
            pl.BlockSpec((None, tm, PROJ_TN), lambda bi, i, j: (bi, i, j)),
            pl.BlockSpec((None, 2, PROJ_TN, tm), lambda bi, i, j: (bi, 0, 0, i)),
        ],
        out_shape=[jax.ShapeDtypeStruct((b, n, width), BF16),
                   jax.ShapeDtypeStruct((b, 2, PROJ_TN, n), BF16)],
        scratch_shapes=[pltpu.VMEM((tm, d), BF16)],
        compiler_params=_cparams("parallel", "parallel", "arbitrary"),
        name="norm_project",
    )(x, norm_g, scale, shift, w_in, *rope_h, *rope_d)


def _softmax_pv(s_loc, s_ctx, vw, vc, sink):
    m = jnp.maximum(jnp.max(s_loc, axis=-1, keepdims=True), jnp.max(s_ctx, axis=-1, keepdims=True))
    if sink is not None:
        m = jnp.maximum(m, sink)
    p_loc = jnp.exp(s_loc - m)
    p_ctx = jnp.exp(s_ctx - m)
    l = jnp.sum(p_loc, axis=-1, keepdims=True) + jnp.sum(p_ctx, axis=-1, keepdims=True)
    if sink is not None:
        l = l + jnp.exp(sink - m)
    o = (jnp.dot(p_loc.astype(BF16), vw, preferred_element_type=F32)
         + jnp.dot(p_ctx.astype(BF16), vc, preferred_element_type=F32))
    return o * (1.0 / l)


def _nt_dot(a, b):
    return lax.dot_general(a, b, (((1,), (1,)), ((), ())), preferred_element_type=F32)


def _na_kernel(q_ref, k_ref, v_ref, kc_ref, vc_ref, bias_ref, o_ref, *, tq, wk):
    i = pl.program_id(2)
    n = k_ref.shape[0]
    rows_q = tq // GRID_W
    kstart = jnp.clip(i * rows_q - NA_WIN_R // 2, 0, (n - wk) // GRID_W) * GRID_W
    kstart = pl.multiple_of(kstart, GRID_W)
    q = q_ref[...]
    kw = k_ref[pl.ds(kstart, wk), :]
    vw = v_ref[pl.ds(kstart, wk), :]
    s_loc = _nt_dot(q, kw) + bias_ref[...]
    s_ctx = _nt_dot(q, kc_ref[...])
    o_ref[...] = _softmax_pv(s_loc, s_ctx, vw, vc_ref[...], None).astype(o_ref.dtype)


def _na_bias(rpb, rows, rows_q, rows_k):
    n_blk = rows // rows_q
    kr = NA_WIN_R
    cols = jnp.arange(GRID_W)
    col_start = jnp.clip(cols - NA_WIN_C // 2, 0, GRID_W - NA_WIN_C)
    col_ok = (cols[None, :] >= col_start[:, None]) & (cols[None, :] < col_start[:, None] + NA_WIN_C)
    col_off = jnp.clip(cols[None, :] - cols[:, None] + NA_WIN_C - 1, 0, 2 * NA_WIN_C - 2)
    out = []
    for blk in (0, 1, n_blk - 1):
        r = blk * rows_q + jnp.arange(rows_q)
        kstart = min(max(blk * rows_q - kr // 2, 0), rows - rows_k)
        krow = kstart + jnp.arange(rows_k)
        start = jnp.clip(r - kr // 2, 0, rows - kr)
        row_ok = (krow[None, :] >= start[:, None]) & (krow[None, :] < start[:, None] + kr)
        row_off = jnp.clip(krow[None, :] - r[:, None] + NA_WIN_R - 1, 0, 2 * NA_WIN_R - 2)
        bias = rpb.astype(F32)[:, row_off][:, :, :, col_off]
        ok = row_ok[:, :, None, None] & col_ok[None, None]
        bias = jnp.where(ok[None], bias, NEG_INF).transpose(0, 1, 3, 2, 4)
        out.append(bias.reshape(rpb.shape[0], rows_q * GRID_W, rows_k * GRID_W))
    return jnp.stack(out, axis=1)


def neighbourhood_attention(p, pc, rpb, *, rows_q=4):
    b, n, _ = p.shape
    lc = pc.shape[1]
    h = rpb.shape[0]
    rows = n // GRID_W
    rows_k = rows_q + NA_WIN_R - 1
    assert rows % rows_q == 0 and rows >= rows_k and rows_q >= NA_WIN_R // 2
    tq, wk = rows_q * GRID_W, rows_k * GRID_W
    n_blk = rows // rows_q
    bias = _na_bias(rpb, rows, rows_q, rows_k)
    kern = functools.partial(_na_kernel, tq=tq, wk=wk)
    cls = lambda i: jnp.where(i == 0, 0, jnp.where(i == n_blk - 1, 2, 1))
    return pl.pallas_call(
        kern,
        grid=(b, h, n_blk),
        in_specs=[
            pl.BlockSpec((None, tq, HEAD_DIM), lambda bi, hi, i: (bi, i, COL_A_Q + hi)),
            pl.BlockSpec((None, n, HEAD_DIM), lambda bi, hi, i: (bi, 0, COL_A_K + hi)),
            pl.BlockSpec((None, n, HEAD_DIM), lambda bi, hi, i: (bi, 0, COL_A_V + hi)),
            pl.BlockSpec((None, lc, HEAD_DIM), lambda bi, hi, i: (bi, 0, COL_A_K + hi)),
            pl.BlockSpec((None, lc, HEAD_DIM), lambda bi, hi, i: (bi, 0, COL_A_V + hi)),
            pl.BlockSpec((None, None, tq, wk), lambda bi, hi, i: (hi, cls(i), 0, 0)),
        ],
        out_specs=pl.BlockSpec((None, tq, HEAD_DIM), lambda bi, hi, i: (bi, i, hi)),
        out_shape=jax.ShapeDtypeStruct((b, n, h * HEAD_DIM), BF16),
        compiler_params=_cparams("parallel", "parallel", "arbitrary"),
        name="neighbourhood_attention",
    )(p, p, p, pc, pc, bias)


def _swa_kernel(sink_ref, q_ref, k_ref, v_ref, kc_ref, vc_ref, o_ref, *, tq, wk):
    hi = pl.program_id(1)
    i = pl.program_id(2)
    n = k_ref.shape[0]
    q0 = i * tq
    kstart = pl.multiple_of(jnp.clip(q0 - SWA_WINDOW, 0, n - wk), SWA_WINDOW)
    q = q_ref[...]
    kw = k_ref[pl.ds(kstart, wk), :]
    vw = v_ref[pl.ds(kstart, wk), :]
    qpos = q0 + lax.broadcasted_iota(jnp.int32, (tq, wk), 0)
    kpos = kstart + lax.broadcasted_iota(jnp.int32, (tq, wk), 1)
    s_loc = jnp.where(jnp.abs(kpos - qpos) <= SWA_WINDOW, _nt_dot(q, kw), NEG_INF)
    s_ctx = _nt_dot(q, kc_ref[...])
    o_ref[...] = _softmax_pv(s_loc, s_ctx, vw, vc_ref[...], sink_ref[hi]).astype(o_ref.dtype)


def window_attention(p, pc, sink, *, tq=256):
    b, n, _ = p.shape
    lc = pc.shape[1]
    hq = sink.shape[0]
    grp = 2
    tq = min(tq, n)
    wk = min(tq + 2 * SWA_WINDOW, n)
    kern = functools.partial(_swa_kernel, tq=tq, wk=wk)
    return pl.pallas_call(
        kern,
        grid=(b, hq, n // tq),
        in_specs=[
            pl.BlockSpec(memory_space=pltpu.SMEM),
            pl.BlockSpec((None, tq, HEAD_DIM), lambda bi, hi, i: (bi, i, COL_D_Q + hi)),
            pl.BlockSpec((None, n, HEAD_DIM), lambda bi, hi, i: (bi, 0, COL_D_K + hi // grp)),
            pl.BlockSpec((None, n, HEAD_DIM), lambda bi, hi, i: (bi, 0, COL_D_V + hi // grp)),
            pl.BlockSpec((None, lc, HEAD_DIM), lambda bi, hi, i: (bi, 0, COL_D_K + hi // grp)),
            pl.BlockSpec((None, lc, HEAD_DIM), lambda bi, hi, i: (bi, 0, COL_D_V + hi // grp)),
        ],
        out_specs=pl.BlockSpec((None, tq, HEAD_DIM), lambda bi, hi, i: (bi, i, hi)),
        out_shape=jax.ShapeDtypeStruct((b, n, hq * HEAD_DIM), BF16),
        compiler_params=_cparams("parallel", "parallel", "arbitrary"),
        name="window_attention",
    )(sink, p, p, p, pc, pc)


def _diff_attn_kernel(lam_ref, qt_ref, k_ref, vt_ref, kc_ref, vct_ref, g_ref, o_ref,
                      qbd_ref, m_ref, l_ref, acc_ref, *, tq, tk, post_scale):
    qt = qt_ref[...].astype(F32)
    row = lax.broadcasted_iota(jnp.int32, qt.shape, 0)
    zero = jnp.zeros_like(qt)
    qbd_ref[:, :tq] = jnp.where(row < DIFF_DIM, qt, zero).astype(BF16)
    qbd_ref[:, tq:] = jnp.where(row >= DIFF_DIM, qt, zero).astype(BF16)
    m_ref[...] = jnp.full(m_ref.shape, -jnp.inf, F32)
    l_ref[...] = jnp.zeros(l_ref.shape, F32)
    acc_ref[...] = jnp.zeros(acc_ref.shape, F32)

    def step(k_tile, vt_tile):
        s = jnp.dot(k_tile, qbd_ref[...], preferred_element_type=F32)
        m_old = m_ref[...]
        m_new = jnp.maximum(m_old, jnp.max(s, axis=0, keepdims=True))
        alpha = jnp.exp(m_old - m_new)
        p = jnp.exp(s - m_new)
        l_ref[...] = alpha * l_ref[...] + jnp.sum(p, axis=0, keepdims=True)
        pv = jnp.dot(vt_tile, p.astype(BF16), preferred_element_type=F32)
        acc_ref[...] = alpha * acc_ref[...] + pv
        m_ref[...] = m_new

    n_kt = k_ref.shape[0] // tk

    def body(kt, carry):
        off = pl.multiple_of(kt * tk, tk)
        step(k_ref[pl.ds(off, tk), :], vt_ref[:, pl.ds(off, tk)])
        return carry

    lax.fori_loop(0, n_kt, body, 0)
    step(kc_ref[...], vct_ref[...])

    lam = lam_ref[0]
    inv = 1.0 / l_ref[...]
    acc = acc_ref[...]
    o_t = acc[:, :tq] * inv[:, :tq] - lam * (acc[:, tq:] * inv[:, tq:])
    o = o_t.T
    ms = jnp.mean(o * o, axis=-1, keepdims=True)
    y = o * lax.rsqrt(ms + NORM_EPS) * g_ref[...]
    o_ref[...] = (y * post_scale).astype(o_ref.dtype)


def diff_attention(p, t, pc, tc, lam, norm_g, post_scale, *, tq=256, tk=512):
    b, n, _ = p.shape
    lc = pc.shape[1]
    h = GROUP_BLOCKS
    tq = min(tq, n)
    tk = min(tk, n)
    assert n % tq == 0 and n % tk == 0
    kern = functools.partial(_diff_attn_kernel, tq=tq, tk=tk, post_scale=post_scale)
    return pl.pallas_call(
        kern,
        grid=(b, h, n // tq),
        in_specs=[
            pl.BlockSpec(memory_space=pltpu.SMEM),
            pl.BlockSpec((None, None, HEAD_DIM, tq), lambda bi, hi, qi: (bi, 0, hi, qi)),
            pl.BlockSpec((None, n, HEAD_DIM), lambda bi, hi, qi: (bi, 0, COL_B_K + hi)),
            pl.BlockSpec((None, None, HEAD_DIM, n), lambda bi, hi, qi: (bi, 1, hi, 0)),
            pl.BlockSpec((None, lc, HEAD_DIM), lambda bi, hi, qi: (bi, 0, COL_B_K + hi)),
            pl.BlockSpec((None, None, HEAD_DIM, lc), lambda bi, hi, qi: (bi, 1, hi, 0)),
            pl.BlockSpec((1, HEAD_DIM), lambda bi, hi, qi: (0, 0)),
        ],
        out_specs=pl.BlockSpec((None, tq, HEAD_DIM), lambda bi, hi, qi: (bi, qi, hi)),
        out_shape=jax.ShapeDtypeStruct((b, n, h * HEAD_DIM), BF16),
        scratch_shapes=[
            pltpu.VMEM((HEAD_DIM, 2 * tq), BF16),
            pltpu.VMEM((1, 2 * tq), F32),
            pltpu.VMEM((1, 2 * tq), F32),
            pltpu.VMEM((HEAD_DIM, 2 * tq), F32),
        ],
        compiler_params=_cparams("parallel", "parallel", "arbitrary"),
        name="diff_attention",
    )(lam, t, p, t, pc, tc, norm_g)


def _retention_kernel(cdec_ref, q_ref, k_ref, v_ref, intra_ref, qdec_ref, kdec_ref, s0_ref, *rest,
                      reverse, final):
    if final:
        of_ref, gate_ref, gng_ref, gnb_ref, o_ref, sfin_ref, s_ref = rest
    else:
        o_ref, sfin_ref, s_ref = rest
    i = pl.program_id(1)
    n_heads = s_ref.shape[0]
    n_chunks = q_ref.shape[0] // RET_CHUNK

    @pl.when(i == 0)
    def _():
        s_ref[...] = s0_ref[...]

    order = range(n_chunks - 1, -1, -1) if reverse else range(n_chunks)
    for c in order:
        rs = slice(c * RET_CHUNK, (c + 1) * RET_CHUNK)
        for h in range(n_heads):
            cs = slice(h * HEAD_DIM, (h + 1) * HEAD_DIM)
            qh, kh, vh = q_ref[rs, cs], k_ref[rs, cs], v_ref[rs, cs]
            qd = (qh.astype(F32) * qdec_ref[h]).astype(BF16)
            kd_t = (kh.astype(F32) * kdec_ref[h]).T.astype(BF16)
            a = (_nt_dot(qh, kh) * intra_ref[h]).astype(BF16)
            s = s_ref[h]
            o = (jnp.dot(a, vh, preferred_element_type=F32)
                 + jnp.dot(qd, s.astype(BF16), preferred_element_type=F32))
            s_ref[h] = cdec_ref[h] * s + jnp.dot(kd_t, vh, preferred_element_type=F32)
            if final:
                o = o + of_ref[rs, cs]
                mu = jnp.mean(o, axis=-1, keepdims=True)
                var = jnp.mean(jnp.square(o - mu), axis=-1, keepdims=True)
                y = (o - mu) * lax.rsqrt(var + GN_EPS) * gng_ref[:, cs] + gnb_ref[:, cs]
                g = gate_ref[rs, cs].astype(F32)
                o = g * (1.0 / (1.0 + jnp.exp(-g))) * y
            o_ref[rs, cs] = o.astype(o_ref.dtype)

    @pl.when(i == pl.num_programs(1) - 1)
    def _():
        sfin_ref[...] = s_ref[...]


def _retention_pass(p, tables, s0, final_inputs, *, reverse, blk_chunks=4):
    b, n, _ = p.shape
    cdec, intra, qdec, kdec = tables
    h = intra.shape[0]
    width = h * HEAD_DIM
    n_chunks = n // RET_CHUNK
    blk_chunks = min(blk_chunks, n_chunks)
    assert n_chunks % blk_chunks == 0
    tb = blk_chunks * RET_CHUNK
    n_blk = n // tb
    pos = (lambda i: n_blk - 1 - i) if reverse else (lambda i: i)
    colblk = lambda c: (lambda bi, i: (bi, pos(i), c // GROUP_BLOCKS))
    tab = pl.BlockSpec((h, RET_CHUNK, HEAD_DIM), lambda bi, i: (0, 0, 0))
    state = pl.BlockSpec((None, h, HEAD_DIM, HEAD_DIM), lambda bi, i: (bi, 0, 0, 0))
    in_specs = [pl.BlockSpec(memory_space=pltpu.SMEM),
                pl.BlockSpec((None, tb, width), colblk(COL_C_Q)),
                pl.BlockSpec((None, tb, width), colblk(COL_C_K)),
                pl.BlockSpec((None, tb, width), colblk(COL_C_V)),
                tab, tab, tab, state]
    args = [cdec, p, p, p, intra, qdec, kdec, s0]
    final = final_inputs is not None
    if final:
        o_fwd, gn_g, gn_b = final_inputs
        in_specs += [pl.BlockSpec((None, tb, width), lambda bi, i: (bi, pos(i), 0)),
                     pl.BlockSpec((None, tb, width), colblk(COL_C_G)),
                     pl.BlockSpec((1, width), lambda bi, i: (0, 0)),
                     pl.BlockSpec((1, width), lambda bi, i: (0, 0))]
        args += [o_fwd, p, gn_g, gn_b]
    kern = functools.partial(_retention_kernel, reverse=reverse, final=final)
    return pl.pallas_call(
        kern,
        grid=(b, n_blk),
        in_specs=in_specs,
        out_specs=[pl.BlockSpec((None, tb, width), lambda bi, i: (bi, pos(i), 0)), state],
        out_shape=[jax.ShapeDtypeStruct((b, n, width), BF16 if final else F32),
                   jax.ShapeDtypeStruct((b, h, HEAD_DIM, HEAD_DIM), F32)],
        scratch_shapes=[pltpu.VMEM((h, HEAD_DIM, HEAD_DIM), F32)],
        compiler_params=_cparams("parallel", "arbitrary"),
        name="retention_bwd" if reverse else "retention_fwd",
    )(*args)


def _retention_tables(decay, reverse):
    lg = jax.nn.log_sigmoid(decay.astype(F32))[:, None, None]
    pos = jnp.arange(RET_CHUNK, dtype=F32)
    rel = pos[:, None] - pos[None, :]
    if reverse:
        rel = -rel
        q_pow, k_pow = RET_CHUNK - pos, pos
    else:
        q_pow, k_pow = pos + 1.0, RET_CHUNK - 1.0 - pos
    intra = jnp.where(rel >= 0, jnp.exp(jnp.maximum(rel, 0.0) * lg), 0.0)
    bc = lambda e: jnp.broadcast_to(jnp.exp(e[None, :, None] * lg), intra.shape)
    cdec = jnp.exp(RET_CHUNK * lg[:, 0, 0])
    return cdec, intra, bc(q_pow), bc(k_pow)


def bidirectional_retention(p, pc, decay_f, decay_b, gn_g, gn_b, with_ctx):
    b = p.shape[0]
    h = decay_f.shape[0]
    tf = _retention_tables(decay_f, False)
    tb = _retention_tables(decay_b, True)
    s0 = jnp.zeros((b, h, HEAD_DIM, HEAD_DIM), F32)
    gn = (gn_g.reshape(1, -1).astype(F32), gn_b.reshape(1, -1).astype(F32))
    oc_f, s_f = _retention_pass(pc, tf, s0, None, reverse=False)
    oc, s_b = _retention_pass(pc, tb, s0, (oc_f,) + gn, reverse=True)
    o_f, _ = _retention_pass(p, tf, s_f, None, reverse=False)
    o, _ = _retention_pass(p, tb, s_b, (o_f,) + gn, reverse=True)
    return o, (oc if with_ctx else None)


def _out_kernel(oa_ref, ob_ref, oc_ref, od_ref, w_ref, x_ref, g1_ref, ng_ref, sc_ref, sh_ref, wr_ref,
                xo_ref, h_ref, aff_ref):
    gw = oa_ref.shape[1]
    acc = jnp.dot(oa_ref[...], w_ref[0 * gw:1 * gw, :], preferred_element_type=F32)
    acc += jnp.dot(ob_ref[...], w_ref[1 * gw:2 * gw, :], preferred_element_type=F32)
    acc += jnp.dot(oc_ref[...], w_ref[2 * gw:3 * gw, :], preferred_element_type=F32)
    acc += jnp.dot(od_ref[...], w_ref[3 * gw:4 * gw, :], preferred_element_type=F32)
    x = x_ref[...] + g1_ref[...] * acc
    xo_ref[...] = x
    y = x * lax.rsqrt(jnp.mean(x * x, axis=-1, keepdims=True) + NORM_EPS) * ng_ref[...]
    h2 = (y * (1.0 + sc_ref[...]) + sh_ref[...]).astype(BF16)
    h_ref[...] = h2
    logits = _nt_dot(wr_ref[...], h2)
    e = jnp.exp(logits - jnp.max(logits, axis=0, keepdims=True))
    aff_ref[...] = e * (1.0 / jnp.sum(e, axis=0, keepdims=True))


def out_project(o_groups, w_out, x, g1, norm_g, scale, shift, w_router_t, *, tm=256):
    b, n, d = x.shape
    n_e = w_router_t.shape[0]
    gw = o_groups[0].shape[-1]
    tm = min(tm, n)
    tile = lambda w: pl.BlockSpec((None, tm, w), lambda bi, i: (bi, i, 0))
    vec = pl.BlockSpec((None, 1, d), lambda bi, i: (bi, 0, 0))
    return pl.pallas_call(
        _out_kernel,
        grid=(b, n // tm),
        in_specs=[tile(gw)] * 4 + [
            pl.BlockSpec((d, d), lambda bi, i: (0, 0)),
            tile(d), vec,
            pl.BlockSpec((1, d), lambda bi, i: (0, 0)),
            vec, vec,
            pl.BlockSpec((n_e, d), lambda bi, i: (0, 0)),
        ],
        out_specs=[tile(d), tile(d), pl.BlockSpec((None, n_e, tm), lambda bi, i: (bi, 0, i))],
        out_shape=[jax.ShapeDtypeStruct((b, n, d), F32),
                   jax.ShapeDtypeStruct((b, n, d), BF16),
                   jax.ShapeDtypeStruct((b, n_e, n), F32)],
        compiler_params=_cparams("parallel", "parallel"),
        name="out_project",
    )(*o_groups, w_out, x, g1, norm_g, scale, shift, w_router_t)


def _ffn_kernel(x_ref, gate_ref, wg_ref, wu_ref, wd_ref, o_ref):
    x = x_ref[...]
    a = jnp.dot(x, wg_ref[...], preferred_element_type=F32)
    u = jnp.dot(x, wu_ref[...], preferred_element_type=F32)
    hm = (a * (1.0 / (1.0 + jnp.exp(-a))) * u).astype(BF16)
    o_ref[...] = jnp.dot(hm, wd_ref[...], preferred_element_type=F32) * gate_ref[...]


def expert_ffn(xin, gate, w_gate, w_up, w_down, *, tc=256):
    b, n_e, cap, d = xin.shape
    ff = w_gate.shape[-1]
    tc = min(tc, cap)
    return pl.pallas_call(
        _ffn_kernel,
        grid=(n_e, b, cap // tc),
        in_specs=[
            pl.BlockSpec((None, None, tc, d), lambda e, bi, i: (bi, e, i, 0)),
            pl.BlockSpec((None, None, tc, 1), lambda e, bi, i: (bi, e, i, 0)),
            pl.BlockSpec((None, d, ff), lambda e, bi, i: (e, 0, 0)),
            pl.BlockSpec((None, d, ff), lambda e, bi, i: (e, 0, 0)),
            pl.BlockSpec((None, ff, d), lambda e, bi, i: (e, 0, 0)),
        ],
        out_specs=pl.BlockSpec((None, None, tc, d), lambda e, bi, i: (bi, e, i, 0)),
        out_shape=jax.ShapeDtypeStruct((b, n_e, cap, d), F32),
        compiler_params=_cparams("parallel", "parallel", "parallel"),
        name="expert_ffn",
    )(xin, gate, w_gate, w_up, w_down)


def _combine_kernel(x_ref, m_ref, g_ref, ng_ref, o_ref, *, final):
    x = x_ref[...] + g_ref[...] * m_ref[...]
    if final:
        x = x * lax.rsqrt(jnp.mean(x * x, axis=-1, keepdims=True) + NORM_EPS) * ng_ref[...]
    o_ref[...] = x


def combine(x, moe, g2, final_g, *, tm=512):
    b, n, d = x.shape
    tm = min(tm, n)
    final = final_g is not None
    ng = final_g if final else jnp.ones((1, d), F32)
    tile = pl.BlockSpec((None, tm, d), lambda bi, i: (bi, i, 0))
    return pl.pallas_call(
        functools.partial(_combine_kernel, final=final),
        grid=(b, n // tm),
        in_specs=[tile, tile, pl.BlockSpec((None, 1, d), lambda bi, i: (bi, 0, 0)),
                  pl.BlockSpec((1, d), lambda bi, i: (0, 0))],
        out_specs=tile,
        out_shape=jax.ShapeDtypeStruct((b, n, d), F32),
        compiler_params=_cparams("parallel", "parallel"),
        name="combine",
    )(x, moe, g2, ng)


def expert_choice_ffn(h2, aff_t, w_gate, w_up, w_down):
    b, t, d = h2.shape
    cap = EC_CAPACITY * t // N_EXPERTS
    gate, idx = lax.top_k(aff_t, cap)
    xin = jax.vmap(lambda hb, ib: hb[ib])(h2, idx)
    y = expert_ffn(xin, gate[..., None], w_gate, w_up, w_down)
    return jax.vmap(lambda yb, ib: jnp.zeros((t, d), yb.dtype).at[ib.reshape(-1)].add(yb.reshape(-1, d)))(y, idx)


def _heads(t, n_heads):
    return t.reshape(t.shape[0], t.shape[1], n_heads, t.shape[-1] // n_heads)


def _context_attention(q, k, v, sink=None):
    b, l, hq, d = q.shape
    hkv = k.shape[2]
    grp = hq // hkv
    qg = q.reshape(b, l, hkv, grp, d)
    s = jnp.einsum('bqkgd,bskd->bkgqs', qg, k).astype(F32)
    if sink is not None:
        s_sink = jnp.broadcast_to(sink.astype(F32).reshape(1, hkv, grp, 1, 1), s.shape[:-1] + (1,))
        p = jax.nn.softmax(jnp.concatenate([s, s_sink], axis=-1), axis=-1)[..., :-1]
    else:
        p = jax.nn.softmax(s, axis=-1)
    o = jnp.einsum('bkgqs,bskd->bqkgd', p.astype(v.dtype), v)
    return o.reshape(b, l, hq * d)


def _context_diff(q, k, v, lam, norm_g, lam_init):
    s = jnp.einsum('bqhmd,bshmd->bhmqs', q, k).astype(F32)
    p = jax.nn.softmax(s, axis=-1)
    w = p[:, :, 0] - lam * p[:, :, 1]
    o = jnp.einsum('bhqs,bshe->bqhe', w.astype(v.dtype), v)
    y = o * lax.rsqrt(jnp.mean(o * o, axis=-1, keepdims=True) + NORM_EPS) * norm_g.astype(F32)
    b, t, h, e = o.shape
    return (y * (1.0 - lam_init)).reshape(b, t, h * e)


def _context_mixers(pc, oc_ret, lam, lam_init, diff_norm_g, swa_sink):
    b, lc, _ = pc.shape
    f = pc.astype(F32)
    blk = lambda c0, nb: f[:, :, c0 * HEAD_DIM:(c0 + nb) * HEAD_DIM]
    h = GROUP_BLOCKS
    o_a = _context_attention(_heads(blk(COL_A_Q, h), h), _heads(blk(COL_A_K, h), h), _heads(blk(COL_A_V, h), h))
    o_b = _context_diff(blk(COL_B_K - h, h).reshape(b, lc, h, 2, DIFF_DIM),
                        blk(COL_B_K, h).reshape(b, lc, h, 2, DIFF_DIM),
                        _heads(blk(COL_B_K + h, h), h), lam, diff_norm_g, lam_init)
    o_d = _context_attention(_heads(blk(COL_D_Q, h), h), _heads(blk(COL_D_K, h // 2), h // 2),
                             _heads(blk(COL_D_V, h // 2), h // 2), swa_sink)
    return [o_a.astype(BF16), o_b.astype(BF16), oc_ret, o_d.astype(BF16)]


def _layer(x, xc, mod, layer_idx, with_ctx, final_g, rope_h, rope_d, rope_id,
           norm1_g, w_in, na_rpb, diff_lambda, diff_norm_g, ret_decay_fwd, ret_decay_bwd, ret_gn_g, ret_gn_b,
           swa_sink, w_out, norm2_g, w_router, w_gate, w_up, w_down):
    b, n, d = x.shape
    row = lambda v: v.reshape(1, -1).astype(F32)
    part = lambda r0, r1, k: mod[r0:r1, None, k * d:(k + 1) * d]
    sh1, sc1, g1, sh2, sc2, g2 = [part(0, b, k) for k in range(6)]
    ctx_rows = lambda k: jnp.broadcast_to(part(b, b + 1, k), (b, 1, d))
    sh1c, sc1c, g1c, sh2c, sc2c, g2c = [ctx_rows(k) for k in range(6)]

    w_in_b = w_in.astype(BF16)
    w_out_b = w_out.astype(BF16)
    w_router_t = w_router.T.astype(BF16)
    wg_b, wu_b, wd_b = w_gate.astype(BF16), w_up.astype(BF16), w_down.astype(BF16)

    p, t = norm_project(x, row(norm1_g), sc1, sh1, w_in_b, rope_h, rope_d)
    pc, tc = norm_project(xc, row(norm1_g), sc1c, sh1c, w_in_b, rope_id, rope_id)

    lam_init = 0.8 - 0.6 * math.exp(-0.3 * layer_idx)
    lq1, lk1, lq2, lk2 = [diff_lambda[k].astype(F32) for k in range(4)]
    lam = jnp.exp(jnp.sum(lq1 * lk1)) - jnp.exp(jnp.sum(lq2 * lk2)) + lam_init

    o_a = neighbourhood_attention(p, pc, na_rpb)
    o_b = diff_attention(p, t, pc, tc, lam.reshape(1), row(diff_norm_g), 1.0 - lam_init)
    o_c, oc_ret = bidirectional_retention(p, pc, ret_decay_fwd, ret_decay_bwd, ret_gn_g, ret_gn_b, with_ctx)
    o_d = window_attention(p, pc, swa_sink.astype(F32))

    x, h2, aff_t = out_project([o_a, o_b, o_c, o_d], w_out_b, x, g1, row(norm2_g), sc2, sh2, w_router_t)
    x = combine(x, expert_choice_ffn(h2, aff_t, wg_b, wu_b, wd_b), g2, final_g)

    if with_ctx:
        oc = _context_mixers(pc, oc_ret, lam, lam_init, diff_norm_g, swa_sink)
        xc, h2c, aff_c = out_project(oc, w_out_b, xc, g1c, row(norm2_g), sc2c, sh2c, w_router_t)
        xc = combine(xc, expert_choice_ffn(h2c, aff_c, wg_b, wu_b, wd_b), g2c, None)
    return x, xc


def kernel(x, c, ctx, c_ctx, w_mod, b_mod, norm1_g, w_in, na_rpb, diff_lambda, diff_norm_g,
           ret_decay_fwd, ret_decay_bwd, ret_gn_g, ret_gn_b, swa_sink, w_out, norm2_g,
           w_router, w_gate, w_up, w_down, final_norm_g):
    b, n, d = x.shape
    depth = w_in.shape[0]
    lc = ctx.shape[1]
    cvec = jnp.zeros((8, d), F32).at[:b].set(c).at[b].set(c_ctx)
    mod = modulation(cvec, w_mod, b_mod.reshape(depth, 1, -1))
    rope_h = _rope_tables(n, HEAD_DIM)
    rope_d = _rope_tables(n, DIFF_DIM)
    rope_id = _identity_rope_tables(lc)
    xc = ctx
    for li in range(depth):
        last = li == depth - 1
        x, xc = _layer(x, xc, mod[li], li, not last, final_norm_g.reshape(1, -1) if last else None,
                       rope_h, rope_d, rope_id,
                       norm1_g[li], w_in[li], na_rpb[li], diff_lambda[li], diff_norm_g[li],
                       ret_decay_fwd[li], ret_decay_bwd[li], ret_gn_g[li], ret_gn_b[li], swa_sink[li],
                       w_out[li], norm2_g[li], w_router[li], w_gate[li], w_up[li], w_down[li])
    return x
```

```python
import functools
import math

import jax
import jax.numpy as jnp
from jax import lax
from jax.experimental import pallas as pl
from jax.experimental.pallas import tpu as pltpu

GRID_W = 64
HEAD_DIM = 128
DIFF_DIM = HEAD_DIM // 2
NA_WIN_R = 8
NA_WIN_C = 16
RET_CHUNK = 128
SWA_WINDOW = 128
N_EXPERTS = 16
EC_CAPACITY = 2
ROPE_BASE = 10000.0
NORM_EPS = 1e-6
GN_EPS = 1e-5
NEG_INF = -1e30
LOG2E = math.log2(math.e)

BF16 = jnp.bfloat16
F32 = jnp.float32

VMEM_LIMIT_BYTES = 48 * 1024 * 1024

GROUP_BLOCKS = 4
COL_A_Q, COL_A_K, COL_A_V = 0, 4, 8
COL_B_K = 16
COL_C_Q, COL_C_K, COL_C_V, COL_C_G = 24, 28, 32, 36
COL_D_Q, COL_D_K, COL_D_V = 40, 44, 46
PROJ_TN = GROUP_BLOCKS * HEAD_DIM


def _cparams(*sem):
    return pltpu.CompilerParams(dimension_semantics=sem, vmem_limit_bytes=VMEM_LIMIT_BYTES)


def _mod_kernel(c_ref, w_ref, b_ref, o_ref):
    c = c_ref[...]
    s = (c * (1.0 / (1.0 + jnp.exp(-c)))).astype(BF16)
    o_ref[...] = jnp.dot(s, w_ref[...].astype(BF16), preferred_element_type=F32) + b_ref[...]


def modulation(cvec, w_mod, b_mod, *, tn=1024):
    depth, d, n6 = w_mod.shape
    return pl.pallas_call(
        _mod_kernel,
        grid=(depth, n6 // tn),
        in_specs=[
            pl.BlockSpec((8, d), lambda l, j: (0, 0)),
            pl.BlockSpec((None, d, tn), lambda l, j: (l, 0, j)),
            pl.BlockSpec((None, 1, tn), lambda l, j: (l, 0, j)),
        ],
        out_specs=pl.BlockSpec((None, 8, tn), lambda l, j: (l, 0, j)),
        out_shape=jax.ShapeDtypeStruct((depth, 8, n6), F32),
        compiler_params=_cparams("parallel", "parallel"),
        name="modulation",
    )(cvec, w_mod, b_mod)


def _rope_tables(n_tok, dim):
    t = jnp.arange(n_tok)
    row = (t // GRID_W).astype(F32)[:, None]
    col = (t % GRID_W).astype(F32)[:, None]
    nf = dim // 4
    lane = jnp.arange(HEAD_DIM)
    quarter = (lane % dim) // nf
    inv = ROPE_BASE ** (-jnp.arange(nf, dtype=F32) / nf)
    ang = jnp.where(quarter[None, :] < 2, row, col) * inv[lane % nf][None, :]
    cos, sin = jnp.cos(ang), jnp.sin(ang)
    even = (quarter % 2 == 0)[None, :]
    return cos, jnp.where(even, -sin, 0.0), jnp.where(even, 0.0, sin)


def _identity_rope_tables(n_tok):
    z = jnp.zeros((n_tok, HEAD_DIM), F32)
    return jnp.ones((n_tok, HEAD_DIM), F32), z, z


def _proj_modes():
    att = HEAD_DIM ** -0.5
    plain = [(None, 1.0)] * GROUP_BLOCKS
    return [
        ([(None, att)] * 4, None), (plain, None), (plain, None),
        ([("d", LOG2E * DIFF_DIM ** -0.5)] * 4, 0), ([("d", 1.0)] * 4, None), (plain, 1),
        ([("h", 1.0)] * 4, None), ([("h", att)] * 4, None), (plain, None), (plain, None),
        ([("h", att)] * 4, None), ([("h", 1.0)] * 2 + [(None, 1.0)] * 2, None),
    ]


def _proj_kernel(x_ref, g_ref, sc_ref, sh_ref, w_ref, ch_ref, sah_ref, sbh_ref, cd_ref, sad_ref, sbd_ref,
                 o_ref, t_ref, h_ref):
    j = pl.program_id(2)

    @pl.when(j == 0)
    def _():
        x = x_ref[...]
        y = x * lax.rsqrt(jnp.mean(x * x, axis=-1, keepdims=True) + NORM_EPS) * g_ref[...]
        h_ref[...] = (y * (1.0 + sc_ref[...]) + sh_ref[...]).astype(BF16)

    acc = jnp.dot(h_ref[...], w_ref[...], preferred_element_type=F32)

    def rope(a, kind):
        if kind is None:
            return a
        c, sa, sb, sh = ((ch_ref, sah_ref, sbh_ref, HEAD_DIM // 4) if kind == "h"
                         else (cd_ref, sad_ref, sbd_ref, DIFF_DIM // 4))
        return (a * c[...] + pltpu.roll(a, HEAD_DIM - sh, 1) * sa[...] + pltpu.roll(a, sh, 1) * sb[...])

    for jj, (blocks, t_slot) in enumerate(_proj_modes()):
        @pl.when(j == jj)
        def _(blocks=blocks, t_slot=t_slot):
            outs = []
            for hb, (kind, scale) in enumerate(blocks):
                a = rope(acc[:, hb * HEAD_DIM:(hb + 1) * HEAD_DIM], kind)
                outs.append(a if scale == 1.0 else a * scale)
            full = jnp.concatenate(outs, axis=1)
            o_ref[...] = full.astype(o_ref.dtype)
            if t_slot is not None:
                t_ref[t_slot] = full.T.astype(t_ref.dtype)


def norm_project(x, norm_g, scale, shift, w_in, rope_h, rope_d, *, tm=1024):
    b, n, d = x.shape
    width = w_in.shape[1]
    tm = min(tm, n)
    n_j = width // PROJ_TN
    tok = lambda bi, i, j: (i, 0)
    return pl.pallas_call(
        _proj_kernel,
        grid=(b, n // tm, n_j),
        in_specs=[
            pl.BlockSpec((None, tm, d), lambda bi, i, j: (bi, i, 0)),
            pl.BlockSpec((1, d), lambda bi, i, j: (0, 0)),
            pl.BlockSpec((None, 1, d), lambda bi, i, j: (bi, 0, 0)),
            pl.BlockSpec((None, 1, d), lambda bi, i, j: (bi, 0, 0)),
            pl.BlockSpec((d, PROJ_TN), lambda bi, i, j: (0, j)),
        ] + [pl.BlockSpec((tm, HEAD_DIM), tok)] * 6,
        out_specs=[
            pl.BlockSpec((None, tm, PROJ_TN), lambda bi, i, j: (bi, i, j)),
            pl.BlockSpec((None, 2, PROJ_TN, tm), lambda bi, i, j: (bi, 0, 0, i)),
        ],
        out_shape=[jax.ShapeDtypeStruct((b, n, width), BF16),
                   jax.ShapeDtypeStruct((b, 2, PROJ_TN, n), BF16)],
        scratch_shapes=[pltpu.VMEM((tm, d), BF16)],
        compiler_params=_cparams("parallel", "parallel", "arbitrary"),
        name="norm_project",
    )(x, norm_g, scale, shift, w_in, *rope_h, *rope_d)


def _softmax_pv(s_loc, s_ctx, vw, vc, sink):
    m = jnp.maximum(jnp.max(s_loc, axis=-1, keepdims=True), jnp.max(s_ctx, axis=-1, keepdims=True))
    if sink is not None:
        m = jnp.maximum(m, sink)
    p_loc = jnp.exp(s_loc - m)
    p_ctx = jnp.exp(s_ctx - m)
    l = jnp.sum(p_loc, axis=-1, keepdims=True) + jnp.sum(p_ctx, axis=-1, keepdims=True)
    if sink is not None:
        l = l + jnp.exp(sink - m)
    o = (jnp.dot(p_loc.astype(BF16), vw, preferred_element_type=F32)
         + jnp.dot(p_ctx.astype(BF16), vc, preferred_element_type=F32))
    return o * (1.0 / l)


def _nt_dot(a, b):
    return lax.dot_general(a, b, (((1,), (1,)), ((), ())), preferred_element_type=F32)


def _na_kernel(q_ref, k_ref, v_ref, kc_ref, vc_ref, bias_ref, o_ref, *, tq, wk):
    i = pl.program_id(2)
    n = k_ref.shape[0]
    rows_q = tq // GRID_W
    kstart = jnp.clip(i * rows_q - NA_WIN_R // 2, 0, (n - wk) // GRID_W) * GRID_W
    kstart = pl.multiple_of(kstart, GRID_W)
    q = q_ref[...]
    kw = k_ref[pl.ds(kstart, wk), :]
    vw = v_ref[pl.ds(kstart, wk), :]
    s_loc = _nt_dot(q, kw) + bias_ref[...]
    s_ctx = _nt_dot(q, kc_ref[...])
    o_ref[...] = _softmax_pv(s_loc, s_ctx, vw, vc_ref[...], None).astype(o_ref.dtype)


def _na_bias(rpb, rows, rows_q, rows_k):
    n_blk = rows // rows_q
    kr = NA_WIN_R
    cols = jnp.arange(GRID_W)
    col_start = jnp.clip(cols - NA_WIN_C // 2, 0, GRID_W - NA_WIN_C)
    col_ok = (cols[None, :] >= col_start[:, None]) & (cols[None, :] < col_start[:, None] + NA_WIN_C)
    col_off = jnp.clip(cols[None, :] - cols[:, None] + NA_WIN_C - 1, 0, 2 * NA_WIN_C - 2)
    out = []
    for blk in (0, 1, n_blk - 1):
        r = blk * rows_q + jnp.arange(rows_q)
        kstart = min(max(blk * rows_q - kr // 2, 0), rows - rows_k)
        krow = kstart + jnp.arange(rows_k)
        start = jnp.clip(r - kr // 2, 0, rows - kr)
        row_ok = (krow[None, :] >= start[:, None]) & (krow[None, :] < start[:, None] + kr)
        row_off = jnp.clip(krow[None, :] - r[:, None] + NA_WIN_R - 1, 0, 2 * NA_WIN_R - 2)
        bias = rpb.astype(F32)[:, row_off][:, :, :, col_off]
        ok = row_ok[:, :, None, None] & col_ok[None, None]
        bias = jnp.where(ok[None], bias, NEG_INF).transpose(0, 1, 3, 2, 4)
        out.append(bias.reshape(rpb.shape[0], rows_q * GRID_W, rows_k * GRID_W))
    return jnp.stack(out, axis=1)


def neighbourhood_attention(p, pc, rpb, *, rows_q=4):
    b, n, _ = p.shape
    lc = pc.shape[1]
    h = rpb.shape[0]
    rows = n // GRID_W
    rows_k = rows_q + NA_WIN_R - 1
    assert rows % rows_q == 0 and rows >= rows_k and rows_q >= NA_WIN_R // 2
    tq, wk = rows_q * GRID_W, rows_k * GRID_W
    n_blk = rows // rows_q
    bias = _na_bias(rpb, rows, rows_q, rows_k)
    kern = functools.partial(_na_kernel, tq=tq, wk=wk)
    cls = lambda i: jnp.where(i == 0, 0, jnp.where(i == n_blk - 1, 2, 1))
    return pl.pallas_call(
        kern,
        grid=(b, h, n_blk),
        in_specs=[
            pl.BlockSpec((None, tq, HEAD_DIM), lambda bi, hi, i: (bi, i, COL_A_Q + hi)),
            pl.BlockSpec((None, n, HEAD_DIM), lambda bi, hi, i: (bi, 0, COL_A_K + hi)),
            pl.BlockSpec((None, n, HEAD_DIM), lambda bi, hi, i: (bi, 0, COL_A_V + hi)),
            pl.BlockSpec((None, lc, HEAD_DIM), lambda bi, hi, i: (bi, 0, COL_A_K + hi)),
            pl.BlockSpec((None, lc, HEAD_DIM), lambda bi, hi, i: (bi, 0, COL_A_V + hi)),
            pl.BlockSpec((None, None, tq, wk), lambda bi, hi, i: (hi, cls(i), 0, 0)),
        ],
        out_specs=pl.BlockSpec((None, tq, HEAD_DIM), lambda bi, hi, i: (bi, i, hi)),
        out_shape=jax.ShapeDtypeStruct((b, n, h * HEAD_DIM), BF16),
        compiler_params=_cparams("parallel", "parallel", "arbitrary"),
        name="neighbourhood_attention",
    )(p, p, p, pc, pc, bias)


def _swa_kernel(sink_ref, q_ref, k_ref, v_ref, kc_ref, vc_ref, o_ref, *, tq, wk):
    hi = pl.program_id(1)
    i = pl.program_id(2)
    n = k_ref.shape[0]
    q0 = i * tq
    kstart = pl.multiple_of(jnp.clip(q0 - SWA_WINDOW, 0, n - wk), SWA_WINDOW)
    q = q_ref[...]
    kw = k_ref[pl.ds(kstart, wk), :]
    vw = v_ref[pl.ds(kstart, wk), :]
    qpos = q0 + lax.broadcasted_iota(jnp.int32, (tq, wk), 0)
    kpos = kstart + lax.broadcasted_iota(jnp.int32, (tq, wk), 1)
    s_loc = jnp.where(jnp.abs(kpos - qpos) <= SWA_WINDOW, _nt_dot(q, kw), NEG_INF)
    s_ctx = _nt_dot(q, kc_ref[...])
    o_ref[...] = _softmax_pv(s_loc, s_ctx, vw, vc_ref[...], sink_ref[hi]).astype(o_ref.dtype)


def window_attention(p, pc, sink, *, tq=256):
    b, n, _ = p.shape
    lc = pc.shape[1]
    hq = sink.shape[0]
    grp = 2
    tq = min(tq, n)
    wk = min(tq + 2 * SWA_WINDOW, n)
    kern = functools.partial(_swa_kernel, tq=tq, wk=wk)
    return pl.pallas_call(
        kern,
        grid=(b, hq, n // tq),
        in_specs=[
            pl.BlockSpec(memory_space=pltpu.SMEM),
            pl.BlockSpec((None, tq, HEAD_DIM), lambda bi, hi, i: (bi, i, COL_D_Q + hi)),
            pl.BlockSpec((None, n, HEAD_DIM), lambda bi, hi, i: (bi, 0, COL_D_K + hi // grp)),
            pl.BlockSpec((None, n, HEAD_DIM), lambda bi, hi, i: (bi, 0, COL_D_V + hi // grp)),
            pl.BlockSpec((None, lc, HEAD_DIM), lambda bi, hi, i: (bi, 0, COL_D_K + hi // grp)),
            pl.BlockSpec((None, lc, HEAD_DIM), lambda bi, hi, i: (bi, 0, COL_D_V + hi // grp)),
        ],
        out_specs=pl.BlockSpec((None, tq, HEAD_DIM), lambda bi, hi, i: (bi, i, hi)),
        out_shape=jax.ShapeDtypeStruct((b, n, hq * HEAD_DIM), BF16),
        compiler_params=_cparams("parallel", "parallel", "arbitrary"),
        name="window_attention",
    )(sink, p, p, p, pc, pc)


DIFF_SLOTS = 3
DIFF_AUG_ROWS = 16


def _diff_attn_kernel(lam_ref, qt_ref, k_ref, vt_ref, kc_ref, vct_ref, g_ref, o_ref,
                      qbd_ref, m_ref, acc_ref, s_scr, p_scr, a_scr, x_scr, *, tq, tk, post_scale):
    qt = qt_ref[...].astype(F32)
    row = lax.broadcasted_iota(jnp.int32, qt.shape, 0)
    zero = jnp.zeros_like(qt)
    qbd_ref[:, :tq] = jnp.where(row < DIFF_DIM, qt, zero).astype(BF16)
    qbd_ref[:, tq:] = jnp.where(row >= DIFF_DIM, qt, zero).astype(BF16)

    def aug(vt_tile):
        r = lax.broadcasted_iota(jnp.int32, (DIFF_AUG_ROWS, vt_tile.shape[1]), 0)
        return jnp.concatenate([vt_tile, jnp.where(r == 0, 1.0, 0.0).astype(BF16)], axis=0)

    s = jnp.dot(kc_ref[...], qbd_ref[...], preferred_element_type=F32)
    m0 = jnp.max(s, axis=0, keepdims=True)
    m_ref[...] = m0
    acc_ref[...] = jnp.dot(aug(vct_ref[...]), jnp.exp2(s - m0).astype(BF16), preferred_element_type=F32)

    def scores(t, slot):
        off = pl.multiple_of(t * tk, tk)
        s = jnp.dot(k_ref[pl.ds(off, tk), :], qbd_ref[...], preferred_element_type=F32)
        s_scr[slot] = s
        x_scr[slot] = jnp.max(s, axis=0, keepdims=True)

    def exponentials(slot):
        m_old = m_ref[...]
        m_new = jnp.maximum(m_old, x_scr[slot])
        a_scr[slot] = jnp.exp2(m_old - m_new)
        p_scr[slot] = jnp.exp2(s_scr[slot] - m_new).astype(BF16)
        m_ref[...] = m_new

    def accumulate(t, slot):
        off = pl.multiple_of(t * tk, tk)
        pv = jnp.dot(aug(vt_ref[:, pl.ds(off, tk)]), p_scr[slot], preferred_element_type=F32)
        acc_ref[...] = a_scr[slot] * acc_ref[...] + pv

    def stage(t, t_mod, do_scores=True, do_acc=True):
        if do_scores:
            scores(t + 1, (t_mod + 1) % DIFF_SLOTS)
        exponentials(t_mod % DIFF_SLOTS)
        if do_acc:
            accumulate(t - 1, (t_mod - 1) % DIFF_SLOTS)

    n_kt = k_ref.shape[0] // tk
    unroll = 2 * DIFF_SLOTS
    trips = (n_kt - 2) // unroll
    scores(0, 0)
    stage(0, 0, do_acc=False)

    def body(i, carry):
        for u in range(unroll):
            stage(1 + unroll * i + u, 1 + u)
        return carry

    lax.fori_loop(0, trips, body, 0)
    for t in range(1 + unroll * trips, n_kt - 1):
        stage(t, t)
    stage(n_kt - 1, n_kt - 1, do_scores=False)
    accumulate(n_kt - 1, (n_kt - 1) % DIFF_SLOTS)

    lam = lam_ref[0]
    inv = 1.0 / acc_ref[HEAD_DIM:HEAD_DIM + 1, :]
    acc = acc_ref[:HEAD_DIM, :]
    o_t = acc[:, :tq] * inv[:, :tq] - lam * (acc[:, tq:] * inv[:, tq:])
    o = o_t.T
    ms = jnp.mean(o * o, axis=-1, keepdims=True)
    y = o * lax.rsqrt(ms + NORM_EPS) * g_ref[...]
    o_ref[...] = (y * post_scale).astype(o_ref.dtype)


def diff_attention(p, t, pc, tc, lam, norm_g, post_scale, *, tq=256, tk=512):
    b, n, _ = p.shape
    lc = pc.shape[1]
    h = GROUP_BLOCKS
    tq = min(tq, n)
    tk = min(tk, n)
    assert n % tq == 0 and n % tk == 0 and n // tk >= 2
    kern = functools.partial(_diff_attn_kernel, tq=tq, tk=tk, post_scale=post_scale)
    return pl.pallas_call(
        kern,
        grid=(b, h, n // tq),
        in_specs=[
            pl.BlockSpec(memory_space=pltpu.SMEM),
            pl.BlockSpec((None, None, HEAD_DIM, tq), lambda bi, hi, qi: (bi, 0, hi, qi)),
            pl.BlockSpec((None, n, HEAD_DIM), lambda bi, hi, qi: (bi, 0, COL_B_K + hi)),
            pl.BlockSpec((None, None, HEAD_DIM, n), lambda bi, hi, qi: (bi, 1, hi, 0)),
            pl.BlockSpec((None, lc, HEAD_DIM), lambda bi, hi, qi: (bi, 0, COL_B_K + hi)),
            pl.BlockSpec((None, None, HEAD_DIM, lc), lambda bi, hi, qi: (bi, 1, hi, 0)),
            pl.BlockSpec((1, HEAD_DIM), lambda bi, hi, qi: (0, 0)),
        ],
        out_specs=pl.BlockSpec((None, tq, HEAD_DIM), lambda bi, hi, qi: (bi, qi, hi)),
        out_shape=jax.ShapeDtypeStruct((b, n, h * HEAD_DIM), BF16),
        scratch_shapes=[
            pltpu.VMEM((HEAD_DIM, 2 * tq), BF16),
            pltpu.VMEM((1, 2 * tq), F32),
            pltpu.VMEM((HEAD_DIM + DIFF_AUG_ROWS, 2 * tq), F32),
            pltpu.VMEM((DIFF_SLOTS, tk, 2 * tq), F32),
            pltpu.VMEM((DIFF_SLOTS, tk, 2 * tq), BF16),
            pltpu.VMEM((DIFF_SLOTS, 1, 2 * tq), F32),
            pltpu.VMEM((DIFF_SLOTS, 1, 2 * tq), F32),
        ],
        compiler_params=_cparams("parallel", "parallel", "arbitrary"),
        name="diff_attention",
    )(lam, t, p, t, pc, tc, norm_g)


def _retention_kernel(cdec_ref, q_ref, k_ref, v_ref, intra_ref, qdec_ref, kdec_ref, s0_ref, *rest,
                      reverse, final):
    if final:
        of_ref, gate_ref, gng_ref, gnb_ref, o_ref, sfin_ref, s_ref = rest
    else:
        o_ref, sfin_ref, s_ref = rest
    i = pl.program_id(1)
    n_heads = s_ref.shape[0]
    n_chunks = q_ref.shape[0] // RET_CHUNK

    @pl.when(i == 0)
    def _():
        s_ref[...] = s0_ref[...]

    order = range(n_chunks - 1, -1, -1) if reverse else range(n_chunks)
    for c in order:
        rs = slice(c * RET_CHUNK, (c + 1) * RET_CHUNK)
        for h in range(n_heads):
            cs = slice(h * HEAD_DIM, (h + 1) * HEAD_DIM)
            qh, kh, vh = q_ref[rs, cs], k_ref[rs, cs], v_ref[rs, cs]
            qd = (qh.astype(F32) * qdec_ref[h]).astype(BF16)
            kd_t = (kh.astype(F32) * kdec_ref[h]).T.astype(BF16)
            a = (_nt_dot(qh, kh) * intra_ref[h]).astype(BF16)
            s = s_ref[h]
            o = (jnp.dot(a, vh, preferred_element_type=F32)
                 + jnp.dot(qd, s.astype(BF16), preferred_element_type=F32))
            s_ref[h] = cdec_ref[h] * s + jnp.dot(kd_t, vh, preferred_element_type=F32)
            if final:
                o = o + of_ref[rs, cs]
                mu = jnp.mean(o, axis=-1, keepdims=True)
                var = jnp.mean(jnp.square(o - mu), axis=-1, keepdims=True)
                y = (o - mu) * lax.rsqrt(var + GN_EPS) * gng_ref[:, cs] + gnb_ref[:, cs]
                g = gate_ref[rs, cs].astype(F32)
                o = g * (1.0 / (1.0 + jnp.exp(-g))) * y
            o_ref[rs, cs] = o.astype(o_ref.dtype)

    @pl.when(i == pl.num_programs(1) - 1)
    def _():
        sfin_ref[...] = s_ref[...]


def _retention_pass(p, tables, s0, final_inputs, *, reverse, blk_chunks=4):
    b, n, _ = p.shape
    cdec, intra, qdec, kdec = tables
    h = intra.shape[0]
    width = h * HEAD_DIM
    n_chunks = n // RET_CHUNK
    blk_chunks = min(blk_chunks, n_chunks)
    assert n_chunks % blk_chunks == 0
    tb = blk_chunks * RET_CHUNK
    n_blk = n // tb
    pos = (lambda i: n_blk - 1 - i) if reverse else (lambda i: i)
    colblk = lambda c: (lambda bi, i: (bi, pos(i), c // GROUP_BLOCKS))
    tab = pl.BlockSpec((h, RET_CHUNK, HEAD_DIM), lambda bi, i: (0, 0, 0))
    state = pl.BlockSpec((None, h, HEAD_DIM, HEAD_DIM), lambda bi, i: (bi, 0, 0, 0))
    in_specs = [pl.BlockSpec(memory_space=pltpu.SMEM),
                pl.BlockSpec((None, tb, width), colblk(COL_C_Q)),
                pl.BlockSpec((None, tb, width), colblk(COL_C_K)),
                pl.BlockSpec((None, tb, width), colblk(COL_C_V)),
                tab, tab, tab, state]
    args = [cdec, p, p, p, intra, qdec, kdec, s0]
    final = final_inputs is not None
    if final:
        o_fwd, gn_g, gn_b = final_inputs
        in_specs += [pl.BlockSpec((None, tb, width), lambda bi, i: (bi, pos(i), 0)),
                     pl.BlockSpec((None, tb, width), colblk(COL_C_G)),
                     pl.BlockSpec((1, width), lambda bi, i: (0, 0)),
                     pl.BlockSpec((1, width), lambda bi, i: (0, 0))]
        args += [o_fwd, p, gn_g, gn_b]
    kern = functools.partial(_retention_kernel, reverse=reverse, final=final)
    return pl.pallas_call(
        kern,
        grid=(b, n_blk),
        in_specs=in_specs,
        out_specs=[pl.BlockSpec((None, tb, width), lambda bi, i: (bi, pos(i), 0)), state],
        out_shape=[jax.ShapeDtypeStruct((b, n, width), BF16 if final else F32),
                   jax.ShapeDtypeStruct((b, h, HEAD_DIM, HEAD_DIM), F32)],
        scratch_shapes=[pltpu.VMEM((h, HEAD_DIM, HEAD_DIM), F32)],
        compiler_params=_cparams("parallel", "arbitrary"),
        name="retention_bwd" if reverse else "retention_fwd",
    )(*args)


def _retention_tables(decay, reverse):
    lg = jax.nn.log_sigmoid(decay.astype(F32))[:, None, None]
    pos = jnp.arange(RET_CHUNK, dtype=F32)
    rel = pos[:, None] - pos[None, :]
    if reverse:
        rel = -rel
        q_pow, k_pow = RET_CHUNK - pos, pos
    else:
        q_pow, k_pow = pos + 1.0, RET_CHUNK - 1.0 - pos
    intra = jnp.where(rel >= 0, jnp.exp(jnp.maximum(rel, 0.0) * lg), 0.0)
    bc = lambda e: jnp.broadcast_to(jnp.exp(e[None, :, None] * lg), intra.shape)
    cdec = jnp.exp(RET_CHUNK * lg[:, 0, 0])
    return cdec, intra, bc(q_pow), bc(k_pow)


def bidirectional_retention(p, pc, decay_f, decay_b, gn_g, gn_b, with_ctx):
    b = p.shape[0]
    h = decay_f.shape[0]
    tf = _retention_tables(decay_f, False)
    tb = _retention_tables(decay_b, True)
    s0 = jnp.zeros((b, h, HEAD_DIM, HEAD_DIM), F32)
    gn = (gn_g.reshape(1, -1).astype(F32), gn_b.reshape(1, -1).astype(F32))
    oc_f, s_f = _retention_pass(pc, tf, s0, None, reverse=False)
    oc, s_b = _retention_pass(pc, tb, s0, (oc_f,) + gn, reverse=True)
    o_f, _ = _retention_pass(p, tf, s_f, None, reverse=False)
    o, _ = _retention_pass(p, tb, s_b, (o_f,) + gn, reverse=True)
    return o, (oc if with_ctx else None)


def _out_kernel(oa_ref, ob_ref, oc_ref, od_ref, w_ref, x_ref, g1_ref, ng_ref, sc_ref, sh_ref, wr_ref,
                xo_ref, h_ref, aff_ref):
    gw = oa_ref.shape[1]
    acc = jnp.dot(oa_ref[...], w_ref[0 * gw:1 * gw, :], preferred_element_type=F32)
    acc += jnp.dot(ob_ref[...], w_ref[1 * gw:2 * gw, :], preferred_element_type=F32)
    acc += jnp.dot(oc_ref[...], w_ref[2 * gw:3 * gw, :], preferred_element_type=F32)
    acc += jnp.dot(od_ref[...], w_ref[3 * gw:4 * gw, :], preferred_element_type=F32)
    x = x_ref[...] + g1_ref[...] * acc
    xo_ref[...] = x
    y = x * lax.rsqrt(jnp.mean(x * x, axis=-1, keepdims=True) + NORM_EPS) * ng_ref[...]
    h2 = (y * (1.0 + sc_ref[...]) + sh_ref[...]).astype(BF16)
    h_ref[...] = h2
    logits = _nt_dot(wr_ref[...], h2)
    e = jnp.exp(logits - jnp.max(logits, axis=0, keepdims=True))
    aff_ref[...] = e * (1.0 / jnp.sum(e, axis=0, keepdims=True))


def out_project(o_groups, w_out, x, g1, norm_g, scale, shift, w_router_t, *, tm=256):
    b, n, d = x.shape
    n_e = w_router_t.shape[0]
    gw = o_groups[0].shape[-1]
    tm = min(tm, n)
    tile = lambda w: pl.BlockSpec((None, tm, w), lambda bi, i: (bi, i, 0))
    vec = pl.BlockSpec((None, 1, d), lambda bi, i: (bi, 0, 0))
    return pl.pallas_call(
        _out_kernel,
        grid=(b, n // tm),
        in_specs=[tile(gw)] * 4 + [
            pl.BlockSpec((d, d), lambda bi, i: (0, 0)),
            tile(d), vec,
            pl.BlockSpec((1, d), lambda bi, i: (0, 0)),
            vec, vec,
            pl.BlockSpec((n_e, d), lambda bi, i: (0, 0)),
        ],
        out_specs=[tile(d), tile(d), pl.BlockSpec((None, n_e, tm), lambda bi, i: (bi, 0, i))],
        out_shape=[jax.ShapeDtypeStruct((b, n, d), F32),
                   jax.ShapeDtypeStruct((b, n, d), BF16),
                   jax.ShapeDtypeStruct((b, n_e, n), F32)],
        compiler_params=_cparams("parallel", "parallel"),
        name="out_project",
    )(*o_groups, w_out, x, g1, norm_g, scale, shift, w_router_t)


def _ffn_kernel(x_ref, gate_ref, wg_ref, wu_ref, wd_ref, o_ref):
    x = x_ref[...]
    a = jnp.dot(x, wg_ref[...], preferred_element_type=F32)
    u = jnp.dot(x, wu_ref[...], preferred_element_type=F32)
    hm = (a * (1.0 / (1.0 + jnp.exp(-a))) * u).astype(BF16)
    o_ref[...] = jnp.dot(hm, wd_ref[...], preferred_element_type=F32) * gate_ref[...]


def expert_ffn(xin, gate, w_gate, w_up, w_down, *, tc=256):
    b, n_e, cap, d = xin.shape
    ff = w_gate.shape[-1]
    tc = min(tc, cap)
    return pl.pallas_call(
        _ffn_kernel,
        grid=(n_e, b, cap // tc),
        in_specs=[
            pl.BlockSpec((None, None, tc, d), lambda e, bi, i: (bi, e, i, 0)),
            pl.BlockSpec((None, None, tc, 1), lambda e, bi, i: (bi, e, i, 0)),
            pl.BlockSpec((None, d, ff), lambda e, bi, i: (e, 0, 0)),
            pl.BlockSpec((None, d, ff), lambda e, bi, i: (e, 0, 0)),
            pl.BlockSpec((None, ff, d), lambda e, bi, i: (e, 0, 0)),
        ],
        out_specs=pl.BlockSpec((None, None, tc, d), lambda e, bi, i: (bi, e, i, 0)),
        out_shape=jax.ShapeDtypeStruct((b, n_e, cap, d), F32),
        compiler_params=_cparams("parallel", "parallel", "parallel"),
        name="expert_ffn",
    )(xin, gate, w_gate, w_up, w_down)


def _combine_kernel(x_ref, m_ref, g_ref, ng_ref, o_ref, *, final):
    x = x_ref[...] + g_ref[...] * m_ref[...]
    if final:
        x = x * lax.rsqrt(jnp.mean(x * x, axis=-1, keepdims=True) + NORM_EPS) * ng_ref[...]
    o_ref[...] = x


def combine(x, moe, g2, final_g, *, tm=512):
    b, n, d = x.shape
    tm = min(tm, n)
    final = final_g is not None
    ng = final_g if final else jnp.ones((1, d), F32)
    tile = pl.BlockSpec((None, tm, d), lambda bi, i: (bi, i, 0))
    return pl.pallas_call(
        functools.partial(_combine_kernel, final=final),
        grid=(b, n // tm),
        in_specs=[tile, tile, pl.BlockSpec((None, 1, d), lambda bi, i: (bi, 0, 0)),
                  pl.BlockSpec((1, d), lambda bi, i: (0, 0))],
        out_specs=tile,
        out_shape=jax.ShapeDtypeStruct((b, n, d), F32),
        compiler_params=_cparams("parallel", "parallel"),
        name="combine",
    )(x, moe, g2, ng)


def expert_choice_ffn(h2, aff_t, w_gate, w_up, w_down):
    b, t, d = h2.shape
    cap = EC_CAPACITY * t // N_EXPERTS
    gate, idx = lax.top_k(aff_t, cap)
    xin = jax.vmap(lambda hb, ib: hb[ib])(h2, idx)
    y = expert_ffn(xin, gate[..., None], w_gate, w_up, w_down)
    flat = (idx + (jnp.arange(b, dtype=idx.dtype) * t)[:, None, None]).reshape(-1)
    return jnp.zeros((b * t, d), y.dtype).at[flat].add(y.reshape(-1, d)).reshape(b, t, d)


def _heads(t, n_heads):
    return t.reshape(t.shape[0], t.shape[1], n_heads, t.shape[-1] // n_heads)


def _context_attention(q, k, v, sink=None):
    b, l, hq, d = q.shape
    hkv = k.shape[2]
    grp = hq // hkv
    qg = q.reshape(b, l, hkv, grp, d)
    s = jnp.einsum('bqkgd,bskd->bkgqs', qg, k).astype(F32)
    if sink is not None:
        s_sink = jnp.broadcast_to(sink.astype(F32).reshape(1, hkv, grp, 1, 1), s.shape[:-1] + (1,))
        p = jax.nn.softmax(jnp.concatenate([s, s_sink], axis=-1), axis=-1)[..., :-1]
    else:
        p = jax.nn.softmax(s, axis=-1)
    o = jnp.einsum('bkgqs,bskd->bqkgd', p.astype(v.dtype), v)
    return o.reshape(b, l, hq * d)


def _context_diff(q, k, v, lam, norm_g, lam_init):
    s = jnp.einsum('bqhmd,bshmd->bhmqs', q, k).astype(F32)
    p = jax.nn.softmax(s, axis=-1)
    w = p[:, :, 0] - lam * p[:, :, 1]
    o = jnp.einsum('bhqs,bshe->bqhe', w.astype(v.dtype), v)
    y = o * lax.rsqrt(jnp.mean(o * o, axis=-1, keepdims=True) + NORM_EPS) * norm_g.astype(F32)
    b, t, h, e = o.shape
    return (y * (1.0 - lam_init)).reshape(b, t, h * e)


def _context_mixers(pc, oc_ret, lam, lam_init, diff_norm_g, swa_sink):
    b, lc, _ = pc.shape
    f = pc.astype(F32)
    blk = lambda c0, nb: f[:, :, c0 * HEAD_DIM:(c0 + nb) * HEAD_DIM]
    h = GROUP_BLOCKS
    o_a = _context_attention(_heads(blk(COL_A_Q, h), h), _heads(blk(COL_A_K, h), h), _heads(blk(COL_A_V, h), h))
    o_b = _context_diff((blk(COL_B_K - h, h) * (1.0 / LOG2E)).reshape(b, lc, h, 2, DIFF_DIM),
                        blk(COL_B_K, h).reshape(b, lc, h, 2, DIFF_DIM),
                        _heads(blk(COL_B_K + h, h), h), lam, diff_norm_g, lam_init)
    o_d = _context_attention(_heads(blk(COL_D_Q, h), h), _heads(blk(COL_D_K, h // 2), h // 2),
                             _heads(blk(COL_D_V, h // 2), h // 2), swa_sink)
    return [o_a.astype(BF16), o_b.astype(BF16), oc_ret, o_d.astype(BF16)]


def _layer(x, xc, mod, layer_idx, with_ctx, final_g, rope_h, rope_d, rope_id,
           norm1_g, w_in, na_rpb, diff_lambda, diff_norm_g, ret_decay_fwd, ret_decay_bwd, ret_gn_g, ret_gn_b,
           swa_sink, w_out, norm2_g, w_router, w_gate, w_up, w_down):
    b, n, d = x.shape
    row = lambda v: v.reshape(1, -1).astype(F32)
    part = lambda r0, r1, k: mod[r0:r1, None, k * d:(k + 1) * d]
    sh1, sc1, g1, sh2, sc2, g2 = [part(0, b, k) for k in range(6)]
    ctx_rows = lambda k: jnp.broadcast_to(part(b, b + 1, k), (b, 1, d))
    sh1c, sc1c, g1c, sh2c, sc2c, g2c = [ctx_rows(k) for k in range(6)]

    w_in_b = w_in.astype(BF16)
    w_out_b = w_out.astype(BF16)
    w_router_t = w_router.T.astype(BF16)
    wg_b, wu_b, wd_b = w_gate.astype(BF16), w_up.astype(BF16), w_down.astype(BF16)

    p, t = norm_project(x, row(norm1_g), sc1, sh1, w_in_b, rope_h, rope_d)
    pc, tc = norm_project(xc, row(norm1_g), sc1c, sh1c, w_in_b, rope_id, rope_id)

    lam_init = 0.8 - 0.6 * math.exp(-0.3 * layer_idx)
    lq1, lk1, lq2, lk2 = [diff_lambda[k].astype(F32) for k in range(4)]
    lam = jnp.exp(jnp.sum(lq1 * lk1)) - jnp.exp(jnp.sum(lq2 * lk2)) + lam_init

    o_a = neighbourhood_attention(p, pc, na_rpb)
    o_b = diff_attention(p, t, pc, tc, lam.reshape(1), row(diff_norm_g), 1.0 - lam_init)
    o_c, oc_ret = bidirectional_retention(p, pc, ret_decay_fwd, ret_decay_bwd, ret_gn_g, ret_gn_b, with_ctx)
    o_d = window_attention(p, pc, swa_sink.astype(F32))

    x, h2, aff_t = out_project([o_a, o_b, o_c, o_d], w_out_b, x, g1, row(norm2_g), sc2, sh2, w_router_t)
    x = combine(x, expert_choice_ffn(h2, aff_t, wg_b, wu_b, wd_b), g2, final_g)

    if with_ctx:
        oc = _context_mixers(pc, oc_ret, lam, lam_init, diff_norm_g, swa_sink)
        xc, h2c, aff_c = out_project(oc, w_out_b, xc, g1c, row(norm2_g), sc2c, sh2c, w_router_t)
        xc = combine(xc, expert_choice_ffn(h2c, aff_c, wg_b, wu_b, wd_b), g2c, None)
    return x, xc


def kernel(x, c, ctx, c_ctx, w_mod, b_mod, norm1_g, w_in, na_rpb, diff_lambda, diff_norm_g,
           ret_decay_fwd, ret_decay_bwd, ret_gn_g, ret_gn_b, swa_sink, w_out, norm2_g,
           w_router, w_gate, w_up, w_down, final_norm_g):
    b, n, d = x.shape
    depth = w_in.shape[0]
    lc = ctx.shape[1]
    cvec = jnp.zeros((8, d), F32).at[:b].set(c).at[b].set(c_ctx)
    mod = modulation(cvec, w_mod, b_mod.reshape(depth, 1, -1))
    rope_h = _rope_tables(n, HEAD_DIM)
    rope_d = _rope_tables(n, DIFF_DIM)
    rope_id = _identity_rope_tables(lc)
    xc = ctx
    for li in range(depth):
        last = li == depth - 1
        x, xc = _layer(x, xc, mod[li], li, not last, final_norm_g.reshape(1, -1) if last else None,
                       rope_h, rope_d, rope_id,
                       norm1_g[li], w_in[li], na_rpb[li], diff_lambda[li], diff_norm_g[li],
                       ret_decay_fwd[li], ret_decay_bwd[li], ret_gn_g[li], ret_gn_b[li], swa_sink[li],
                       w_out[li], norm2_g[li], w_router[li], w_gate[li], w_up[li], w_down[li])
    return x
```

```python
import functools
import math

import jax
import jax.numpy as jnp
from jax import lax
from jax.experimental import pallas as pl
from jax.experimental.pallas import tpu as pltpu

GRID_W = 64
HEAD_DIM = 128
DIFF_DIM = HEAD_DIM // 2
NA_WIN_R = 8
NA_WIN_C = 16
RET_CHUNK = 128
SWA_WINDOW = 128
N_EXPERTS = 16
EC_CAPACITY = 2
ROPE_BASE = 10000.0
NORM_EPS = 1e-6
GN_EPS = 1e-5
NEG_INF = -1e30
LOG2E = math.log2(math.e)

BF16 = jnp.bfloat16
F32 = jnp.float32

VMEM_LIMIT_BYTES = 48 * 1024 * 1024

GROUP_BLOCKS = 4
COL_A_Q, COL_A_K, COL_A_V = 0, 4, 8
COL_B_K = 16
COL_C_Q, COL_C_K, COL_C_V, COL_C_G = 24, 28, 32, 36
COL_D_Q, COL_D_K, COL_D_V = 40, 44, 46
PROJ_TN = GROUP_BLOCKS * HEAD_DIM


def _cparams(*sem):
    return pltpu.CompilerParams(dimension_semantics=sem, vmem_limit_bytes=VMEM_LIMIT_BYTES)


def _mod_kernel(c_ref, w_ref, b_ref, o_ref):
    c = c_ref[...]
    s = (c * (1.0 / (1.0 + jnp.exp(-c)))).astype(BF16)
    o_ref[...] = jnp.dot(s, w_ref[...].astype(BF16), preferred_element_type=F32) + b_ref[...]


def modulation(cvec, w_mod, b_mod, *, tn=1024):
    depth, d, n6 = w_mod.shape
    return pl.pallas_call(
        _mod_kernel,
        grid=(depth, n6 // tn),
        in_specs=[
            pl.BlockSpec((8, d), lambda l, j: (0, 0)),
            pl.BlockSpec((None, d, tn), lambda l, j: (l, 0, j)),
            pl.BlockSpec((None, 1, tn), lambda l, j: (l, 0, j)),
        ],
        out_specs=pl.BlockSpec((None, 8, tn), lambda l, j: (l, 0, j)),
        out_shape=jax.ShapeDtypeStruct((depth, 8, n6), F32),
        compiler_params=_cparams("parallel", "parallel"),
        name="modulation",
    )(cvec, w_mod, b_mod)


def _rope_tables(n_tok, dim):
    t = jnp.arange(n_tok)
    row = (t // GRID_W).astype(F32)[:, None]
    col = (t % GRID_W).astype(F32)[:, None]
    nf = dim // 4
    lane = jnp.arange(HEAD_DIM)
    quarter = (lane % dim) // nf
    inv = ROPE_BASE ** (-jnp.arange(nf, dtype=F32) / nf)
    ang = jnp.where(quarter[None, :] < 2, row, col) * inv[lane % nf][None, :]
    cos, sin = jnp.cos(ang), jnp.sin(ang)
    even = (quarter % 2 == 0)[None, :]
    return cos, jnp.where(even, -sin, 0.0), jnp.where(even, 0.0, sin)


def _identity_rope_tables(n_tok):
    z = jnp.zeros((n_tok, HEAD_DIM), F32)
    return jnp.ones((n_tok, HEAD_DIM), F32), z, z


def _proj_modes():
    att = HEAD_DIM ** -0.5
    plain = [(None, 1.0)] * GROUP_BLOCKS
    return [
        ([(None, att)] * 4, None), (plain, None), (plain, None),
        ([("d", LOG2E * DIFF_DIM ** -0.5)] * 4, 0), ([("d", 1.0)] * 4, None), (plain, 1),
        ([("h", 1.0)] * 4, None), ([("h", att)] * 4, None), (plain, None), (plain, None),
        ([("h", att)] * 4, None), ([("h", 1.0)] * 2 + [(None, 1.0)] * 2, None),
    ]


def _proj_kernel(x_ref, g_ref, sc_ref, sh_ref, w_ref, ch_ref, sah_ref, sbh_ref, cd_ref, sad_ref, sbd_ref,
                 o_ref, t_ref):
    x = x_ref[...]
    y = x * lax.rsqrt(jnp.mean(x * x, axis=-1, keepdims=True) + NORM_EPS) * g_ref[...]
    h = (y * (1.0 + sc_ref[...]) + sh_ref[...]).astype(BF16)

    def rope(a, kind):
        if kind is None:
            return a
        c, sa, sb, sh = ((ch_ref, sah_ref, sbh_ref, HEAD_DIM // 4) if kind == "h"
                         else (cd_ref, sad_ref, sbd_ref, DIFF_DIM // 4))
        return (a * c[...] + pltpu.roll(a, HEAD_DIM - sh, 1) * sa[...] + pltpu.roll(a, sh, 1) * sb[...])

    for j, (blocks, t_slot) in enumerate(_proj_modes()):
        cols = slice(j * PROJ_TN, (j + 1) * PROJ_TN)
        acc = jnp.dot(h, w_ref[:, cols], preferred_element_type=F32)
        outs = []
        for hb, (kind, scale) in enumerate(blocks):
            a = rope(acc[:, hb * HEAD_DIM:(hb + 1) * HEAD_DIM], kind)
            outs.append(a if scale == 1.0 else a * scale)
        full = jnp.concatenate(outs, axis=1)
        o_ref[:, cols] = full.astype(o_ref.dtype)
        if t_slot is not None:
            t_ref[t_slot] = full.T.astype(t_ref.dtype)


def norm_project(x, norm_g, scale, shift, w_in, rope_h, rope_d, *, tm=256):
    b, n, d = x.shape
    width = w_in.shape[1]
    assert width == len(_proj_modes()) * PROJ_TN
    tm = min(tm, n)
    tok = lambda bi, i: (i, 0)
    return pl.pallas_call(
        _proj_kernel,
        grid=(b, n // tm),
        in_specs=[
            pl.BlockSpec((None, tm, d), lambda bi, i: (bi, i, 0)),
            pl.BlockSpec((1, d), lambda bi, i: (0, 0)),
            pl.BlockSpec((None, 1, d), lambda bi, i: (bi, 0, 0)),
            pl.BlockSpec((None, 1, d), lambda bi, i: (bi, 0, 0)),
            pl.BlockSpec((d, width), lambda bi, i: (0, 0), pipeline_mode=pl.Buffered(1)),
        ] + [pl.BlockSpec((tm, HEAD_DIM), tok)] * 6,
        out_specs=[
            pl.BlockSpec((None, tm, width), lambda bi, i: (bi, i, 0)),
            pl.BlockSpec((None, 2, PROJ_TN, tm), lambda bi, i: (bi, 0, 0, i)),
        ],
        out_shape=[jax.ShapeDtypeStruct((b, n, width), BF16),
                   jax.ShapeDtypeStruct((b, 2, PROJ_TN, n), BF16)],
        compiler_params=_cparams("parallel", "parallel"),
        name="norm_project",
    )(x, norm_g, scale, shift, w_in, *rope_h, *rope_d)


def _softmax_pv(s_loc, s_ctx, vw, vc, sink):
    m = jnp.maximum(jnp.max(s_loc, axis=-1, keepdims=True), jnp.max(s_ctx, axis=-1, keepdims=True))
    if sink is not None:
        m = jnp.maximum(m, sink)
    p_loc = jnp.exp(s_loc - m)
    p_ctx = jnp.exp(s_ctx - m)
    l = jnp.sum(p_loc, axis=-1, keepdims=True) + jnp.sum(p_ctx, axis=-1, keepdims=True)
    if sink is not None:
        l = l + jnp.exp(sink - m)
    o = (jnp.dot(p_loc.astype(BF16), vw, preferred_element_type=F32)
         + jnp.dot(p_ctx.astype(BF16), vc, preferred_element_type=F32))
    return o * (1.0 / l)


def _nt_dot(a, b):
    return lax.dot_general(a, b, (((1,), (1,)), ((), ())), preferred_element_type=F32)


def _na_kernel(q_ref, k_ref, v_ref, kc_ref, vc_ref, bias_ref, o_ref, *, tq, wk):
    i = pl.program_id(2)
    n = k_ref.shape[0]
    rows_q = tq // GRID_W
    kstart = jnp.clip(i * rows_q - NA_WIN_R // 2, 0, (n - wk) // GRID_W) * GRID_W
    kstart = pl.multiple_of(kstart, GRID_W)
    q = q_ref[...]
    kw = k_ref[pl.ds(kstart, wk), :]
    vw = v_ref[pl.ds(kstart, wk), :]
    s_loc = _nt_dot(q, kw) + bias_ref[...]
    s_ctx = _nt_dot(q, kc_ref[...])
    o_ref[...] = _softmax_pv(s_loc, s_ctx, vw, vc_ref[...], None).astype(o_ref.dtype)


def _na_bias(rpb, rows, rows_q, rows_k):
    n_blk = rows // rows_q
    kr = NA_WIN_R
    cols = jnp.arange(GRID_W)
    col_start = jnp.clip(cols - NA_WIN_C // 2, 0, GRID_W - NA_WIN_C)
    col_ok = (cols[None, :] >= col_start[:, None]) & (cols[None, :] < col_start[:, None] + NA_WIN_C)
    col_off = jnp.clip(cols[None, :] - cols[:, None] + NA_WIN_C - 1, 0, 2 * NA_WIN_C - 2)
    out = []
    for blk in (0, 1, n_blk - 1):
        r = blk * rows_q + jnp.arange(rows_q)
        kstart = min(max(blk * rows_q - kr // 2, 0), rows - rows_k)
        krow = kstart + jnp.arange(rows_k)
        start = jnp.clip(r - kr // 2, 0, rows - kr)
        row_ok = (krow[None, :] >= start[:, None]) & (krow[None, :] < start[:, None] + kr)
        row_off = jnp.clip(krow[None, :] - r[:, None] + NA_WIN_R - 1, 0, 2 * NA_WIN_R - 2)
        bias = rpb.astype(F32)[:, row_off][:, :, :, col_off]
        ok = row_ok[:, :, None, None] & col_ok[None, None]
        bias = jnp.where(ok[None], bias, NEG_INF).transpose(0, 1, 3, 2, 4)
        out.append(bias.reshape(rpb.shape[0], rows_q * GRID_W, rows_k * GRID_W))
    return jnp.stack(out, axis=1)


def neighbourhood_attention(p, pc, rpb, *, rows_q=4):
    b, n, _ = p.shape
    lc = pc.shape[1]
    h = rpb.shape[0]
    rows = n // GRID_W
    rows_k = rows_q + NA_WIN_R - 1
    assert rows % rows_q == 0 and rows >= rows_k and rows_q >= NA_WIN_R // 2
    tq, wk = rows_q * GRID_W, rows_k * GRID_W
    n_blk = rows // rows_q
    bias = _na_bias(rpb, rows, rows_q, rows_k)
    kern = functools.partial(_na_kernel, tq=tq, wk=wk)
    cls = lambda i: jnp.where(i == 0, 0, jnp.where(i == n_blk - 1, 2, 1))
    return pl.pallas_call(
        kern,
        grid=(b, h, n_blk),
        in_specs=[
            pl.BlockSpec((None, tq, HEAD_DIM), lambda bi, hi, i: (bi, i, COL_A_Q + hi)),
            pl.BlockSpec((None, n, HEAD_DIM), lambda bi, hi, i: (bi, 0, COL_A_K + hi)),
            pl.BlockSpec((None, n, HEAD_DIM), lambda bi, hi, i: (bi, 0, COL_A_V + hi)),
            pl.BlockSpec((None, lc, HEAD_DIM), lambda bi, hi, i: (bi, 0, COL_A_K + hi)),
            pl.BlockSpec((None, lc, HEAD_DIM), lambda bi, hi, i: (bi, 0, COL_A_V + hi)),
            pl.BlockSpec((None, None, tq, wk), lambda bi, hi, i: (hi, cls(i), 0, 0)),
        ],
        out_specs=pl.BlockSpec((None, tq, HEAD_DIM), lambda bi, hi, i: (bi, i, hi)),
        out_shape=jax.ShapeDtypeStruct((b, n, h * HEAD_DIM), BF16),
        compiler_params=_cparams("parallel", "parallel", "arbitrary"),
        name="neighbourhood_attention",
    )(p, p, p, pc, pc, bias)


def _swa_kernel(sink_ref, q_ref, k_ref, v_ref, kc_ref, vc_ref, o_ref, *, tq, wk):
    hi = pl.program_id(1)
    i = pl.program_id(2)
    n = k_ref.shape[0]
    q0 = i * tq
    kstart = pl.multiple_of(jnp.clip(q0 - SWA_WINDOW, 0, n - wk), SWA_WINDOW)
    q = q_ref[...]
    kw = k_ref[pl.ds(kstart, wk), :]
    vw = v_ref[pl.ds(kstart, wk), :]
    qpos = q0 + lax.broadcasted_iota(jnp.int32, (tq, wk), 0)
    kpos = kstart + lax.broadcasted_iota(jnp.int32, (tq, wk), 1)
    s_loc = jnp.where(jnp.abs(kpos - qpos) <= SWA_WINDOW, _nt_dot(q, kw), NEG_INF)
    s_ctx = _nt_dot(q, kc_ref[...])
    o_ref[...] = _softmax_pv(s_loc, s_ctx, vw, vc_ref[...], sink_ref[hi]).astype(o_ref.dtype)


def window_attention(p, pc, sink, *, tq=256):
    b, n, _ = p.shape
    lc = pc.shape[1]
    hq = sink.shape[0]
    grp = 2
    tq = min(tq, n)
    wk = min(tq + 2 * SWA_WINDOW, n)
    kern = functools.partial(_swa_kernel, tq=tq, wk=wk)
    return pl.pallas_call(
        kern,
        grid=(b, hq, n // tq),
        in_specs=[
            pl.BlockSpec(memory_space=pltpu.SMEM),
            pl.BlockSpec((None, tq, HEAD_DIM), lambda bi, hi, i: (bi, i, COL_D_Q + hi)),
            pl.BlockSpec((None, n, HEAD_DIM), lambda bi, hi, i: (bi, 0, COL_D_K + hi // grp)),
            pl.BlockSpec((None, n, HEAD_DIM), lambda bi, hi, i: (bi, 0, COL_D_V + hi // grp)),
            pl.BlockSpec((None, lc, HEAD_DIM), lambda bi, hi, i: (bi, 0, COL_D_K + hi // grp)),
            pl.BlockSpec((None, lc, HEAD_DIM), lambda bi, hi, i: (bi, 0, COL_D_V + hi // grp)),
        ],
        out_specs=pl.BlockSpec((None, tq, HEAD_DIM), lambda bi, hi, i: (bi, i, hi)),
        out_shape=jax.ShapeDtypeStruct((b, n, hq * HEAD_DIM), BF16),
        compiler_params=_cparams("parallel", "parallel", "arbitrary"),
        name="window_attention",
    )(sink, p, p, p, pc, pc)


DIFF_SLOTS = 3
DIFF_AUG_ROWS = 16


def _diff_attn_kernel(lam_ref, qt_ref, k_ref, vt_ref, kc_ref, vct_ref, g_ref, o_ref,
                      qbd_ref, m_ref, acc_ref, s_scr, p_scr, a_scr, x_scr, *, tq, tk, post_scale):
    qt = qt_ref[...].astype(F32)
    row = lax.broadcasted_iota(jnp.int32, qt.shape, 0)
    zero = jnp.zeros_like(qt)
    qbd_ref[:, :tq] = jnp.where(row < DIFF_DIM, qt, zero).astype(BF16)
    qbd_ref[:, tq:] = jnp.where(row >= DIFF_DIM, qt, zero).astype(BF16)

    def aug(vt_tile):
        r = lax.broadcasted_iota(jnp.int32, (DIFF_AUG_ROWS, vt_tile.shape[1]), 0)
        return jnp.concatenate([vt_tile, jnp.where(r == 0, 1.0, 0.0).astype(BF16)], axis=0)

    s = jnp.dot(kc_ref[...], qbd_ref[...], preferred_element_type=F32)
    m0 = jnp.max(s, axis=0, keepdims=True)
    m_ref[...] = m0
    acc_ref[...] = jnp.dot(aug(vct_ref[...]), jnp.exp2(s - m0).astype(BF16), preferred_element_type=F32)

    def scores(t, slot):
        off = pl.multiple_of(t * tk, tk)
        s = jnp.dot(k_ref[pl.ds(off, tk), :], qbd_ref[...], preferred_element_type=F32)
        s_scr[slot] = s
        x_scr[slot] = jnp.max(s, axis=0, keepdims=True)

    def exponentials(slot):
        m_old = m_ref[...]
        m_new = jnp.maximum(m_old, x_scr[slot])
        a_scr[slot] = jnp.exp2(m_old - m_new)
        p_scr[slot] = jnp.exp2(s_scr[slot] - m_new).astype(BF16)
        m_ref[...] = m_new

    def accumulate(t, slot):
        off = pl.multiple_of(t * tk, tk)
        pv = jnp.dot(aug(vt_ref[:, pl.ds(off, tk)]), p_scr[slot], preferred_element_type=F32)
        acc_ref[...] = a_scr[slot] * acc_ref[...] + pv

    def stage(t, t_mod, do_scores=True, do_acc=True):
        if do_scores:
            scores(t + 1, (t_mod + 1) % DIFF_SLOTS)
        exponentials(t_mod % DIFF_SLOTS)
        if do_acc:
            accumulate(t - 1, (t_mod - 1) % DIFF_SLOTS)

    n_kt = k_ref.shape[0] // tk
    unroll = 2 * DIFF_SLOTS
    trips = (n_kt - 2) // unroll
    scores(0, 0)
    stage(0, 0, do_acc=False)

    def body(i, carry):
        for u in range(unroll):
            stage(1 + unroll * i + u, 1 + u)
        return carry

    lax.fori_loop(0, trips, body, 0)
    for t in range(1 + unroll * trips, n_kt - 1):
        stage(t, t)
    stage(n_kt - 1, n_kt - 1, do_scores=False)
    accumulate(n_kt - 1, (n_kt - 1) % DIFF_SLOTS)

    lam = lam_ref[0]
    inv = 1.0 / acc_ref[HEAD_DIM:HEAD_DIM + 1, :]
    acc = acc_ref[:HEAD_DIM, :]
    o_t = acc[:, :tq] * inv[:, :tq] - lam * (acc[:, tq:] * inv[:, tq:])
    o = o_t.T
    ms = jnp.mean(o * o, axis=-1, keepdims=True)
    y = o * lax.rsqrt(ms + NORM_EPS) * g_ref[...]
    o_ref[...] = (y * post_scale).astype(o_ref.dtype)


def diff_attention(p, t, pc, tc, lam, norm_g, post_scale, *, tq=256, tk=512):
    b, n, _ = p.shape
    lc = pc.shape[1]
    h = GROUP_BLOCKS
    tq = min(tq, n)
    tk = min(tk, n)
    assert n % tq == 0 and n % tk == 0 and n // tk >= 2
    kern = functools.partial(_diff_attn_kernel, tq=tq, tk=tk, post_scale=post_scale)
    return pl.pallas_call(
        kern,
        grid=(b, h, n // tq),
        in_specs=[
            pl.BlockSpec(memory_space=pltpu.SMEM),
            pl.BlockSpec((None, None, HEAD_DIM, tq), lambda bi, hi, qi: (bi, 0, hi, qi)),
            pl.BlockSpec((None, n, HEAD_DIM), lambda bi, hi, qi: (bi, 0, COL_B_K + hi)),
            pl.BlockSpec((None, None, HEAD_DIM, n), lambda bi, hi, qi: (bi, 1, hi, 0)),
            pl.BlockSpec((None, lc, HEAD_DIM), lambda bi, hi, qi: (bi, 0, COL_B_K + hi)),
            pl.BlockSpec((None, None, HEAD_DIM, lc), lambda bi, hi, qi: (bi, 1, hi, 0)),
            pl.BlockSpec((1, HEAD_DIM), lambda bi, hi, qi: (0, 0)),
        ],
        out_specs=pl.BlockSpec((None, tq, HEAD_DIM), lambda bi, hi, qi: (bi, qi, hi)),
        out_shape=jax.ShapeDtypeStruct((b, n, h * HEAD_DIM), BF16),
        scratch_shapes=[
            pltpu.VMEM((HEAD_DIM, 2 * tq), BF16),
            pltpu.VMEM((1, 2 * tq), F32),
            pltpu.VMEM((HEAD_DIM + DIFF_AUG_ROWS, 2 * tq), F32),
            pltpu.VMEM((DIFF_SLOTS, tk, 2 * tq), F32),
            pltpu.VMEM((DIFF_SLOTS, tk, 2 * tq), BF16),
            pltpu.VMEM((DIFF_SLOTS, 1, 2 * tq), F32),
            pltpu.VMEM((DIFF_SLOTS, 1, 2 * tq), F32),
        ],
        compiler_params=_cparams("parallel", "parallel", "arbitrary"),
        name="diff_attention",
    )(lam, t, p, t, pc, tc, norm_g)


def _retention_kernel(cdec_ref, q_ref, k_ref, v_ref, intra_ref, qdec_ref, kdec_ref, s0_ref, *rest,
                      reverse, final):
    if final:
        of_ref, gate_ref, gng_ref, gnb_ref, o_ref, sfin_ref, s_ref = rest
    else:
        o_ref, sfin_ref, s_ref = rest
    i = pl.program_id(1)
    n_heads = s_ref.shape[0]
    n_chunks = q_ref.shape[0] // RET_CHUNK

    @pl.when(i == 0)
    def _():
        s_ref[...] = s0_ref[...]

    order = range(n_chunks - 1, -1, -1) if reverse else range(n_chunks)
    for c in order:
        rs = slice(c * RET_CHUNK, (c + 1) * RET_CHUNK)
        for h in range(n_heads):
            cs = slice(h * HEAD_DIM, (h + 1) * HEAD_DIM)
            qh, kh, vh = q_ref[rs, cs], k_ref[rs, cs], v_ref[rs, cs]
            qd = (qh.astype(F32) * qdec_ref[h]).astype(BF16)
            kd_t = (kh.astype(F32) * kdec_ref[h]).T.astype(BF16)
            a = (_nt_dot(qh, kh) * intra_ref[h]).astype(BF16)
            s = s_ref[h]
            o = (jnp.dot(a, vh, preferred_element_type=F32)
                 + jnp.dot(qd, s.astype(BF16), preferred_element_type=F32))
            s_ref[h] = cdec_ref[h] * s + jnp.dot(kd_t, vh, preferred_element_type=F32)
            if final:
                o = o + of_ref[rs, cs]
                mu = jnp.mean(o, axis=-1, keepdims=True)
                var = jnp.mean(jnp.square(o - mu), axis=-1, keepdims=True)
                y = (o - mu) * lax.rsqrt(var + GN_EPS) * gng_ref[:, cs] + gnb_ref[:, cs]
                g = gate_ref[rs, cs].astype(F32)
                o = g * (1.0 / (1.0 + jnp.exp(-g))) * y
            o_ref[rs, cs] = o.astype(o_ref.dtype)

    @pl.when(i == pl.num_programs(1) - 1)
    def _():
        sfin_ref[...] = s_ref[...]


def _retention_pass(p, tables, s0, final_inputs, *, reverse, blk_chunks=4):
    b, n, _ = p.shape
    cdec, intra, qdec, kdec = tables
    h = intra.shape[0]
    width = h * HEAD_DIM
    n_chunks = n // RET_CHUNK
    blk_chunks = min(blk_chunks, n_chunks)
    assert n_chunks % blk_chunks == 0
    tb = blk_chunks * RET_CHUNK
    n_blk = n // tb
    pos = (lambda i: n_blk - 1 - i) if reverse else (lambda i: i)
    colblk = lambda c: (lambda bi, i: (bi, pos(i), c // GROUP_BLOCKS))
    tab = pl.BlockSpec((h, RET_CHUNK, HEAD_DIM), lambda bi, i: (0, 0, 0))
    state = pl.BlockSpec((None, h, HEAD_DIM, HEAD_DIM), lambda bi, i: (bi, 0, 0, 0))
    in_specs = [pl.BlockSpec(memory_space=pltpu.SMEM),
                pl.BlockSpec((None, tb, width), colblk(COL_C_Q)),
                pl.BlockSpec((None, tb, width), colblk(COL_C_K)),
                pl.BlockSpec((None, tb, width), colblk(COL_C_V)),
                tab, tab, tab, state]
    args = [cdec, p, p, p, intra, qdec, kdec, s0]
    final = final_inputs is not None
    if final:
        o_fwd, gn_g, gn_b = final_inputs
        in_specs += [pl.BlockSpec((None, tb, width), lambda bi, i: (bi, pos(i), 0)),
                     pl.BlockSpec((None, tb, width), colblk(COL_C_G)),
                     pl.BlockSpec((1, width), lambda bi, i: (0, 0)),
                     pl.BlockSpec((1, width), lambda bi, i: (0, 0))]
        args += [o_fwd, p, gn_g, gn_b]
    kern = functools.partial(_retention_kernel, reverse=reverse, final=final)
    return pl.pallas_call(
        kern,
        grid=(b, n_blk),
        in_specs=in_specs,
        out_specs=[pl.BlockSpec((None, tb, width), lambda bi, i: (bi, pos(i), 0)), state],
        out_shape=[jax.ShapeDtypeStruct((b, n, width), BF16 if final else F32),
                   jax.ShapeDtypeStruct((b, h, HEAD_DIM, HEAD_DIM), F32)],
        scratch_shapes=[pltpu.VMEM((h, HEAD_DIM, HEAD_DIM), F32)],
        compiler_params=_cparams("parallel", "arbitrary"),
        name="retention_bwd" if reverse else "retention_fwd",
    )(*args)


def _retention_tables(decay, reverse):
    lg = jax.nn.log_sigmoid(decay.astype(F32))[:, None, None]
    pos = jnp.arange(RET_CHUNK, dtype=F32)
    rel = pos[:, None] - pos[None, :]
    if reverse:
        rel = -rel
        q_pow, k_pow = RET_CHUNK - pos, pos
    else:
        q_pow, k_pow = pos + 1.0, RET_CHUNK - 1.0 - pos
    intra = jnp.where(rel >= 0, jnp.exp(jnp.maximum(rel, 0.0) * lg), 0.0)
    bc = lambda e: jnp.broadcast_to(jnp.exp(e[None, :, None] * lg), intra.shape)
    cdec = jnp.exp(RET_CHUNK * lg[:, 0, 0])
    return cdec, intra, bc(q_pow), bc(k_pow)


def bidirectional_retention(p, pc, decay_f, decay_b, gn_g, gn_b, with_ctx):
    b = p.shape[0]
    h = decay_f.shape[0]
    tf = _retention_tables(decay_f, False)
    tb = _retention_tables(decay_b, True)
    s0 = jnp.zeros((b, h, HEAD_DIM, HEAD_DIM), F32)
    gn = (gn_g.reshape(1, -1).astype(F32), gn_b.reshape(1, -1).astype(F32))
    oc_f, s_f = _retention_pass(pc, tf, s0, None, reverse=False)
    oc, s_b = _retention_pass(pc, tb, s0, (oc_f,) + gn, reverse=True)
    o_f, _ = _retention_pass(p, tf, s_f, None, reverse=False)
    o, _ = _retention_pass(p, tb, s_b, (o_f,) + gn, reverse=True)
    return o, (oc if with_ctx else None)


def _out_kernel(oa_ref, ob_ref, oc_ref, od_ref, w_ref, x_ref, g1_ref, ng_ref, sc_ref, sh_ref, wr_ref,
                xo_ref, h_ref, aff_ref):
    gw = oa_ref.shape[1]
    acc = jnp.dot(oa_ref[...], w_ref[0 * gw:1 * gw, :], preferred_element_type=F32)
    acc += jnp.dot(ob_ref[...], w_ref[1 * gw:2 * gw, :], preferred_element_type=F32)
    acc += jnp.dot(oc_ref[...], w_ref[2 * gw:3 * gw, :], preferred_element_type=F32)
    acc += jnp.dot(od_ref[...], w_ref[3 * gw:4 * gw, :], preferred_element_type=F32)
    x = x_ref[...] + g1_ref[...] * acc
    xo_ref[...] = x
    y = x * lax.rsqrt(jnp.mean(x * x, axis=-1, keepdims=True) + NORM_EPS) * ng_ref[...]
    h2 = (y * (1.0 + sc_ref[...]) + sh_ref[...]).astype(BF16)
    h_ref[...] = h2
    logits = _nt_dot(wr_ref[...], h2)
    e = jnp.exp(logits - jnp.max(logits, axis=0, keepdims=True))
    aff_ref[...] = e * (1.0 / jnp.sum(e, axis=0, keepdims=True))


def out_project(o_groups, w_out, x, g1, norm_g, scale, shift, w_router_t, *, tm=256):
    b, n, d = x.shape
    n_e = w_router_t.shape[0]
    gw = o_groups[0].shape[-1]
    tm = min(tm, n)
    tile = lambda w: pl.BlockSpec((None, tm, w), lambda bi, i: (bi, i, 0))
    vec = pl.BlockSpec((None, 1, d), lambda bi, i: (bi, 0, 0))
    return pl.pallas_call(
        _out_kernel,
        grid=(b, n // tm),
        in_specs=[tile(gw)] * 4 + [
            pl.BlockSpec((d, d), lambda bi, i: (0, 0)),
            tile(d), vec,
            pl.BlockSpec((1, d), lambda bi, i: (0, 0)),
            vec, vec,
            pl.BlockSpec((n_e, d), lambda bi, i: (0, 0)),
        ],
        out_specs=[tile(d), tile(d), pl.BlockSpec((None, n_e, tm), lambda bi, i: (bi, 0, i))],
        out_shape=[jax.ShapeDtypeStruct((b, n, d), F32),
                   jax.ShapeDtypeStruct((b, n, d), BF16),
                   jax.ShapeDtypeStruct((b, n_e, n), F32)],
        compiler_params=_cparams("parallel", "parallel"),
        name="out_project",
    )(*o_groups, w_out, x, g1, norm_g, scale, shift, w_router_t)


def _ffn_kernel(x_ref, gate_ref, wg_ref, wu_ref, wd_ref, o_ref):
    x = x_ref[...]
    a = jnp.dot(x, wg_ref[...], preferred_element_type=F32)
    u = jnp.dot(x, wu_ref[...], preferred_element_type=F32)
    hm = (a * (1.0 / (1.0 + jnp.exp(-a))) * u).astype(BF16)
    o_ref[...] = jnp.dot(hm, wd_ref[...], preferred_element_type=F32) * gate_ref[...]


def expert_ffn(xin, gate, w_gate, w_up, w_down, *, tc=256):
    b, n_e, cap, d = xin.shape
    ff = w_gate.shape[-1]
    tc = min(tc, cap)
    return pl.pallas_call(
        _ffn_kernel,
        grid=(n_e, b, cap // tc),
        in_specs=[
            pl.BlockSpec((None, None, tc, d), lambda e, bi, i: (bi, e, i, 0)),
            pl.BlockSpec((None, None, tc, 1), lambda e, bi, i: (bi, e, i, 0)),
            pl.BlockSpec((None, d, ff), lambda e, bi, i: (e, 0, 0)),
            pl.BlockSpec((None, d, ff), lambda e, bi, i: (e, 0, 0)),
            pl.BlockSpec((None, ff, d), lambda e, bi, i: (e, 0, 0)),
        ],
        out_specs=pl.BlockSpec((None, None, tc, d), lambda e, bi, i: (bi, e, i, 0)),
        out_shape=jax.ShapeDtypeStruct((b, n_e, cap, d), F32),
        compiler_params=_cparams("parallel", "parallel", "parallel"),
        name="expert_ffn",
    )(xin, gate, w_gate, w_up, w_down)


def _combine_kernel(x_ref, m_ref, g_ref, ng_ref, o_ref, *, final):
    x = x_ref[...] + g_ref[...] * m_ref[...]
    if final:
        x = x * lax.rsqrt(jnp.mean(x * x, axis=-1, keepdims=True) + NORM_EPS) * ng_ref[...]
    o_ref[...] = x


def combine(x, moe, g2, final_g, *, tm=512):
    b, n, d = x.shape
    tm = min(tm, n)
    final = final_g is not None
    ng = final_g if final else jnp.ones((1, d), F32)
    tile = pl.BlockSpec((None, tm, d), lambda bi, i: (bi, i, 0))
    return pl.pallas_call(
        functools.partial(_combine_kernel, final=final),
        grid=(b, n // tm),
        in_specs=[tile, tile, pl.BlockSpec((None, 1, d), lambda bi, i: (bi, 0, 0)),
                  pl.BlockSpec((1, d), lambda bi, i: (0, 0))],
        out_specs=tile,
        out_shape=jax.ShapeDtypeStruct((b, n, d), F32),
        compiler_params=_cparams("parallel", "parallel"),
        name="combine",
    )(x, moe, g2, ng)


def expert_choice_ffn(h2, aff_t, w_gate, w_up, w_down):
    b, t, d = h2.shape
    cap = EC_CAPACITY * t // N_EXPERTS
    gate, idx = lax.top_k(aff_t, cap)
    xin = jax.vmap(lambda hb, ib: hb[ib])(h2, idx)
    y = expert_ffn(xin, gate[..., None], w_gate, w_up, w_down)
    flat = (idx + (jnp.arange(b, dtype=idx.dtype) * t)[:, None, None]).reshape(-1)
    return jnp.zeros((b * t, d), y.dtype).at[flat].add(y.reshape(-1, d)).reshape(b, t, d)


def _ctx_attn_kernel(scal_ref, q_ref, k_ref, v_ref, g_ref, o_ref, *, kind, post_scale):
    q, k, v = q_ref[...], k_ref[...], v_ref[...]
    if kind == "diff":
        qf = q.astype(F32)
        lane = lax.broadcasted_iota(jnp.int32, qf.shape, 1)

        def probs(keep):
            s = _nt_dot(jnp.where(keep, qf, 0.0).astype(BF16), k)
            e = jnp.exp2(s - jnp.max(s, axis=-1, keepdims=True))
            return e * (1.0 / jnp.sum(e, axis=-1, keepdims=True))

        w = probs(lane < DIFF_DIM) - scal_ref[0] * probs(lane >= DIFF_DIM)
        o = jnp.dot(w.astype(BF16), v, preferred_element_type=F32)
        o = o * lax.rsqrt(jnp.mean(o * o, axis=-1, keepdims=True) + NORM_EPS) * g_ref[...] * post_scale
    else:
        s = _nt_dot(q, k)
        m = jnp.max(s, axis=-1, keepdims=True)
        if kind == "sink":
            sink = scal_ref[pl.program_id(1)]
            m = jnp.maximum(m, sink)
        e = jnp.exp(s - m)
        l = jnp.sum(e, axis=-1, keepdims=True)
        if kind == "sink":
            l = l + jnp.exp(sink - m)
        o = jnp.dot((e * (1.0 / l)).astype(BF16), v, preferred_element_type=F32)
    o_ref[...] = o.astype(o_ref.dtype)


def _context_attention(pc, scal, norm_g, kind, post_scale, q_col, k_col, v_col, grp):
    b, lc, _ = pc.shape
    h = GROUP_BLOCKS
    col = lambda c0, div: (lambda bi, hi: (bi, 0, c0 + hi // div))
    return pl.pallas_call(
        functools.partial(_ctx_attn_kernel, kind=kind, post_scale=post_scale),
        grid=(b, h),
        in_specs=[
            pl.BlockSpec(memory_space=pltpu.SMEM),
            pl.BlockSpec((None, lc, HEAD_DIM), col(q_col, 1)),
            pl.BlockSpec((None, lc, HEAD_DIM), col(k_col, grp)),
            pl.BlockSpec((None, lc, HEAD_DIM), col(v_col, grp)),
            pl.BlockSpec((1, HEAD_DIM), lambda bi, hi: (0, 0)),
        ],
        out_specs=pl.BlockSpec((None, lc, HEAD_DIM), lambda bi, hi: (bi, 0, hi)),
        out_shape=jax.ShapeDtypeStruct((b, lc, h * HEAD_DIM), BF16),
        compiler_params=_cparams("parallel", "parallel"),
        name="context_attention_" + kind,
    )(scal, pc, pc, pc, norm_g)


def _context_mixers(pc, oc_ret, lam, lam_init, diff_norm_g, swa_sink):
    h = GROUP_BLOCKS
    zeros = jnp.zeros((h,), F32)
    ones = jnp.ones((1, HEAD_DIM), F32)
    o_a = _context_attention(pc, zeros, ones, "plain", 1.0, COL_A_Q, COL_A_K, COL_A_V, 1)
    o_b = _context_attention(pc, lam.reshape(1), diff_norm_g.reshape(1, HEAD_DIM).astype(F32), "diff",
                             1.0 - lam_init, COL_B_K - h, COL_B_K, COL_B_K + h, 1)
    o_d = _context_attention(pc, swa_sink.astype(F32), ones, "sink", 1.0, COL_D_Q, COL_D_K, COL_D_V, 2)
    return [o_a, o_b, oc_ret, o_d]


def _layer(x, xc, mod, layer_idx, with_ctx, final_g, rope_h, rope_d, rope_id,
           norm1_g, w_in, na_rpb, diff_lambda, diff_norm_g, ret_decay_fwd, ret_decay_bwd, ret_gn_g, ret_gn_b,
           swa_sink, w_out, norm2_g, w_router, w_gate, w_up, w_down):
    b, n, d = x.shape
    row = lambda v: v.reshape(1, -1).astype(F32)
    part = lambda r0, r1, k: mod[r0:r1, None, k * d:(k + 1) * d]
    sh1, sc1, g1, sh2, sc2, g2 = [part(0, b, k) for k in range(6)]
    ctx_rows = lambda k: jnp.broadcast_to(part(b, b + 1, k), (b, 1, d))
    sh1c, sc1c, g1c, sh2c, sc2c, g2c = [ctx_rows(k) for k in range(6)]

    w_in_b = w_in.astype(BF16)
    w_out_b = w_out.astype(BF16)
    w_router_t = w_router.T.astype(BF16)
    wg_b, wu_b, wd_b = w_gate.astype(BF16), w_up.astype(BF16), w_down.astype(BF16)

    p, t = norm_project(x, row(norm1_g), sc1, sh1, w_in_b, rope_h, rope_d)
    pc, tc = norm_project(xc, row(norm1_g), sc1c, sh1c, w_in_b, rope_id, rope_id)

    lam_init = 0.8 - 0.6 * math.exp(-0.3 * layer_idx)
    lq1, lk1, lq2, lk2 = [diff_lambda[k].astype(F32) for k in range(4)]
    lam = jnp.exp(jnp.sum(lq1 * lk1)) - jnp.exp(jnp.sum(lq2 * lk2)) + lam_init

    o_a = neighbourhood_attention(p, pc, na_rpb)
    o_b = diff_attention(p, t, pc, tc, lam.reshape(1), row(diff_norm_g), 1.0 - lam_init)
    o_c, oc_ret = bidirectional_retention(p, pc, ret_decay_fwd, ret_decay_bwd, ret_gn_g, ret_gn_b, with_ctx)
    o_d = window_attention(p, pc, swa_sink.astype(F32))

    x, h2, aff_t = out_project([o_a, o_b, o_c, o_d], w_out_b, x, g1, row(norm2_g), sc2, sh2, w_router_t)
    x = combine(x, expert_choice_ffn(h2, aff_t, wg_b, wu_b, wd_b), g2, final_g)

    if with_ctx:
        oc = _context_mixers(pc, oc_ret, lam, lam_init, diff_norm_g, swa_sink)
        xc, h2c, aff_c = out_project(oc, w_out_b, xc, g1c, row(norm2_g), sc2c, sh2c, w_router_t)
        xc = combine(xc, expert_choice_ffn(h2c, aff_c, wg_b, wu_b, wd_b), g2c, None)
    return x, xc


def kernel(x, c, ctx, c_ctx, w_mod, b_mod, norm1_g, w_in, na_rpb, diff_lambda, diff_norm_g,
           ret_decay_fwd, ret_decay_bwd, ret_gn_g, ret_gn_b, swa_sink, w_out, norm2_g,
           w_router, w_gate, w_up, w_down, final_norm_g):
    b, n, d = x.shape
    depth = w_in.shape[0]
    lc = ctx.shape[1]
    cvec = jnp.zeros((8, d), F32).at[:b].set(c).at[b].set(c_ctx)
    mod = modulation(cvec, w_mod, b_mod.reshape(depth, 1, -1))
    rope_h = _rope_tables(n, HEAD_DIM)
    rope_d = _rope_tables(n, DIFF_DIM)
    rope_id = _identity_rope_tables(lc)
    xc = ctx
    for li in range(depth):
        last = li == depth - 1
        x, xc = _layer(x, xc, mod[li], li, not last, final_norm_g.reshape(1, -1) if last else None,
                       rope_h, rope_d, rope_id,
                       norm1_g[li], w_in[li], na_rpb[li], diff_lambda[li], diff_norm_g[li],
                       ret_decay_fwd[li], ret_decay_bwd[li], ret_gn_g[li], ret_gn_b[li], swa_sink[li],
                       w_out[li], norm2_g[li], w_router[li], w_gate[li], w_up[li], w_down[li])
    return x
```

```python
import functools
import math

import jax
import jax.numpy as jnp
from jax import lax
from jax.experimental import pallas as pl
from jax.experimental.pallas import tpu as pltpu

GRID_W = 64
HEAD_DIM = 128
DIFF_DIM = HEAD_DIM // 2
NA_WIN_R = 8
NA_WIN_C = 16
RET_CHUNK = 128
SWA_WINDOW = 128
N_EXPERTS = 16
EC_CAPACITY = 2
ROPE_BASE = 10000.0
NORM_EPS = 1e-6
GN_EPS = 1e-5
NEG_INF = -1e30
LOG2E = math.log2(math.e)

BF16 = jnp.bfloat16
F32 = jnp.float32

VMEM_LIMIT_BYTES = 48 * 1024 * 1024

GROUP_BLOCKS = 4
COL_A_Q, COL_A_K, COL_A_V = 0, 4, 8
COL_B_K = 16
COL_C_Q, COL_C_K, COL_C_V, COL_C_G = 24, 28, 32, 36
COL_D_Q, COL_D_K, COL_D_V = 40, 44, 46
PROJ_TN = GROUP_BLOCKS * HEAD_DIM


def _cparams(*sem):
    return pltpu.CompilerParams(dimension_semantics=sem, vmem_limit_bytes=VMEM_LIMIT_BYTES)


def _mod_kernel(c_ref, w_ref, b_ref, o_ref):
    c = c_ref[...]
    s = (c * (1.0 / (1.0 + jnp.exp(-c)))).astype(BF16)
    o_ref[...] = jnp.dot(s, w_ref[...].astype(BF16), preferred_element_type=F32) + b_ref[...]


def modulation(cvec, w_mod, b_mod, *, tn=1024):
    depth, d, n6 = w_mod.shape
    return pl.pallas_call(
        _mod_kernel,
        grid=(depth, n6 // tn),
        in_specs=[
            pl.BlockSpec((8, d), lambda l, j: (0, 0)),
            pl.BlockSpec((None, d, tn), lambda l, j: (l, 0, j)),
            pl.BlockSpec((None, 1, tn), lambda l, j: (l, 0, j)),
        ],
        out_specs=pl.BlockSpec((None, 8, tn), lambda l, j: (l, 0, j)),
        out_shape=jax.ShapeDtypeStruct((depth, 8, n6), F32),
        compiler_params=_cparams("parallel", "parallel"),
        name="modulation",
    )(cvec, w_mod, b_mod)


def _rope_tables(n_tok, dim):
    t = jnp.arange(n_tok)
    row = (t // GRID_W).astype(F32)[:, None]
    col = (t % GRID_W).astype(F32)[:, None]
    nf = dim // 4
    lane = jnp.arange(HEAD_DIM)
    quarter = (lane % dim) // nf
    inv = ROPE_BASE ** (-jnp.arange(nf, dtype=F32) / nf)
    ang = jnp.where(quarter[None, :] < 2, row, col) * inv[lane % nf][None, :]
    cos, sin = jnp.cos(ang), jnp.sin(ang)
    even = (quarter % 2 == 0)[None, :]
    return cos, jnp.where(even, -sin, 0.0), jnp.where(even, 0.0, sin)


def _identity_rope_tables(n_tok):
    z = jnp.zeros((n_tok, HEAD_DIM), F32)
    return jnp.ones((n_tok, HEAD_DIM), F32), z, z


def _proj_modes():
    att = HEAD_DIM ** -0.5
    plain = [(None, 1.0)] * GROUP_BLOCKS
    return [
        ([(None, att)] * 4, None), (plain, None), (plain, None),
        ([("d", LOG2E * DIFF_DIM ** -0.5)] * 4, 0), ([("d", 1.0)] * 4, None), (plain, 1),
        ([("h", 1.0)] * 4, None), ([("h", att)] * 4, None), (plain, None), (plain, None),
        ([("h", att)] * 4, None), ([("h", 1.0)] * 2 + [(None, 1.0)] * 2, None),
    ]


def _proj_kernel(x_ref, g_ref, sc_ref, sh_ref, w_ref, ch_ref, sah_ref, sbh_ref, cd_ref, sad_ref, sbd_ref,
                 o_ref, t_ref):
    x = x_ref[...]
    y = x * lax.rsqrt(jnp.mean(x * x, axis=-1, keepdims=True) + NORM_EPS) * g_ref[...]
    h = (y * (1.0 + sc_ref[...]) + sh_ref[...]).astype(BF16)

    def rope(a, kind):
        if kind is None:
            return a
        c, sa, sb, sh = ((ch_ref, sah_ref, sbh_ref, HEAD_DIM // 4) if kind == "h"
                         else (cd_ref, sad_ref, sbd_ref, DIFF_DIM // 4))
        return (a * c[...] + pltpu.roll(a, HEAD_DIM - sh, 1) * sa[...] + pltpu.roll(a, sh, 1) * sb[...])

    for j, (blocks, t_slot) in enumerate(_proj_modes()):
        cols = slice(j * PROJ_TN, (j + 1) * PROJ_TN)
        acc = jnp.dot(h, w_ref[:, cols], preferred_element_type=F32)
        outs = []
        for hb, (kind, scale) in enumerate(blocks):
            a = rope(acc[:, hb * HEAD_DIM:(hb + 1) * HEAD_DIM], kind)
            outs.append(a if scale == 1.0 else a * scale)
        full = jnp.concatenate(outs, axis=1)
        o_ref[:, cols] = full.astype(o_ref.dtype)
        if t_slot is not None:
            t_ref[t_slot] = full.T.astype(t_ref.dtype)


def norm_project(x, norm_g, scale, shift, w_in, rope_h, rope_d, *, tm=256):
    b, n, d = x.shape
    width = w_in.shape[1]
    assert width == len(_proj_modes()) * PROJ_TN
    tm = min(tm, n)
    tok = lambda bi, i: (i, 0)
    return pl.pallas_call(
        _proj_kernel,
        grid=(b, n // tm),
        in_specs=[
            pl.BlockSpec((None, tm, d), lambda bi, i: (bi, i, 0)),
            pl.BlockSpec((1, d), lambda bi, i: (0, 0)),
            pl.BlockSpec((None, 1, d), lambda bi, i: (bi, 0, 0)),
            pl.BlockSpec((None, 1, d), lambda bi, i: (bi, 0, 0)),
            pl.BlockSpec((d, width), lambda bi, i: (0, 0), pipeline_mode=pl.Buffered(1)),
        ] + [pl.BlockSpec((tm, HEAD_DIM), tok)] * 6,
        out_specs=[
            pl.BlockSpec((None, tm, width), lambda bi, i: (bi, i, 0)),
            pl.BlockSpec((None, 2, PROJ_TN, tm), lambda bi, i: (bi, 0, 0, i)),
        ],
        out_shape=[jax.ShapeDtypeStruct((b, n, width), BF16),
                   jax.ShapeDtypeStruct((b, 2, PROJ_TN, n), BF16)],
        compiler_params=_cparams("parallel", "parallel"),
        name="norm_project",
    )(x, norm_g, scale, shift, w_in, *rope_h, *rope_d)


def _softmax_pv(s_loc, s_ctx, vw, vc, sink):
    m = jnp.maximum(jnp.max(s_loc, axis=-1, keepdims=True), jnp.max(s_ctx, axis=-1, keepdims=True))
    if sink is not None:
        m = jnp.maximum(m, sink)
    p_loc = jnp.exp(s_loc - m)
    p_ctx = jnp.exp(s_ctx - m)
    l = jnp.sum(p_loc, axis=-1, keepdims=True) + jnp.sum(p_ctx, axis=-1, keepdims=True)
    if sink is not None:
        l = l + jnp.exp(sink - m)
    o = (jnp.dot(p_loc.astype(BF16), vw, preferred_element_type=F32)
         + jnp.dot(p_ctx.astype(BF16), vc, preferred_element_type=F32))
    return o * (1.0 / l)


def _nt_dot(a, b):
    return lax.dot_general(a, b, (((1,), (1,)), ((), ())), preferred_element_type=F32)


def _na_kernel(q_ref, k_ref, v_ref, kc_ref, vc_ref, bias_ref, o_ref, *, tq, wk):
    i = pl.program_id(2)
    n = k_ref.shape[0]
    rows_q = tq // GRID_W
    kstart = jnp.clip(i * rows_q - NA_WIN_R // 2, 0, (n - wk) // GRID_W) * GRID_W
    kstart = pl.multiple_of(kstart, GRID_W)
    q = q_ref[...]
    kw = k_ref[pl.ds(kstart, wk), :]
    vw = v_ref[pl.ds(kstart, wk), :]
    s_loc = _nt_dot(q, kw) + bias_ref[...]
    s_ctx = _nt_dot(q, kc_ref[...])
    o_ref[...] = _softmax_pv(s_loc, s_ctx, vw, vc_ref[...], None).astype(o_ref.dtype)


def _na_bias(rpb, rows, rows_q, rows_k):
    n_blk = rows // rows_q
    kr = NA_WIN_R
    cols = jnp.arange(GRID_W)
    col_start = jnp.clip(cols - NA_WIN_C // 2, 0, GRID_W - NA_WIN_C)
    col_ok = (cols[None, :] >= col_start[:, None]) & (cols[None, :] < col_start[:, None] + NA_WIN_C)
    col_off = jnp.clip(cols[None, :] - cols[:, None] + NA_WIN_C - 1, 0, 2 * NA_WIN_C - 2)
    out = []
    for blk in (0, 1, n_blk - 1):
        r = blk * rows_q + jnp.arange(rows_q)
        kstart = min(max(blk * rows_q - kr // 2, 0), rows - rows_k)
        krow = kstart + jnp.arange(rows_k)
        start = jnp.clip(r - kr // 2, 0, rows - kr)
        row_ok = (krow[None, :] >= start[:, None]) & (krow[None, :] < start[:, None] + kr)
        row_off = jnp.clip(krow[None, :] - r[:, None] + NA_WIN_R - 1, 0, 2 * NA_WIN_R - 2)
        bias = rpb.astype(F32)[:, row_off][:, :, :, col_off]
        ok = row_ok[:, :, None, None] & col_ok[None, None]
        bias = jnp.where(ok[None], bias, NEG_INF).transpose(0, 1, 3, 2, 4)
        out.append(bias.reshape(rpb.shape[0], rows_q * GRID_W, rows_k * GRID_W))
    return jnp.stack(out, axis=1)


def neighbourhood_attention(p, pc, rpb, *, rows_q=4):
    b, n, _ = p.shape
    lc = pc.shape[1]
    h = rpb.shape[0]
    rows = n // GRID_W
    rows_k = rows_q + NA_WIN_R - 1
    assert rows % rows_q == 0 and rows >= rows_k and rows_q >= NA_WIN_R // 2
    tq, wk = rows_q * GRID_W, rows_k * GRID_W
    n_blk = rows // rows_q
    bias = _na_bias(rpb, rows, rows_q, rows_k)
    kern = functools.partial(_na_kernel, tq=tq, wk=wk)
    cls = lambda i: jnp.where(i == 0, 0, jnp.where(i == n_blk - 1, 2, 1))
    return pl.pallas_call(
        kern,
        grid=(b, h, n_blk),
        in_specs=[
            pl.BlockSpec((None, tq, HEAD_DIM), lambda bi, hi, i: (bi, i, COL_A_Q + hi)),
            pl.BlockSpec((None, n, HEAD_DIM), lambda bi, hi, i: (bi, 0, COL_A_K + hi)),
            pl.BlockSpec((None, n, HEAD_DIM), lambda bi, hi, i: (bi, 0, COL_A_V + hi)),
            pl.BlockSpec((None, lc, HEAD_DIM), lambda bi, hi, i: (bi, 0, COL_A_K + hi)),
            pl.BlockSpec((None, lc, HEAD_DIM), lambda bi, hi, i: (bi, 0, COL_A_V + hi)),
            pl.BlockSpec((None, None, tq, wk), lambda bi, hi, i: (hi, cls(i), 0, 0)),
        ],
        out_specs=pl.BlockSpec((None, tq, HEAD_DIM), lambda bi, hi, i: (bi, i, hi)),
        out_shape=jax.ShapeDtypeStruct((b, n, h * HEAD_DIM), BF16),
        compiler_params=_cparams("parallel", "parallel", "arbitrary"),
        name="neighbourhood_attention",
    )(p, p, p, pc, pc, bias)


def _swa_kernel(sink_ref, q_ref, k_ref, v_ref, kc_ref, vc_ref, o_ref, *, tq, wk):
    hi = pl.program_id(1)
    i = pl.program_id(2)
    n = k_ref.shape[0]
    q0 = i * tq
    kstart = pl.multiple_of(jnp.clip(q0 - SWA_WINDOW, 0, n - wk), SWA_WINDOW)
    q = q_ref[...]
    kw = k_ref[pl.ds(kstart, wk), :]
    vw = v_ref[pl.ds(kstart, wk), :]
    qpos = q0 + lax.broadcasted_iota(jnp.int32, (tq, wk), 0)
    kpos = kstart + lax.broadcasted_iota(jnp.int32, (tq, wk), 1)
    s_loc = jnp.where(jnp.abs(kpos - qpos) <= SWA_WINDOW, _nt_dot(q, kw), NEG_INF)
    s_ctx = _nt_dot(q, kc_ref[...])
    o_ref[...] = _softmax_pv(s_loc, s_ctx, vw, vc_ref[...], sink_ref[hi]).astype(o_ref.dtype)


def window_attention(p, pc, sink, *, tq=256):
    b, n, _ = p.shape
    lc = pc.shape[1]
    hq = sink.shape[0]
    grp = 2
    tq = min(tq, n)
    wk = min(tq + 2 * SWA_WINDOW, n)
    kern = functools.partial(_swa_kernel, tq=tq, wk=wk)
    return pl.pallas_call(
        kern,
        grid=(b, hq, n // tq),
        in_specs=[
            pl.BlockSpec(memory_space=pltpu.SMEM),
            pl.BlockSpec((None, tq, HEAD_DIM), lambda bi, hi, i: (bi, i, COL_D_Q + hi)),
            pl.BlockSpec((None, n, HEAD_DIM), lambda bi, hi, i: (bi, 0, COL_D_K + hi // grp)),
            pl.BlockSpec((None, n, HEAD_DIM), lambda bi, hi, i: (bi, 0, COL_D_V + hi // grp)),
            pl.BlockSpec((None, lc, HEAD_DIM), lambda bi, hi, i: (bi, 0, COL_D_K + hi // grp)),
            pl.BlockSpec((None, lc, HEAD_DIM), lambda bi, hi, i: (bi, 0, COL_D_V + hi // grp)),
        ],
        out_specs=pl.BlockSpec((None, tq, HEAD_DIM), lambda bi, hi, i: (bi, i, hi)),
        out_shape=jax.ShapeDtypeStruct((b, n, hq * HEAD_DIM), BF16),
        compiler_params=_cparams("parallel", "parallel", "arbitrary"),
        name="window_attention",
    )(sink, p, p, p, pc, pc)


DIFF_SLOTS = 3
DIFF_AUG_ROWS = 16


def _diff_attn_kernel(lam_ref, qt_ref, k_ref, vt_ref, kc_ref, vct_ref, g_ref, o_ref,
                      qbd_ref, m_ref, acc_ref, s_scr, p_scr, a_scr, x_scr, *, tq, tk, post_scale):
    qt = qt_ref[...].astype(F32)
    row = lax.broadcasted_iota(jnp.int32, qt.shape, 0)
    zero = jnp.zeros_like(qt)
    qbd_ref[:, :tq] = jnp.where(row < DIFF_DIM, qt, zero).astype(BF16)
    qbd_ref[:, tq:] = jnp.where(row >= DIFF_DIM, qt, zero).astype(BF16)

    def aug(vt_tile):
        r = lax.broadcasted_iota(jnp.int32, (DIFF_AUG_ROWS, vt_tile.shape[1]), 0)
        return jnp.concatenate([vt_tile, jnp.where(r == 0, 1.0, 0.0).astype(BF16)], axis=0)

    s = jnp.dot(kc_ref[...], qbd_ref[...], preferred_element_type=F32)
    m0 = jnp.max(s, axis=0, keepdims=True)
    m_ref[...] = m0
    acc_ref[...] = jnp.dot(aug(vct_ref[...]), jnp.exp2(s - m0).astype(BF16), preferred_element_type=F32)

    def scores(t, slot):
        off = pl.multiple_of(t * tk, tk)
        s = jnp.dot(k_ref[pl.ds(off, tk), :], qbd_ref[...], preferred_element_type=F32)
        s_scr[slot] = s
        x_scr[slot] = jnp.max(s, axis=0, keepdims=True)

    def exponentials(slot):
        m_old = m_ref[...]
        m_new = jnp.maximum(m_old, x_scr[slot])
        a_scr[slot] = jnp.exp2(m_old - m_new)
        p_scr[slot] = jnp.exp2(s_scr[slot] - m_new).astype(BF16)
        m_ref[...] = m_new

    def accumulate(t, slot):
        off = pl.multiple_of(t * tk, tk)
        pv = jnp.dot(aug(vt_ref[:, pl.ds(off, tk)]), p_scr[slot], preferred_element_type=F32)
        acc_ref[...] = a_scr[slot] * acc_ref[...] + pv

    def stage(t, t_mod, do_scores=True, do_acc=True):
        if do_scores:
            scores(t + 1, (t_mod + 1) % DIFF_SLOTS)
        exponentials(t_mod % DIFF_SLOTS)
        if do_acc:
            accumulate(t - 1, (t_mod - 1) % DIFF_SLOTS)

    n_kt = k_ref.shape[0] // tk
    unroll = 2 * DIFF_SLOTS
    trips = (n_kt - 2) // unroll
    scores(0, 0)
    stage(0, 0, do_acc=False)

    def body(i, carry):
        for u in range(unroll):
            stage(1 + unroll * i + u, 1 + u)
        return carry

    lax.fori_loop(0, trips, body, 0)
    for t in range(1 + unroll * trips, n_kt - 1):
        stage(t, t)
    stage(n_kt - 1, n_kt - 1, do_scores=False)
    accumulate(n_kt - 1, (n_kt - 1) % DIFF_SLOTS)

    lam = lam_ref[0]
    inv = 1.0 / acc_ref[HEAD_DIM:HEAD_DIM + 1, :]
    acc = acc_ref[:HEAD_DIM, :]
    o_t = acc[:, :tq] * inv[:, :tq] - lam * (acc[:, tq:] * inv[:, tq:])
    o = o_t.T
    ms = jnp.mean(o * o, axis=-1, keepdims=True)
    y = o * lax.rsqrt(ms + NORM_EPS) * g_ref[...]
    o_ref[...] = (y * post_scale).astype(o_ref.dtype)


def diff_attention(p, t, pc, tc, lam, norm_g, post_scale, *, tq=256, tk=512):
    b, n, _ = p.shape
    lc = pc.shape[1]
    h = GROUP_BLOCKS
    tq = min(tq, n)
    tk = min(tk, n)
    assert n % tq == 0 and n % tk == 0 and n // tk >= 2
    kern = functools.partial(_diff_attn_kernel, tq=tq, tk=tk, post_scale=post_scale)
    return pl.pallas_call(
        kern,
        grid=(b, h, n // tq),
        in_specs=[
            pl.BlockSpec(memory_space=pltpu.SMEM),
            pl.BlockSpec((None, None, HEAD_DIM, tq), lambda bi, hi, qi: (bi, 0, hi, qi)),
            pl.BlockSpec((None, n, HEAD_DIM), lambda bi, hi, qi: (bi, 0, COL_B_K + hi)),
            pl.BlockSpec((None, None, HEAD_DIM, n), lambda bi, hi, qi: (bi, 1, hi, 0)),
            pl.BlockSpec((None, lc, HEAD_DIM), lambda bi, hi, qi: (bi, 0, COL_B_K + hi)),
            pl.BlockSpec((None, None, HEAD_DIM, lc), lambda bi, hi, qi: (bi, 1, hi, 0)),
            pl.BlockSpec((1, HEAD_DIM), lambda bi, hi, qi: (0, 0)),
        ],
        out_specs=pl.BlockSpec((None, tq, HEAD_DIM), lambda bi, hi, qi: (bi, qi, hi)),
        out_shape=jax.ShapeDtypeStruct((b, n, h * HEAD_DIM), BF16),
        scratch_shapes=[
            pltpu.VMEM((HEAD_DIM, 2 * tq), BF16),
            pltpu.VMEM((1, 2 * tq), F32),
            pltpu.VMEM((HEAD_DIM + DIFF_AUG_ROWS, 2 * tq), F32),
            pltpu.VMEM((DIFF_SLOTS, tk, 2 * tq), F32),
            pltpu.VMEM((DIFF_SLOTS, tk, 2 * tq), BF16),
            pltpu.VMEM((DIFF_SLOTS, 1, 2 * tq), F32),
            pltpu.VMEM((DIFF_SLOTS, 1, 2 * tq), F32),
        ],
        compiler_params=_cparams("parallel", "parallel", "arbitrary"),
        name="diff_attention",
    )(lam, t, p, t, pc, tc, norm_g)


def _retention_kernel(cdec_ref, q_ref, k_ref, v_ref, intra_ref, qdec_ref, kdec_ref, s0_ref, *rest,
                      reverse, final):
    if final:
        of_ref, gate_ref, gng_ref, gnb_ref, o_ref, sfin_ref, s_ref = rest
    else:
        o_ref, sfin_ref, s_ref = rest
    i = pl.program_id(1)
    n_heads = s_ref.shape[0]
    n_chunks = q_ref.shape[0] // RET_CHUNK

    @pl.when(i == 0)
    def _():
        s_ref[...] = s0_ref[...]

    order = range(n_chunks - 1, -1, -1) if reverse else range(n_chunks)
    for c in order:
        rs = slice(c * RET_CHUNK, (c + 1) * RET_CHUNK)
        for h in range(n_heads):
            cs = slice(h * HEAD_DIM, (h + 1) * HEAD_DIM)
            qh, kh, vh = q_ref[rs, cs], k_ref[rs, cs], v_ref[rs, cs]
            qd = (qh.astype(F32) * qdec_ref[h]).astype(BF16)
            kd_t = (kh.astype(F32) * kdec_ref[h]).T.astype(BF16)
            a = (_nt_dot(qh, kh) * intra_ref[h]).astype(BF16)
            s = s_ref[h]
            o = (jnp.dot(a, vh, preferred_element_type=F32)
                 + jnp.dot(qd, s.astype(BF16), preferred_element_type=F32))
            s_ref[h] = cdec_ref[h] * s + jnp.dot(kd_t, vh, preferred_element_type=F32)
            if final:
                o = o + of_ref[rs, cs]
                mu = jnp.mean(o, axis=-1, keepdims=True)
                var = jnp.mean(jnp.square(o - mu), axis=-1, keepdims=True)
                y = (o - mu) * lax.rsqrt(var + GN_EPS) * gng_ref[:, cs] + gnb_ref[:, cs]
                g = gate_ref[rs, cs].astype(F32)
                o = g * (1.0 / (1.0 + jnp.exp(-g))) * y
            o_ref[rs, cs] = o.astype(o_ref.dtype)

    @pl.when(i == pl.num_programs(1) - 1)
    def _():
        sfin_ref[...] = s_ref[...]


def _retention_pass(p, tables, s0, final_inputs, *, reverse, blk_chunks=4):
    b, n, _ = p.shape
    cdec, intra, qdec, kdec = tables
    h = intra.shape[0]
    width = h * HEAD_DIM
    n_chunks = n // RET_CHUNK
    blk_chunks = min(blk_chunks, n_chunks)
    assert n_chunks % blk_chunks == 0
    tb = blk_chunks * RET_CHUNK
    n_blk = n // tb
    pos = (lambda i: n_blk - 1 - i) if reverse else (lambda i: i)
    colblk = lambda c: (lambda bi, i: (bi, pos(i), c // GROUP_BLOCKS))
    tab = pl.BlockSpec((h, RET_CHUNK, HEAD_DIM), lambda bi, i: (0, 0, 0))
    state = pl.BlockSpec((None, h, HEAD_DIM, HEAD_DIM), lambda bi, i: (bi, 0, 0, 0))
    in_specs = [pl.BlockSpec(memory_space=pltpu.SMEM),
                pl.BlockSpec((None, tb, width), colblk(COL_C_Q)),
                pl.BlockSpec((None, tb, width), colblk(COL_C_K)),
                pl.BlockSpec((None, tb, width), colblk(COL_C_V)),
                tab, tab, tab, state]
    args = [cdec, p, p, p, intra, qdec, kdec, s0]
    final = final_inputs is not None
    if final:
        o_fwd, gn_g, gn_b = final_inputs
        in_specs += [pl.BlockSpec((None, tb, width), lambda bi, i: (bi, pos(i), 0)),
                     pl.BlockSpec((None, tb, width), colblk(COL_C_G)),
                     pl.BlockSpec((1, width), lambda bi, i: (0, 0)),
                     pl.BlockSpec((1, width), lambda bi, i: (0, 0))]
        args += [o_fwd, p, gn_g, gn_b]
    kern = functools.partial(_retention_kernel, reverse=reverse, final=final)
    return pl.pallas_call(
        kern,
        grid=(b, n_blk),
        in_specs=in_specs,
        out_specs=[pl.BlockSpec((None, tb, width), lambda bi, i: (bi, pos(i), 0)), state],
        out_shape=[jax.ShapeDtypeStruct((b, n, width), BF16 if final else F32),
                   jax.ShapeDtypeStruct((b, h, HEAD_DIM, HEAD_DIM), F32)],
        scratch_shapes=[pltpu.VMEM((h, HEAD_DIM, HEAD_DIM), F32)],
        compiler_params=_cparams("parallel", "arbitrary"),
        name="retention_bwd" if reverse else "retention_fwd",
    )(*args)


def _retention_tables(decay, reverse):
    lg = jax.nn.log_sigmoid(decay.astype(F32))[:, None, None]
    pos = jnp.arange(RET_CHUNK, dtype=F32)
    rel = pos[:, None] - pos[None, :]
    if reverse:
        rel = -rel
        q_pow, k_pow = RET_CHUNK - pos, pos
    else:
        q_pow, k_pow = pos + 1.0, RET_CHUNK - 1.0 - pos
    intra = jnp.where(rel >= 0, jnp.exp(jnp.maximum(rel, 0.0) * lg), 0.0)
    bc = lambda e: jnp.broadcast_to(jnp.exp(e[None, :, None] * lg), intra.shape)
    cdec = jnp.exp(RET_CHUNK * lg[:, 0, 0])
    return cdec, intra, bc(q_pow), bc(k_pow)


def bidirectional_retention(p, pc, decay_f, decay_b, gn_g, gn_b, with_ctx):
    b = p.shape[0]
    h = decay_f.shape[0]
    tf = _retention_tables(decay_f, False)
    tb = _retention_tables(decay_b, True)
    s0 = jnp.zeros((b, h, HEAD_DIM, HEAD_DIM), F32)
    gn = (gn_g.reshape(1, -1).astype(F32), gn_b.reshape(1, -1).astype(F32))
    oc_f, s_f = _retention_pass(pc, tf, s0, None, reverse=False)
    oc, s_b = _retention_pass(pc, tb, s0, (oc_f,) + gn, reverse=True)
    o_f, _ = _retention_pass(p, tf, s_f, None, reverse=False)
    o, _ = _retention_pass(p, tb, s_b, (o_f,) + gn, reverse=True)
    return o, (oc if with_ctx else None)


def _out_kernel(oa_ref, ob_ref, oc_ref, od_ref, w_ref, x_ref, g1_ref, ng_ref, sc_ref, sh_ref, wr_ref,
                xo_ref, h_ref, aff_ref):
    gw = oa_ref.shape[1]
    acc = jnp.dot(oa_ref[...], w_ref[0 * gw:1 * gw, :], preferred_element_type=F32)
    acc += jnp.dot(ob_ref[...], w_ref[1 * gw:2 * gw, :], preferred_element_type=F32)
    acc += jnp.dot(oc_ref[...], w_ref[2 * gw:3 * gw, :], preferred_element_type=F32)
    acc += jnp.dot(od_ref[...], w_ref[3 * gw:4 * gw, :], preferred_element_type=F32)
    x = x_ref[...] + g1_ref[...] * acc
    xo_ref[...] = x
    y = x * lax.rsqrt(jnp.mean(x * x, axis=-1, keepdims=True) + NORM_EPS) * ng_ref[...]
    h2 = (y * (1.0 + sc_ref[...]) + sh_ref[...]).astype(BF16)
    h_ref[...] = h2
    logits = _nt_dot(wr_ref[...], h2)
    e = jnp.exp(logits - jnp.max(logits, axis=0, keepdims=True))
    aff_ref[...] = e * (1.0 / jnp.sum(e, axis=0, keepdims=True))


def out_project(o_groups, w_out, x, g1, norm_g, scale, shift, w_router_t, *, tm=256):
    b, n, d = x.shape
    n_e = w_router_t.shape[0]
    gw = o_groups[0].shape[-1]
    tm = min(tm, n)
    tile = lambda w: pl.BlockSpec((None, tm, w), lambda bi, i: (bi, i, 0))
    vec = pl.BlockSpec((None, 1, d), lambda bi, i: (bi, 0, 0))
    return pl.pallas_call(
        _out_kernel,
        grid=(b, n // tm),
        in_specs=[tile(gw)] * 4 + [
            pl.BlockSpec((d, d), lambda bi, i: (0, 0)),
            tile(d), vec,
            pl.BlockSpec((1, d), lambda bi, i: (0, 0)),
            vec, vec,
            pl.BlockSpec((n_e, d), lambda bi, i: (0, 0)),
        ],
        out_specs=[tile(d), tile(d), pl.BlockSpec((None, n_e, tm), lambda bi, i: (bi, 0, i))],
        out_shape=[jax.ShapeDtypeStruct((b, n, d), F32),
                   jax.ShapeDtypeStruct((b, n, d), BF16),
                   jax.ShapeDtypeStruct((b, n_e, n), F32)],
        compiler_params=_cparams("parallel", "parallel"),
        name="out_project",
    )(*o_groups, w_out, x, g1, norm_g, scale, shift, w_router_t)


def _ffn_kernel(x_ref, gate_ref, wg_ref, wu_ref, wd_ref, o_ref):
    x = x_ref[...]
    a = jnp.dot(x, wg_ref[...], preferred_element_type=F32)
    u = jnp.dot(x, wu_ref[...], preferred_element_type=F32)
    hm = (a * (1.0 / (1.0 + jnp.exp(-a))) * u).astype(BF16)
    y = jnp.dot(hm, wd_ref[...], preferred_element_type=F32) * gate_ref[...]
    o_ref[...] = y.astype(o_ref.dtype)


def expert_ffn(xin, gate, w_gate, w_up, w_down, *, tc=256):
    b, n_e, cap, d = xin.shape
    ff = w_gate.shape[-1]
    tc = min(tc, cap)
    return pl.pallas_call(
        _ffn_kernel,
        grid=(n_e, b, cap // tc),
        in_specs=[
            pl.BlockSpec((None, None, tc, d), lambda e, bi, i: (bi, e, i, 0)),
            pl.BlockSpec((None, None, tc, 1), lambda e, bi, i: (bi, e, i, 0)),
            pl.BlockSpec((None, d, ff), lambda e, bi, i: (e, 0, 0)),
            pl.BlockSpec((None, d, ff), lambda e, bi, i: (e, 0, 0)),
            pl.BlockSpec((None, ff, d), lambda e, bi, i: (e, 0, 0)),
        ],
        out_specs=pl.BlockSpec((None, None, tc, d), lambda e, bi, i: (bi, e, i, 0)),
        out_shape=jax.ShapeDtypeStruct((b, n_e, cap, d), BF16),
        compiler_params=_cparams("parallel", "parallel", "parallel"),
        name="expert_ffn",
    )(xin, gate, w_gate, w_up, w_down)


def _scatter_kernel(rlo_ref, nrd_ref, x_ref, g_ref, ng_ref, idx_ref, y_hbm, o_ref, ybuf, sem, acc_ref,
                    *, sr, final):
    bi, i, nt = pl.program_id(0), pl.program_id(1), pl.num_programs(1)
    n_e, n_grp = idx_ref.shape[0], idx_ref.shape[1]
    tm = x_ref.shape[0]
    n_rounds = nrd_ref[bi * nt + i]

    def group(e, k):
        return jnp.minimum(rlo_ref[(bi * n_e + e) * nt + i] + k, n_grp - 1)

    def slot_tokens(e, k):
        in_range = rlo_ref[(bi * n_e + e) * nt + i] + k < n_grp
        return jnp.where(in_range, idx_ref[e, group(e, k)], -1)

    def copies(k, slot):
        return [pltpu.make_async_copy(
            y_hbm.at[bi, e, pl.ds(pl.multiple_of(group(e, k) * sr, sr), sr), :],
            ybuf.at[slot, pl.ds(e * sr, sr), :], sem.at[slot]) for e in range(n_e)]

    for c in copies(0, 0):
        c.start()
    acc_ref[...] = jnp.zeros(acc_ref.shape, F32)
    tok = i * tm + lax.broadcasted_iota(jnp.int32, (tm, sr), 0)

    def body(k, carry):
        slot = k % 2
        for c in copies(k, slot):
            c.wait()

        @pl.when(k + 1 < n_rounds)
        def _():
            for c in copies(k + 1, 1 - slot):
                c.start()

        onehot = jnp.concatenate(
            [jnp.where(tok == slot_tokens(e, k), 1.0, 0.0) for e in range(n_e)], axis=1)
        acc_ref[...] += jnp.dot(onehot.astype(BF16), ybuf[slot], preferred_element_type=F32)
        return carry

    lax.fori_loop(0, n_rounds, body, 0)
    x = x_ref[...] + g_ref[...] * acc_ref[...]
    if final:
        x = x * lax.rsqrt(jnp.mean(x * x, axis=-1, keepdims=True) + NORM_EPS) * ng_ref[...]
    o_ref[...] = x


def scatter_combine(x, y, idx, g2, final_g, *, tm=256):
    b, t, d = x.shape
    n_e, cap = idx.shape[1], idx.shape[2]
    tm = min(tm, t)
    sr = min(HEAD_DIM, cap)
    assert t % tm == 0 and cap % sr == 0
    nt, n_grp = t // tm, cap // sr
    edges = jnp.arange(nt + 1, dtype=idx.dtype) * tm
    pos = jnp.sum(idx[..., None] < edges, axis=2, dtype=jnp.int32)
    lo, hi = pos[..., :-1], pos[..., 1:]
    rlo = jnp.minimum(lo // sr, n_grp - 1)
    rhi = jnp.maximum(hi - 1, lo) // sr
    n_rounds = jnp.max(jnp.minimum(rhi, n_grp - 1) - rlo + 1, axis=1)
    final = final_g is not None
    ng = final_g if final else jnp.ones((1, d), F32)
    tile = pl.BlockSpec((None, tm, d), lambda bi, i, *_: (bi, i, 0))
    return pl.pallas_call(
        functools.partial(_scatter_kernel, sr=sr, final=final),
        grid_spec=pltpu.PrefetchScalarGridSpec(
            num_scalar_prefetch=2,
            grid=(b, nt),
            in_specs=[tile,
                      pl.BlockSpec((None, 1, d), lambda bi, i, *_: (bi, 0, 0)),
                      pl.BlockSpec((1, d), lambda bi, i, *_: (0, 0)),
                      pl.BlockSpec((None, n_e, n_grp, 1, sr), lambda bi, i, *_: (bi, 0, 0, 0, 0)),
                      pl.BlockSpec(memory_space=pl.ANY)],
            out_specs=tile,
            scratch_shapes=[pltpu.VMEM((2, n_e * sr, d), BF16),
                            pltpu.SemaphoreType.DMA((2,)),
                            pltpu.VMEM((tm, d), F32)]),
        out_shape=jax.ShapeDtypeStruct((b, t, d), F32),
        compiler_params=_cparams("parallel", "arbitrary"),
        name="scatter_combine",
    )(rlo.reshape(-1), n_rounds.reshape(-1), x, g2, ng, idx.reshape(b, n_e, n_grp, 1, sr), y)


def expert_choice_ffn(x, h2, aff_t, g2, final_g, w_gate, w_up, w_down):
    b, t, d = h2.shape
    cap = EC_CAPACITY * t // N_EXPERTS
    gate, idx = lax.top_k(aff_t, cap)
    idx, gate = lax.sort_key_val(idx, gate, dimension=-1)
    xin = jax.vmap(lambda hb, ib: hb[ib])(h2, idx)
    y = expert_ffn(xin, gate[..., None], w_gate, w_up, w_down)
    return scatter_combine(x, y, idx, g2, final_g)


def _ctx_attn_kernel(scal_ref, q_ref, k_ref, v_ref, g_ref, o_ref, *, kind, post_scale):
    q, k, v = q_ref[...], k_ref[...], v_ref[...]
    if kind == "diff":
        qf = q.astype(F32)
        lane = lax.broadcasted_iota(jnp.int32, qf.shape, 1)

        def probs(keep):
            s = _nt_dot(jnp.where(keep, qf, 0.0).astype(BF16), k)
            e = jnp.exp2(s - jnp.max(s, axis=-1, keepdims=True))
            return e * (1.0 / jnp.sum(e, axis=-1, keepdims=True))

        w = probs(lane < DIFF_DIM) - scal_ref[0] * probs(lane >= DIFF_DIM)
        o = jnp.dot(w.astype(BF16), v, preferred_element_type=F32)
        o = o * lax.rsqrt(jnp.mean(o * o, axis=-1, keepdims=True) + NORM_EPS) * g_ref[...] * post_scale
    else:
        s = _nt_dot(q, k)
        m = jnp.max(s, axis=-1, keepdims=True)
        if kind == "sink":
            sink = scal_ref[pl.program_id(1)]
            m = jnp.maximum(m, sink)
        e = jnp.exp(s - m)
        l = jnp.sum(e, axis=-1, keepdims=True)
        if kind == "sink":
            l = l + jnp.exp(sink - m)
        o = jnp.dot((e * (1.0 / l)).astype(BF16), v, preferred_element_type=F32)
    o_ref[...] = o.astype(o_ref.dtype)


def _context_attention(pc, scal, norm_g, kind, post_scale, q_col, k_col, v_col, grp):
    b, lc, _ = pc.shape
    h = GROUP_BLOCKS
    col = lambda c0, div: (lambda bi, hi: (bi, 0, c0 + hi // div))
    return pl.pallas_call(
        functools.partial(_ctx_attn_kernel, kind=kind, post_scale=post_scale),
        grid=(b, h),
        in_specs=[
            pl.BlockSpec(memory_space=pltpu.SMEM),
            pl.BlockSpec((None, lc, HEAD_DIM), col(q_col, 1)),
            pl.BlockSpec((None, lc, HEAD_DIM), col(k_col, grp)),
            pl.BlockSpec((None, lc, HEAD_DIM), col(v_col, grp)),
            pl.BlockSpec((1, HEAD_DIM), lambda bi, hi: (0, 0)),
        ],
        out_specs=pl.BlockSpec((None, lc, HEAD_DIM), lambda bi, hi: (bi, 0, hi)),
        out_shape=jax.ShapeDtypeStruct((b, lc, h * HEAD_DIM), BF16),
        compiler_params=_cparams("parallel", "parallel"),
        name="context_attention_" + kind,
    )(scal, pc, pc, pc, norm_g)


def _context_mixers(pc, oc_ret, lam, lam_init, diff_norm_g, swa_sink):
    h = GROUP_BLOCKS
    zeros = jnp.zeros((h,), F32)
    ones = jnp.ones((1, HEAD_DIM), F32)
    o_a = _context_attention(pc, zeros, ones, "plain", 1.0, COL_A_Q, COL_A_K, COL_A_V, 1)
    o_b = _context_attention(pc, lam.reshape(1), diff_norm_g.reshape(1, HEAD_DIM).astype(F32), "diff",
                             1.0 - lam_init, COL_B_K - h, COL_B_K, COL_B_K + h, 1)
    o_d = _context_attention(pc, swa_sink.astype(F32), ones, "sink", 1.0, COL_D_Q, COL_D_K, COL_D_V, 2)
    return [o_a, o_b, oc_ret, o_d]


def _layer(x, xc, mod, layer_idx, with_ctx, final_g, rope_h, rope_d, rope_id,
           norm1_g, w_in, na_rpb, diff_lambda, diff_norm_g, ret_decay_fwd, ret_decay_bwd, ret_gn_g, ret_gn_b,
           swa_sink, w_out, norm2_g, w_router, w_gate, w_up, w_down):
    b, n, d = x.shape
    row = lambda v: v.reshape(1, -1).astype(F32)
    part = lambda r0, r1, k: mod[r0:r1, None, k * d:(k + 1) * d]
    sh1, sc1, g1, sh2, sc2, g2 = [part(0, b, k) for k in range(6)]
    ctx_rows = lambda k: jnp.broadcast_to(part(b, b + 1, k), (b, 1, d))
    sh1c, sc1c, g1c, sh2c, sc2c, g2c = [ctx_rows(k) for k in range(6)]

    w_in_b = w_in.astype(BF16)
    w_out_b = w_out.astype(BF16)
    w_router_t = w_router.T.astype(BF16)
    wg_b, wu_b, wd_b = w_gate.astype(BF16), w_up.astype(BF16), w_down.astype(BF16)

    p, t = norm_project(x, row(norm1_g), sc1, sh1, w_in_b, rope_h, rope_d)
    pc, tc = norm_project(xc, row(norm1_g), sc1c, sh1c, w_in_b, rope_id, rope_id)

    lam_init = 0.8 - 0.6 * math.exp(-0.3 * layer_idx)
    lq1, lk1, lq2, lk2 = [diff_lambda[k].astype(F32) for k in range(4)]
    lam = jnp.exp(jnp.sum(lq1 * lk1)) - jnp.exp(jnp.sum(lq2 * lk2)) + lam_init

    o_a = neighbourhood_attention(p, pc, na_rpb)
    o_b = diff_attention(p, t, pc, tc, lam.reshape(1), row(diff_norm_g), 1.0 - lam_init)
    o_c, oc_ret = bidirectional_retention(p, pc, ret_decay_fwd, ret_decay_bwd, ret_gn_g, ret_gn_b, with_ctx)
    o_d = window_attention(p, pc, swa_sink.astype(F32))

    x, h2, aff_t = out_project([o_a, o_b, o_c, o_d], w_out_b, x, g1, row(norm2_g), sc2, sh2, w_router_t)
    x = expert_choice_ffn(x, h2, aff_t, g2, final_g, wg_b, wu_b, wd_b)

    if with_ctx:
        oc = _context_mixers(pc, oc_ret, lam, lam_init, diff_norm_g, swa_sink)
        xc, h2c, aff_c = out_project(oc, w_out_b, xc, g1c, row(norm2_g), sc2c, sh2c, w_router_t)
        xc = expert_choice_ffn(xc, h2c, aff_c, g2c, None, wg_b, wu_b, wd_b)
    return x, xc


def kernel(x, c, ctx, c_ctx, w_mod, b_mod, norm1_g, w_in, na_rpb, diff_lambda, diff_norm_g,
           ret_decay_fwd, ret_decay_bwd, ret_gn_g, ret_gn_b, swa_sink, w_out, norm2_g,
           w_router, w_gate, w_up, w_down, final_norm_g):
    b, n, d = x.shape
    depth = w_in.shape[0]
    lc = ctx.shape[1]
    cvec = jnp.zeros((8, d), F32).at[:b].set(c).at[b].set(c_ctx)
    mod = modulation(cvec, w_mod, b_mod.reshape(depth, 1, -1))
    rope_h = _rope_tables(n, HEAD_DIM)
    rope_d = _rope_tables(n, DIFF_DIM)
    rope_id = _identity_rope_tables(lc)
    xc = ctx
    for li in range(depth):
        last = li == depth - 1
        x, xc = _layer(x, xc, mod[li], li, not last, final_norm_g.reshape(1, -1) if last else None,
                       rope_h, rope_d, rope_id,
                       norm1_g[li], w_in[li], na_rpb[li], diff_lambda[li], diff_norm_g[li],
                       ret_decay_fwd[li], ret_decay_bwd[li], ret_gn_g[li], ret_gn_b[li], swa_sink[li],
                       w_out[li], norm2_g[li], w_router[li], w_gate[li], w_up[li], w_down[li])
    return x
```

```python
import functools
import math

import jax
import jax.numpy as jnp
from jax import lax
from jax.experimental import pallas as pl
from jax.experimental.pallas import tpu as pltpu

GRID_W = 64
HEAD_DIM = 128
DIFF_DIM = HEAD_DIM // 2
NA_WIN_R = 8
NA_WIN_C = 16
RET_CHUNK = 128
SWA_WINDOW = 128
N_EXPERTS = 16
EC_CAPACITY = 2
ROPE_BASE = 10000.0
NORM_EPS = 1e-6
GN_EPS = 1e-5
NEG_INF = -1e30
LOG2E = math.log2(math.e)

BF16 = jnp.bfloat16
F32 = jnp.float32

VMEM_LIMIT_BYTES = 48 * 1024 * 1024

GROUP_BLOCKS = 4
COL_A_Q, COL_A_K, COL_A_V = 0, 4, 8
COL_B_K = 16
COL_C_Q, COL_C_K, COL_C_V, COL_C_G = 24, 28, 32, 36
COL_D_Q, COL_D_K, COL_D_V = 40, 44, 46
PROJ_TN = GROUP_BLOCKS * HEAD_DIM


def _cparams(*sem):
    return pltpu.CompilerParams(dimension_semantics=sem, vmem_limit_bytes=VMEM_LIMIT_BYTES)


def _mod_kernel(c_ref, w_ref, b_ref, o_ref):
    c = c_ref[...]
    s = (c * (1.0 / (1.0 + jnp.exp(-c)))).astype(BF16)
    o_ref[...] = jnp.dot(s, w_ref[...].astype(BF16), preferred_element_type=F32) + b_ref[...]


def modulation(cvec, w_mod, b_mod, *, tn=1024):
    depth, d, n6 = w_mod.shape
    return pl.pallas_call(
        _mod_kernel,
        grid=(depth, n6 // tn),
        in_specs=[
            pl.BlockSpec((8, d), lambda l, j: (0, 0)),
            pl.BlockSpec((None, d, tn), lambda l, j: (l, 0, j)),
            pl.BlockSpec((None, 1, tn), lambda l, j: (l, 0, j)),
        ],
        out_specs=pl.BlockSpec((None, 8, tn), lambda l, j: (l, 0, j)),
        out_shape=jax.ShapeDtypeStruct((depth, 8, n6), F32),
        compiler_params=_cparams("parallel", "parallel"),
        name="modulation",
    )(cvec, w_mod, b_mod)


def _rope_tables(n_tok, dim):
    t = jnp.arange(n_tok)
    row = (t // GRID_W).astype(F32)[:, None]
    col = (t % GRID_W).astype(F32)[:, None]
    nf = dim // 4
    lane = jnp.arange(HEAD_DIM)
    quarter = (lane % dim) // nf
    inv = ROPE_BASE ** (-jnp.arange(nf, dtype=F32) / nf)
    ang = jnp.where(quarter[None, :] < 2, row, col) * inv[lane % nf][None, :]
    cos, sin = jnp.cos(ang), jnp.sin(ang)
    even = (quarter % 2 == 0)[None, :]
    return cos, jnp.where(even, -sin, 0.0), jnp.where(even, 0.0, sin)


def _identity_rope_tables(n_tok):
    z = jnp.zeros((n_tok, HEAD_DIM), F32)
    return jnp.ones((n_tok, HEAD_DIM), F32), z, z


def _proj_modes():
    att = HEAD_DIM ** -0.5
    plain = [(None, 1.0)] * GROUP_BLOCKS
    return [
        ([(None, att)] * 4, None), (plain, None), (plain, None),
        ([("d", LOG2E * DIFF_DIM ** -0.5)] * 4, 0), ([("d", 1.0)] * 4, None), (plain, 1),
        ([("h", 1.0)] * 4, None), ([("h", att)] * 4, None), (plain, None), (plain, None),
        ([("h", att)] * 4, None), ([("h", 1.0)] * 2 + [(None, 1.0)] * 2, None),
    ]


def _proj_kernel(x_ref, g_ref, sc_ref, sh_ref, w_ref, ch_ref, sah_ref, sbh_ref, cd_ref, sad_ref, sbd_ref,
                 o_ref, t_ref):
    x = x_ref[...]
    y = x * lax.rsqrt(jnp.mean(x * x, axis=-1, keepdims=True) + NORM_EPS) * g_ref[...]
    h = (y * (1.0 + sc_ref[...]) + sh_ref[...]).astype(BF16)

    def rope(a, kind):
        if kind is None:
            return a
        c, sa, sb, sh = ((ch_ref, sah_ref, sbh_ref, HEAD_DIM // 4) if kind == "h"
                         else (cd_ref, sad_ref, sbd_ref, DIFF_DIM // 4))
        return (a * c[...] + pltpu.roll(a, HEAD_DIM - sh, 1) * sa[...] + pltpu.roll(a, sh, 1) * sb[...])

    for j, (blocks, t_slot) in enumerate(_proj_modes()):
        cols = slice(j * PROJ_TN, (j + 1) * PROJ_TN)
        acc = jnp.dot(h, w_ref[:, cols], preferred_element_type=F32)
        outs = []
        for hb, (kind, scale) in enumerate(blocks):
            a = rope(acc[:, hb * HEAD_DIM:(hb + 1) * HEAD_DIM], kind)
            outs.append(a if scale == 1.0 else a * scale)
        full = jnp.concatenate(outs, axis=1)
        o_ref[:, cols] = full.astype(o_ref.dtype)
        if t_slot is not None:
            t_ref[t_slot] = full.T.astype(t_ref.dtype)


def norm_project(x, norm_g, scale, shift, w_in, rope_h, rope_d, *, tm=256):
    b, n, d = x.shape
    width = w_in.shape[1]
    assert width == len(_proj_modes()) * PROJ_TN
    tm = min(tm, n)
    tok = lambda bi, i: (i, 0)
    return pl.pallas_call(
        _proj_kernel,
        grid=(b, n // tm),
        in_specs=[
            pl.BlockSpec((None, tm, d), lambda bi, i: (bi, i, 0)),
            pl.BlockSpec((1, d), lambda bi, i: (0, 0)),
            pl.BlockSpec((None, 1, d), lambda bi, i: (bi, 0, 0)),
            pl.BlockSpec((None, 1, d), lambda bi, i: (bi, 0, 0)),
            pl.BlockSpec((d, width), lambda bi, i: (0, 0), pipeline_mode=pl.Buffered(1)),
        ] + [pl.BlockSpec((tm, HEAD_DIM), tok)] * 6,
        out_specs=[
            pl.BlockSpec((None, tm, width), lambda bi, i: (bi, i, 0)),
            pl.BlockSpec((None, 2, PROJ_TN, tm), lambda bi, i: (bi, 0, 0, i)),
        ],
        out_shape=[jax.ShapeDtypeStruct((b, n, width), BF16),
                   jax.ShapeDtypeStruct((b, 2, PROJ_TN, n), BF16)],
        compiler_params=_cparams("parallel", "parallel"),
        name="norm_project",
    )(x, norm_g, scale, shift, w_in, *rope_h, *rope_d)


def _softmax_pv(s_loc, s_ctx, vw, vc, sink):
    m = jnp.maximum(jnp.max(s_loc, axis=-1, keepdims=True), jnp.max(s_ctx, axis=-1, keepdims=True))
    if sink is not None:
        m = jnp.maximum(m, sink)
    p_loc = jnp.exp(s_loc - m)
    p_ctx = jnp.exp(s_ctx - m)
    l = jnp.sum(p_loc, axis=-1, keepdims=True) + jnp.sum(p_ctx, axis=-1, keepdims=True)
    if sink is not None:
        l = l + jnp.exp(sink - m)
    o = (jnp.dot(p_loc.astype(BF16), vw, preferred_element_type=F32)
         + jnp.dot(p_ctx.astype(BF16), vc, preferred_element_type=F32))
    return o * (1.0 / l)


def _nt_dot(a, b):
    return lax.dot_general(a, b, (((1,), (1,)), ((), ())), preferred_element_type=F32)


def _na_kernel(q_ref, k_ref, v_ref, kc_ref, vc_ref, bias_ref, o_ref, *, tq, wk):
    i = pl.program_id(2)
    n = k_ref.shape[0]
    rows_q = tq // GRID_W
    kstart = jnp.clip(i * rows_q - NA_WIN_R // 2, 0, (n - wk) // GRID_W) * GRID_W
    kstart = pl.multiple_of(kstart, GRID_W)
    q = q_ref[...]
    kw = k_ref[pl.ds(kstart, wk), :]
    vw = v_ref[pl.ds(kstart, wk), :]
    s_loc = _nt_dot(q, kw) + bias_ref[...]
    s_ctx = _nt_dot(q, kc_ref[...])
    o_ref[...] = _softmax_pv(s_loc, s_ctx, vw, vc_ref[...], None).astype(o_ref.dtype)


def _na_bias(rpb, rows, rows_q, rows_k):
    n_blk = rows // rows_q
    kr = NA_WIN_R
    cols = jnp.arange(GRID_W)
    col_start = jnp.clip(cols - NA_WIN_C // 2, 0, GRID_W - NA_WIN_C)
    col_ok = (cols[None, :] >= col_start[:, None]) & (cols[None, :] < col_start[:, None] + NA_WIN_C)
    col_off = jnp.clip(cols[None, :] - cols[:, None] + NA_WIN_C - 1, 0, 2 * NA_WIN_C - 2)
    out = []
    for blk in (0, 1, n_blk - 1):
        r = blk * rows_q + jnp.arange(rows_q)
        kstart = min(max(blk * rows_q - kr // 2, 0), rows - rows_k)
        krow = kstart + jnp.arange(rows_k)
        start = jnp.clip(r - kr // 2, 0, rows - kr)
        row_ok = (krow[None, :] >= start[:, None]) & (krow[None, :] < start[:, None] + kr)
        row_off = jnp.clip(krow[None, :] - r[:, None] + NA_WIN_R - 1, 0, 2 * NA_WIN_R - 2)
        bias = jnp.einsum('abr,hrc,qkc->habqk', jax.nn.one_hot(row_off, 2 * NA_WIN_R - 1, dtype=F32),
                          rpb.astype(F32), jax.nn.one_hot(col_off, 2 * NA_WIN_C - 1, dtype=F32),
                          precision=lax.Precision.HIGHEST)
        ok = row_ok[:, :, None, None] & col_ok[None, None]
        bias = jnp.where(ok[None], bias, NEG_INF).transpose(0, 1, 3, 2, 4)
        out.append(bias.reshape(rpb.shape[0], rows_q * GRID_W, rows_k * GRID_W))
    return jnp.stack(out, axis=1)


def neighbourhood_attention(p, pc, rpb, *, rows_q=4):
    b, n, _ = p.shape
    lc = pc.shape[1]
    h = rpb.shape[0]
    rows = n // GRID_W
    rows_k = rows_q + NA_WIN_R - 1
    assert rows % rows_q == 0 and rows >= rows_k and rows_q >= NA_WIN_R // 2
    tq, wk = rows_q * GRID_W, rows_k * GRID_W
    n_blk = rows // rows_q
    bias = _na_bias(rpb, rows, rows_q, rows_k)
    kern = functools.partial(_na_kernel, tq=tq, wk=wk)
    cls = lambda i: jnp.where(i == 0, 0, jnp.where(i == n_blk - 1, 2, 1))
    return pl.pallas_call(
        kern,
        grid=(b, h, n_blk),
        in_specs=[
            pl.BlockSpec((None, tq, HEAD_DIM), lambda bi, hi, i: (bi, i, COL_A_Q + hi)),
            pl.BlockSpec((None, n, HEAD_DIM), lambda bi, hi, i: (bi, 0, COL_A_K + hi)),
            pl.BlockSpec((None, n, HEAD_DIM), lambda bi, hi, i: (bi, 0, COL_A_V + hi)),
            pl.BlockSpec((None, lc, HEAD_DIM), lambda bi, hi, i: (bi, 0, COL_A_K + hi)),
            pl.BlockSpec((None, lc, HEAD_DIM), lambda bi, hi, i: (bi, 0, COL_A_V + hi)),
            pl.BlockSpec((None, None, tq, wk), lambda bi, hi, i: (hi, cls(i), 0, 0)),
        ],
        out_specs=pl.BlockSpec((None, tq, HEAD_DIM), lambda bi, hi, i: (bi, i, hi)),
        out_shape=jax.ShapeDtypeStruct((b, n, h * HEAD_DIM), BF16),
        compiler_params=_cparams("parallel", "parallel", "arbitrary"),
        name="neighbourhood_attention",
    )(p, p, p, pc, pc, bias)


def _swa_kernel(sink_ref, q_ref, k_ref, v_ref, kc_ref, vc_ref, o_ref, *, tq, wk):
    hi = pl.program_id(1)
    i = pl.program_id(2)
    n = k_ref.shape[0]
    q0 = i * tq
    kstart = pl.multiple_of(jnp.clip(q0 - SWA_WINDOW, 0, n - wk), SWA_WINDOW)
    q = q_ref[...]
    kw = k_ref[pl.ds(kstart, wk), :]
    vw = v_ref[pl.ds(kstart, wk), :]
    qpos = q0 + lax.broadcasted_iota(jnp.int32, (tq, wk), 0)
    kpos = kstart + lax.broadcasted_iota(jnp.int32, (tq, wk), 1)
    s_loc = jnp.where(jnp.abs(kpos - qpos) <= SWA_WINDOW, _nt_dot(q, kw), NEG_INF)
    s_ctx = _nt_dot(q, kc_ref[...])
    o_ref[...] = _softmax_pv(s_loc, s_ctx, vw, vc_ref[...], sink_ref[hi]).astype(o_ref.dtype)


def window_attention(p, pc, sink, *, tq=256):
    b, n, _ = p.shape
    lc = pc.shape[1]
    hq = sink.shape[0]
    grp = 2
    tq = min(tq, n)
    wk = min(tq + 2 * SWA_WINDOW, n)
    kern = functools.partial(_swa_kernel, tq=tq, wk=wk)
    return pl.pallas_call(
        kern,
        grid=(b, hq, n // tq),
        in_specs=[
            pl.BlockSpec(memory_space=pltpu.SMEM),
            pl.BlockSpec((None, tq, HEAD_DIM), lambda bi, hi, i: (bi, i, COL_D_Q + hi)),
            pl.BlockSpec((None, n, HEAD_DIM), lambda bi, hi, i: (bi, 0, COL_D_K + hi // grp)),
            pl.BlockSpec((None, n, HEAD_DIM), lambda bi, hi, i: (bi, 0, COL_D_V + hi // grp)),
            pl.BlockSpec((None, lc, HEAD_DIM), lambda bi, hi, i: (bi, 0, COL_D_K + hi // grp)),
            pl.BlockSpec((None, lc, HEAD_DIM), lambda bi, hi, i: (bi, 0, COL_D_V + hi // grp)),
        ],
        out_specs=pl.BlockSpec((None, tq, HEAD_DIM), lambda bi, hi, i: (bi, i, hi)),
        out_shape=jax.ShapeDtypeStruct((b, n, hq * HEAD_DIM), BF16),
        compiler_params=_cparams("parallel", "parallel", "arbitrary"),
        name="window_attention",
    )(sink, p, p, p, pc, pc)


DIFF_SLOTS = 2
DIFF_UNROLL = 6
DIFF_AUG_ROWS = 16


def _diff_attn_kernel(lam_ref, qt_ref, k_ref, vt_ref, kc_ref, vct_ref, g_ref, o_ref,
                      qbd_ref, m_ref, acc_ref, s_scr, x_scr, *, tq, tk, post_scale):
    qt = qt_ref[...].astype(F32)
    row = lax.broadcasted_iota(jnp.int32, qt.shape, 0)
    zero = jnp.zeros_like(qt)
    qbd_ref[:, :tq] = jnp.where(row < DIFF_DIM, qt, zero).astype(BF16)
    qbd_ref[:, tq:] = jnp.where(row >= DIFF_DIM, qt, zero).astype(BF16)

    def aug(vt_tile):
        r = lax.broadcasted_iota(jnp.int32, (DIFF_AUG_ROWS, vt_tile.shape[1]), 0)
        return jnp.concatenate([vt_tile, jnp.where(r == 0, 1.0, 0.0).astype(BF16)], axis=0)

    s = jnp.dot(kc_ref[...], qbd_ref[...], preferred_element_type=F32)
    m0 = jnp.max(s, axis=0, keepdims=True)
    m_ref[...] = m0
    acc_ref[...] = jnp.dot(aug(vct_ref[...]), jnp.exp2(s - m0).astype(BF16), preferred_element_type=F32)

    def scores(t, slot):
        off = pl.multiple_of(t * tk, tk)
        s = jnp.dot(k_ref[pl.ds(off, tk), :], qbd_ref[...], preferred_element_type=F32)
        s_scr[slot] = s
        x_scr[slot] = jnp.max(s, axis=0, keepdims=True)

    def accumulate(t, slot):
        m_old = m_ref[...]
        m_new = jnp.maximum(m_old, x_scr[slot])
        alpha = jnp.exp2(m_old - m_new)
        p = jnp.exp2(s_scr[slot] - m_new).astype(BF16)
        m_ref[...] = m_new
        off = pl.multiple_of(t * tk, tk)
        pv = jnp.dot(aug(vt_ref[:, pl.ds(off, tk)]), p, preferred_element_type=F32)
        acc_ref[...] = alpha * acc_ref[...] + pv

    def stage(t, t_mod, do_scores=True):
        if do_scores:
            scores(t + 1, (t_mod + 1) % DIFF_SLOTS)
        accumulate(t, t_mod % DIFF_SLOTS)

    n_kt = k_ref.shape[0] // tk
    trips = (n_kt - 1) // DIFF_UNROLL
    scores(0, 0)

    def body(i, carry):
        for u in range(DIFF_UNROLL):
            stage(DIFF_UNROLL * i + u, u)
        return carry

    lax.fori_loop(0, trips, body, 0)
    for t in range(DIFF_UNROLL * trips, n_kt - 1):
        stage(t, t)
    stage(n_kt - 1, n_kt - 1, do_scores=False)

    lam = lam_ref[0]
    inv = 1.0 / acc_ref[HEAD_DIM:HEAD_DIM + 1, :]
    acc = acc_ref[:HEAD_DIM, :]
    o_t = acc[:, :tq] * inv[:, :tq] - lam * (acc[:, tq:] * inv[:, tq:])
    o = o_t.T
    ms = jnp.mean(o * o, axis=-1, keepdims=True)
    y = o * lax.rsqrt(ms + NORM_EPS) * g_ref[...]
    o_ref[...] = (y * post_scale).astype(o_ref.dtype)


def diff_attention(p, t, pc, tc, lam, norm_g, post_scale, *, tq=512, tk=512):
    b, n, _ = p.shape
    lc = pc.shape[1]
    h = GROUP_BLOCKS
    tq = min(tq, n)
    tk = min(tk, n)
    assert n % tq == 0 and n % tk == 0 and DIFF_UNROLL % DIFF_SLOTS == 0
    kern = functools.partial(_diff_attn_kernel, tq=tq, tk=tk, post_scale=post_scale)
    return pl.pallas_call(
        kern,
        grid=(b, h, n // tq),
        in_specs=[
            pl.BlockSpec(memory_space=pltpu.SMEM),
            pl.BlockSpec((None, None, HEAD_DIM, tq), lambda bi, hi, qi: (bi, 0, hi, qi)),
            pl.BlockSpec((None, n, HEAD_DIM), lambda bi, hi, qi: (bi, 0, COL_B_K + hi)),
            pl.BlockSpec((None, None, HEAD_DIM, n), lambda bi, hi, qi: (bi, 1, hi, 0)),
            pl.BlockSpec((None, lc, HEAD_DIM), lambda bi, hi, qi: (bi, 0, COL_B_K + hi)),
            pl.BlockSpec((None, None, HEAD_DIM, lc), lambda bi, hi, qi: (bi, 1, hi, 0)),
            pl.BlockSpec((1, HEAD_DIM), lambda bi, hi, qi: (0, 0)),
        ],
        out_specs=pl.BlockSpec((None, tq, HEAD_DIM), lambda bi, hi, qi: (bi, qi, hi)),
        out_shape=jax.ShapeDtypeStruct((b, n, h * HEAD_DIM), BF16),
        scratch_shapes=[
            pltpu.VMEM((HEAD_DIM, 2 * tq), BF16),
            pltpu.VMEM((1, 2 * tq), F32),
            pltpu.VMEM((HEAD_DIM + DIFF_AUG_ROWS, 2 * tq), F32),
            pltpu.VMEM((DIFF_SLOTS, tk, 2 * tq), F32),
            pltpu.VMEM((DIFF_SLOTS, 1, 2 * tq), F32),
        ],
        compiler_params=_cparams("parallel", "parallel", "arbitrary"),
        name="diff_attention",
    )(lam, t, p, t, pc, tc, norm_g)


def _retention_kernel(cdec_ref, q_ref, k_ref, v_ref, intra_ref, qdec_ref, kdec_ref, s0_ref, *rest,
                      reverse, final):
    if final:
        of_ref, gate_ref, gng_ref, gnb_ref, o_ref, sfin_ref, s_ref = rest
    else:
        o_ref, sfin_ref, s_ref = rest
    i = pl.program_id(1)
    n_heads = s_ref.shape[0]
    n_chunks = q_ref.shape[0] // RET_CHUNK

    @pl.when(i == 0)
    def _():
        s_ref[...] = s0_ref[...]

    order = range(n_chunks - 1, -1, -1) if reverse else range(n_chunks)
    for c in order:
        rs = slice(c * RET_CHUNK, (c + 1) * RET_CHUNK)
        for h in range(n_heads):
            cs = slice(h * HEAD_DIM, (h + 1) * HEAD_DIM)
            qh, kh, vh = q_ref[rs, cs], k_ref[rs, cs], v_ref[rs, cs]
            qd = (qh.astype(F32) * qdec_ref[h]).astype(BF16)
            kd_t = (kh.astype(F32) * kdec_ref[h]).T.astype(BF16)
            a = (_nt_dot(qh, kh) * intra_ref[h]).astype(BF16)
            s = s_ref[h]
            o = (jnp.dot(a, vh, preferred_element_type=F32)
                 + jnp.dot(qd, s.astype(BF16), preferred_element_type=F32))
            s_ref[h] = cdec_ref[h] * s + jnp.dot(kd_t, vh, preferred_element_type=F32)
            if final:
                o = o + of_ref[rs, cs]
                mu = jnp.mean(o, axis=-1, keepdims=True)
                var = jnp.mean(jnp.square(o - mu), axis=-1, keepdims=True)
                y = (o - mu) * lax.rsqrt(var + GN_EPS) * gng_ref[:, cs] + gnb_ref[:, cs]
                g = gate_ref[rs, cs].astype(F32)
                o = g * (1.0 / (1.0 + jnp.exp(-g))) * y
            o_ref[rs, cs] = o.astype(o_ref.dtype)

    @pl.when(i == pl.num_programs(1) - 1)
    def _():
        sfin_ref[...] = s_ref[...]


def _retention_pass(p, tables, s0, final_inputs, *, reverse, blk_chunks=4):
    b, n, _ = p.shape
    cdec, intra, qdec, kdec = tables
    h = intra.shape[0]
    width = h * HEAD_DIM
    n_chunks = n // RET_CHUNK
    blk_chunks = min(blk_chunks, n_chunks)
    assert n_chunks % blk_chunks == 0
    tb = blk_chunks * RET_CHUNK
    n_blk = n // tb
    pos = (lambda i: n_blk - 1 - i) if reverse else (lambda i: i)
    colblk = lambda c: (lambda bi, i: (bi, pos(i), c // GROUP_BLOCKS))
    tab = pl.BlockSpec((h, RET_CHUNK, HEAD_DIM), lambda bi, i: (0, 0, 0))
    state = pl.BlockSpec((None, h, HEAD_DIM, HEAD_DIM), lambda bi, i: (bi, 0, 0, 0))
    in_specs = [pl.BlockSpec(memory_space=pltpu.SMEM),
                pl.BlockSpec((None, tb, width), colblk(COL_C_Q)),
                pl.BlockSpec((None, tb, width), colblk(COL_C_K)),
                pl.BlockSpec((None, tb, width), colblk(COL_C_V)),
                tab, tab, tab, state]
    args = [cdec, p, p, p, intra, qdec, kdec, s0]
    final = final_inputs is not None
    if final:
        o_fwd, gn_g, gn_b = final_inputs
        in_specs += [pl.BlockSpec((None, tb, width), lambda bi, i: (bi, pos(i), 0)),
                     pl.BlockSpec((None, tb, width), colblk(COL_C_G)),
                     pl.BlockSpec((1, width), lambda bi, i: (0, 0)),
                     pl.BlockSpec((1, width), lambda bi, i: (0, 0))]
        args += [o_fwd, p, gn_g, gn_b]
    kern = functools.partial(_retention_kernel, reverse=reverse, final=final)
    return pl.pallas_call(
        kern,
        grid=(b, n_blk),
        in_specs=in_specs,
        out_specs=[pl.BlockSpec((None, tb, width), lambda bi, i: (bi, pos(i), 0)), state],
        out_shape=[jax.ShapeDtypeStruct((b, n, width), BF16 if final else F32),
                   jax.ShapeDtypeStruct((b, h, HEAD_DIM, HEAD_DIM), F32)],
        scratch_shapes=[pltpu.VMEM((h, HEAD_DIM, HEAD_DIM), F32)],
        compiler_params=_cparams("parallel", "arbitrary"),
        name="retention_bwd" if reverse else "retention_fwd",
    )(*args)


def _retention_tables(decay, reverse):
    lg = jax.nn.log_sigmoid(decay.astype(F32))[:, None, None]
    pos = jnp.arange(RET_CHUNK, dtype=F32)
    rel = pos[:, None] - pos[None, :]
    if reverse:
        rel = -rel
        q_pow, k_pow = RET_CHUNK - pos, pos
    else:
        q_pow, k_pow = pos + 1.0, RET_CHUNK - 1.0 - pos
    intra = jnp.where(rel >= 0, jnp.exp(jnp.maximum(rel, 0.0) * lg), 0.0)
    bc = lambda e: jnp.broadcast_to(jnp.exp(e[None, :, None] * lg), intra.shape)
    cdec = jnp.exp(RET_CHUNK * lg[:, 0, 0])
    return cdec, intra, bc(q_pow), bc(k_pow)


def bidirectional_retention(p, pc, decay_f, decay_b, gn_g, gn_b, with_ctx):
    b = p.shape[0]
    h = decay_f.shape[0]
    tf = _retention_tables(decay_f, False)
    tb = _retention_tables(decay_b, True)
    s0 = jnp.zeros((b, h, HEAD_DIM, HEAD_DIM), F32)
    gn = (gn_g.reshape(1, -1).astype(F32), gn_b.reshape(1, -1).astype(F32))
    oc_f, s_f = _retention_pass(pc, tf, s0, None, reverse=False)
    oc, s_b = _retention_pass(pc, tb, s0, (oc_f,) + gn, reverse=True)
    o_f, _ = _retention_pass(p, tf, s_f, None, reverse=False)
    o, _ = _retention_pass(p, tb, s_b, (o_f,) + gn, reverse=True)
    return o, (oc if with_ctx else None)


def _out_kernel(oa_ref, ob_ref, oc_ref, od_ref, w_ref, x_ref, g1_ref, ng_ref, sc_ref, sh_ref, wr_ref,
                xo_ref, h_ref, aff_ref):
    gw = oa_ref.shape[1]
    acc = jnp.dot(oa_ref[...], w_ref[0 * gw:1 * gw, :], preferred_element_type=F32)
    acc += jnp.dot(ob_ref[...], w_ref[1 * gw:2 * gw, :], preferred_element_type=F32)
    acc += jnp.dot(oc_ref[...], w_ref[2 * gw:3 * gw, :], preferred_element_type=F32)
    acc += jnp.dot(od_ref[...], w_ref[3 * gw:4 * gw, :], preferred_element_type=F32)
    x = x_ref[...] + g1_ref[...] * acc
    xo_ref[...] = x
    y = x * lax.rsqrt(jnp.mean(x * x, axis=-1, keepdims=True) + NORM_EPS) * ng_ref[...]
    h2 = (y * (1.0 + sc_ref[...]) + sh_ref[...]).astype(BF16)
    h_ref[...] = h2
    logits = _nt_dot(wr_ref[...], h2)
    e = jnp.exp(logits - jnp.max(logits, axis=0, keepdims=True))
    aff_ref[...] = e * (1.0 / jnp.sum(e, axis=0, keepdims=True))


def out_project(o_groups, w_out, x, g1, norm_g, scale, shift, w_router_t, *, tm=256):
    b, n, d = x.shape
    n_e = w_router_t.shape[0]
    gw = o_groups[0].shape[-1]
    tm = min(tm, n)
    tile = lambda w: pl.BlockSpec((None, tm, w), lambda bi, i: (bi, i, 0))
    vec = pl.BlockSpec((None, 1, d), lambda bi, i: (bi, 0, 0))
    return pl.pallas_call(
        _out_kernel,
        grid=(b, n // tm),
        in_specs=[tile(gw)] * 4 + [
            pl.BlockSpec((d, d), lambda bi, i: (0, 0)),
            tile(d), vec,
            pl.BlockSpec((1, d), lambda bi, i: (0, 0)),
            vec, vec,
            pl.BlockSpec((n_e, d), lambda bi, i: (0, 0)),
        ],
        out_specs=[tile(d), tile(d), pl.BlockSpec((None, n_e, tm), lambda bi, i: (bi, 0, i))],
        out_shape=[jax.ShapeDtypeStruct((b, n, d), F32),
                   jax.ShapeDtypeStruct((b, n, d), BF16),
                   jax.ShapeDtypeStruct((b, n_e, n), F32)],
        compiler_params=_cparams("parallel", "parallel"),
        name="out_project",
    )(*o_groups, w_out, x, g1, norm_g, scale, shift, w_router_t)


def _ffn_kernel(x_ref, gate_ref, wg_ref, wu_ref, wd_ref, o_ref):
    x = x_ref[...]
    a = jnp.dot(x, wg_ref[...], preferred_element_type=F32)
    u = jnp.dot(x, wu_ref[...], preferred_element_type=F32)
    hm = (a * (1.0 / (1.0 + jnp.exp(-a))) * u).astype(BF16)
    y = jnp.dot(hm, wd_ref[...], preferred_element_type=F32) * gate_ref[...]
    o_ref[...] = y.astype(o_ref.dtype)


def expert_ffn(xin, gate, w_gate, w_up, w_down, *, tc=256):
    b, n_e, cap, d = xin.shape
    ff = w_gate.shape[-1]
    tc = min(tc, cap)
    return pl.pallas_call(
        _ffn_kernel,
        grid=(n_e, b, cap // tc),
        in_specs=[
            pl.BlockSpec((None, None, tc, d), lambda e, bi, i: (bi, e, i, 0)),
            pl.BlockSpec((None, None, tc, 1), lambda e, bi, i: (bi, e, i, 0)),
            pl.BlockSpec((None, d, ff), lambda e, bi, i: (e, 0, 0)),
            pl.BlockSpec((None, d, ff), lambda e, bi, i: (e, 0, 0)),
            pl.BlockSpec((None, ff, d), lambda e, bi, i: (e, 0, 0)),
        ],
        out_specs=pl.BlockSpec((None, None, tc, d), lambda e, bi, i: (bi, e, i, 0)),
        out_shape=jax.ShapeDtypeStruct((b, n_e, cap, d), BF16),
        compiler_params=_cparams("parallel", "parallel", "parallel"),
        name="expert_ffn",
    )(xin, gate, w_gate, w_up, w_down)


SCATTER_GROUP = 64


def _scatter_kernel(rlo_ref, nrd_ref, x_ref, g_ref, ng_ref, idx_ref, y_hbm, o_ref, ybuf, sem, acc_ref,
                    *, sr, final):
    bi, i, nt = pl.program_id(0), pl.program_id(1), pl.num_programs(1)
    n_e, n_grp = idx_ref.shape[0], idx_ref.shape[1]
    tm = x_ref.shape[0]
    n_rounds = nrd_ref[bi * nt + i]

    def group(e, k):
        return jnp.minimum(rlo_ref[(bi * n_e + e) * nt + i] + k, n_grp - 1)

    def slot_tokens(e, k):
        in_range = rlo_ref[(bi * n_e + e) * nt + i] + k < n_grp
        return jnp.where(in_range, idx_ref[e, group(e, k)], -1)

    def copies(k, slot):
        return [pltpu.make_async_copy(
            y_hbm.at[bi, e, pl.ds(pl.multiple_of(group(e, k) * sr, sr), sr), :],
            ybuf.at[slot, pl.ds(e * sr, sr), :], sem.at[slot]) for e in range(n_e)]

    for c in copies(0, 0):
        c.start()
    acc_ref[...] = jnp.zeros(acc_ref.shape, F32)
    per_row = HEAD_DIM // sr
    tok = i * tm + lax.broadcasted_iota(jnp.int32, (tm, HEAD_DIM), 0)

    def body(k, carry):
        slot = k % 2
        for c in copies(k, slot):
            c.wait()

        @pl.when(k + 1 < n_rounds)
        def _():
            for c in copies(k + 1, 1 - slot):
                c.start()

        rows = [jnp.concatenate([slot_tokens(e + j, k) for j in range(per_row)], axis=1)
                for e in range(0, n_e, per_row)]
        onehot = jnp.concatenate([jnp.where(tok == r, 1.0, 0.0) for r in rows], axis=1)
        acc_ref[...] += jnp.dot(onehot.astype(BF16), ybuf[slot], preferred_element_type=F32)
        return carry

    lax.fori_loop(0, n_rounds, body, 0)
    x = x_ref[...] + g_ref[...] * acc_ref[...]
    if final:
        x = x * lax.rsqrt(jnp.mean(x * x, axis=-1, keepdims=True) + NORM_EPS) * ng_ref[...]
    o_ref[...] = x


def scatter_combine(x, y, idx, g2, final_g, *, tm=256):
    b, t, d = x.shape
    n_e, cap = idx.shape[1], idx.shape[2]
    tm = min(tm, t)
    sr = min(SCATTER_GROUP, cap)
    assert t % tm == 0 and cap % sr == 0 and HEAD_DIM % sr == 0 and n_e % (HEAD_DIM // sr) == 0
    nt, n_grp = t // tm, cap // sr
    edges = jnp.arange(nt + 1, dtype=idx.dtype) * tm
    pos = jnp.sum(idx[..., None] < edges, axis=2, dtype=jnp.int32)
    lo, hi = pos[..., :-1], pos[..., 1:]
    rlo = jnp.minimum(lo // sr, n_grp - 1)
    rhi = jnp.maximum(hi - 1, lo) // sr
    n_rounds = jnp.max(jnp.minimum(rhi, n_grp - 1) - rlo + 1, axis=1)
    final = final_g is not None
    ng = final_g if final else jnp.ones((1, d), F32)
    tile = pl.BlockSpec((None, tm, d), lambda bi, i, *_: (bi, i, 0))
    return pl.pallas_call(
        functools.partial(_scatter_kernel, sr=sr, final=final),
        grid_spec=pltpu.PrefetchScalarGridSpec(
            num_scalar_prefetch=2,
            grid=(b, nt),
            in_specs=[tile,
                      pl.BlockSpec((None, 1, d), lambda bi, i, *_: (bi, 0, 0)),
                      pl.BlockSpec((1, d), lambda bi, i, *_: (0, 0)),
                      pl.BlockSpec((None, n_e, n_grp, 1, sr), lambda bi, i, *_: (bi, 0, 0, 0, 0)),
                      pl.BlockSpec(memory_space=pl.ANY)],
            out_specs=tile,
            scratch_shapes=[pltpu.VMEM((2, n_e * sr, d), BF16),
                            pltpu.SemaphoreType.DMA((2,)),
                            pltpu.VMEM((tm, d), F32)]),
        out_shape=jax.ShapeDtypeStruct((b, t, d), F32),
        compiler_params=_cparams("parallel", "arbitrary"),
        name="scatter_combine",
    )(rlo.reshape(-1), n_rounds.reshape(-1), x, g2, ng, idx.reshape(b, n_e, n_grp, 1, sr), y)


def expert_choice_ffn(x, h2, aff_t, g2, final_g, w_gate, w_up, w_down):
    b, t, d = h2.shape
    cap = EC_CAPACITY * t // N_EXPERTS
    gate, idx = lax.top_k(aff_t, cap)
    idx, gate = lax.sort_key_val(idx, gate, dimension=-1)
    xin = jax.vmap(lambda hb, ib: hb[ib])(h2, idx)
    y = expert_ffn(xin, gate[..., None], w_gate, w_up, w_down)
    return scatter_combine(x, y, idx, g2, final_g)


def _ctx_attn_kernel(scal_ref, q_ref, k_ref, v_ref, g_ref, o_ref, *, kind, post_scale):
    q, k, v = q_ref[...], k_ref[...], v_ref[...]
    if kind == "diff":
        qf = q.astype(F32)
        lane = lax.broadcasted_iota(jnp.int32, qf.shape, 1)

        def probs(keep):
            s = _nt_dot(jnp.where(keep, qf, 0.0).astype(BF16), k)
            e = jnp.exp2(s - jnp.max(s, axis=-1, keepdims=True))
            return e * (1.0 / jnp.sum(e, axis=-1, keepdims=True))

        w = probs(lane < DIFF_DIM) - scal_ref[0] * probs(lane >= DIFF_DIM)
        o = jnp.dot(w.astype(BF16), v, preferred_element_type=F32)
        o = o * lax.rsqrt(jnp.mean(o * o, axis=-1, keepdims=True) + NORM_EPS) * g_ref[...] * post_scale
    else:
        s = _nt_dot(q, k)
        m = jnp.max(s, axis=-1, keepdims=True)
        if kind == "sink":
            sink = scal_ref[pl.program_id(1)]
            m = jnp.maximum(m, sink)
        e = jnp.exp(s - m)
        l = jnp.sum(e, axis=-1, keepdims=True)
        if kind == "sink":
            l = l + jnp.exp(sink - m)
        o = jnp.dot((e * (1.0 / l)).astype(BF16), v, preferred_element_type=F32)
    o_ref[...] = o.astype(o_ref.dtype)


def _context_attention(pc, scal, norm_g, kind, post_scale, q_col, k_col, v_col, grp):
    b, lc, _ = pc.shape
    h = GROUP_BLOCKS
    col = lambda c0, div: (lambda bi, hi: (bi, 0, c0 + hi // div))
    return pl.pallas_call(
        functools.partial(_ctx_attn_kernel, kind=kind, post_scale=post_scale),
        grid=(b, h),
        in_specs=[
            pl.BlockSpec(memory_space=pltpu.SMEM),
            pl.BlockSpec((None, lc, HEAD_DIM), col(q_col, 1)),
            pl.BlockSpec((None, lc, HEAD_DIM), col(k_col, grp)),
            pl.BlockSpec((None, lc, HEAD_DIM), col(v_col, grp)),
            pl.BlockSpec((1, HEAD_DIM), lambda bi, hi: (0, 0)),
        ],
        out_specs=pl.BlockSpec((None, lc, HEAD_DIM), lambda bi, hi: (bi, 0, hi)),
        out_shape=jax.ShapeDtypeStruct((b, lc, h * HEAD_DIM), BF16),
        compiler_params=_cparams("parallel", "parallel"),
        name="context_attention_" + kind,
    )(scal, pc, pc, pc, norm_g)


def _context_mixers(pc, oc_ret, lam, lam_init, diff_norm_g, swa_sink):
    h = GROUP_BLOCKS
    zeros = jnp.zeros((h,), F32)
    ones = jnp.ones((1, HEAD_DIM), F32)
    o_a = _context_attention(pc, zeros, ones, "plain", 1.0, COL_A_Q, COL_A_K, COL_A_V, 1)
    o_b = _context_attention(pc, lam.reshape(1), diff_norm_g.reshape(1, HEAD_DIM).astype(F32), "diff",
                             1.0 - lam_init, COL_B_K - h, COL_B_K, COL_B_K + h, 1)
    o_d = _context_attention(pc, swa_sink.astype(F32), ones, "sink", 1.0, COL_D_Q, COL_D_K, COL_D_V, 2)
    return [o_a, o_b, oc_ret, o_d]


def _layer(x, xc, mod, layer_idx, with_ctx, final_g, rope_h, rope_d, rope_id,
           norm1_g, w_in, na_rpb, diff_lambda, diff_norm_g, ret_decay_fwd, ret_decay_bwd, ret_gn_g, ret_gn_b,
           swa_sink, w_out, norm2_g, w_router, w_gate, w_up, w_down):
    b, n, d = x.shape
    row = lambda v: v.reshape(1, -1).astype(F32)
    part = lambda r0, r1, k: mod[r0:r1, None, k * d:(k + 1) * d]
    sh1, sc1, g1, sh2, sc2, g2 = [part(0, b, k) for k in range(6)]
    ctx_rows = lambda k: jnp.broadcast_to(part(b, b + 1, k), (b, 1, d))
    sh1c, sc1c, g1c, sh2c, sc2c, g2c = [ctx_rows(k) for k in range(6)]

    w_in_b = w_in.astype(BF16)
    w_out_b = w_out.astype(BF16)
    w_router_t = w_router.T.astype(BF16)
    wg_b, wu_b, wd_b = w_gate.astype(BF16), w_up.astype(BF16), w_down.astype(BF16)

    p, t = norm_project(x, row(norm1_g), sc1, sh1, w_in_b, rope_h, rope_d)
    pc, tc = norm_project(xc, row(norm1_g), sc1c, sh1c, w_in_b, rope_id, rope_id)

    lam_init = 0.8 - 0.6 * math.exp(-0.3 * layer_idx)
    lq1, lk1, lq2, lk2 = [diff_lambda[k].astype(F32) for k in range(4)]
    lam = jnp.exp(jnp.sum(lq1 * lk1)) - jnp.exp(jnp.sum(lq2 * lk2)) + lam_init

    o_a = neighbourhood_attention(p, pc, na_rpb)
    o_b = diff_attention(p, t, pc, tc, lam.reshape(1), row(diff_norm_g), 1.0 - lam_init)
    o_c, oc_ret = bidirectional_retention(p, pc, ret_decay_fwd, ret_decay_bwd, ret_gn_g, ret_gn_b, with_ctx)
    o_d = window_attention(p, pc, swa_sink.astype(F32))

    x, h2, aff_t = out_project([o_a, o_b, o_c, o_d], w_out_b, x, g1, row(norm2_g), sc2, sh2, w_router_t)
    x = expert_choice_ffn(x, h2, aff_t, g2, final_g, wg_b, wu_b, wd_b)

    if with_ctx:
        oc = _context_mixers(pc, oc_ret, lam, lam_init, diff_norm_g, swa_sink)
        xc, h2c, aff_c = out_project(oc, w_out_b, xc, g1c, row(norm2_g), sc2c, sh2c, w_router_t)
        xc = expert_choice_ffn(xc, h2c, aff_c, g2c, None, wg_b, wu_b, wd_b)
    return x, xc


def kernel(x, c, ctx, c_ctx, w_mod, b_mod, norm1_g, w_in, na_rpb, diff_lambda, diff_norm_g,
           ret_decay_fwd, ret_decay_bwd, ret_gn_g, ret_gn_b, swa_sink, w_out, norm2_g,
           w_router, w_gate, w_up, w_down, final_norm_g):
    b, n, d = x.shape
    depth = w_in.shape[0]
    lc = ctx.shape[1]
    cvec = jnp.zeros((8, d), F32).at[:b].set(c).at[b].set(c_ctx)
    mod = modulation(cvec, w_mod, b_mod.reshape(depth, 1, -1))
    rope_h = _rope_tables(n, HEAD_DIM)
    rope_d = _rope_tables(n, DIFF_DIM)
    rope_id = _identity_rope_tables(lc)
    xc = ctx
    for li in range(depth):
        last = li == depth - 1
        x, xc = _layer(x, xc, mod[li], li, not last, final_norm_g.reshape(1, -1) if last else None,
                       rope_h, rope_d, rope_id,
                       norm1_g[li], w_in[li], na_rpb[li], diff_lambda[li], diff_norm_g[li],
                       ret_decay_fwd[li], ret_decay_bwd[li], ret_gn_g[li], ret_gn_b[li], swa_sink[li],
                       w_out[li], norm2_g[li], w_router[li], w_gate[li], w_up[li], w_down[li])
    return x
```

```python
import functools
import math

import jax
import jax.numpy as jnp
from jax import lax
from jax.experimental import pallas as pl
from jax.experimental.pallas import tpu as pltpu

GRID_W = 64
HEAD_DIM = 128
DIFF_DIM = HEAD_DIM // 2
NA_WIN_R = 8
NA_WIN_C = 16
RET_CHUNK = 128
SWA_WINDOW = 128
N_EXPERTS = 16
EC_CAPACITY = 2
ROPE_BASE = 10000.0
NORM_EPS = 1e-6
GN_EPS = 1e-5
NEG_INF = -1e30
LOG2E = math.log2(math.e)

BF16 = jnp.bfloat16
F32 = jnp.float32

VMEM_LIMIT_BYTES = 48 * 1024 * 1024

GROUP_BLOCKS = 4
COL_A_Q, COL_A_K, COL_A_V = 0, 4, 8
COL_B_K = 16
COL_C_Q, COL_C_K, COL_C_V, COL_C_G = 24, 28, 32, 36
COL_D_Q, COL_D_K, COL_D_V = 40, 44, 46
PROJ_TN = GROUP_BLOCKS * HEAD_DIM


def _cparams(*sem):
    return pltpu.CompilerParams(dimension_semantics=sem, vmem_limit_bytes=VMEM_LIMIT_BYTES)


def _mod_kernel(c_ref, w_ref, b_ref, o_ref):
    c = c_ref[...]
    s = (c * (1.0 / (1.0 + jnp.exp(-c)))).astype(BF16)
    o_ref[...] = jnp.dot(s, w_ref[...].astype(BF16), preferred_element_type=F32) + b_ref[...]


def modulation(cvec, w_mod, b_mod, *, tn=1024):
    depth, d, n6 = w_mod.shape
    return pl.pallas_call(
        _mod_kernel,
        grid=(depth, n6 // tn),
        in_specs=[
            pl.BlockSpec((8, d), lambda l, j: (0, 0)),
            pl.BlockSpec((None, d, tn), lambda l, j: (l, 0, j)),
            pl.BlockSpec((None, 1, tn), lambda l, j: (l, 0, j)),
        ],
        out_specs=pl.BlockSpec((None, 8, tn), lambda l, j: (l, 0, j)),
        out_shape=jax.ShapeDtypeStruct((depth, 8, n6), F32),
        compiler_params=_cparams("parallel", "parallel"),
        name="modulation",
    )(cvec, w_mod, b_mod)


def _rope_tables(n_tok, dim):
    t = jnp.arange(n_tok)
    row = (t // GRID_W).astype(F32)[:, None]
    col = (t % GRID_W).astype(F32)[:, None]
    nf = dim // 4
    lane = jnp.arange(HEAD_DIM)
    quarter = (lane % dim) // nf
    inv = ROPE_BASE ** (-jnp.arange(nf, dtype=F32) / nf)
    ang = jnp.where(quarter[None, :] < 2, row, col) * inv[lane % nf][None, :]
    cos, sin = jnp.cos(ang), jnp.sin(ang)
    even = (quarter % 2 == 0)[None, :]
    return cos, jnp.where(even, -sin, 0.0), jnp.where(even, 0.0, sin)


def _identity_rope_tables(n_tok):
    z = jnp.zeros((n_tok, HEAD_DIM), F32)
    return jnp.ones((n_tok, HEAD_DIM), F32), z, z


def _proj_modes():
    att = HEAD_DIM ** -0.5
    plain = [(None, 1.0)] * GROUP_BLOCKS
    return [
        ([(None, att)] * 4, None), (plain, None), (plain, None),
        ([("d", LOG2E * DIFF_DIM ** -0.5)] * 4, 0), ([("d", 1.0)] * 4, None), (plain, 1),
        ([("h", 1.0)] * 4, None), ([("h", att)] * 4, None), (plain, None), (plain, None),
        ([("h", att)] * 4, None), ([("h", 1.0)] * 2 + [(None, 1.0)] * 2, None),
    ]


def _proj_kernel(x_ref, g_ref, sc_ref, sh_ref, w_ref, ch_ref, sah_ref, sbh_ref, cd_ref, sad_ref, sbd_ref,
                 o_ref, t_ref):
    x = x_ref[...]
    y = x * lax.rsqrt(jnp.mean(x * x, axis=-1, keepdims=True) + NORM_EPS) * g_ref[...]
    h = (y * (1.0 + sc_ref[...]) + sh_ref[...]).astype(BF16)

    def rope(a, kind):
        if kind is None:
            return a
        c, sa, sb, sh = ((ch_ref, sah_ref, sbh_ref, HEAD_DIM // 4) if kind == "h"
                         else (cd_ref, sad_ref, sbd_ref, DIFF_DIM // 4))
        return (a * c[...] + pltpu.roll(a, HEAD_DIM - sh, 1) * sa[...] + pltpu.roll(a, sh, 1) * sb[...])

    for j, (blocks, t_slot) in enumerate(_proj_modes()):
        cols = slice(j * PROJ_TN, (j + 1) * PROJ_TN)
        acc = jnp.dot(h, w_ref[:, cols], preferred_element_type=F32)
        outs = []
        for hb, (kind, scale) in enumerate(blocks):
            a = rope(acc[:, hb * HEAD_DIM:(hb + 1) * HEAD_DIM], kind)
            outs.append(a if scale == 1.0 else a * scale)
        full = jnp.concatenate(outs, axis=1)
        o_ref[:, cols] = full.astype(o_ref.dtype)
        if t_slot is not None:
            t_ref[t_slot] = full.T.astype(t_ref.dtype)


def norm_project(x, norm_g, scale, shift, w_in, rope_h, rope_d, *, tm=256):
    b, n, d = x.shape
    width = w_in.shape[1]
    assert width == len(_proj_modes()) * PROJ_TN
    tm = min(tm, n)
    tok = lambda bi, i: (i, 0)
    return pl.pallas_call(
        _proj_kernel,
        grid=(b, n // tm),
        in_specs=[
            pl.BlockSpec((None, tm, d), lambda bi, i: (bi, i, 0)),
            pl.BlockSpec((1, d), lambda bi, i: (0, 0)),
            pl.BlockSpec((None, 1, d), lambda bi, i: (bi, 0, 0)),
            pl.BlockSpec((None, 1, d), lambda bi, i: (bi, 0, 0)),
            pl.BlockSpec((d, width), lambda bi, i: (0, 0), pipeline_mode=pl.Buffered(1)),
        ] + [pl.BlockSpec((tm, HEAD_DIM), tok)] * 6,
        out_specs=[
            pl.BlockSpec((None, tm, width), lambda bi, i: (bi, i, 0)),
            pl.BlockSpec((None, 2, PROJ_TN, tm), lambda bi, i: (bi, 0, 0, i)),
        ],
        out_shape=[jax.ShapeDtypeStruct((b, n, width), BF16),
                   jax.ShapeDtypeStruct((b, 2, PROJ_TN, n), BF16)],
        compiler_params=_cparams("parallel", "parallel"),
        name="norm_project",
    )(x, norm_g, scale, shift, w_in, *rope_h, *rope_d)


def _softmax_pv(s_loc, s_ctx, vw, vc, sink):
    m = jnp.maximum(jnp.max(s_loc, axis=-1, keepdims=True), jnp.max(s_ctx, axis=-1, keepdims=True))
    if sink is not None:
        m = jnp.maximum(m, sink)
    p_loc = jnp.exp(s_loc - m)
    p_ctx = jnp.exp(s_ctx - m)
    l = jnp.sum(p_loc, axis=-1, keepdims=True) + jnp.sum(p_ctx, axis=-1, keepdims=True)
    if sink is not None:
        l = l + jnp.exp(sink - m)
    o = (jnp.dot(p_loc.astype(BF16), vw, preferred_element_type=F32)
         + jnp.dot(p_ctx.astype(BF16), vc, preferred_element_type=F32))
    return o * (1.0 / l)


def _nt_dot(a, b):
    return lax.dot_general(a, b, (((1,), (1,)), ((), ())), preferred_element_type=F32)


LOCAL_SUB_BLOCKS = 4


def _na_kernel(q_ref, k_ref, v_ref, kc_ref, vc_ref, bias_ref, o_ref, *, tq, wk, n_blk):
    n = k_ref.shape[0]
    rows_q = tq // GRID_W
    for sub in range(q_ref.shape[0] // tq):
        blk = pl.program_id(2) * (q_ref.shape[0] // tq) + sub
        kstart = jnp.clip(blk * rows_q - NA_WIN_R // 2, 0, (n - wk) // GRID_W) * GRID_W
        kstart = pl.multiple_of(kstart, GRID_W)
        cls = jnp.where(blk == 0, 0, jnp.where(blk == n_blk - 1, 2, 1))
        q = q_ref[sub * tq:(sub + 1) * tq, :]
        kw = k_ref[pl.ds(kstart, wk), :]
        vw = v_ref[pl.ds(kstart, wk), :]
        s_loc = _nt_dot(q, kw) + bias_ref[cls]
        s_ctx = _nt_dot(q, kc_ref[...])
        o_ref[sub * tq:(sub + 1) * tq, :] = _softmax_pv(s_loc, s_ctx, vw, vc_ref[...], None).astype(o_ref.dtype)


def _na_bias(rpb, rows, rows_q, rows_k):
    n_blk = rows // rows_q
    kr = NA_WIN_R
    cols = jnp.arange(GRID_W)
    col_start = jnp.clip(cols - NA_WIN_C // 2, 0, GRID_W - NA_WIN_C)
    col_ok = (cols[None, :] >= col_start[:, None]) & (cols[None, :] < col_start[:, None] + NA_WIN_C)
    col_off = jnp.clip(cols[None, :] - cols[:, None] + NA_WIN_C - 1, 0, 2 * NA_WIN_C - 2)
    out = []
    for blk in (0, 1, n_blk - 1):
        r = blk * rows_q + jnp.arange(rows_q)
        kstart = min(max(blk * rows_q - kr // 2, 0), rows - rows_k)
        krow = kstart + jnp.arange(rows_k)
        start = jnp.clip(r - kr // 2, 0, rows - kr)
        row_ok = (krow[None, :] >= start[:, None]) & (krow[None, :] < start[:, None] + kr)
        row_off = jnp.clip(krow[None, :] - r[:, None] + NA_WIN_R - 1, 0, 2 * NA_WIN_R - 2)
        bias = jnp.einsum('abr,hrc,qkc->habqk', jax.nn.one_hot(row_off, 2 * NA_WIN_R - 1, dtype=F32),
                          rpb.astype(F32), jax.nn.one_hot(col_off, 2 * NA_WIN_C - 1, dtype=F32),
                          precision=lax.Precision.HIGHEST)
        ok = row_ok[:, :, None, None] & col_ok[None, None]
        bias = jnp.where(ok[None], bias, NEG_INF).transpose(0, 1, 3, 2, 4)
        out.append(bias.reshape(rpb.shape[0], rows_q * GRID_W, rows_k * GRID_W))
    return jnp.stack(out, axis=1)


def neighbourhood_attention(p, pc, rpb, *, rows_q=4):
    b, n, _ = p.shape
    lc = pc.shape[1]
    h = rpb.shape[0]
    rows = n // GRID_W
    rows_k = rows_q + NA_WIN_R - 1
    assert rows % rows_q == 0 and rows >= rows_k and rows_q >= NA_WIN_R // 2
    tq, wk = rows_q * GRID_W, rows_k * GRID_W
    n_blk = rows // rows_q
    sub = LOCAL_SUB_BLOCKS if n_blk % LOCAL_SUB_BLOCKS == 0 else 1
    bias = _na_bias(rpb, rows, rows_q, rows_k)
    kern = functools.partial(_na_kernel, tq=tq, wk=wk, n_blk=n_blk)
    return pl.pallas_call(
        kern,
        grid=(b, h, n_blk // sub),
        in_specs=[
            pl.BlockSpec((None, sub * tq, HEAD_DIM), lambda bi, hi, i: (bi, i, COL_A_Q + hi)),
            pl.BlockSpec((None, n, HEAD_DIM), lambda bi, hi, i: (bi, 0, COL_A_K + hi)),
            pl.BlockSpec((None, n, HEAD_DIM), lambda bi, hi, i: (bi, 0, COL_A_V + hi)),
            pl.BlockSpec((None, lc, HEAD_DIM), lambda bi, hi, i: (bi, 0, COL_A_K + hi)),
            pl.BlockSpec((None, lc, HEAD_DIM), lambda bi, hi, i: (bi, 0, COL_A_V + hi)),
            pl.BlockSpec((None, 3, tq, wk), lambda bi, hi, i: (hi, 0, 0, 0)),
        ],
        out_specs=pl.BlockSpec((None, sub * tq, HEAD_DIM), lambda bi, hi, i: (bi, i, hi)),
        out_shape=jax.ShapeDtypeStruct((b, n, h * HEAD_DIM), BF16),
        compiler_params=_cparams("parallel", "parallel", "arbitrary"),
        name="neighbourhood_attention",
    )(p, p, p, pc, pc, bias)


def _swa_kernel(sink_ref, q_ref, k_ref, v_ref, kc_ref, vc_ref, o_ref, *, tq, wk):
    hi = pl.program_id(1)
    n = k_ref.shape[0]
    for sub in range(q_ref.shape[0] // tq):
        q0 = pl.program_id(2) * q_ref.shape[0] + sub * tq
        kstart = pl.multiple_of(jnp.clip(q0 - SWA_WINDOW, 0, n - wk), SWA_WINDOW)
        q = q_ref[sub * tq:(sub + 1) * tq, :]
        kw = k_ref[pl.ds(kstart, wk), :]
        vw = v_ref[pl.ds(kstart, wk), :]
        qpos = q0 + lax.broadcasted_iota(jnp.int32, (tq, wk), 0)
        kpos = kstart + lax.broadcasted_iota(jnp.int32, (tq, wk), 1)
        s_loc = jnp.where(jnp.abs(kpos - qpos) <= SWA_WINDOW, _nt_dot(q, kw), NEG_INF)
        s_ctx = _nt_dot(q, kc_ref[...])
        o = _softmax_pv(s_loc, s_ctx, vw, vc_ref[...], sink_ref[hi])
        o_ref[sub * tq:(sub + 1) * tq, :] = o.astype(o_ref.dtype)


def window_attention(p, pc, sink, *, tq=256):
    b, n, _ = p.shape
    lc = pc.shape[1]
    hq = sink.shape[0]
    grp = 2
    tq = min(tq, n)
    wk = min(tq + 2 * SWA_WINDOW, n)
    sub = LOCAL_SUB_BLOCKS if (n // tq) % LOCAL_SUB_BLOCKS == 0 else 1
    kern = functools.partial(_swa_kernel, tq=tq, wk=wk)
    return pl.pallas_call(
        kern,
        grid=(b, hq, n // (sub * tq)),
        in_specs=[
            pl.BlockSpec(memory_space=pltpu.SMEM),
            pl.BlockSpec((None, sub * tq, HEAD_DIM), lambda bi, hi, i: (bi, i, COL_D_Q + hi)),
            pl.BlockSpec((None, n, HEAD_DIM), lambda bi, hi, i: (bi, 0, COL_D_K + hi // grp)),
            pl.BlockSpec((None, n, HEAD_DIM), lambda bi, hi, i: (bi, 0, COL_D_V + hi // grp)),
            pl.BlockSpec((None, lc, HEAD_DIM), lambda bi, hi, i: (bi, 0, COL_D_K + hi // grp)),
            pl.BlockSpec((None, lc, HEAD_DIM), lambda bi, hi, i: (bi, 0, COL_D_V + hi // grp)),
        ],
        out_specs=pl.BlockSpec((None, sub * tq, HEAD_DIM), lambda bi, hi, i: (bi, i, hi)),
        out_shape=jax.ShapeDtypeStruct((b, n, hq * HEAD_DIM), BF16),
        compiler_params=_cparams("parallel", "parallel", "arbitrary"),
        name="window_attention",
    )(sink, p, p, p, pc, pc)


DIFF_SLOTS = 2
DIFF_UNROLL = 6
DIFF_AUG_ROWS = 16


def _diff_attn_kernel(lam_ref, qt_ref, k_ref, vt_ref, kc_ref, vct_ref, g_ref, o_ref,
                      qbd_ref, m_ref, acc_ref, s_scr, x_scr, *, tq, tk, post_scale):
    qt = qt_ref[...].astype(F32)
    row = lax.broadcasted_iota(jnp.int32, qt.shape, 0)
    zero = jnp.zeros_like(qt)
    qbd_ref[:, :tq] = jnp.where(row < DIFF_DIM, qt, zero).astype(BF16)
    qbd_ref[:, tq:] = jnp.where(row >= DIFF_DIM, qt, zero).astype(BF16)

    def aug(vt_tile):
        r = lax.broadcasted_iota(jnp.int32, (DIFF_AUG_ROWS, vt_tile.shape[1]), 0)
        return jnp.concatenate([vt_tile, jnp.where(r == 0, 1.0, 0.0).astype(BF16)], axis=0)

    s = jnp.dot(kc_ref[...], qbd_ref[...], preferred_element_type=F32)
    m0 = jnp.max(s, axis=0, keepdims=True)
    m_ref[...] = m0
    acc_ref[...] = jnp.dot(aug(vct_ref[...]), jnp.exp2(s - m0).astype(BF16), preferred_element_type=F32)

    def scores(t, slot):
        off = pl.multiple_of(t * tk, tk)
        s = jnp.dot(k_ref[pl.ds(off, tk), :], qbd_ref[...], preferred_element_type=F32)
        s_scr[slot] = s
        x_scr[slot] = jnp.max(s, axis=0, keepdims=True)

    def accumulate(t, slot):
        m_old = m_ref[...]
        m_new = jnp.maximum(m_old, x_scr[slot])
        alpha = jnp.exp2(m_old - m_new)
        p = jnp.exp2(s_scr[slot] - m_new).astype(BF16)
        m_ref[...] = m_new
        off = pl.multiple_of(t * tk, tk)
        pv = jnp.dot(aug(vt_ref[:, pl.ds(off, tk)]), p, preferred_element_type=F32)
        acc_ref[...] = alpha * acc_ref[...] + pv

    def stage(t, t_mod, do_scores=True):
        if do_scores:
            scores(t + 1, (t_mod + 1) % DIFF_SLOTS)
        accumulate(t, t_mod % DIFF_SLOTS)

    n_kt = k_ref.shape[0] // tk
    trips = (n_kt - 1) // DIFF_UNROLL
    scores(0, 0)

    def body(i, carry):
        for u in range(DIFF_UNROLL):
            stage(DIFF_UNROLL * i + u, u)
        return carry

    lax.fori_loop(0, trips, body, 0)
    for t in range(DIFF_UNROLL * trips, n_kt - 1):
        stage(t, t)
    stage(n_kt - 1, n_kt - 1, do_scores=False)

    lam = lam_ref[0]
    inv = 1.0 / acc_ref[HEAD_DIM:HEAD_DIM + 1, :]
    acc = acc_ref[:HEAD_DIM, :]
    o_t = acc[:, :tq] * inv[:, :tq] - lam * (acc[:, tq:] * inv[:, tq:])
    o = o_t.T
    ms = jnp.mean(o * o, axis=-1, keepdims=True)
    y = o * lax.rsqrt(ms + NORM_EPS) * g_ref[...]
    o_ref[...] = (y * post_scale).astype(o_ref.dtype)


def diff_attention(p, t, pc, tc, lam, norm_g, post_scale, *, tq=512, tk=512):
    b, n, _ = p.shape
    lc = pc.shape[1]
    h = GROUP_BLOCKS
    tq = min(tq, n)
    tk = min(tk, n)
    assert n % tq == 0 and n % tk == 0 and DIFF_UNROLL % DIFF_SLOTS == 0
    kern = functools.partial(_diff_attn_kernel, tq=tq, tk=tk, post_scale=post_scale)
    return pl.pallas_call(
        kern,
        grid=(b, h, n // tq),
        in_specs=[
            pl.BlockSpec(memory_space=pltpu.SMEM),
            pl.BlockSpec((None, None, HEAD_DIM, tq), lambda bi, hi, qi: (bi, 0, hi, qi)),
            pl.BlockSpec((None, n, HEAD_DIM), lambda bi, hi, qi: (bi, 0, COL_B_K + hi)),
            pl.BlockSpec((None, None, HEAD_DIM, n), lambda bi, hi, qi: (bi, 1, hi, 0)),
            pl.BlockSpec((None, lc, HEAD_DIM), lambda bi, hi, qi: (bi, 0, COL_B_K + hi)),
            pl.BlockSpec((None, None, HEAD_DIM, lc), lambda bi, hi, qi: (bi, 1, hi, 0)),
            pl.BlockSpec((1, HEAD_DIM), lambda bi, hi, qi: (0, 0)),
        ],
        out_specs=pl.BlockSpec((None, tq, HEAD_DIM), lambda bi, hi, qi: (bi, qi, hi)),
        out_shape=jax.ShapeDtypeStruct((b, n, h * HEAD_DIM), BF16),
        scratch_shapes=[
            pltpu.VMEM((HEAD_DIM, 2 * tq), BF16),
            pltpu.VMEM((1, 2 * tq), F32),
            pltpu.VMEM((HEAD_DIM + DIFF_AUG_ROWS, 2 * tq), F32),
            pltpu.VMEM((DIFF_SLOTS, tk, 2 * tq), F32),
            pltpu.VMEM((DIFF_SLOTS, 1, 2 * tq), F32),
        ],
        compiler_params=_cparams("parallel", "parallel", "arbitrary"),
        name="diff_attention",
    )(lam, t, p, t, pc, tc, norm_g)


def _retention_kernel(cdec_ref, q_ref, k_ref, v_ref, intra_ref, qdec_ref, kdec_ref, s0_ref, *rest,
                      reverse, final):
    if final:
        of_ref, gate_ref, gng_ref, gnb_ref, o_ref, sfin_ref, s_ref = rest
    else:
        o_ref, sfin_ref, s_ref = rest
    i = pl.program_id(1)
    n_heads = s_ref.shape[0]
    n_chunks = q_ref.shape[0] // RET_CHUNK

    @pl.when(i == 0)
    def _():
        s_ref[...] = s0_ref[...]

    order = range(n_chunks - 1, -1, -1) if reverse else range(n_chunks)
    for c in order:
        rs = slice(c * RET_CHUNK, (c + 1) * RET_CHUNK)
        for h in range(n_heads):
            cs = slice(h * HEAD_DIM, (h + 1) * HEAD_DIM)
            qh, kh, vh = q_ref[rs, cs], k_ref[rs, cs], v_ref[rs, cs]
            qd = (qh.astype(F32) * qdec_ref[h]).astype(BF16)
            kd_t = (kh.astype(F32) * kdec_ref[h]).T.astype(BF16)
            a = (_nt_dot(qh, kh) * intra_ref[h]).astype(BF16)
            s = s_ref[h]
            o = (jnp.dot(a, vh, preferred_element_type=F32)
                 + jnp.dot(qd, s.astype(BF16), preferred_element_type=F32))
            s_ref[h] = cdec_ref[h] * s + jnp.dot(kd_t, vh, preferred_element_type=F32)
            if final:
                o = o + of_ref[rs, cs]
                mu = jnp.mean(o, axis=-1, keepdims=True)
                var = jnp.mean(jnp.square(o - mu), axis=-1, keepdims=True)
                y = (o - mu) * lax.rsqrt(var + GN_EPS) * gng_ref[:, cs] + gnb_ref[:, cs]
                g = gate_ref[rs, cs].astype(F32)
                o = g * (1.0 / (1.0 + jnp.exp(-g))) * y
            o_ref[rs, cs] = o.astype(o_ref.dtype)

    @pl.when(i == pl.num_programs(1) - 1)
    def _():
        sfin_ref[...] = s_ref[...]


def _retention_pass(p, tables, s0, final_inputs, *, reverse, blk_chunks=4):
    b, n, _ = p.shape
    cdec, intra, qdec, kdec = tables
    h = intra.shape[0]
    width = h * HEAD_DIM
    n_chunks = n // RET_CHUNK
    blk_chunks = min(blk_chunks, n_chunks)
    assert n_chunks % blk_chunks == 0
    tb = blk_chunks * RET_CHUNK
    n_blk = n // tb
    pos = (lambda i: n_blk - 1 - i) if reverse else (lambda i: i)
    colblk = lambda c: (lambda bi, i: (bi, pos(i), c // GROUP_BLOCKS))
    tab = pl.BlockSpec((h, RET_CHUNK, HEAD_DIM), lambda bi, i: (0, 0, 0))
    state = pl.BlockSpec((None, h, HEAD_DIM, HEAD_DIM), lambda bi, i: (bi, 0, 0, 0))
    in_specs = [pl.BlockSpec(memory_space=pltpu.SMEM),
                pl.BlockSpec((None, tb, width), colblk(COL_C_Q)),
                pl.BlockSpec((None, tb, width), colblk(COL_C_K)),
                pl.BlockSpec((None, tb, width), colblk(COL_C_V)),
                tab, tab, tab, state]
    args = [cdec, p, p, p, intra, qdec, kdec, s0]
    final = final_inputs is not None
    if final:
        o_fwd, gn_g, gn_b = final_inputs
        in_specs += [pl.BlockSpec((None, tb, width), lambda bi, i: (bi, pos(i), 0)),
                     pl.BlockSpec((None, tb, width), colblk(COL_C_G)),
                     pl.BlockSpec((1, width), lambda bi, i: (0, 0)),
                     pl.BlockSpec((1, width), lambda bi, i: (0, 0))]
        args += [o_fwd, p, gn_g, gn_b]
    kern = functools.partial(_retention_kernel, reverse=reverse, final=final)
    return pl.pallas_call(
        kern,
        grid=(b, n_blk),
        in_specs=in_specs,
        out_specs=[pl.BlockSpec((None, tb, width), lambda bi, i: (bi, pos(i), 0)), state],
        out_shape=[jax.ShapeDtypeStruct((b, n, width), BF16 if final else F32),
                   jax.ShapeDtypeStruct((b, h, HEAD_DIM, HEAD_DIM), F32)],
        scratch_shapes=[pltpu.VMEM((h, HEAD_DIM, HEAD_DIM), F32)],
        compiler_params=_cparams("parallel", "arbitrary"),
        name="retention_bwd" if reverse else "retention_fwd",
    )(*args)


def _retention_tables(decay, reverse):
    lg = jax.nn.log_sigmoid(decay.astype(F32))[:, None, None]
    pos = jnp.arange(RET_CHUNK, dtype=F32)
    rel = pos[:, None] - pos[None, :]
    if reverse:
        rel = -rel
        q_pow, k_pow = RET_CHUNK - pos, pos
    else:
        q_pow, k_pow = pos + 1.0, RET_CHUNK - 1.0 - pos
    intra = jnp.where(rel >= 0, jnp.exp(jnp.maximum(rel, 0.0) * lg), 0.0)
    bc = lambda e: jnp.broadcast_to(jnp.exp(e[None, :, None] * lg), intra.shape)
    cdec = jnp.exp(RET_CHUNK * lg[:, 0, 0])
    return cdec, intra, bc(q_pow), bc(k_pow)


def bidirectional_retention(p, pc, decay_f, decay_b, gn_g, gn_b, with_ctx):
    b = p.shape[0]
    h = decay_f.shape[0]
    tf = _retention_tables(decay_f, False)
    tb = _retention_tables(decay_b, True)
    s0 = jnp.zeros((b, h, HEAD_DIM, HEAD_DIM), F32)
    gn = (gn_g.reshape(1, -1).astype(F32), gn_b.reshape(1, -1).astype(F32))
    oc_f, s_f = _retention_pass(pc, tf, s0, None, reverse=False)
    oc, s_b = _retention_pass(pc, tb, s0, (oc_f,) + gn, reverse=True)
    o_f, _ = _retention_pass(p, tf, s_f, None, reverse=False)
    o, _ = _retention_pass(p, tb, s_b, (o_f,) + gn, reverse=True)
    return o, (oc if with_ctx else None)


def _out_kernel(oa_ref, ob_ref, oc_ref, od_ref, w_ref, x_ref, g1_ref, ng_ref, sc_ref, sh_ref, wr_ref,
                xo_ref, h_ref, aff_ref):
    gw = oa_ref.shape[1]
    acc = jnp.dot(oa_ref[...], w_ref[0 * gw:1 * gw, :], preferred_element_type=F32)
    acc += jnp.dot(ob_ref[...], w_ref[1 * gw:2 * gw, :], preferred_element_type=F32)
    acc += jnp.dot(oc_ref[...], w_ref[2 * gw:3 * gw, :], preferred_element_type=F32)
    acc += jnp.dot(od_ref[...], w_ref[3 * gw:4 * gw, :], preferred_element_type=F32)
    x = x_ref[...] + g1_ref[...] * acc
    xo_ref[...] = x
    y = x * lax.rsqrt(jnp.mean(x * x, axis=-1, keepdims=True) + NORM_EPS) * ng_ref[...]
    h2 = (y * (1.0 + sc_ref[...]) + sh_ref[...]).astype(BF16)
    h_ref[...] = h2
    logits = _nt_dot(wr_ref[...], h2)
    e = jnp.exp(logits - jnp.max(logits, axis=0, keepdims=True))
    aff_ref[...] = e * (1.0 / jnp.sum(e, axis=0, keepdims=True))


def out_project(o_groups, w_out, x, g1, norm_g, scale, shift, w_router_t, *, tm=256):
    b, n, d = x.shape
    n_e = w_router_t.shape[0]
    gw = o_groups[0].shape[-1]
    tm = min(tm, n)
    tile = lambda w: pl.BlockSpec((None, tm, w), lambda bi, i: (bi, i, 0))
    vec = pl.BlockSpec((None, 1, d), lambda bi, i: (bi, 0, 0))
    return pl.pallas_call(
        _out_kernel,
        grid=(b, n // tm),
        in_specs=[tile(gw)] * 4 + [
            pl.BlockSpec((d, d), lambda bi, i: (0, 0)),
            tile(d), vec,
            pl.BlockSpec((1, d), lambda bi, i: (0, 0)),
            vec, vec,
            pl.BlockSpec((n_e, d), lambda bi, i: (0, 0)),
        ],
        out_specs=[tile(d), tile(d), pl.BlockSpec((None, n_e, tm), lambda bi, i: (bi, 0, i))],
        out_shape=[jax.ShapeDtypeStruct((b, n, d), F32),
                   jax.ShapeDtypeStruct((b, n, d), BF16),
                   jax.ShapeDtypeStruct((b, n_e, n), F32)],
        compiler_params=_cparams("parallel", "parallel"),
        name="out_project",
    )(*o_groups, w_out, x, g1, norm_g, scale, shift, w_router_t)


def _ffn_kernel(x_ref, gate_ref, wg_ref, wu_ref, wd_ref, o_ref):
    x = x_ref[...]
    a = jnp.dot(x, wg_ref[...], preferred_element_type=F32)
    u = jnp.dot(x, wu_ref[...], preferred_element_type=F32)
    hm = (a * (1.0 / (1.0 + jnp.exp(-a))) * u).astype(BF16)
    y = jnp.dot(hm, wd_ref[...], preferred_element_type=F32) * gate_ref[...]
    o_ref[...] = y.astype(o_ref.dtype)


def expert_ffn(xin, gate, w_gate, w_up, w_down, *, tc=256):
    b, n_e, cap, d = xin.shape
    ff = w_gate.shape[-1]
    tc = min(tc, cap)
    return pl.pallas_call(
        _ffn_kernel,
        grid=(n_e, b, cap // tc),
        in_specs=[
            pl.BlockSpec((None, None, tc, d), lambda e, bi, i: (bi, e, i, 0)),
            pl.BlockSpec((None, None, tc, 1), lambda e, bi, i: (bi, e, i, 0)),
            pl.BlockSpec((None, d, ff), lambda e, bi, i: (e, 0, 0)),
            pl.BlockSpec((None, d, ff), lambda e, bi, i: (e, 0, 0)),
            pl.BlockSpec((None, ff, d), lambda e, bi, i: (e, 0, 0)),
        ],
        out_specs=pl.BlockSpec((None, None, tc, d), lambda e, bi, i: (bi, e, i, 0)),
        out_shape=jax.ShapeDtypeStruct((b, n_e, cap, d), BF16),
        compiler_params=_cparams("parallel", "parallel", "parallel"),
        name="expert_ffn",
    )(xin, gate, w_gate, w_up, w_down)


SCATTER_GROUP = 64


def _scatter_kernel(rlo_ref, nrd_ref, x_ref, g_ref, ng_ref, idx_ref, y_hbm, o_ref, ybuf, sem, acc_ref,
                    base_ref, *, sr, final):
    bi, i, nt = pl.program_id(0), pl.program_id(1), pl.num_programs(1)
    n_e, n_grp = idx_ref.shape[0], idx_ref.shape[1]
    tm = x_ref.shape[0]
    n_rounds = nrd_ref[bi * nt + i]

    def group(e, k, tile):
        return jnp.minimum(rlo_ref[(bi * n_e + e) * nt + tile] + k, n_grp - 1)

    def slot_tokens(e, k):
        in_range = rlo_ref[(bi * n_e + e) * nt + i] + k < n_grp
        return jnp.where(in_range, idx_ref[e, group(e, k, i)], -1)

    def copies(k, slot, tile):
        return [pltpu.make_async_copy(
            y_hbm.at[bi, e, pl.ds(pl.multiple_of(group(e, k, tile) * sr, sr), sr), :],
            ybuf.at[slot, pl.ds(e * sr, sr), :], sem.at[slot]) for e in range(n_e)]

    @pl.when(i == 0)
    def _():
        base_ref[0] = 0
        for c in copies(0, 0, i):
            c.start()

    base = base_ref[0]
    acc_ref[...] = jnp.zeros(acc_ref.shape, F32)
    per_row = HEAD_DIM // sr
    tok = i * tm + lax.broadcasted_iota(jnp.int32, (tm, HEAD_DIM), 0)

    def body(k, carry):
        slot = (base + k) % 2
        for c in copies(k, slot, i):
            c.wait()

        @pl.when(k + 1 < n_rounds)
        def _():
            for c in copies(k + 1, 1 - slot, i):
                c.start()

        @pl.when(jnp.logical_and(k + 1 == n_rounds, i + 1 < nt))
        def _():
            for c in copies(0, 1 - slot, i + 1):
                c.start()

        rows = [jnp.concatenate([slot_tokens(e + j, k) for j in range(per_row)], axis=1)
                for e in range(0, n_e, per_row)]
        onehot = jnp.concatenate([jnp.where(tok == r, 1.0, 0.0) for r in rows], axis=1)
        acc_ref[...] += jnp.dot(onehot.astype(BF16), ybuf[slot], preferred_element_type=F32)
        return carry

    lax.fori_loop(0, n_rounds, body, 0)
    base_ref[0] = (base + n_rounds) % 2
    x = x_ref[...] + g_ref[...] * acc_ref[...]
    if final:
        x = x * lax.rsqrt(jnp.mean(x * x, axis=-1, keepdims=True) + NORM_EPS) * ng_ref[...]
    o_ref[...] = x


def scatter_combine(x, y, idx, g2, final_g, *, tm=256):
    b, t, d = x.shape
    n_e, cap = idx.shape[1], idx.shape[2]
    tm = min(tm, t)
    sr = min(SCATTER_GROUP, cap)
    assert t % tm == 0 and cap % sr == 0 and HEAD_DIM % sr == 0 and n_e % (HEAD_DIM // sr) == 0
    nt, n_grp = t // tm, cap // sr
    edges = jnp.arange(nt + 1, dtype=idx.dtype) * tm
    pos = jnp.sum(idx[..., None] < edges, axis=2, dtype=jnp.int32)
    lo, hi = pos[..., :-1], pos[..., 1:]
    rlo = jnp.minimum(lo // sr, n_grp - 1)
    rhi = jnp.maximum(hi - 1, lo) // sr
    n_rounds = jnp.max(jnp.minimum(rhi, n_grp - 1) - rlo + 1, axis=1)
    final = final_g is not None
    ng = final_g if final else jnp.ones((1, d), F32)
    tile = pl.BlockSpec((None, tm, d), lambda bi, i, *_: (bi, i, 0))
    return pl.pallas_call(
        functools.partial(_scatter_kernel, sr=sr, final=final),
        grid_spec=pltpu.PrefetchScalarGridSpec(
            num_scalar_prefetch=2,
            grid=(b, nt),
            in_specs=[tile,
                      pl.BlockSpec((None, 1, d), lambda bi, i, *_: (bi, 0, 0)),
                      pl.BlockSpec((1, d), lambda bi, i, *_: (0, 0)),
                      pl.BlockSpec((None, n_e, n_grp, 1, sr), lambda bi, i, *_: (bi, 0, 0, 0, 0)),
                      pl.BlockSpec(memory_space=pl.ANY)],
            out_specs=tile,
            scratch_shapes=[pltpu.VMEM((2, n_e * sr, d), BF16),
                            pltpu.SemaphoreType.DMA((2,)),
                            pltpu.VMEM((tm, d), F32),
                            pltpu.SMEM((1,), jnp.int32)]),
        out_shape=jax.ShapeDtypeStruct((b, t, d), F32),
        compiler_params=_cparams("parallel", "arbitrary"),
        name="scatter_combine",
    )(rlo.reshape(-1), n_rounds.reshape(-1), x, g2, ng, idx.reshape(b, n_e, n_grp, 1, sr), y)


def expert_choice_ffn(x, h2, aff_t, g2, final_g, w_gate, w_up, w_down):
    b, t, d = h2.shape
    cap = EC_CAPACITY * t // N_EXPERTS
    gate, idx = lax.top_k(aff_t, cap)
    idx, gate = lax.sort_key_val(idx, gate, dimension=-1)
    xin = jax.vmap(lambda hb, ib: hb[ib])(h2, idx)
    y = expert_ffn(xin, gate[..., None], w_gate, w_up, w_down)
    return scatter_combine(x, y, idx, g2, final_g)


def _ctx_attn_kernel(scal_ref, q_ref, k_ref, v_ref, g_ref, o_ref, *, kind, post_scale):
    q, k, v = q_ref[...], k_ref[...], v_ref[...]
    if kind == "diff":
        qf = q.astype(F32)
        lane = lax.broadcasted_iota(jnp.int32, qf.shape, 1)

        def probs(keep):
            s = _nt_dot(jnp.where(keep, qf, 0.0).astype(BF16), k)
            e = jnp.exp2(s - jnp.max(s, axis=-1, keepdims=True))
            return e * (1.0 / jnp.sum(e, axis=-1, keepdims=True))

        w = probs(lane < DIFF_DIM) - scal_ref[0] * probs(lane >= DIFF_DIM)
        o = jnp.dot(w.astype(BF16), v, preferred_element_type=F32)
        o = o * lax.rsqrt(jnp.mean(o * o, axis=-1, keepdims=True) + NORM_EPS) * g_ref[...] * post_scale
    else:
        s = _nt_dot(q, k)
        m = jnp.max(s, axis=-1, keepdims=True)
        if kind == "sink":
            sink = scal_ref[pl.program_id(1)]
            m = jnp.maximum(m, sink)
        e = jnp.exp(s - m)
        l = jnp.sum(e, axis=-1, keepdims=True)
        if kind == "sink":
            l = l + jnp.exp(sink - m)
        o = jnp.dot((e * (1.0 / l)).astype(BF16), v, preferred_element_type=F32)
    o_ref[...] = o.astype(o_ref.dtype)


def _context_attention(pc, scal, norm_g, kind, post_scale, q_col, k_col, v_col, grp):
    b, lc, _ = pc.shape
    h = GROUP_BLOCKS
    col = lambda c0, div: (lambda bi, hi: (bi, 0, c0 + hi // div))
    return pl.pallas_call(
        functools.partial(_ctx_attn_kernel, kind=kind, post_scale=post_scale),
        grid=(b, h),
        in_specs=[
            pl.BlockSpec(memory_space=pltpu.SMEM),
            pl.BlockSpec((None, lc, HEAD_DIM), col(q_col, 1)),
            pl.BlockSpec((None, lc, HEAD_DIM), col(k_col, grp)),
            pl.BlockSpec((None, lc, HEAD_DIM), col(v_col, grp)),
            pl.BlockSpec((1, HEAD_DIM), lambda bi, hi: (0, 0)),
        ],
        out_specs=pl.BlockSpec((None, lc, HEAD_DIM), lambda bi, hi: (bi, 0, hi)),
        out_shape=jax.ShapeDtypeStruct((b, lc, h * HEAD_DIM), BF16),
        compiler_params=_cparams("parallel", "parallel"),
        name="context_attention_" + kind,
    )(scal, pc, pc, pc, norm_g)


def _context_mixers(pc, oc_ret, lam, lam_init, diff_norm_g, swa_sink):
    h = GROUP_BLOCKS
    zeros = jnp.zeros((h,), F32)
    ones = jnp.ones((1, HEAD_DIM), F32)
    o_a = _context_attention(pc, zeros, ones, "plain", 1.0, COL_A_Q, COL_A_K, COL_A_V, 1)
    o_b = _context_attention(pc, lam.reshape(1), diff_norm_g.reshape(1, HEAD_DIM).astype(F32), "diff",
                             1.0 - lam_init, COL_B_K - h, COL_B_K, COL_B_K + h, 1)
    o_d = _context_attention(pc, swa_sink.astype(F32), ones, "sink", 1.0, COL_D_Q, COL_D_K, COL_D_V, 2)
    return [o_a, o_b, oc_ret, o_d]


def _layer(x, xc, mod, layer_idx, with_ctx, final_g, rope_h, rope_d, rope_id,
           norm1_g, w_in, na_rpb, diff_lambda, diff_norm_g, ret_decay_fwd, ret_decay_bwd, ret_gn_g, ret_gn_b,
           swa_sink, w_out, norm2_g, w_router, w_gate, w_up, w_down):
    b, n, d = x.shape
    row = lambda v: v.reshape(1, -1).astype(F32)
    part = lambda r0, r1, k: mod[r0:r1, None, k * d:(k + 1) * d]
    sh1, sc1, g1, sh2, sc2, g2 = [part(0, b, k) for k in range(6)]
    ctx_rows = lambda k: jnp.broadcast_to(part(b, b + 1, k), (b, 1, d))
    sh1c, sc1c, g1c, sh2c, sc2c, g2c = [ctx_rows(k) for k in range(6)]

    w_in_b = w_in.astype(BF16)
    w_out_b = w_out.astype(BF16)
    w_router_t = w_router.T.astype(BF16)
    wg_b, wu_b, wd_b = w_gate.astype(BF16), w_up.astype(BF16), w_down.astype(BF16)

    p, t = norm_project(x, row(norm1_g), sc1, sh1, w_in_b, rope_h, rope_d)
    pc, tc = norm_project(xc, row(norm1_g), sc1c, sh1c, w_in_b, rope_id, rope_id)

    lam_init = 0.8 - 0.6 * math.exp(-0.3 * layer_idx)
    lq1, lk1, lq2, lk2 = [diff_lambda[k].astype(F32) for k in range(4)]
    lam = jnp.exp(jnp.sum(lq1 * lk1)) - jnp.exp(jnp.sum(lq2 * lk2)) + lam_init

    o_a = neighbourhood_attention(p, pc, na_rpb)
    o_b = diff_attention(p, t, pc, tc, lam.reshape(1), row(diff_norm_g), 1.0 - lam_init)
    o_c, oc_ret = bidirectional_retention(p, pc, ret_decay_fwd, ret_decay_bwd, ret_gn_g, ret_gn_b, with_ctx)
    o_d = window_attention(p, pc, swa_sink.astype(F32))

    x, h2, aff_t = out_project([o_a, o_b, o_c, o_d], w_out_b, x, g1, row(norm2_g), sc2, sh2, w_router_t)
    x = expert_choice_ffn(x, h2, aff_t, g2, final_g, wg_b, wu_b, wd_b)

    if with_ctx:
        oc = _context_mixers(pc, oc_ret, lam, lam_init, diff_norm_g, swa_sink)
        xc, h2c, aff_c = out_project(oc, w_out_b, xc, g1c, row(norm2_g), sc2c, sh2c, w_router_t)
        xc = expert_choice_ffn(xc, h2c, aff_c, g2c, None, wg_b, wu_b, wd_b)
    return x, xc


def kernel(x, c, ctx, c_ctx, w_mod, b_mod, norm1_g, w_in, na_rpb, diff_lambda, diff_norm_g,
           ret_decay_fwd, ret_decay_bwd, ret_gn_g, ret_gn_b, swa_sink, w_out, norm2_g,
           w_router, w_gate, w_up, w_down, final_norm_g):
    b, n, d = x.shape
    depth = w_in.shape[0]
    lc = ctx.shape[1]
    cvec = jnp.zeros((8, d), F32).at[:b].set(c).at[b].set(c_ctx)
    mod = modulation(cvec, w_mod, b_mod.reshape(depth, 1, -1))
    rope_h = _rope_tables(n, HEAD_DIM)
    rope_d = _rope_tables(n, DIFF_DIM)
    rope_id = _identity_rope_tables(lc)
    xc = ctx
    for li in range(depth):
        last = li == depth - 1
        x, xc = _layer(x, xc, mod[li], li, not last, final_norm_g.reshape(1, -1) if last else None,
                       rope_h, rope_d, rope_id,
                       norm1_g[li], w_in[li], na_rpb[li], diff_lambda[li], diff_norm_g[li],
                       ret_decay_fwd[li], ret_decay_bwd[li], ret_gn_g[li], ret_gn_b[li], swa_sink[li],
                       w_out[li], norm2_g[li], w_router[li], w_gate[li], w_up[li], w_down[li])
    return x
```

```python
import functools
import math

import jax
import jax.numpy as jnp
from jax import lax
from jax.experimental import pallas as pl
from jax.experimental.pallas import tpu as pltpu

GRID_W = 64
HEAD_DIM = 128
DIFF_DIM = HEAD_DIM // 2
NA_WIN_R = 8
NA_WIN_C = 16
RET_CHUNK = 128
SWA_WINDOW = 128
N_EXPERTS = 16
EC_CAPACITY = 2
ROPE_BASE = 10000.0
NORM_EPS = 1e-6
GN_EPS = 1e-5
NEG_INF = -1e30
LOG2E = math.log2(math.e)

BF16 = jnp.bfloat16
F32 = jnp.float32

VMEM_LIMIT_BYTES = 48 * 1024 * 1024

GROUP_BLOCKS = 4
COL_A_Q, COL_A_K, COL_A_V = 0, 4, 8
COL_B_K = 16
COL_C_Q, COL_C_K, COL_C_V, COL_C_G = 24, 28, 32, 36
COL_D_Q, COL_D_K, COL_D_V = 40, 44, 46
PROJ_TN = GROUP_BLOCKS * HEAD_DIM


def _cparams(*sem):
    return pltpu.CompilerParams(dimension_semantics=sem, vmem_limit_bytes=VMEM_LIMIT_BYTES)


def _mod_kernel(c_ref, w_ref, b_ref, o_ref):
    c = c_ref[...]
    s = (c * (1.0 / (1.0 + jnp.exp(-c)))).astype(BF16)
    o_ref[...] = jnp.dot(s, w_ref[...].astype(BF16), preferred_element_type=F32) + b_ref[...]


def modulation(cvec, w_mod, b_mod, *, tn=1024):
    depth, d, n6 = w_mod.shape
    return pl.pallas_call(
        _mod_kernel,
        grid=(depth, n6 // tn),
        in_specs=[
            pl.BlockSpec((8, d), lambda l, j: (0, 0)),
            pl.BlockSpec((None, d, tn), lambda l, j: (l, 0, j)),
            pl.BlockSpec((None, 1, tn), lambda l, j: (l, 0, j)),
        ],
        out_specs=pl.BlockSpec((None, 8, tn), lambda l, j: (l, 0, j)),
        out_shape=jax.ShapeDtypeStruct((depth, 8, n6), F32),
        compiler_params=_cparams("parallel", "parallel"),
        name="modulation",
    )(cvec, w_mod, b_mod)


def _rope_tables(n_tok, dim):
    t = jnp.arange(n_tok)
    row = (t // GRID_W).astype(F32)[:, None]
    col = (t % GRID_W).astype(F32)[:, None]
    nf = dim // 4
    lane = jnp.arange(HEAD_DIM)
    quarter = (lane % dim) // nf
    inv = ROPE_BASE ** (-jnp.arange(nf, dtype=F32) / nf)
    ang = jnp.where(quarter[None, :] < 2, row, col) * inv[lane % nf][None, :]
    cos, sin = jnp.cos(ang), jnp.sin(ang)
    even = (quarter % 2 == 0)[None, :]
    return cos, jnp.where(even, -sin, 0.0), jnp.where(even, 0.0, sin)


def _identity_rope_tables(n_tok):
    z = jnp.zeros((n_tok, HEAD_DIM), F32)
    return jnp.ones((n_tok, HEAD_DIM), F32), z, z


def _proj_modes():
    att = HEAD_DIM ** -0.5
    plain = [(None, 1.0)] * GROUP_BLOCKS
    return [
        ([(None, att)] * 4, None), (plain, None), (plain, None),
        ([("d", LOG2E * DIFF_DIM ** -0.5)] * 4, 0), ([("d", 1.0)] * 4, None), (plain, 1),
        ([("h", 1.0)] * 4, None), ([("h", att)] * 4, None), (plain, None), (plain, None),
        ([("h", att)] * 4, None), ([("h", 1.0)] * 2 + [(None, 1.0)] * 2, None),
    ]


def _proj_kernel(x_ref, g_ref, sc_ref, sh_ref, w_ref, ch_ref, sah_ref, sbh_ref, cd_ref, sad_ref, sbd_ref,
                 o_ref, t_ref):
    x = x_ref[...]
    y = x * lax.rsqrt(jnp.mean(x * x, axis=-1, keepdims=True) + NORM_EPS) * g_ref[...]
    h = (y * (1.0 + sc_ref[...]) + sh_ref[...]).astype(BF16)

    def rope(a, kind):
        if kind is None:
            return a
        c, sa, sb, sh = ((ch_ref, sah_ref, sbh_ref, HEAD_DIM // 4) if kind == "h"
                         else (cd_ref, sad_ref, sbd_ref, DIFF_DIM // 4))
        return (a * c[...] + pltpu.roll(a, HEAD_DIM - sh, 1) * sa[...] + pltpu.roll(a, sh, 1) * sb[...])

    for j, (blocks, t_slot) in enumerate(_proj_modes()):
        cols = slice(j * PROJ_TN, (j + 1) * PROJ_TN)
        acc = jnp.dot(h, w_ref[:, cols], preferred_element_type=F32)
        outs = []
        for hb, (kind, scale) in enumerate(blocks):
            a = rope(acc[:, hb * HEAD_DIM:(hb + 1) * HEAD_DIM], kind)
            outs.append(a if scale == 1.0 else a * scale)
        full = jnp.concatenate(outs, axis=1)
        o_ref[:, cols] = full.astype(o_ref.dtype)
        if t_slot is not None:
            t_ref[t_slot] = full.T.astype(t_ref.dtype)


def norm_project(x, norm_g, scale, shift, w_in, rope_h, rope_d, *, tm=256):
    b, n, d = x.shape
    width = w_in.shape[1]
    assert width == len(_proj_modes()) * PROJ_TN
    tm = min(tm, n)
    tok = lambda bi, i: (i, 0)
    return pl.pallas_call(
        _proj_kernel,
        grid=(b, n // tm),
        in_specs=[
            pl.BlockSpec((None, tm, d), lambda bi, i: (bi, i, 0)),
            pl.BlockSpec((1, d), lambda bi, i: (0, 0)),
            pl.BlockSpec((None, 1, d), lambda bi, i: (bi, 0, 0)),
            pl.BlockSpec((None, 1, d), lambda bi, i: (bi, 0, 0)),
            pl.BlockSpec((d, width), lambda bi, i: (0, 0), pipeline_mode=pl.Buffered(1)),
        ] + [pl.BlockSpec((tm, HEAD_DIM), tok)] * 6,
        out_specs=[
            pl.BlockSpec((None, tm, width), lambda bi, i: (bi, i, 0)),
            pl.BlockSpec((None, 2, PROJ_TN, tm), lambda bi, i: (bi, 0, 0, i)),
        ],
        out_shape=[jax.ShapeDtypeStruct((b, n, width), BF16),
                   jax.ShapeDtypeStruct((b, 2, PROJ_TN, n), BF16)],
        compiler_params=_cparams("parallel", "parallel"),
        name="norm_project",
    )(x, norm_g, scale, shift, w_in, *rope_h, *rope_d)


def _softmax_pv(s_loc, s_ctx, vw, vc, sink):
    m = jnp.maximum(jnp.max(s_loc, axis=-1, keepdims=True), jnp.max(s_ctx, axis=-1, keepdims=True))
    if sink is not None:
        m = jnp.maximum(m, sink)
    p_loc = jnp.exp(s_loc - m)
    p_ctx = jnp.exp(s_ctx - m)
    l = jnp.sum(p_loc, axis=-1, keepdims=True) + jnp.sum(p_ctx, axis=-1, keepdims=True)
    if sink is not None:
        l = l + jnp.exp(sink - m)
    o = (jnp.dot(p_loc.astype(BF16), vw, preferred_element_type=F32)
         + jnp.dot(p_ctx.astype(BF16), vc, preferred_element_type=F32))
    return o * (1.0 / l)


def _nt_dot(a, b):
    return lax.dot_general(a, b, (((1,), (1,)), ((), ())), preferred_element_type=F32)


LOCAL_SUB_BLOCKS = 4


def _na_kernel(q_ref, k_ref, v_ref, kc_ref, vc_ref, bias_ref, o_ref, *, tq, wk, n_blk):
    n = k_ref.shape[0]
    rows_q = tq // GRID_W
    for sub in range(q_ref.shape[0] // tq):
        blk = pl.program_id(2) * (q_ref.shape[0] // tq) + sub
        kstart = jnp.clip(blk * rows_q - NA_WIN_R // 2, 0, (n - wk) // GRID_W) * GRID_W
        kstart = pl.multiple_of(kstart, GRID_W)
        cls = jnp.where(blk == 0, 0, jnp.where(blk == n_blk - 1, 2, 1))
        q = q_ref[sub * tq:(sub + 1) * tq, :]
        kw = k_ref[pl.ds(kstart, wk), :]
        vw = v_ref[pl.ds(kstart, wk), :]
        s_loc = _nt_dot(q, kw) + bias_ref[cls]
        s_ctx = _nt_dot(q, kc_ref[...])
        o_ref[sub * tq:(sub + 1) * tq, :] = _softmax_pv(s_loc, s_ctx, vw, vc_ref[...], None).astype(o_ref.dtype)


def _na_bias(rpb, rows, rows_q, rows_k):
    n_blk = rows // rows_q
    kr = NA_WIN_R
    cols = jnp.arange(GRID_W)
    col_start = jnp.clip(cols - NA_WIN_C // 2, 0, GRID_W - NA_WIN_C)
    col_ok = (cols[None, :] >= col_start[:, None]) & (cols[None, :] < col_start[:, None] + NA_WIN_C)
    col_off = jnp.clip(cols[None, :] - cols[:, None] + NA_WIN_C - 1, 0, 2 * NA_WIN_C - 2)
    out = []
    for blk in (0, 1, n_blk - 1):
        r = blk * rows_q + jnp.arange(rows_q)
        kstart = min(max(blk * rows_q - kr // 2, 0), rows - rows_k)
        krow = kstart + jnp.arange(rows_k)
        start = jnp.clip(r - kr // 2, 0, rows - kr)
        row_ok = (krow[None, :] >= start[:, None]) & (krow[None, :] < start[:, None] + kr)
        row_off = jnp.clip(krow[None, :] - r[:, None] + NA_WIN_R - 1, 0, 2 * NA_WIN_R - 2)
        bias = jnp.einsum('abr,hrc,qkc->habqk', jax.nn.one_hot(row_off, 2 * NA_WIN_R - 1, dtype=F32),
                          rpb.astype(F32), jax.nn.one_hot(col_off, 2 * NA_WIN_C - 1, dtype=F32),
                          precision=lax.Precision.HIGHEST)
        ok = row_ok[:, :, None, None] & col_ok[None, None]
        bias = jnp.where(ok[None], bias, NEG_INF).transpose(0, 1, 3, 2, 4)
        out.append(bias.reshape(rpb.shape[0], rows_q * GRID_W, rows_k * GRID_W))
    return jnp.stack(out, axis=1)


def neighbourhood_attention(p, pc, rpb, *, rows_q=4):
    b, n, _ = p.shape
    lc = pc.shape[1]
    h = rpb.shape[0]
    rows = n // GRID_W
    rows_k = rows_q + NA_WIN_R - 1
    assert rows % rows_q == 0 and rows >= rows_k and rows_q >= NA_WIN_R // 2
    tq, wk = rows_q * GRID_W, rows_k * GRID_W
    n_blk = rows // rows_q
    sub = LOCAL_SUB_BLOCKS if n_blk % LOCAL_SUB_BLOCKS == 0 else 1
    bias = _na_bias(rpb, rows, rows_q, rows_k)
    kern = functools.partial(_na_kernel, tq=tq, wk=wk, n_blk=n_blk)
    return pl.pallas_call(
        kern,
        grid=(b, h, n_blk // sub),
        in_specs=[
            pl.BlockSpec((None, sub * tq, HEAD_DIM), lambda bi, hi, i: (bi, i, COL_A_Q + hi)),
            pl.BlockSpec((None, n, HEAD_DIM), lambda bi, hi, i: (bi, 0, COL_A_K + hi)),
            pl.BlockSpec((None, n, HEAD_DIM), lambda bi, hi, i: (bi, 0, COL_A_V + hi)),
            pl.BlockSpec((None, lc, HEAD_DIM), lambda bi, hi, i: (bi, 0, COL_A_K + hi)),
            pl.BlockSpec((None, lc, HEAD_DIM), lambda bi, hi, i: (bi, 0, COL_A_V + hi)),
            pl.BlockSpec((None, 3, tq, wk), lambda bi, hi, i: (hi, 0, 0, 0)),
        ],
        out_specs=pl.BlockSpec((None, sub * tq, HEAD_DIM), lambda bi, hi, i: (bi, i, hi)),
        out_shape=jax.ShapeDtypeStruct((b, n, h * HEAD_DIM), BF16),
        compiler_params=_cparams("parallel", "parallel", "arbitrary"),
        name="neighbourhood_attention",
    )(p, p, p, pc, pc, bias)


def _swa_kernel(sink_ref, q_ref, k_ref, v_ref, kc_ref, vc_ref, o_ref, *, tq, wk):
    hi = pl.program_id(1)
    n = k_ref.shape[0]
    for sub in range(q_ref.shape[0] // tq):
        q0 = pl.program_id(2) * q_ref.shape[0] + sub * tq
        kstart = pl.multiple_of(jnp.clip(q0 - SWA_WINDOW, 0, n - wk), SWA_WINDOW)
        q = q_ref[sub * tq:(sub + 1) * tq, :]
        kw = k_ref[pl.ds(kstart, wk), :]
        vw = v_ref[pl.ds(kstart, wk), :]
        qpos = q0 + lax.broadcasted_iota(jnp.int32, (tq, wk), 0)
        kpos = kstart + lax.broadcasted_iota(jnp.int32, (tq, wk), 1)
        s_loc = jnp.where(jnp.abs(kpos - qpos) <= SWA_WINDOW, _nt_dot(q, kw), NEG_INF)
        s_ctx = _nt_dot(q, kc_ref[...])
        o = _softmax_pv(s_loc, s_ctx, vw, vc_ref[...], sink_ref[hi])
        o_ref[sub * tq:(sub + 1) * tq, :] = o.astype(o_ref.dtype)


def window_attention(p, pc, sink, *, tq=256):
    b, n, _ = p.shape
    lc = pc.shape[1]
    hq = sink.shape[0]
    grp = 2
    tq = min(tq, n)
    wk = min(tq + 2 * SWA_WINDOW, n)
    sub = LOCAL_SUB_BLOCKS if (n // tq) % LOCAL_SUB_BLOCKS == 0 else 1
    kern = functools.partial(_swa_kernel, tq=tq, wk=wk)
    return pl.pallas_call(
        kern,
        grid=(b, hq, n // (sub * tq)),
        in_specs=[
            pl.BlockSpec(memory_space=pltpu.SMEM),
            pl.BlockSpec((None, sub * tq, HEAD_DIM), lambda bi, hi, i: (bi, i, COL_D_Q + hi)),
            pl.BlockSpec((None, n, HEAD_DIM), lambda bi, hi, i: (bi, 0, COL_D_K + hi // grp)),
            pl.BlockSpec((None, n, HEAD_DIM), lambda bi, hi, i: (bi, 0, COL_D_V + hi // grp)),
            pl.BlockSpec((None, lc, HEAD_DIM), lambda bi, hi, i: (bi, 0, COL_D_K + hi // grp)),
            pl.BlockSpec((None, lc, HEAD_DIM), lambda bi, hi, i: (bi, 0, COL_D_V + hi // grp)),
        ],
        out_specs=pl.BlockSpec((None, sub * tq, HEAD_DIM), lambda bi, hi, i: (bi, i, hi)),
        out_shape=jax.ShapeDtypeStruct((b, n, hq * HEAD_DIM), BF16),
        compiler_params=_cparams("parallel", "parallel", "arbitrary"),
        name="window_attention",
    )(sink, p, p, p, pc, pc)


DIFF_SLOTS = 2
DIFF_UNROLL = 6
DIFF_AUG_ROWS = 16


def _diff_attn_kernel(lam_ref, qt_ref, k_ref, vt_ref, kc_ref, vct_ref, g_ref, o_ref,
                      qbd_ref, m_ref, acc_ref, s_scr, x_scr, *, tq, tk, post_scale):
    qt = qt_ref[...].astype(F32)
    row = lax.broadcasted_iota(jnp.int32, qt.shape, 0)
    zero = jnp.zeros_like(qt)
    qbd_ref[:, :tq] = jnp.where(row < DIFF_DIM, qt, zero).astype(BF16)
    qbd_ref[:, tq:] = jnp.where(row >= DIFF_DIM, qt, zero).astype(BF16)

    def aug(vt_tile):
        r = lax.broadcasted_iota(jnp.int32, (DIFF_AUG_ROWS, vt_tile.shape[1]), 0)
        return jnp.concatenate([vt_tile, jnp.where(r == 0, 1.0, 0.0).astype(BF16)], axis=0)

    s = jnp.dot(kc_ref[...], qbd_ref[...], preferred_element_type=F32)
    m0 = jnp.max(s, axis=0, keepdims=True)
    m_ref[...] = m0
    acc_ref[...] = jnp.dot(aug(vct_ref[...]), jnp.exp2(s - m0).astype(BF16), preferred_element_type=F32)

    def scores(t, slot):
        off = pl.multiple_of(t * tk, tk)
        s = jnp.dot(k_ref[pl.ds(off, tk), :], qbd_ref[...], preferred_element_type=F32)
        s_scr[slot] = s
        x_scr[slot] = jnp.max(s, axis=0, keepdims=True)

    def accumulate(t, slot):
        m_old = m_ref[...]
        m_new = jnp.maximum(m_old, x_scr[slot])
        alpha = jnp.exp2(m_old - m_new)
        p = jnp.exp2(s_scr[slot] - m_new).astype(BF16)
        m_ref[...] = m_new
        off = pl.multiple_of(t * tk, tk)
        pv = jnp.dot(aug(vt_ref[:, pl.ds(off, tk)]), p, preferred_element_type=F32)
        acc_ref[...] = alpha * acc_ref[...] + pv

    def stage(t, t_mod, do_scores=True):
        if do_scores:
            scores(t + 1, (t_mod + 1) % DIFF_SLOTS)
        accumulate(t, t_mod % DIFF_SLOTS)

    n_kt = k_ref.shape[0] // tk
    trips = (n_kt - 1) // DIFF_UNROLL
    scores(0, 0)

    def body(i, carry):
        for u in range(DIFF_UNROLL):
            stage(DIFF_UNROLL * i + u, u)
        return carry

    lax.fori_loop(0, trips, body, 0)
    for t in range(DIFF_UNROLL * trips, n_kt - 1):
        stage(t, t)
    stage(n_kt - 1, n_kt - 1, do_scores=False)

    lam = lam_ref[0]
    inv = 1.0 / acc_ref[HEAD_DIM:HEAD_DIM + 1, :]
    acc = acc_ref[:HEAD_DIM, :]
    o_t = acc[:, :tq] * inv[:, :tq] - lam * (acc[:, tq:] * inv[:, tq:])
    o = o_t.T
    ms = jnp.mean(o * o, axis=-1, keepdims=True)
    y = o * lax.rsqrt(ms + NORM_EPS) * g_ref[...]
    o_ref[...] = (y * post_scale).astype(o_ref.dtype)


def diff_attention(p, t, pc, tc, lam, norm_g, post_scale, *, tq=512, tk=512):
    b, n, _ = p.shape
    lc = pc.shape[1]
    h = GROUP_BLOCKS
    tq = min(tq, n)
    tk = min(tk, n)
    assert n % tq == 0 and n % tk == 0 and DIFF_UNROLL % DIFF_SLOTS == 0
    kern = functools.partial(_diff_attn_kernel, tq=tq, tk=tk, post_scale=post_scale)
    return pl.pallas_call(
        kern,
        grid=(b, h, n // tq),
        in_specs=[
            pl.BlockSpec(memory_space=pltpu.SMEM),
            pl.BlockSpec((None, None, HEAD_DIM, tq), lambda bi, hi, qi: (bi, 0, hi, qi)),
            pl.BlockSpec((None, n, HEAD_DIM), lambda bi, hi, qi: (bi, 0, COL_B_K + hi)),
            pl.BlockSpec((None, None, HEAD_DIM, n), lambda bi, hi, qi: (bi, 1, hi, 0)),
            pl.BlockSpec((None, lc, HEAD_DIM), lambda bi, hi, qi: (bi, 0, COL_B_K + hi)),
            pl.BlockSpec((None, None, HEAD_DIM, lc), lambda bi, hi, qi: (bi, 1, hi, 0)),
            pl.BlockSpec((1, HEAD_DIM), lambda bi, hi, qi: (0, 0)),
        ],
        out_specs=pl.BlockSpec((None, tq, HEAD_DIM), lambda bi, hi, qi: (bi, qi, hi)),
        out_shape=jax.ShapeDtypeStruct((b, n, h * HEAD_DIM), BF16),
        scratch_shapes=[
            pltpu.VMEM((HEAD_DIM, 2 * tq), BF16),
            pltpu.VMEM((1, 2 * tq), F32),
            pltpu.VMEM((HEAD_DIM + DIFF_AUG_ROWS, 2 * tq), F32),
            pltpu.VMEM((DIFF_SLOTS, tk, 2 * tq), F32),
            pltpu.VMEM((DIFF_SLOTS, 1, 2 * tq), F32),
        ],
        compiler_params=_cparams("parallel", "parallel", "arbitrary"),
        name="diff_attention",
    )(lam, t, p, t, pc, tc, norm_g)


def _retention_kernel(cdec_ref, q_ref, k_ref, v_ref, intra_ref, qdec_ref, kdec_ref, s0_ref, *rest,
                      reverse, final):
    if final:
        of_ref, gate_ref, gng_ref, gnb_ref, o_ref, sfin_ref, s_ref = rest
    else:
        o_ref, sfin_ref, s_ref = rest
    i = pl.program_id(1)
    n_heads = s_ref.shape[0]
    n_chunks = q_ref.shape[0] // RET_CHUNK

    @pl.when(i == 0)
    def _():
        s_ref[...] = s0_ref[...]

    order = range(n_chunks - 1, -1, -1) if reverse else range(n_chunks)
    for c in order:
        rs = slice(c * RET_CHUNK, (c + 1) * RET_CHUNK)
        for h in range(n_heads):
            cs = slice(h * HEAD_DIM, (h + 1) * HEAD_DIM)
            qh, kh, vh = q_ref[rs, cs], k_ref[rs, cs], v_ref[rs, cs]
            qd = (qh.astype(F32) * qdec_ref[h]).astype(BF16)
            kd_t = (kh.astype(F32) * kdec_ref[h]).T.astype(BF16)
            a = (_nt_dot(qh, kh) * intra_ref[h]).astype(BF16)
            s = s_ref[h]
            o = (jnp.dot(a, vh, preferred_element_type=F32)
                 + jnp.dot(qd, s.astype(BF16), preferred_element_type=F32))
            s_ref[h] = cdec_ref[h] * s + jnp.dot(kd_t, vh, preferred_element_type=F32)
            if final:
                o = o + of_ref[rs, cs]
                mu = jnp.mean(o, axis=-1, keepdims=True)
                var = jnp.mean(jnp.square(o - mu), axis=-1, keepdims=True)
                y = (o - mu) * lax.rsqrt(var + GN_EPS) * gng_ref[:, cs] + gnb_ref[:, cs]
                g = gate_ref[rs, cs].astype(F32)
                o = g * (1.0 / (1.0 + jnp.exp(-g))) * y
            o_ref[rs, cs] = o.astype(o_ref.dtype)

    @pl.when(i == pl.num_programs(1) - 1)
    def _():
        sfin_ref[...] = s_ref[...]


def _retention_pass(p, tables, s0, final_inputs, *, reverse, blk_chunks=4):
    b, n, _ = p.shape
    cdec, intra, qdec, kdec = tables
    h = intra.shape[0]
    width = h * HEAD_DIM
    n_chunks = n // RET_CHUNK
    blk_chunks = min(blk_chunks, n_chunks)
    assert n_chunks % blk_chunks == 0
    tb = blk_chunks * RET_CHUNK
    n_blk = n // tb
    pos = (lambda i: n_blk - 1 - i) if reverse else (lambda i: i)
    colblk = lambda c: (lambda bi, i: (bi, pos(i), c // GROUP_BLOCKS))
    tab = pl.BlockSpec((h, RET_CHUNK, HEAD_DIM), lambda bi, i: (0, 0, 0))
    state = pl.BlockSpec((None, h, HEAD_DIM, HEAD_DIM), lambda bi, i: (bi, 0, 0, 0))
    in_specs = [pl.BlockSpec(memory_space=pltpu.SMEM),
                pl.BlockSpec((None, tb, width), colblk(COL_C_Q)),
                pl.BlockSpec((None, tb, width), colblk(COL_C_K)),
                pl.BlockSpec((None, tb, width), colblk(COL_C_V)),
                tab, tab, tab, state]
    args = [cdec, p, p, p, intra, qdec, kdec, s0]
    final = final_inputs is not None
    if final:
        o_fwd, gn_g, gn_b = final_inputs
        in_specs += [pl.BlockSpec((None, tb, width), lambda bi, i: (bi, pos(i), 0)),
                     pl.BlockSpec((None, tb, width), colblk(COL_C_G)),
                     pl.BlockSpec((1, width), lambda bi, i: (0, 0)),
                     pl.BlockSpec((1, width), lambda bi, i: (0, 0))]
        args += [o_fwd, p, gn_g, gn_b]
    kern = functools.partial(_retention_kernel, reverse=reverse, final=final)
    return pl.pallas_call(
        kern,
        grid=(b, n_blk),
        in_specs=in_specs,
        out_specs=[pl.BlockSpec((None, tb, width), lambda bi, i: (bi, pos(i), 0)), state],
        out_shape=[jax.ShapeDtypeStruct((b, n, width), BF16 if final else F32),
                   jax.ShapeDtypeStruct((b, h, HEAD_DIM, HEAD_DIM), F32)],
        scratch_shapes=[pltpu.VMEM((h, HEAD_DIM, HEAD_DIM), F32)],
        compiler_params=_cparams("parallel", "arbitrary"),
        name="retention_bwd" if reverse else "retention_fwd",
    )(*args)


def _retention_tables(decay, reverse):
    lg = jax.nn.log_sigmoid(decay.astype(F32))[:, None, None]
    pos = jnp.arange(RET_CHUNK, dtype=F32)
    rel = pos[:, None] - pos[None, :]
    if reverse:
        rel = -rel
        q_pow, k_pow = RET_CHUNK - pos, pos
    else:
        q_pow, k_pow = pos + 1.0, RET_CHUNK - 1.0 - pos
    intra = jnp.where(rel >= 0, jnp.exp(jnp.maximum(rel, 0.0) * lg), 0.0)
    bc = lambda e: jnp.broadcast_to(jnp.exp(e[None, :, None] * lg), intra.shape)
    cdec = jnp.exp(RET_CHUNK * lg[:, 0, 0])
    return cdec, intra, bc(q_pow), bc(k_pow)


def bidirectional_retention(p, pc, decay_f, decay_b, gn_g, gn_b, with_ctx):
    b = p.shape[0]
    h = decay_f.shape[0]
    tf = _retention_tables(decay_f, False)
    tb = _retention_tables(decay_b, True)
    s0 = jnp.zeros((b, h, HEAD_DIM, HEAD_DIM), F32)
    gn = (gn_g.reshape(1, -1).astype(F32), gn_b.reshape(1, -1).astype(F32))
    oc_f, s_f = _retention_pass(pc, tf, s0, None, reverse=False)
    oc, s_b = _retention_pass(pc, tb, s0, (oc_f,) + gn, reverse=True)
    o_f, _ = _retention_pass(p, tf, s_f, None, reverse=False)
    o, _ = _retention_pass(p, tb, s_b, (o_f,) + gn, reverse=True)
    return o, (oc if with_ctx else None)


OUT_SUB_ROWS = 256


def _out_kernel(oa_ref, ob_ref, oc_ref, od_ref, w_ref, x_ref, g1_ref, ng_ref, sc_ref, sh_ref, wr_ref,
                xo_ref, h_ref, aff_ref):
    gw = oa_ref.shape[1]
    rows = min(OUT_SUB_ROWS, x_ref.shape[0])
    for sub in range(x_ref.shape[0] // rows):
        rs = slice(sub * rows, (sub + 1) * rows)
        acc = jnp.dot(oa_ref[rs, :], w_ref[0 * gw:1 * gw, :], preferred_element_type=F32)
        acc += jnp.dot(ob_ref[rs, :], w_ref[1 * gw:2 * gw, :], preferred_element_type=F32)
        acc += jnp.dot(oc_ref[rs, :], w_ref[2 * gw:3 * gw, :], preferred_element_type=F32)
        acc += jnp.dot(od_ref[rs, :], w_ref[3 * gw:4 * gw, :], preferred_element_type=F32)
        x = x_ref[rs, :] + g1_ref[...] * acc
        xo_ref[rs, :] = x
        y = x * lax.rsqrt(jnp.mean(x * x, axis=-1, keepdims=True) + NORM_EPS) * ng_ref[...]
        h2 = (y * (1.0 + sc_ref[...]) + sh_ref[...]).astype(BF16)
        h_ref[rs, :] = h2
        logits = _nt_dot(wr_ref[...], h2)
        e = jnp.exp(logits - jnp.max(logits, axis=0, keepdims=True))
        aff_ref[:, rs] = e * (1.0 / jnp.sum(e, axis=0, keepdims=True))


def out_project(o_groups, w_out, x, g1, norm_g, scale, shift, w_router_t, *, tm=2 * OUT_SUB_ROWS):
    b, n, d = x.shape
    n_e = w_router_t.shape[0]
    gw = o_groups[0].shape[-1]
    tm = min(tm, n)
    tile = lambda w: pl.BlockSpec((None, tm, w), lambda bi, i: (bi, i, 0))
    vec = pl.BlockSpec((None, 1, d), lambda bi, i: (bi, 0, 0))
    return pl.pallas_call(
        _out_kernel,
        grid=(b, n // tm),
        in_specs=[tile(gw)] * 4 + [
            pl.BlockSpec((d, d), lambda bi, i: (0, 0), pipeline_mode=pl.Buffered(1)),
            tile(d), vec,
            pl.BlockSpec((1, d), lambda bi, i: (0, 0)),
            vec, vec,
            pl.BlockSpec((n_e, d), lambda bi, i: (0, 0)),
        ],
        out_specs=[tile(d), tile(d), pl.BlockSpec((None, n_e, tm), lambda bi, i: (bi, 0, i))],
        out_shape=[jax.ShapeDtypeStruct((b, n, d), F32),
                   jax.ShapeDtypeStruct((b, n, d), BF16),
                   jax.ShapeDtypeStruct((b, n_e, n), F32)],
        compiler_params=_cparams("parallel", "parallel"),
        name="out_project",
    )(*o_groups, w_out, x, g1, norm_g, scale, shift, w_router_t)


def _ffn_kernel(x_ref, gate_ref, wg_ref, wu_ref, wd_ref, o_ref):
    x = x_ref[...]
    a = jnp.dot(x, wg_ref[...], preferred_element_type=F32)
    u = jnp.dot(x, wu_ref[...], preferred_element_type=F32)
    hm = (a * (1.0 / (1.0 + jnp.exp(-a))) * u).astype(BF16)
    y = jnp.dot(hm, wd_ref[...], preferred_element_type=F32) * gate_ref[...]
    o_ref[...] = y.astype(o_ref.dtype)


def expert_ffn(xin, gate, w_gate, w_up, w_down, *, tc=256):
    b, n_e, cap, d = xin.shape
    ff = w_gate.shape[-1]
    tc = min(tc, cap)
    return pl.pallas_call(
        _ffn_kernel,
        grid=(n_e, b, cap // tc),
        in_specs=[
            pl.BlockSpec((None, None, tc, d), lambda e, bi, i: (bi, e, i, 0)),
            pl.BlockSpec((None, None, tc, 1), lambda e, bi, i: (bi, e, i, 0)),
            pl.BlockSpec((None, d, ff), lambda e, bi, i: (e, 0, 0)),
            pl.BlockSpec((None, d, ff), lambda e, bi, i: (e, 0, 0)),
            pl.BlockSpec((None, ff, d), lambda e, bi, i: (e, 0, 0)),
        ],
        out_specs=pl.BlockSpec((None, None, tc, d), lambda e, bi, i: (bi, e, i, 0)),
        out_shape=jax.ShapeDtypeStruct((b, n_e, cap, d), BF16),
        compiler_params=_cparams("parallel", "parallel", "parallel"),
        name="expert_ffn",
    )(xin, gate, w_gate, w_up, w_down)


SCATTER_GROUP = 64


def _scatter_kernel(rlo_ref, nrd_ref, x_ref, g_ref, ng_ref, idx_ref, y_hbm, o_ref, ybuf, sem, acc_ref,
                    base_ref, *, sr, final):
    bi, i, nt = pl.program_id(0), pl.program_id(1), pl.num_programs(1)
    n_e, n_grp = idx_ref.shape[0], idx_ref.shape[1]
    tm = x_ref.shape[0]
    n_rounds = nrd_ref[bi * nt + i]

    def group(e, k, tile):
        return jnp.minimum(rlo_ref[(bi * n_e + e) * nt + tile] + k, n_grp - 1)

    def slot_tokens(e, k):
        in_range = rlo_ref[(bi * n_e + e) * nt + i] + k < n_grp
        return jnp.where(in_range, idx_ref[e, group(e, k, i)], -1)

    def copies(k, slot, tile):
        return [pltpu.make_async_copy(
            y_hbm.at[bi, e, pl.ds(pl.multiple_of(group(e, k, tile) * sr, sr), sr), :],
            ybuf.at[slot, pl.ds(e * sr, sr), :], sem.at[slot]) for e in range(n_e)]

    @pl.when(i == 0)
    def _():
        base_ref[0] = 0
        for c in copies(0, 0, i):
            c.start()

    base = base_ref[0]
    acc_ref[...] = jnp.zeros(acc_ref.shape, F32)
    per_row = HEAD_DIM // sr
    tok = i * tm + lax.broadcasted_iota(jnp.int32, (tm, HEAD_DIM), 0)

    def body(k, carry):
        slot = (base + k) % 2
        for c in copies(k, slot, i):
            c.wait()

        @pl.when(k + 1 < n_rounds)
        def _():
            for c in copies(k + 1, 1 - slot, i):
                c.start()

        @pl.when(jnp.logical_and(k + 1 == n_rounds, i + 1 < nt))
        def _():
            for c in copies(0, 1 - slot, i + 1):
                c.start()

        rows = [jnp.concatenate([slot_tokens(e + j, k) for j in range(per_row)], axis=1)
                for e in range(0, n_e, per_row)]
        onehot = jnp.concatenate([jnp.where(tok == r, 1.0, 0.0) for r in rows], axis=1)
        acc_ref[...] += jnp.dot(onehot.astype(BF16), ybuf[slot], preferred_element_type=F32)
        return carry

    lax.fori_loop(0, n_rounds, body, 0)
    base_ref[0] = (base + n_rounds) % 2
    x = x_ref[...] + g_ref[...] * acc_ref[...]
    if final:
        x = x * lax.rsqrt(jnp.mean(x * x, axis=-1, keepdims=True) + NORM_EPS) * ng_ref[...]
    o_ref[...] = x


def scatter_combine(x, y, idx, g2, final_g, *, tm=256):
    b, t, d = x.shape
    n_e, cap = idx.shape[1], idx.shape[2]
    tm = min(tm, t)
    sr = min(SCATTER_GROUP, cap)
    assert t % tm == 0 and cap % sr == 0 and HEAD_DIM % sr == 0 and n_e % (HEAD_DIM // sr) == 0
    nt, n_grp = t // tm, cap // sr
    edges = jnp.arange(nt + 1, dtype=idx.dtype) * tm
    pos = jnp.sum(idx[..., None] < edges, axis=2, dtype=jnp.int32)
    lo, hi = pos[..., :-1], pos[..., 1:]
    rlo = jnp.minimum(lo // sr, n_grp - 1)
    rhi = jnp.maximum(hi - 1, lo) // sr
    n_rounds = jnp.max(jnp.minimum(rhi, n_grp - 1) - rlo + 1, axis=1)
    final = final_g is not None
    ng = final_g if final else jnp.ones((1, d), F32)
    tile = pl.BlockSpec((None, tm, d), lambda bi, i, *_: (bi, i, 0))
    return pl.pallas_call(
        functools.partial(_scatter_kernel, sr=sr, final=final),
        grid_spec=pltpu.PrefetchScalarGridSpec(
            num_scalar_prefetch=2,
            grid=(b, nt),
            in_specs=[tile,
                      pl.BlockSpec((None, 1, d), lambda bi, i, *_: (bi, 0, 0)),
                      pl.BlockSpec((1, d), lambda bi, i, *_: (0, 0)),
                      pl.BlockSpec((None, n_e, n_grp, 1, sr), lambda bi, i, *_: (bi, 0, 0, 0, 0)),
                      pl.BlockSpec(memory_space=pl.ANY)],
            out_specs=tile,
            scratch_shapes=[pltpu.VMEM((2, n_e * sr, d), BF16),
                            pltpu.SemaphoreType.DMA((2,)),
                            pltpu.VMEM((tm, d), F32),
                            pltpu.SMEM((1,), jnp.int32)]),
        out_shape=jax.ShapeDtypeStruct((b, t, d), F32),
        compiler_params=_cparams("parallel", "arbitrary"),
        name="scatter_combine",
    )(rlo.reshape(-1), n_rounds.reshape(-1), x, g2, ng, idx.reshape(b, n_e, n_grp, 1, sr), y)


def expert_choice_ffn(x, h2, aff_t, g2, final_g, w_gate, w_up, w_down):
    b, t, d = h2.shape
    cap = EC_CAPACITY * t // N_EXPERTS
    gate, idx = lax.top_k(aff_t, cap)
    idx, gate = lax.sort_key_val(idx, gate, dimension=-1)
    xin = jax.vmap(lambda hb, ib: hb[ib])(h2, idx)
    y = expert_ffn(xin, gate[..., None], w_gate, w_up, w_down)
    return scatter_combine(x, y, idx, g2, final_g)


def _ctx_attn_kernel(scal_ref, q_ref, k_ref, v_ref, g_ref, o_ref, *, kind, post_scale):
    q, k, v = q_ref[...], k_ref[...], v_ref[...]
    if kind == "diff":
        qf = q.astype(F32)
        lane = lax.broadcasted_iota(jnp.int32, qf.shape, 1)

        def probs(keep):
            s = _nt_dot(jnp.where(keep, qf, 0.0).astype(BF16), k)
            e = jnp.exp2(s - jnp.max(s, axis=-1, keepdims=True))
            return e * (1.0 / jnp.sum(e, axis=-1, keepdims=True))

        w = probs(lane < DIFF_DIM) - scal_ref[0] * probs(lane >= DIFF_DIM)
        o = jnp.dot(w.astype(BF16), v, preferred_element_type=F32)
        o = o * lax.rsqrt(jnp.mean(o * o, axis=-1, keepdims=True) + NORM_EPS) * g_ref[...] * post_scale
    else:
        s = _nt_dot(q, k)
        m = jnp.max(s, axis=-1, keepdims=True)
        if kind == "sink":
            sink = scal_ref[pl.program_id(1)]
            m = jnp.maximum(m, sink)
        e = jnp.exp(s - m)
        l = jnp.sum(e, axis=-1, keepdims=True)
        if kind == "sink":
            l = l + jnp.exp(sink - m)
        o = jnp.dot((e * (1.0 / l)).astype(BF16), v, preferred_element_type=F32)
    o_ref[...] = o.astype(o_ref.dtype)


def _context_attention(pc, scal, norm_g, kind, post_scale, q_col, k_col, v_col, grp):
    b, lc, _ = pc.shape
    h = GROUP_BLOCKS
    col = lambda c0, div: (lambda bi, hi: (bi, 0, c0 + hi // div))
    return pl.pallas_call(
        functools.partial(_ctx_attn_kernel, kind=kind, post_scale=post_scale),
        grid=(b, h),
        in_specs=[
            pl.BlockSpec(memory_space=pltpu.SMEM),
            pl.BlockSpec((None, lc, HEAD_DIM), col(q_col, 1)),
            pl.BlockSpec((None, lc, HEAD_DIM), col(k_col, grp)),
            pl.BlockSpec((None, lc, HEAD_DIM), col(v_col, grp)),
            pl.BlockSpec((1, HEAD_DIM), lambda bi, hi: (0, 0)),
        ],
        out_specs=pl.BlockSpec((None, lc, HEAD_DIM), lambda bi, hi: (bi, 0, hi)),
        out_shape=jax.ShapeDtypeStruct((b, lc, h * HEAD_DIM), BF16),
        compiler_params=_cparams("parallel", "parallel"),
        name="context_attention_" + kind,
    )(scal, pc, pc, pc, norm_g)


def _context_mixers(pc, oc_ret, lam, lam_init, diff_norm_g, swa_sink):
    h = GROUP_BLOCKS
    zeros = jnp.zeros((h,), F32)
    ones = jnp.ones((1, HEAD_DIM), F32)
    o_a = _context_attention(pc, zeros, ones, "plain", 1.0, COL_A_Q, COL_A_K, COL_A_V, 1)
    o_b = _context_attention(pc, lam.reshape(1), diff_norm_g.reshape(1, HEAD_DIM).astype(F32), "diff",
                             1.0 - lam_init, COL_B_K - h, COL_B_K, COL_B_K + h, 1)
    o_d = _context_attention(pc, swa_sink.astype(F32), ones, "sink", 1.0, COL_D_Q, COL_D_K, COL_D_V, 2)
    return [o_a, o_b, oc_ret, o_d]


def _layer(x, xc, mod, layer_idx, with_ctx, final_g, rope_h, rope_d, rope_id,
           norm1_g, w_in, na_rpb, diff_lambda, diff_norm_g, ret_decay_fwd, ret_decay_bwd, ret_gn_g, ret_gn_b,
           swa_sink, w_out, norm2_g, w_router, w_gate, w_up, w_down):
    b, n, d = x.shape
    row = lambda v: v.reshape(1, -1).astype(F32)
    part = lambda r0, r1, k: mod[r0:r1, None, k * d:(k + 1) * d]
    sh1, sc1, g1, sh2, sc2, g2 = [part(0, b, k) for k in range(6)]
    ctx_rows = lambda k: jnp.broadcast_to(part(b, b + 1, k), (b, 1, d))
    sh1c, sc1c, g1c, sh2c, sc2c, g2c = [ctx_rows(k) for k in range(6)]

    w_in_b = w_in.astype(BF16)
    w_out_b = w_out.astype(BF16)
    w_router_t = w_router.T.astype(BF16)
    wg_b, wu_b, wd_b = w_gate.astype(BF16), w_up.astype(BF16), w_down.astype(BF16)

    p, t = norm_project(x, row(norm1_g), sc1, sh1, w_in_b, rope_h, rope_d)
    pc, tc = norm_project(xc, row(norm1_g), sc1c, sh1c, w_in_b, rope_id, rope_id)

    lam_init = 0.8 - 0.6 * math.exp(-0.3 * layer_idx)
    lq1, lk1, lq2, lk2 = [diff_lambda[k].astype(F32) for k in range(4)]
    lam = jnp.exp(jnp.sum(lq1 * lk1)) - jnp.exp(jnp.sum(lq2 * lk2)) + lam_init

    o_a = neighbourhood_attention(p, pc, na_rpb)
    o_b = diff_attention(p, t, pc, tc, lam.reshape(1), row(diff_norm_g), 1.0 - lam_init)
    o_c, oc_ret = bidirectional_retention(p, pc, ret_decay_fwd, ret_decay_bwd, ret_gn_g, ret_gn_b, with_ctx)
    o_d = window_attention(p, pc, swa_sink.astype(F32))

    x, h2, aff_t = out_project([o_a, o_b, o_c, o_d], w_out_b, x, g1, row(norm2_g), sc2, sh2, w_router_t)
    x = expert_choice_ffn(x, h2, aff_t, g2, final_g, wg_b, wu_b, wd_b)

    if with_ctx:
        oc = _context_mixers(pc, oc_ret, lam, lam_init, diff_norm_g, swa_sink)
        xc, h2c, aff_c = out_project(oc, w_out_b, xc, g1c, row(norm2_g), sc2c, sh2c, w_router_t)
        xc = expert_choice_ffn(xc, h2c, aff_c, g2c, None, wg_b, wu_b, wd_b)
    return x, xc


def kernel(x, c, ctx, c_ctx, w_mod, b_mod, norm1_g, w_in, na_rpb, diff_lambda, diff_norm_g,
           ret_decay_fwd, ret_decay_bwd, ret_gn_g, ret_gn_b, swa_sink, w_out, norm2_g,
           w_router, w_gate, w_up, w_down, final_norm_g):
    b, n, d = x.shape
    depth = w_in.shape[0]
    lc = ctx.shape[1]
    cvec = jnp.zeros((8, d), F32).at[:b].set(c).at[b].set(c_ctx)
    mod = modulation(cvec, w_mod, b_mod.reshape(depth, 1, -1))
    rope_h = _rope_tables(n, HEAD_DIM)
    rope_d = _rope_tables(n, DIFF_DIM)
    rope_id = _identity_rope_tables(lc)
    xc = ctx
    for li in range(depth):
        last = li == depth - 1
        x, xc = _layer(x, xc, mod[li], li, not last, final_norm_g.reshape(1, -1) if last else None,
                       rope_h, rope_d, rope_id,
                       norm1_g[li], w_in[li], na_rpb[li], diff_lambda[li], diff_norm_g[li],
                       ret_decay_fwd[li], ret_decay_bwd[li], ret_gn_g[li], ret_gn_b[li], swa_sink[li],
                       w_out[li], norm2_g[li], w_router[li], w_gate[li], w_up[li], w_down[li])
    return x
```

```python
import functools
import math

import jax
import jax.numpy as jnp
from jax import lax
from jax.experimental import pallas as pl
from jax.experimental.pallas import tpu as pltpu

GRID_W = 64
HEAD_DIM = 128
DIFF_DIM = HEAD_DIM // 2
NA_WIN_R = 8
NA_WIN_C = 16
RET_CHUNK = 128
SWA_WINDOW = 128
N_EXPERTS = 16
EC_CAPACITY = 2
ROPE_BASE = 10000.0
NORM_EPS = 1e-6
GN_EPS = 1e-5
NEG_INF = -1e30
LOG2E = math.log2(math.e)

BF16 = jnp.bfloat16
F32 = jnp.float32

VMEM_LIMIT_BYTES = 48 * 1024 * 1024

GROUP_BLOCKS = 4
COL_A_Q, COL_A_K, COL_A_V = 0, 4, 8
COL_B_K = 16
COL_C_Q, COL_C_K, COL_C_V, COL_C_G = 24, 28, 32, 36
COL_D_Q, COL_D_K, COL_D_V = 40, 44, 46
PROJ_TN = GROUP_BLOCKS * HEAD_DIM


def _cparams(*sem):
    return pltpu.CompilerParams(dimension_semantics=sem, vmem_limit_bytes=VMEM_LIMIT_BYTES)


def _mod_kernel(c_ref, w_ref, b_ref, o_ref):
    c = c_ref[...]
    s = (c * (1.0 / (1.0 + jnp.exp(-c)))).astype(BF16)
    o_ref[...] = jnp.dot(s, w_ref[...].astype(BF16), preferred_element_type=F32) + b_ref[...]


def modulation(cvec, w_mod, b_mod, *, tn=1024):
    depth, d, n6 = w_mod.shape
    return pl.pallas_call(
        _mod_kernel,
        grid=(depth, n6 // tn),
        in_specs=[
            pl.BlockSpec((8, d), lambda l, j: (0, 0)),
            pl.BlockSpec((None, d, tn), lambda l, j: (l, 0, j)),
            pl.BlockSpec((None, 1, tn), lambda l, j: (l, 0, j)),
        ],
        out_specs=pl.BlockSpec((None, 8, tn), lambda l, j: (l, 0, j)),
        out_shape=jax.ShapeDtypeStruct((depth, 8, n6), F32),
        compiler_params=_cparams("parallel", "parallel"),
        name="modulation",
    )(cvec, w_mod, b_mod)


def _rope_tables(n_tok, dim):
    t = jnp.arange(n_tok)
    row = (t // GRID_W).astype(F32)[:, None]
    col = (t % GRID_W).astype(F32)[:, None]
    nf = dim // 4
    lane = jnp.arange(HEAD_DIM)
    quarter = (lane % dim) // nf
    inv = ROPE_BASE ** (-jnp.arange(nf, dtype=F32) / nf)
    ang = jnp.where(quarter[None, :] < 2, row, col) * inv[lane % nf][None, :]
    cos, sin = jnp.cos(ang), jnp.sin(ang)
    even = (quarter % 2 == 0)[None, :]
    return cos, jnp.where(even, -sin, 0.0), jnp.where(even, 0.0, sin)


def _identity_rope_tables(n_tok):
    z = jnp.zeros((n_tok, HEAD_DIM), F32)
    return jnp.ones((n_tok, HEAD_DIM), F32), z, z


def _proj_modes():
    att = HEAD_DIM ** -0.5
    plain = [(None, 1.0)] * GROUP_BLOCKS
    return [
        ([(None, att)] * 4, None), (plain, None), (plain, None),
        ([("d", LOG2E * DIFF_DIM ** -0.5)] * 4, 0), ([("d", 1.0)] * 4, None), (plain, 1),
        ([("h", 1.0)] * 4, None), ([("h", att)] * 4, None), (plain, None), (plain, None),
        ([("h", att)] * 4, None), ([("h", 1.0)] * 2 + [(None, 1.0)] * 2, None),
    ]


def _proj_kernel(x_ref, g_ref, sc_ref, sh_ref, w_ref, ch_ref, sah_ref, sbh_ref, cd_ref, sad_ref, sbd_ref,
                 o_ref, t_ref):
    x = x_ref[...]
    y = x * lax.rsqrt(jnp.mean(x * x, axis=-1, keepdims=True) + NORM_EPS) * g_ref[...]
    h = (y * (1.0 + sc_ref[...]) + sh_ref[...]).astype(BF16)

    def rope(a, kind):
        if kind is None:
            return a
        c, sa, sb, sh = ((ch_ref, sah_ref, sbh_ref, HEAD_DIM // 4) if kind == "h"
                         else (cd_ref, sad_ref, sbd_ref, DIFF_DIM // 4))
        return (a * c[...] + pltpu.roll(a, HEAD_DIM - sh, 1) * sa[...] + pltpu.roll(a, sh, 1) * sb[...])

    for j, (blocks, t_slot) in enumerate(_proj_modes()):
        cols = slice(j * PROJ_TN, (j + 1) * PROJ_TN)
        acc = jnp.dot(h, w_ref[:, cols], preferred_element_type=F32)
        outs = []
        for hb, (kind, scale) in enumerate(blocks):
            a = rope(acc[:, hb * HEAD_DIM:(hb + 1) * HEAD_DIM], kind)
            outs.append(a if scale == 1.0 else a * scale)
        full = jnp.concatenate(outs, axis=1)
        o_ref[:, cols] = full.astype(o_ref.dtype)
        if t_slot is not None:
            t_ref[t_slot] = full.T.astype(t_ref.dtype)


def norm_project(x, norm_g, scale, shift, w_in, rope_h, rope_d, *, tm=256):
    b, n, d = x.shape
    width = w_in.shape[1]
    assert width == len(_proj_modes()) * PROJ_TN
    tm = min(tm, n)
    tok = lambda bi, i: (i, 0)
    return pl.pallas_call(
        _proj_kernel,
        grid=(b, n // tm),
        in_specs=[
            pl.BlockSpec((None, tm, d), lambda bi, i: (bi, i, 0)),
            pl.BlockSpec((1, d), lambda bi, i: (0, 0)),
            pl.BlockSpec((None, 1, d), lambda bi, i: (bi, 0, 0)),
            pl.BlockSpec((None, 1, d), lambda bi, i: (bi, 0, 0)),
            pl.BlockSpec((d, width), lambda bi, i: (0, 0), pipeline_mode=pl.Buffered(1)),
        ] + [pl.BlockSpec((tm, HEAD_DIM), tok)] * 6,
        out_specs=[
            pl.BlockSpec((None, tm, width), lambda bi, i: (bi, i, 0)),
            pl.BlockSpec((None, 2, PROJ_TN, tm), lambda bi, i: (bi, 0, 0, i)),
        ],
        out_shape=[jax.ShapeDtypeStruct((b, n, width), BF16),
                   jax.ShapeDtypeStruct((b, 2, PROJ_TN, n), BF16)],
        compiler_params=_cparams("parallel", "parallel"),
        name="norm_project",
    )(x, norm_g, scale, shift, w_in, *rope_h, *rope_d)


def _softmax(s_loc, s_ctx, sink):
    m = jnp.maximum(jnp.max(s_loc, axis=-1, keepdims=True), jnp.max(s_ctx, axis=-1, keepdims=True))
    if sink is not None:
        m = jnp.maximum(m, sink)
    p_loc = jnp.exp(s_loc - m)
    p_ctx = jnp.exp(s_ctx - m)
    l = jnp.sum(p_loc, axis=-1, keepdims=True) + jnp.sum(p_ctx, axis=-1, keepdims=True)
    if sink is not None:
        l = l + jnp.exp(sink - m)
    return p_loc.astype(BF16), p_ctx.astype(BF16), 1.0 / l


def _local_attention(scores, values, vc_ref, sink, o_ref, tq):
    n_sub = o_ref.shape[0] // tq
    s = [scores(sub) for sub in range(n_sub)]
    p = [_softmax(s_loc, s_ctx, sink) for s_loc, s_ctx in s]
    for sub, (p_loc, p_ctx, inv_l) in enumerate(p):
        o = (jnp.dot(p_loc, values(sub), preferred_element_type=F32)
             + jnp.dot(p_ctx, vc_ref[...], preferred_element_type=F32))
        o_ref[sub * tq:(sub + 1) * tq, :] = (o * inv_l).astype(o_ref.dtype)


def _nt_dot(a, b):
    return lax.dot_general(a, b, (((1,), (1,)), ((), ())), preferred_element_type=F32)


LOCAL_SUB_BLOCKS = 4


def _na_kernel(q_ref, k_ref, v_ref, kc_ref, vc_ref, bias_ref, o_ref, *, tq, wk, n_blk):
    n = k_ref.shape[0]
    rows_q = tq // GRID_W
    n_sub = q_ref.shape[0] // tq

    def kstart(sub):
        blk = pl.program_id(2) * n_sub + sub
        start = jnp.clip(blk * rows_q - NA_WIN_R // 2, 0, (n - wk) // GRID_W) * GRID_W
        return blk, pl.multiple_of(start, GRID_W)

    def scores(sub):
        blk, start = kstart(sub)
        cls = jnp.where(blk == 0, 0, jnp.where(blk == n_blk - 1, 2, 1))
        q = q_ref[sub * tq:(sub + 1) * tq, :]
        return _nt_dot(q, k_ref[pl.ds(start, wk), :]) + bias_ref[cls], _nt_dot(q, kc_ref[...])

    _local_attention(scores, lambda sub: v_ref[pl.ds(kstart(sub)[1], wk), :], vc_ref, None, o_ref, tq)


def _na_bias(rpb, rows, rows_q, rows_k):
    n_blk = rows // rows_q
    kr = NA_WIN_R
    cols = jnp.arange(GRID_W)
    col_start = jnp.clip(cols - NA_WIN_C // 2, 0, GRID_W - NA_WIN_C)
    col_ok = (cols[None, :] >= col_start[:, None]) & (cols[None, :] < col_start[:, None] + NA_WIN_C)
    col_off = jnp.clip(cols[None, :] - cols[:, None] + NA_WIN_C - 1, 0, 2 * NA_WIN_C - 2)
    out = []
    for blk in (0, 1, n_blk - 1):
        r = blk * rows_q + jnp.arange(rows_q)
        kstart = min(max(blk * rows_q - kr // 2, 0), rows - rows_k)
        krow = kstart + jnp.arange(rows_k)
        start = jnp.clip(r - kr // 2, 0, rows - kr)
        row_ok = (krow[None, :] >= start[:, None]) & (krow[None, :] < start[:, None] + kr)
        row_off = jnp.clip(krow[None, :] - r[:, None] + NA_WIN_R - 1, 0, 2 * NA_WIN_R - 2)
        bias = jnp.einsum('abr,hrc,qkc->habqk', jax.nn.one_hot(row_off, 2 * NA_WIN_R - 1, dtype=F32),
                          rpb.astype(F32), jax.nn.one_hot(col_off, 2 * NA_WIN_C - 1, dtype=F32),
                          precision=lax.Precision.HIGHEST)
        ok = row_ok[:, :, None, None] & col_ok[None, None]
        bias = jnp.where(ok[None], bias, NEG_INF).transpose(0, 1, 3, 2, 4)
        out.append(bias.reshape(rpb.shape[0], rows_q * GRID_W, rows_k * GRID_W))
    return jnp.stack(out, axis=1)


def neighbourhood_attention(p, pc, rpb, *, rows_q=4):
    b, n, _ = p.shape
    lc = pc.shape[1]
    h = rpb.shape[0]
    rows = n // GRID_W
    rows_k = rows_q + NA_WIN_R - 1
    assert rows % rows_q == 0 and rows >= rows_k and rows_q >= NA_WIN_R // 2
    tq, wk = rows_q * GRID_W, rows_k * GRID_W
    n_blk = rows // rows_q
    sub = LOCAL_SUB_BLOCKS if n_blk % LOCAL_SUB_BLOCKS == 0 else 1
    bias = _na_bias(rpb, rows, rows_q, rows_k)
    kern = functools.partial(_na_kernel, tq=tq, wk=wk, n_blk=n_blk)
    return pl.pallas_call(
        kern,
        grid=(b, h, n_blk // sub),
        in_specs=[
            pl.BlockSpec((None, sub * tq, HEAD_DIM), lambda bi, hi, i: (bi, i, COL_A_Q + hi)),
            pl.BlockSpec((None, n, HEAD_DIM), lambda bi, hi, i: (bi, 0, COL_A_K + hi)),
            pl.BlockSpec((None, n, HEAD_DIM), lambda bi, hi, i: (bi, 0, COL_A_V + hi)),
            pl.BlockSpec((None, lc, HEAD_DIM), lambda bi, hi, i: (bi, 0, COL_A_K + hi)),
            pl.BlockSpec((None, lc, HEAD_DIM), lambda bi, hi, i: (bi, 0, COL_A_V + hi)),
            pl.BlockSpec((None, 3, tq, wk), lambda bi, hi, i: (hi, 0, 0, 0)),
        ],
        out_specs=pl.BlockSpec((None, sub * tq, HEAD_DIM), lambda bi, hi, i: (bi, i, hi)),
        out_shape=jax.ShapeDtypeStruct((b, n, h * HEAD_DIM), BF16),
        compiler_params=_cparams("parallel", "parallel", "arbitrary"),
        name="neighbourhood_attention",
    )(p, p, p, pc, pc, bias)


def _swa_kernel(sink_ref, q_ref, k_ref, v_ref, kc_ref, vc_ref, o_ref, *, tq, wk):
    n = k_ref.shape[0]

    def window(sub):
        q0 = pl.program_id(2) * q_ref.shape[0] + sub * tq
        return q0, pl.multiple_of(jnp.clip(q0 - SWA_WINDOW, 0, n - wk), SWA_WINDOW)

    def scores(sub):
        q0, start = window(sub)
        q = q_ref[sub * tq:(sub + 1) * tq, :]
        qpos = q0 + lax.broadcasted_iota(jnp.int32, (tq, wk), 0)
        kpos = start + lax.broadcasted_iota(jnp.int32, (tq, wk), 1)
        s_loc = jnp.where(jnp.abs(kpos - qpos) <= SWA_WINDOW, _nt_dot(q, k_ref[pl.ds(start, wk), :]), NEG_INF)
        return s_loc, _nt_dot(q, kc_ref[...])

    _local_attention(scores, lambda sub: v_ref[pl.ds(window(sub)[1], wk), :], vc_ref,
                     sink_ref[pl.program_id(1)], o_ref, tq)


def window_attention(p, pc, sink, *, tq=256):
    b, n, _ = p.shape
    lc = pc.shape[1]
    hq = sink.shape[0]
    grp = 2
    tq = min(tq, n)
    wk = min(tq + 2 * SWA_WINDOW, n)
    sub = LOCAL_SUB_BLOCKS if (n // tq) % LOCAL_SUB_BLOCKS == 0 else 1
    kern = functools.partial(_swa_kernel, tq=tq, wk=wk)
    return pl.pallas_call(
        kern,
        grid=(b, hq, n // (sub * tq)),
        in_specs=[
            pl.BlockSpec(memory_space=pltpu.SMEM),
            pl.BlockSpec((None, sub * tq, HEAD_DIM), lambda bi, hi, i: (bi, i, COL_D_Q + hi)),
            pl.BlockSpec((None, n, HEAD_DIM), lambda bi, hi, i: (bi, 0, COL_D_K + hi // grp)),
            pl.BlockSpec((None, n, HEAD_DIM), lambda bi, hi, i: (bi, 0, COL_D_V + hi // grp)),
            pl.BlockSpec((None, lc, HEAD_DIM), lambda bi, hi, i: (bi, 0, COL_D_K + hi // grp)),
            pl.BlockSpec((None, lc, HEAD_DIM), lambda bi, hi, i: (bi, 0, COL_D_V + hi // grp)),
        ],
        out_specs=pl.BlockSpec((None, sub * tq, HEAD_DIM), lambda bi, hi, i: (bi, i, hi)),
        out_shape=jax.ShapeDtypeStruct((b, n, hq * HEAD_DIM), BF16),
        compiler_params=_cparams("parallel", "parallel", "arbitrary"),
        name="window_attention",
    )(sink, p, p, p, pc, pc)


DIFF_SLOTS = 2
DIFF_UNROLL = 6
DIFF_AUG_ROWS = 16


def _diff_attn_kernel(lam_ref, qt_ref, k_ref, vt_ref, kc_ref, vct_ref, g_ref, o_ref,
                      qbd_ref, m_ref, acc_ref, s_scr, x_scr, *, tq, tk, post_scale):
    qt = qt_ref[...].astype(F32)
    row = lax.broadcasted_iota(jnp.int32, qt.shape, 0)
    zero = jnp.zeros_like(qt)
    qbd_ref[:, :tq] = jnp.where(row < DIFF_DIM, qt, zero).astype(BF16)
    qbd_ref[:, tq:] = jnp.where(row >= DIFF_DIM, qt, zero).astype(BF16)

    def aug(vt_tile):
        r = lax.broadcasted_iota(jnp.int32, (DIFF_AUG_ROWS, vt_tile.shape[1]), 0)
        return jnp.concatenate([vt_tile, jnp.where(r == 0, 1.0, 0.0).astype(BF16)], axis=0)

    s = jnp.dot(kc_ref[...], qbd_ref[...], preferred_element_type=F32)
    m0 = jnp.max(s, axis=0, keepdims=True)
    m_ref[...] = m0
    acc_ref[...] = jnp.dot(aug(vct_ref[...]), jnp.exp2(s - m0).astype(BF16), preferred_element_type=F32)

    def scores(t, slot):
        off = pl.multiple_of(t * tk, tk)
        s = jnp.dot(k_ref[pl.ds(off, tk), :], qbd_ref[...], preferred_element_type=F32)
        s_scr[slot] = s
        x_scr[slot] = jnp.max(s, axis=0, keepdims=True)

    def accumulate(t, slot):
        m_old = m_ref[...]
        m_new = jnp.maximum(m_old, x_scr[slot])
        alpha = jnp.exp2(m_old - m_new)
        p = jnp.exp2(s_scr[slot] - m_new).astype(BF16)
        m_ref[...] = m_new
        off = pl.multiple_of(t * tk, tk)
        pv = jnp.dot(aug(vt_ref[:, pl.ds(off, tk)]), p, preferred_element_type=F32)
        acc_ref[...] = alpha * acc_ref[...] + pv

    def stage(t, t_mod, do_scores=True):
        if do_scores:
            scores(t + 1, (t_mod + 1) % DIFF_SLOTS)
        accumulate(t, t_mod % DIFF_SLOTS)

    n_kt = k_ref.shape[0] // tk
    trips = (n_kt - 1) // DIFF_UNROLL
    scores(0, 0)

    def body(i, carry):
        for u in range(DIFF_UNROLL):
            stage(DIFF_UNROLL * i + u, u)
        return carry

    lax.fori_loop(0, trips, body, 0)
    for t in range(DIFF_UNROLL * trips, n_kt - 1):
        stage(t, t)
    stage(n_kt - 1, n_kt - 1, do_scores=False)

    lam = lam_ref[0]
    inv = 1.0 / acc_ref[HEAD_DIM:HEAD_DIM + 1, :]
    acc = acc_ref[:HEAD_DIM, :]
    o_t = acc[:, :tq] * inv[:, :tq] - lam * (acc[:, tq:] * inv[:, tq:])
    o = o_t.T
    ms = jnp.mean(o * o, axis=-1, keepdims=True)
    y = o * lax.rsqrt(ms + NORM_EPS) * g_ref[...]
    o_ref[...] = (y * post_scale).astype(o_ref.dtype)


def diff_attention(p, t, pc, tc, lam, norm_g, post_scale, *, tq=512, tk=512):
    b, n, _ = p.shape
    lc = pc.shape[1]
    h = GROUP_BLOCKS
    tq = min(tq, n)
    tk = min(tk, n)
    assert n % tq == 0 and n % tk == 0 and DIFF_UNROLL % DIFF_SLOTS == 0
    kern = functools.partial(_diff_attn_kernel, tq=tq, tk=tk, post_scale=post_scale)
    return pl.pallas_call(
        kern,
        grid=(b, h, n // tq),
        in_specs=[
            pl.BlockSpec(memory_space=pltpu.SMEM),
            pl.BlockSpec((None, None, HEAD_DIM, tq), lambda bi, hi, qi: (bi, 0, hi, qi)),
            pl.BlockSpec((None, n, HEAD_DIM), lambda bi, hi, qi: (bi, 0, COL_B_K + hi)),
            pl.BlockSpec((None, None, HEAD_DIM, n), lambda bi, hi, qi: (bi, 1, hi, 0)),
            pl.BlockSpec((None, lc, HEAD_DIM), lambda bi, hi, qi: (bi, 0, COL_B_K + hi)),
            pl.BlockSpec((None, None, HEAD_DIM, lc), lambda bi, hi, qi: (bi, 1, hi, 0)),
            pl.BlockSpec((1, HEAD_DIM), lambda bi, hi, qi: (0, 0)),
        ],
        out_specs=pl.BlockSpec((None, tq, HEAD_DIM), lambda bi, hi, qi: (bi, qi, hi)),
        out_shape=jax.ShapeDtypeStruct((b, n, h * HEAD_DIM), BF16),
        scratch_shapes=[
            pltpu.VMEM((HEAD_DIM, 2 * tq), BF16),
            pltpu.VMEM((1, 2 * tq), F32),
            pltpu.VMEM((HEAD_DIM + DIFF_AUG_ROWS, 2 * tq), F32),
            pltpu.VMEM((DIFF_SLOTS, tk, 2 * tq), F32),
            pltpu.VMEM((DIFF_SLOTS, 1, 2 * tq), F32),
        ],
        compiler_params=_cparams("parallel", "parallel", "arbitrary"),
        name="diff_attention",
    )(lam, t, p, t, pc, tc, norm_g)


def _retention_kernel(cdec_ref, q_ref, k_ref, v_ref, intra_ref, qdec_ref, kdec_ref, s0_ref, *rest,
                      reverse, final):
    if final:
        of_ref, gate_ref, gng_ref, gnb_ref, o_ref, sfin_ref, s_ref = rest
    else:
        o_ref, sfin_ref, s_ref = rest
    i = pl.program_id(1)
    n_heads = s_ref.shape[0]
    n_chunks = q_ref.shape[0] // RET_CHUNK

    @pl.when(i == 0)
    def _():
        s_ref[...] = s0_ref[...]

    order = range(n_chunks - 1, -1, -1) if reverse else range(n_chunks)
    units = [(c, h) for c in order for h in range(n_heads)]
    window = lambda c, h: (slice(c * RET_CHUNK, (c + 1) * RET_CHUNK), slice(h * HEAD_DIM, (h + 1) * HEAD_DIM))

    intra_o, kv, qdec = {}, {}, {}
    for c, h in units:
        rs, cs = window(c, h)
        qh, kh, vh = q_ref[rs, cs], k_ref[rs, cs], v_ref[rs, cs]
        qdec[c, h] = (qh.astype(F32) * qdec_ref[h]).astype(BF16)
        kd_t = (kh.astype(F32) * kdec_ref[h]).T.astype(BF16)
        a = (_nt_dot(qh, kh) * intra_ref[h]).astype(BF16)
        intra_o[c, h] = jnp.dot(a, vh, preferred_element_type=F32)
        kv[c, h] = jnp.dot(kd_t, vh, preferred_element_type=F32)

    for c, h in units:
        rs, cs = window(c, h)
        if True:
            s = s_ref[h]
            o = intra_o[c, h] + jnp.dot(qdec[c, h], s.astype(BF16), preferred_element_type=F32)
            s_ref[h] = cdec_ref[h] * s + kv[c, h]
            if final:
                o = o + of_ref[rs, cs]
                mu = jnp.mean(o, axis=-1, keepdims=True)
                var = jnp.mean(jnp.square(o - mu), axis=-1, keepdims=True)
                y = (o - mu) * lax.rsqrt(var + GN_EPS) * gng_ref[:, cs] + gnb_ref[:, cs]
                g = gate_ref[rs, cs].astype(F32)
                o = g * (1.0 / (1.0 + jnp.exp(-g))) * y
            o_ref[rs, cs] = o.astype(o_ref.dtype)

    @pl.when(i == pl.num_programs(1) - 1)
    def _():
        sfin_ref[...] = s_ref[...]


def _retention_pass(p, tables, s0, final_inputs, *, reverse, blk_chunks=4):
    b, n, _ = p.shape
    cdec, intra, qdec, kdec = tables
    h = intra.shape[0]
    width = h * HEAD_DIM
    n_chunks = n // RET_CHUNK
    blk_chunks = min(blk_chunks, n_chunks)
    assert n_chunks % blk_chunks == 0
    tb = blk_chunks * RET_CHUNK
    n_blk = n // tb
    pos = (lambda i: n_blk - 1 - i) if reverse else (lambda i: i)
    colblk = lambda c: (lambda bi, i: (bi, pos(i), c // GROUP_BLOCKS))
    tab = pl.BlockSpec((h, RET_CHUNK, HEAD_DIM), lambda bi, i: (0, 0, 0))
    state = pl.BlockSpec((None, h, HEAD_DIM, HEAD_DIM), lambda bi, i: (bi, 0, 0, 0))
    in_specs = [pl.BlockSpec(memory_space=pltpu.SMEM),
                pl.BlockSpec((None, tb, width), colblk(COL_C_Q)),
                pl.BlockSpec((None, tb, width), colblk(COL_C_K)),
                pl.BlockSpec((None, tb, width), colblk(COL_C_V)),
                tab, tab, tab, state]
    args = [cdec, p, p, p, intra, qdec, kdec, s0]
    final = final_inputs is not None
    if final:
        o_fwd, gn_g, gn_b = final_inputs
        in_specs += [pl.BlockSpec((None, tb, width), lambda bi, i: (bi, pos(i), 0)),
                     pl.BlockSpec((None, tb, width), colblk(COL_C_G)),
                     pl.BlockSpec((1, width), lambda bi, i: (0, 0)),
                     pl.BlockSpec((1, width), lambda bi, i: (0, 0))]
        args += [o_fwd, p, gn_g, gn_b]
    kern = functools.partial(_retention_kernel, reverse=reverse, final=final)
    return pl.pallas_call(
        kern,
        grid=(b, n_blk),
        in_specs=in_specs,
        out_specs=[pl.BlockSpec((None, tb, width), lambda bi, i: (bi, pos(i), 0)), state],
        out_shape=[jax.ShapeDtypeStruct((b, n, width), BF16 if final else F32),
                   jax.ShapeDtypeStruct((b, h, HEAD_DIM, HEAD_DIM), F32)],
        scratch_shapes=[pltpu.VMEM((h, HEAD_DIM, HEAD_DIM), F32)],
        compiler_params=_cparams("parallel", "arbitrary"),
        name="retention_bwd" if reverse else "retention_fwd",
    )(*args)


def _retention_tables(decay, reverse):
    lg = jax.nn.log_sigmoid(decay.astype(F32))[:, None, None]
    pos = jnp.arange(RET_CHUNK, dtype=F32)
    rel = pos[:, None] - pos[None, :]
    if reverse:
        rel = -rel
        q_pow, k_pow = RET_CHUNK - pos, pos
    else:
        q_pow, k_pow = pos + 1.0, RET_CHUNK - 1.0 - pos
    intra = jnp.where(rel >= 0, jnp.exp(jnp.maximum(rel, 0.0) * lg), 0.0)
    bc = lambda e: jnp.broadcast_to(jnp.exp(e[None, :, None] * lg), intra.shape)
    cdec = jnp.exp(RET_CHUNK * lg[:, 0, 0])
    return cdec, intra, bc(q_pow), bc(k_pow)


def bidirectional_retention(p, pc, decay_f, decay_b, gn_g, gn_b, with_ctx):
    b = p.shape[0]
    h = decay_f.shape[0]
    tf = _retention_tables(decay_f, False)
    tb = _retention_tables(decay_b, True)
    s0 = jnp.zeros((b, h, HEAD_DIM, HEAD_DIM), F32)
    gn = (gn_g.reshape(1, -1).astype(F32), gn_b.reshape(1, -1).astype(F32))
    oc_f, s_f = _retention_pass(pc, tf, s0, None, reverse=False)
    oc, s_b = _retention_pass(pc, tb, s0, (oc_f,) + gn, reverse=True)
    o_f, _ = _retention_pass(p, tf, s_f, None, reverse=False)
    o, _ = _retention_pass(p, tb, s_b, (o_f,) + gn, reverse=True)
    return o, (oc if with_ctx else None)


OUT_SUB_ROWS = 256


def _out_kernel(oa_ref, ob_ref, oc_ref, od_ref, w_ref, x_ref, g1_ref, ng_ref, sc_ref, sh_ref, wr_ref,
                xo_ref, h_ref, aff_ref):
    gw = oa_ref.shape[1]
    rows = min(OUT_SUB_ROWS, x_ref.shape[0])
    subs = [slice(sub * rows, (sub + 1) * rows) for sub in range(x_ref.shape[0] // rows)]
    accs = []
    for rs in subs:
        acc = jnp.dot(oa_ref[rs, :], w_ref[0 * gw:1 * gw, :], preferred_element_type=F32)
        acc += jnp.dot(ob_ref[rs, :], w_ref[1 * gw:2 * gw, :], preferred_element_type=F32)
        acc += jnp.dot(oc_ref[rs, :], w_ref[2 * gw:3 * gw, :], preferred_element_type=F32)
        acc += jnp.dot(od_ref[rs, :], w_ref[3 * gw:4 * gw, :], preferred_element_type=F32)
        accs.append(acc)
    for rs, acc in zip(subs, accs):
        x = x_ref[rs, :] + g1_ref[...] * acc
        xo_ref[rs, :] = x
        y = x * lax.rsqrt(jnp.mean(x * x, axis=-1, keepdims=True) + NORM_EPS) * ng_ref[...]
        h2 = (y * (1.0 + sc_ref[...]) + sh_ref[...]).astype(BF16)
        h_ref[rs, :] = h2
        logits = _nt_dot(wr_ref[...], h2)
        e = jnp.exp(logits - jnp.max(logits, axis=0, keepdims=True))
        aff_ref[:, rs] = e * (1.0 / jnp.sum(e, axis=0, keepdims=True))


def out_project(o_groups, w_out, x, g1, norm_g, scale, shift, w_router_t, *, tm=2 * OUT_SUB_ROWS):
    b, n, d = x.shape
    n_e = w_router_t.shape[0]
    gw = o_groups[0].shape[-1]
    tm = min(tm, n)
    tile = lambda w: pl.BlockSpec((None, tm, w), lambda bi, i: (bi, i, 0))
    vec = pl.BlockSpec((None, 1, d), lambda bi, i: (bi, 0, 0))
    return pl.pallas_call(
        _out_kernel,
        grid=(b, n // tm),
        in_specs=[tile(gw)] * 4 + [
            pl.BlockSpec((d, d), lambda bi, i: (0, 0), pipeline_mode=pl.Buffered(1)),
            tile(d), vec,
            pl.BlockSpec((1, d), lambda bi, i: (0, 0)),
            vec, vec,
            pl.BlockSpec((n_e, d), lambda bi, i: (0, 0)),
        ],
        out_specs=[tile(d), tile(d), pl.BlockSpec((None, n_e, tm), lambda bi, i: (bi, 0, i))],
        out_shape=[jax.ShapeDtypeStruct((b, n, d), F32),
                   jax.ShapeDtypeStruct((b, n, d), BF16),
                   jax.ShapeDtypeStruct((b, n_e, n), F32)],
        compiler_params=_cparams("parallel", "parallel"),
        name="out_project",
    )(*o_groups, w_out, x, g1, norm_g, scale, shift, w_router_t)


def _ffn_kernel(x_ref, gate_ref, wg_ref, wu_ref, wd_ref, o_ref):
    x = x_ref[...]
    a = jnp.dot(x, wg_ref[...], preferred_element_type=F32)
    u = jnp.dot(x, wu_ref[...], preferred_element_type=F32)
    hm = (a * (1.0 / (1.0 + jnp.exp(-a))) * u).astype(BF16)
    y = jnp.dot(hm, wd_ref[...], preferred_element_type=F32) * gate_ref[...]
    o_ref[...] = y.astype(o_ref.dtype)


def expert_ffn(xin, gate, w_gate, w_up, w_down, *, tc=256):
    b, n_e, cap, d = xin.shape
    ff = w_gate.shape[-1]
    tc = min(tc, cap)
    return pl.pallas_call(
        _ffn_kernel,
        grid=(n_e, b, cap // tc),
        in_specs=[
            pl.BlockSpec((None, None, tc, d), lambda e, bi, i: (bi, e, i, 0)),
            pl.BlockSpec((None, None, tc, 1), lambda e, bi, i: (bi, e, i, 0)),
            pl.BlockSpec((None, d, ff), lambda e, bi, i: (e, 0, 0)),
            pl.BlockSpec((None, d, ff), lambda e, bi, i: (e, 0, 0)),
            pl.BlockSpec((None, ff, d), lambda e, bi, i: (e, 0, 0)),
        ],
        out_specs=pl.BlockSpec((None, None, tc, d), lambda e, bi, i: (bi, e, i, 0)),
        out_shape=jax.ShapeDtypeStruct((b, n_e, cap, d), BF16),
        compiler_params=_cparams("parallel", "parallel", "parallel"),
        name="expert_ffn",
    )(xin, gate, w_gate, w_up, w_down)


SCATTER_GROUP = 64


def _scatter_kernel(rlo_ref, nrd_ref, x_ref, g_ref, ng_ref, idx_ref, y_hbm, o_ref, ybuf, sem, acc_ref,
                    base_ref, *, sr, final):
    bi, i, nt = pl.program_id(0), pl.program_id(1), pl.num_programs(1)
    n_e, n_grp = idx_ref.shape[0], idx_ref.shape[1]
    tm = x_ref.shape[0]
    n_rounds = nrd_ref[bi * nt + i]

    def group(e, k, tile):
        return jnp.minimum(rlo_ref[(bi * n_e + e) * nt + tile] + k, n_grp - 1)

    def slot_tokens(e, k):
        in_range = rlo_ref[(bi * n_e + e) * nt + i] + k < n_grp
        return jnp.where(in_range, idx_ref[e, group(e, k, i)], -1)

    def copies(k, slot, tile):
        return [pltpu.make_async_copy(
            y_hbm.at[bi, e, pl.ds(pl.multiple_of(group(e, k, tile) * sr, sr), sr), :],
            ybuf.at[slot, pl.ds(e * sr, sr), :], sem.at[slot]) for e in range(n_e)]

    @pl.when(i == 0)
    def _():
        base_ref[0] = 0
        for c in copies(0, 0, i):
            c.start()

    base = base_ref[0]
    acc_ref[...] = jnp.zeros(acc_ref.shape, F32)
    per_row = HEAD_DIM // sr
    tok = i * tm + lax.broadcasted_iota(jnp.int32, (tm, HEAD_DIM), 0)

    def body(k, carry):
        slot = (base + k) % 2
        for c in copies(k, slot, i):
            c.wait()

        @pl.when(k + 1 < n_rounds)
        def _():
            for c in copies(k + 1, 1 - slot, i):
                c.start()

        @pl.when(jnp.logical_and(k + 1 == n_rounds, i + 1 < nt))
        def _():
            for c in copies(0, 1 - slot, i + 1):
                c.start()

        rows = [jnp.concatenate([slot_tokens(e + j, k) for j in range(per_row)], axis=1)
                for e in range(0, n_e, per_row)]
        onehot = jnp.concatenate([jnp.where(tok == r, 1.0, 0.0) for r in rows], axis=1)
        acc_ref[...] += jnp.dot(onehot.astype(BF16), ybuf[slot], preferred_element_type=F32)
        return carry

    lax.fori_loop(0, n_rounds, body, 0)
    base_ref[0] = (base + n_rounds) % 2
    x = x_ref[...] + g_ref[...] * acc_ref[...]
    if final:
        x = x * lax.rsqrt(jnp.mean(x * x, axis=-1, keepdims=True) + NORM_EPS) * ng_ref[...]
    o_ref[...] = x


def scatter_combine(x, y, idx, g2, final_g, *, tm=256):
    b, t, d = x.shape
    n_e, cap = idx.shape[1], idx.shape[2]
    tm = min(tm, t)
    sr = min(SCATTER_GROUP, cap)
    assert t % tm == 0 and cap % sr == 0 and HEAD_DIM % sr == 0 and n_e % (HEAD_DIM // sr) == 0
    nt, n_grp = t // tm, cap // sr
    edges = jnp.arange(nt + 1, dtype=idx.dtype) * tm
    pos = jnp.sum(idx[..., None] < edges, axis=2, dtype=jnp.int32)
    lo, hi = pos[..., :-1], pos[..., 1:]
    rlo = jnp.minimum(lo // sr, n_grp - 1)
    rhi = jnp.maximum(hi - 1, lo) // sr
    n_rounds = jnp.max(jnp.minimum(rhi, n_grp - 1) - rlo + 1, axis=1)
    final = final_g is not None
    ng = final_g if final else jnp.ones((1, d), F32)
    tile = pl.BlockSpec((None, tm, d), lambda bi, i, *_: (bi, i, 0))
    return pl.pallas_call(
        functools.partial(_scatter_kernel, sr=sr, final=final),
        grid_spec=pltpu.PrefetchScalarGridSpec(
            num_scalar_prefetch=2,
            grid=(b, nt),
            in_specs=[tile,
                      pl.BlockSpec((None, 1, d), lambda bi, i, *_: (bi, 0, 0)),
                      pl.BlockSpec((1, d), lambda bi, i, *_: (0, 0)),
                      pl.BlockSpec((None, n_e, n_grp, 1, sr), lambda bi, i, *_: (bi, 0, 0, 0, 0)),
                      pl.BlockSpec(memory_space=pl.ANY)],
            out_specs=tile,
            scratch_shapes=[pltpu.VMEM((2, n_e * sr, d), BF16),
                            pltpu.SemaphoreType.DMA((2,)),
                            pltpu.VMEM((tm, d), F32),
                            pltpu.SMEM((1,), jnp.int32)]),
        out_shape=jax.ShapeDtypeStruct((b, t, d), F32),
        compiler_params=_cparams("parallel", "arbitrary"),
        name="scatter_combine",
    )(rlo.reshape(-1), n_rounds.reshape(-1), x, g2, ng, idx.reshape(b, n_e, n_grp, 1, sr), y)


def expert_choice_ffn(x, h2, aff_t, g2, final_g, w_gate, w_up, w_down):
    b, t, d = h2.shape
    cap = EC_CAPACITY * t // N_EXPERTS
    gate, idx = lax.top_k(aff_t, cap)
    idx, gate = lax.sort_key_val(idx, gate, dimension=-1)
    xin = jax.vmap(lambda hb, ib: hb[ib])(h2, idx)
    y = expert_ffn(xin, gate[..., None], w_gate, w_up, w_down)
    return scatter_combine(x, y, idx, g2, final_g)


def _ctx_attn_kernel(scal_ref, q_ref, k_ref, v_ref, g_ref, o_ref, *, kind, post_scale):
    q, k, v = q_ref[...], k_ref[...], v_ref[...]
    if kind == "diff":
        qf = q.astype(F32)
        lane = lax.broadcasted_iota(jnp.int32, qf.shape, 1)

        def probs(keep):
            s = _nt_dot(jnp.where(keep, qf, 0.0).astype(BF16), k)
            e = jnp.exp2(s - jnp.max(s, axis=-1, keepdims=True))
            return e * (1.0 / jnp.sum(e, axis=-1, keepdims=True))

        w = probs(lane < DIFF_DIM) - scal_ref[0] * probs(lane >= DIFF_DIM)
        o = jnp.dot(w.astype(BF16), v, preferred_element_type=F32)
        o = o * lax.rsqrt(jnp.mean(o * o, axis=-1, keepdims=True) + NORM_EPS) * g_ref[...] * post_scale
    else:
        s = _nt_dot(q, k)
        m = jnp.max(s, axis=-1, keepdims=True)
        if kind == "sink":
            sink = scal_ref[pl.program_id(1)]
            m = jnp.maximum(m, sink)
        e = jnp.exp(s - m)
        l = jnp.sum(e, axis=-1, keepdims=True)
        if kind == "sink":
            l = l + jnp.exp(sink - m)
        o = jnp.dot((e * (1.0 / l)).astype(BF16), v, preferred_element_type=F32)
    o_ref[...] = o.astype(o_ref.dtype)


def _context_attention(pc, scal, norm_g, kind, post_scale, q_col, k_col, v_col, grp):
    b, lc, _ = pc.shape
    h = GROUP_BLOCKS
    col = lambda c0, div: (lambda bi, hi: (bi, 0, c0 + hi // div))
    return pl.pallas_call(
        functools.partial(_ctx_attn_kernel, kind=kind, post_scale=post_scale),
        grid=(b, h),
        in_specs=[
            pl.BlockSpec(memory_space=pltpu.SMEM),
            pl.BlockSpec((None, lc, HEAD_DIM), col(q_col, 1)),
            pl.BlockSpec((None, lc, HEAD_DIM), col(k_col, grp)),
            pl.BlockSpec((None, lc, HEAD_DIM), col(v_col, grp)),
            pl.BlockSpec((1, HEAD_DIM), lambda bi, hi: (0, 0)),
        ],
        out_specs=pl.BlockSpec((None, lc, HEAD_DIM), lambda bi, hi: (bi, 0, hi)),
        out_shape=jax.ShapeDtypeStruct((b, lc, h * HEAD_DIM), BF16),
        compiler_params=_cparams("parallel", "parallel"),
        name="context_attention_" + kind,
    )(scal, pc, pc, pc, norm_g)


def _context_mixers(pc, oc_ret, lam, lam_init, diff_norm_g, swa_sink):
    h = GROUP_BLOCKS
    zeros = jnp.zeros((h,), F32)
    ones = jnp.ones((1, HEAD_DIM), F32)
    o_a = _context_attention(pc, zeros, ones, "plain", 1.0, COL_A_Q, COL_A_K, COL_A_V, 1)
    o_b = _context_attention(pc, lam.reshape(1), diff_norm_g.reshape(1, HEAD_DIM).astype(F32), "diff",
                             1.0 - lam_init, COL_B_K - h, COL_B_K, COL_B_K + h, 1)
    o_d = _context_attention(pc, swa_sink.astype(F32), ones, "sink", 1.0, COL_D_Q, COL_D_K, COL_D_V, 2)
    return [o_a, o_b, oc_ret, o_d]


def _layer(x, xc, mod, layer_idx, with_ctx, final_g, rope_h, rope_d, rope_id,
           norm1_g, w_in, na_rpb, diff_lambda, diff_norm_g, ret_decay_fwd, ret_decay_bwd, ret_gn_g, ret_gn_b,
           swa_sink, w_out, norm2_g, w_router, w_gate, w_up, w_down):
    b, n, d = x.shape
    row = lambda v: v.reshape(1, -1).astype(F32)
    part = lambda r0, r1, k: mod[r0:r1, None, k * d:(k + 1) * d]
    sh1, sc1, g1, sh2, sc2, g2 = [part(0, b, k) for k in range(6)]
    ctx_rows = lambda k: jnp.broadcast_to(part(b, b + 1, k), (b, 1, d))
    sh1c, sc1c, g1c, sh2c, sc2c, g2c = [ctx_rows(k) for k in range(6)]

    w_in_b = w_in.astype(BF16)
    w_out_b = w_out.astype(BF16)
    w_router_t = w_router.T.astype(BF16)
    wg_b, wu_b, wd_b = w_gate.astype(BF16), w_up.astype(BF16), w_down.astype(BF16)

    p, t = norm_project(x, row(norm1_g), sc1, sh1, w_in_b, rope_h, rope_d)
    pc, tc = norm_project(xc, row(norm1_g), sc1c, sh1c, w_in_b, rope_id, rope_id)

    lam_init = 0.8 - 0.6 * math.exp(-0.3 * layer_idx)
    lq1, lk1, lq2, lk2 = [diff_lambda[k].astype(F32) for k in range(4)]
    lam = jnp.exp(jnp.sum(lq1 * lk1)) - jnp.exp(jnp.sum(lq2 * lk2)) + lam_init

    o_a = neighbourhood_attention(p, pc, na_rpb)
    o_b = diff_attention(p, t, pc, tc, lam.reshape(1), row(diff_norm_g), 1.0 - lam_init)
    o_c, oc_ret = bidirectional_retention(p, pc, ret_decay_fwd, ret_decay_bwd, ret_gn_g, ret_gn_b, with_ctx)
    o_d = window_attention(p, pc, swa_sink.astype(F32))

    x, h2, aff_t = out_project([o_a, o_b, o_c, o_d], w_out_b, x, g1, row(norm2_g), sc2, sh2, w_router_t)
    x = expert_choice_ffn(x, h2, aff_t, g2, final_g, wg_b, wu_b, wd_b)

    if with_ctx:
        oc = _context_mixers(pc, oc_ret, lam, lam_init, diff_norm_g, swa_sink)
        xc, h2c, aff_c = out_project(oc, w_out_b, xc, g1c, row(norm2_g), sc2c, sh2c, w_router_t)
        xc = expert_choice_ffn(xc, h2c, aff_c, g2c, None, wg_b, wu_b, wd_b)
    return x, xc


def kernel(x, c, ctx, c_ctx, w_mod, b_mod, norm1_g, w_in, na_rpb, diff_lambda, diff_norm_g,
           ret_decay_fwd, ret_decay_bwd, ret_gn_g, ret_gn_b, swa_sink, w_out, norm2_g,
           w_router, w_gate, w_up, w_down, final_norm_g):
    b, n, d = x.shape
    depth = w_in.shape[0]
    lc = ctx.shape[1]
    cvec = jnp.zeros((8, d), F32).at[:b].set(c).at[b].set(c_ctx)
    mod = modulation(cvec, w_mod, b_mod.reshape(depth, 1, -1))
    rope_h = _rope_tables(n, HEAD_DIM)
    rope_d = _rope_tables(n, DIFF_DIM)
    rope_id = _identity_rope_tables(lc)
    xc = ctx
    for li in range(depth):
        last = li == depth - 1
        x, xc = _layer(x, xc, mod[li], li, not last, final_norm_g.reshape(1, -1) if last else None,
                       rope_h, rope_d, rope_id,
                       norm1_g[li], w_in[li], na_rpb[li], diff_lambda[li], diff_norm_g[li],
                       ret_decay_fwd[li], ret_decay_bwd[li], ret_gn_g[li], ret_gn_b[li], swa_sink[li],
                       w_out[li], norm2_g[li], w_router[li], w_gate[li], w_up[li], w_down[li])
    return x
```

```python
import functools
import math

import jax
import jax.numpy as jnp
from jax import lax
from jax.experimental import pallas as pl
from jax.experimental.pallas import tpu as pltpu

GRID_W = 64
HEAD_DIM = 128
DIFF_DIM = HEAD_DIM // 2
NA_WIN_R = 8
NA_WIN_C = 16
RET_CHUNK = 128
SWA_WINDOW = 128
N_EXPERTS = 16
EC_CAPACITY = 2
ROPE_BASE = 10000.0
NORM_EPS = 1e-6
GN_EPS = 1e-5
NEG_INF = -1e30
LOG2E = math.log2(math.e)

BF16 = jnp.bfloat16
F32 = jnp.float32

VMEM_LIMIT_BYTES = 48 * 1024 * 1024

GROUP_BLOCKS = 4
COL_A_Q, COL_A_K, COL_A_V = 0, 4, 8
COL_B_K = 16
COL_C_Q, COL_C_K, COL_C_V, COL_C_G = 24, 28, 32, 36
COL_D_Q, COL_D_K, COL_D_V = 40, 44, 46
PROJ_TN = GROUP_BLOCKS * HEAD_DIM


def _cparams(*sem):
    return pltpu.CompilerParams(dimension_semantics=sem, vmem_limit_bytes=VMEM_LIMIT_BYTES)


def _mod_kernel(c_ref, w_ref, b_ref, o_ref):
    c = c_ref[...]
    s = (c * (1.0 / (1.0 + jnp.exp(-c)))).astype(BF16)
    o_ref[...] = jnp.dot(s, w_ref[...].astype(BF16), preferred_element_type=F32) + b_ref[...]


def modulation(cvec, w_mod, b_mod, *, tn=1024):
    depth, d, n6 = w_mod.shape
    return pl.pallas_call(
        _mod_kernel,
        grid=(depth, n6 // tn),
        in_specs=[
            pl.BlockSpec((8, d), lambda l, j: (0, 0)),
            pl.BlockSpec((None, d, tn), lambda l, j: (l, 0, j)),
            pl.BlockSpec((None, 1, tn), lambda l, j: (l, 0, j)),
        ],
        out_specs=pl.BlockSpec((None, 8, tn), lambda l, j: (l, 0, j)),
        out_shape=jax.ShapeDtypeStruct((depth, 8, n6), F32),
        compiler_params=_cparams("parallel", "parallel"),
        name="modulation",
    )(cvec, w_mod, b_mod)


def _rope_tables(n_tok, dim):
    t = jnp.arange(n_tok)
    row = (t // GRID_W).astype(F32)[:, None]
    col = (t % GRID_W).astype(F32)[:, None]
    nf = dim // 4
    lane = jnp.arange(HEAD_DIM)
    quarter = (lane % dim) // nf
    inv = ROPE_BASE ** (-jnp.arange(nf, dtype=F32) / nf)
    ang = jnp.where(quarter[None, :] < 2, row, col) * inv[lane % nf][None, :]
    cos, sin = jnp.cos(ang), jnp.sin(ang)
    even = (quarter % 2 == 0)[None, :]
    return cos, jnp.where(even, -sin, 0.0), jnp.where(even, 0.0, sin)


def _identity_rope_tables(n_tok):
    z = jnp.zeros((n_tok, HEAD_DIM), F32)
    return jnp.ones((n_tok, HEAD_DIM), F32), z, z


def _proj_modes():
    att = HEAD_DIM ** -0.5
    plain = [(None, 1.0)] * GROUP_BLOCKS
    return [
        ([(None, att)] * 4, None), (plain, None), (plain, None),
        ([("d", LOG2E * DIFF_DIM ** -0.5)] * 4, 0), ([("d", 1.0)] * 4, None), (plain, 1),
        ([("h", 1.0)] * 4, None), ([("h", att)] * 4, None), (plain, None), (plain, None),
        ([("h", att)] * 4, None), ([("h", 1.0)] * 2 + [(None, 1.0)] * 2, None),
    ]


def _proj_kernel(x_ref, g_ref, sc_ref, sh_ref, w_ref, ch_ref, sah_ref, sbh_ref, cd_ref, sad_ref, sbd_ref,
                 o_ref, t_ref):
    x = x_ref[...]
    y = x * lax.rsqrt(jnp.mean(x * x, axis=-1, keepdims=True) + NORM_EPS) * g_ref[...]
    h = (y * (1.0 + sc_ref[...]) + sh_ref[...]).astype(BF16)

    def rope(a, kind):
        if kind is None:
            return a
        c, sa, sb, sh = ((ch_ref, sah_ref, sbh_ref, HEAD_DIM // 4) if kind == "h"
                         else (cd_ref, sad_ref, sbd_ref, DIFF_DIM // 4))
        return (a * c[...] + pltpu.roll(a, HEAD_DIM - sh, 1) * sa[...] + pltpu.roll(a, sh, 1) * sb[...])

    for j, (blocks, t_slot) in enumerate(_proj_modes()):
        cols = slice(j * PROJ_TN, (j + 1) * PROJ_TN)
        acc = jnp.dot(h, w_ref[:, cols], preferred_element_type=F32)
        outs = []
        for hb, (kind, scale) in enumerate(blocks):
            a = rope(acc[:, hb * HEAD_DIM:(hb + 1) * HEAD_DIM], kind)
            outs.append(a if scale == 1.0 else a * scale)
        full = jnp.concatenate(outs, axis=1)
        o_ref[:, cols] = full.astype(o_ref.dtype)
        if t_slot is not None:
            t_ref[t_slot] = full.T.astype(t_ref.dtype)


def norm_project(x, norm_g, scale, shift, w_in, rope_h, rope_d, *, tm=256):
    b, n, d = x.shape
    width = w_in.shape[1]
    assert width == len(_proj_modes()) * PROJ_TN
    tm = min(tm, n)
    tok = lambda bi, i: (i, 0)
    return pl.pallas_call(
        _proj_kernel,
        grid=(b, n // tm),
        in_specs=[
            pl.BlockSpec((None, tm, d), lambda bi, i: (bi, i, 0)),
            pl.BlockSpec((1, d), lambda bi, i: (0, 0)),
            pl.BlockSpec((None, 1, d), lambda bi, i: (bi, 0, 0)),
            pl.BlockSpec((None, 1, d), lambda bi, i: (bi, 0, 0)),
            pl.BlockSpec((d, width), lambda bi, i: (0, 0), pipeline_mode=pl.Buffered(1)),
        ] + [pl.BlockSpec((tm, HEAD_DIM), tok)] * 6,
        out_specs=[
            pl.BlockSpec((None, tm, width), lambda bi, i: (bi, i, 0)),
            pl.BlockSpec((None, 2, PROJ_TN, tm), lambda bi, i: (bi, 0, 0, i)),
        ],
        out_shape=[jax.ShapeDtypeStruct((b, n, width), BF16),
                   jax.ShapeDtypeStruct((b, 2, PROJ_TN, n), BF16)],
        compiler_params=_cparams("parallel", "parallel"),
        name="norm_project",
    )(x, norm_g, scale, shift, w_in, *rope_h, *rope_d)


def _softmax(s_loc, s_ctx, sink):
    m = jnp.maximum(jnp.max(s_loc, axis=-1, keepdims=True), jnp.max(s_ctx, axis=-1, keepdims=True))
    if sink is not None:
        m = jnp.maximum(m, sink)
    p_loc = jnp.exp(s_loc - m)
    p_ctx = jnp.exp(s_ctx - m)
    l = jnp.sum(p_loc, axis=-1, keepdims=True) + jnp.sum(p_ctx, axis=-1, keepdims=True)
    if sink is not None:
        l = l + jnp.exp(sink - m)
    return p_loc.astype(BF16), p_ctx.astype(BF16), 1.0 / l


def _local_attention(scores, values, vc_ref, sink, o_ref, tq):
    n_sub = o_ref.shape[0] // tq
    s = [scores(sub) for sub in range(n_sub)]
    p = [_softmax(s_loc, s_ctx, sink) for s_loc, s_ctx in s]
    for sub, (p_loc, p_ctx, inv_l) in enumerate(p):
        o = (jnp.dot(p_loc, values(sub), preferred_element_type=F32)
             + jnp.dot(p_ctx, vc_ref[...], preferred_element_type=F32))
        o_ref[sub * tq:(sub + 1) * tq, :] = (o * inv_l).astype(o_ref.dtype)


def _nt_dot(a, b):
    return lax.dot_general(a, b, (((1,), (1,)), ((), ())), preferred_element_type=F32)


LOCAL_SUB_BLOCKS = 4


def _na_kernel(q_ref, k_ref, v_ref, kc_ref, vc_ref, bias_ref, o_ref, *, tq, wk, n_blk):
    n = k_ref.shape[0]
    rows_q = tq // GRID_W
    n_sub = q_ref.shape[0] // tq

    def kstart(sub):
        blk = pl.program_id(2) * n_sub + sub
        start = jnp.clip(blk * rows_q - NA_WIN_R // 2, 0, (n - wk) // GRID_W) * GRID_W
        return blk, pl.multiple_of(start, GRID_W)

    def scores(sub):
        blk, start = kstart(sub)
        cls = jnp.where(blk == 0, 0, jnp.where(blk == n_blk - 1, 2, 1))
        q = q_ref[sub * tq:(sub + 1) * tq, :]
        return _nt_dot(q, k_ref[pl.ds(start, wk), :]) + bias_ref[cls], _nt_dot(q, kc_ref[...])

    _local_attention(scores, lambda sub: v_ref[pl.ds(kstart(sub)[1], wk), :], vc_ref, None, o_ref, tq)


def _na_bias(rpb, rows, rows_q, rows_k):
    n_blk = rows // rows_q
    kr = NA_WIN_R
    cols = jnp.arange(GRID_W)
    col_start = jnp.clip(cols - NA_WIN_C // 2, 0, GRID_W - NA_WIN_C)
    col_ok = (cols[None, :] >= col_start[:, None]) & (cols[None, :] < col_start[:, None] + NA_WIN_C)
    col_off = jnp.clip(cols[None, :] - cols[:, None] + NA_WIN_C - 1, 0, 2 * NA_WIN_C - 2)
    out = []
    for blk in (0, 1, n_blk - 1):
        r = blk * rows_q + jnp.arange(rows_q)
        kstart = min(max(blk * rows_q - kr // 2, 0), rows - rows_k)
        krow = kstart + jnp.arange(rows_k)
        start = jnp.clip(r - kr // 2, 0, rows - kr)
        row_ok = (krow[None, :] >= start[:, None]) & (krow[None, :] < start[:, None] + kr)
        row_off = jnp.clip(krow[None, :] - r[:, None] + NA_WIN_R - 1, 0, 2 * NA_WIN_R - 2)
        bias = jnp.einsum('abr,hrc,qkc->habqk', jax.nn.one_hot(row_off, 2 * NA_WIN_R - 1, dtype=F32),
                          rpb.astype(F32), jax.nn.one_hot(col_off, 2 * NA_WIN_C - 1, dtype=F32),
                          precision=lax.Precision.HIGHEST)
        ok = row_ok[:, :, None, None] & col_ok[None, None]
        bias = jnp.where(ok[None], bias, NEG_INF).transpose(0, 1, 3, 2, 4)
        out.append(bias.reshape(rpb.shape[0], rows_q * GRID_W, rows_k * GRID_W))
    return jnp.stack(out, axis=1)


def neighbourhood_attention(p, pc, rpb, *, rows_q=4):
    b, n, _ = p.shape
    lc = pc.shape[1]
    h = rpb.shape[0]
    rows = n // GRID_W
    rows_k = rows_q + NA_WIN_R - 1
    assert rows % rows_q == 0 and rows >= rows_k and rows_q >= NA_WIN_R // 2
    tq, wk = rows_q * GRID_W, rows_k * GRID_W
    n_blk = rows // rows_q
    sub = LOCAL_SUB_BLOCKS if n_blk % LOCAL_SUB_BLOCKS == 0 else 1
    bias = _na_bias(rpb, rows, rows_q, rows_k)
    kern = functools.partial(_na_kernel, tq=tq, wk=wk, n_blk=n_blk)
    return pl.pallas_call(
        kern,
        grid=(b, h, n_blk // sub),
        in_specs=[
            pl.BlockSpec((None, sub * tq, HEAD_DIM), lambda bi, hi, i: (bi, i, COL_A_Q + hi)),
            pl.BlockSpec((None, n, HEAD_DIM), lambda bi, hi, i: (bi, 0, COL_A_K + hi)),
            pl.BlockSpec((None, n, HEAD_DIM), lambda bi, hi, i: (bi, 0, COL_A_V + hi)),
            pl.BlockSpec((None, lc, HEAD_DIM), lambda bi, hi, i: (bi, 0, COL_A_K + hi)),
            pl.BlockSpec((None, lc, HEAD_DIM), lambda bi, hi, i: (bi, 0, COL_A_V + hi)),
            pl.BlockSpec((None, 3, tq, wk), lambda bi, hi, i: (hi, 0, 0, 0)),
        ],
        out_specs=pl.BlockSpec((None, sub * tq, HEAD_DIM), lambda bi, hi, i: (bi, i, hi)),
        out_shape=jax.ShapeDtypeStruct((b, n, h * HEAD_DIM), BF16),
        compiler_params=_cparams("parallel", "parallel", "arbitrary"),
        name="neighbourhood_attention",
    )(p, p, p, pc, pc, bias)


def _swa_kernel(sink_ref, q_ref, k_ref, v_ref, kc_ref, vc_ref, o_ref, *, tq, wk):
    n = k_ref.shape[0]

    def window(sub):
        q0 = pl.program_id(2) * q_ref.shape[0] + sub * tq
        return q0, pl.multiple_of(jnp.clip(q0 - SWA_WINDOW, 0, n - wk), SWA_WINDOW)

    def scores(sub):
        q0, start = window(sub)
        q = q_ref[sub * tq:(sub + 1) * tq, :]
        qpos = q0 + lax.broadcasted_iota(jnp.int32, (tq, wk), 0)
        kpos = start + lax.broadcasted_iota(jnp.int32, (tq, wk), 1)
        s_loc = jnp.where(jnp.abs(kpos - qpos) <= SWA_WINDOW, _nt_dot(q, k_ref[pl.ds(start, wk), :]), NEG_INF)
        return s_loc, _nt_dot(q, kc_ref[...])

    _local_attention(scores, lambda sub: v_ref[pl.ds(window(sub)[1], wk), :], vc_ref,
                     sink_ref[pl.program_id(1)], o_ref, tq)


def window_attention(p, pc, sink, *, tq=256):
    b, n, _ = p.shape
    lc = pc.shape[1]
    hq = sink.shape[0]
    grp = 2
    tq = min(tq, n)
    wk = min(tq + 2 * SWA_WINDOW, n)
    sub = LOCAL_SUB_BLOCKS if (n // tq) % LOCAL_SUB_BLOCKS == 0 else 1
    kern = functools.partial(_swa_kernel, tq=tq, wk=wk)
    return pl.pallas_call(
        kern,
        grid=(b, hq, n // (sub * tq)),
        in_specs=[
            pl.BlockSpec(memory_space=pltpu.SMEM),
            pl.BlockSpec((None, sub * tq, HEAD_DIM), lambda bi, hi, i: (bi, i, COL_D_Q + hi)),
            pl.BlockSpec((None, n, HEAD_DIM), lambda bi, hi, i: (bi, 0, COL_D_K + hi // grp)),
            pl.BlockSpec((None, n, HEAD_DIM), lambda bi, hi, i: (bi, 0, COL_D_V + hi // grp)),
            pl.BlockSpec((None, lc, HEAD_DIM), lambda bi, hi, i: (bi, 0, COL_D_K + hi // grp)),
            pl.BlockSpec((None, lc, HEAD_DIM), lambda bi, hi, i: (bi, 0, COL_D_V + hi // grp)),
        ],
        out_specs=pl.BlockSpec((None, sub * tq, HEAD_DIM), lambda bi, hi, i: (bi, i, hi)),
        out_shape=jax.ShapeDtypeStruct((b, n, hq * HEAD_DIM), BF16),
        compiler_params=_cparams("parallel", "parallel", "arbitrary"),
        name="window_attention",
    )(sink, p, p, p, pc, pc)


DIFF_SLOTS = 2
DIFF_UNROLL = 6
DIFF_AUG_ROWS = 16


def _diff_attn_kernel(lam_ref, qt_ref, k_ref, vt_ref, kc_ref, vct_ref, g_ref, o_ref,
                      qbd_ref, m_ref, acc_ref, s_scr, x_scr, *, tq, tk, post_scale):
    qt = qt_ref[...].astype(F32)
    row = lax.broadcasted_iota(jnp.int32, qt.shape, 0)
    zero = jnp.zeros_like(qt)
    qbd_ref[:, :tq] = jnp.where(row < DIFF_DIM, qt, zero).astype(BF16)
    qbd_ref[:, tq:] = jnp.where(row >= DIFF_DIM, qt, zero).astype(BF16)

    def aug(vt_tile):
        r = lax.broadcasted_iota(jnp.int32, (DIFF_AUG_ROWS, vt_tile.shape[1]), 0)
        return jnp.concatenate([vt_tile, jnp.where(r == 0, 1.0, 0.0).astype(BF16)], axis=0)

    s = jnp.dot(kc_ref[...], qbd_ref[...], preferred_element_type=F32)
    m0 = jnp.max(s, axis=0, keepdims=True)
    m_ref[...] = m0
    acc_ref[...] = jnp.dot(aug(vct_ref[...]), jnp.exp2(s - m0).astype(BF16), preferred_element_type=F32)

    def scores(t, slot):
        off = pl.multiple_of(t * tk, tk)
        s = jnp.dot(k_ref[pl.ds(off, tk), :], qbd_ref[...], preferred_element_type=F32)
        s_scr[slot] = s
        x_scr[slot] = jnp.max(s, axis=0, keepdims=True)

    def accumulate(t, slot):
        m_old = m_ref[...]
        m_new = jnp.maximum(m_old, x_scr[slot])
        alpha = jnp.exp2(m_old - m_new)
        p = jnp.exp2(s_scr[slot] - m_new).astype(BF16)
        m_ref[...] = m_new
        off = pl.multiple_of(t * tk, tk)
        pv = jnp.dot(aug(vt_ref[:, pl.ds(off, tk)]), p, preferred_element_type=F32)
        acc_ref[...] = alpha * acc_ref[...] + pv

    def stage(t, t_mod, do_scores=True):
        if do_scores:
            scores(t + 1, (t_mod + 1) % DIFF_SLOTS)
        accumulate(t, t_mod % DIFF_SLOTS)

    n_kt = k_ref.shape[0] // tk
    trips = (n_kt - 1) // DIFF_UNROLL
    scores(0, 0)

    def body(i, carry):
        for u in range(DIFF_UNROLL):
            stage(DIFF_UNROLL * i + u, u)
        return carry

    lax.fori_loop(0, trips, body, 0)
    for t in range(DIFF_UNROLL * trips, n_kt - 1):
        stage(t, t)
    stage(n_kt - 1, n_kt - 1, do_scores=False)

    lam = lam_ref[0]
    inv = 1.0 / acc_ref[HEAD_DIM:HEAD_DIM + 1, :]
    acc = acc_ref[:HEAD_DIM, :]
    o_t = acc[:, :tq] * inv[:, :tq] - lam * (acc[:, tq:] * inv[:, tq:])
    o = o_t.T
    ms = jnp.mean(o * o, axis=-1, keepdims=True)
    y = o * lax.rsqrt(ms + NORM_EPS) * g_ref[...]
    o_ref[...] = (y * post_scale).astype(o_ref.dtype)


def diff_attention(p, t, pc, tc, lam, norm_g, post_scale, *, tq=512, tk=512):
    b, n, _ = p.shape
    lc = pc.shape[1]
    h = GROUP_BLOCKS
    tq = min(tq, n)
    tk = min(tk, n)
    assert n % tq == 0 and n % tk == 0 and DIFF_UNROLL % DIFF_SLOTS == 0
    kern = functools.partial(_diff_attn_kernel, tq=tq, tk=tk, post_scale=post_scale)
    return pl.pallas_call(
        kern,
        grid=(b, h, n // tq),
        in_specs=[
            pl.BlockSpec(memory_space=pltpu.SMEM),
            pl.BlockSpec((None, None, HEAD_DIM, tq), lambda bi, hi, qi: (bi, 0, hi, qi)),
            pl.BlockSpec((None, n, HEAD_DIM), lambda bi, hi, qi: (bi, 0, COL_B_K + hi)),
            pl.BlockSpec((None, None, HEAD_DIM, n), lambda bi, hi, qi: (bi, 1, hi, 0)),
            pl.BlockSpec((None, lc, HEAD_DIM), lambda bi, hi, qi: (bi, 0, COL_B_K + hi)),
            pl.BlockSpec((None, None, HEAD_DIM, lc), lambda bi, hi, qi: (bi, 1, hi, 0)),
            pl.BlockSpec((1, HEAD_DIM), lambda bi, hi, qi: (0, 0)),
        ],
        out_specs=pl.BlockSpec((None, tq, HEAD_DIM), lambda bi, hi, qi: (bi, qi, hi)),
        out_shape=jax.ShapeDtypeStruct((b, n, h * HEAD_DIM), BF16),
        scratch_shapes=[
            pltpu.VMEM((HEAD_DIM, 2 * tq), BF16),
            pltpu.VMEM((1, 2 * tq), F32),
            pltpu.VMEM((HEAD_DIM + DIFF_AUG_ROWS, 2 * tq), F32),
            pltpu.VMEM((DIFF_SLOTS, tk, 2 * tq), F32),
            pltpu.VMEM((DIFF_SLOTS, 1, 2 * tq), F32),
        ],
        compiler_params=_cparams("parallel", "parallel", "arbitrary"),
        name="diff_attention",
    )(lam, t, p, t, pc, tc, norm_g)


def _retention_kernel(cdec_ref, q_ref, k_ref, v_ref, intra_ref, qdec_ref, kdec_ref, s0_ref, *rest,
                      reverse, final):
    if final:
        of_ref, gate_ref, gng_ref, gnb_ref, o_ref, sfin_ref, s_ref = rest
    else:
        o_ref, sfin_ref, s_ref = rest
    i = pl.program_id(1)
    n_heads = s_ref.shape[0]
    n_chunks = q_ref.shape[0] // RET_CHUNK

    @pl.when(i == 0)
    def _():
        s_ref[...] = s0_ref[...]

    order = range(n_chunks - 1, -1, -1) if reverse else range(n_chunks)
    units = [(c, h) for c in order for h in range(n_heads)]
    window = lambda c, h: (slice(c * RET_CHUNK, (c + 1) * RET_CHUNK), slice(h * HEAD_DIM, (h + 1) * HEAD_DIM))

    intra_o, kv, qdec = {}, {}, {}
    for c, h in units:
        rs, cs = window(c, h)
        qh, kh, vh = q_ref[rs, cs], k_ref[rs, cs], v_ref[rs, cs]
        qdec[c, h] = (qh.astype(F32) * qdec_ref[h]).astype(BF16)
        kd_t = (kh.astype(F32) * kdec_ref[h]).T.astype(BF16)
        a = (_nt_dot(qh, kh) * intra_ref[h]).astype(BF16)
        intra_o[c, h] = jnp.dot(a, vh, preferred_element_type=F32)
        kv[c, h] = jnp.dot(kd_t, vh, preferred_element_type=F32)

    for c, h in units:
        rs, cs = window(c, h)
        if True:
            s = s_ref[h]
            o = intra_o[c, h] + jnp.dot(qdec[c, h], s.astype(BF16), preferred_element_type=F32)
            s_ref[h] = cdec_ref[h] * s + kv[c, h]
            if final:
                o = o + of_ref[rs, cs]
                mu = jnp.mean(o, axis=-1, keepdims=True)
                var = jnp.mean(jnp.square(o - mu), axis=-1, keepdims=True)
                y = (o - mu) * lax.rsqrt(var + GN_EPS) * gng_ref[:, cs] + gnb_ref[:, cs]
                g = gate_ref[rs, cs].astype(F32)
                o = g * (1.0 / (1.0 + jnp.exp(-g))) * y
            o_ref[rs, cs] = o.astype(o_ref.dtype)

    @pl.when(i == pl.num_programs(1) - 1)
    def _():
        sfin_ref[...] = s_ref[...]


def _retention_pass(p, tables, s0, final_inputs, *, reverse, blk_chunks=4):
    b, n, _ = p.shape
    cdec, intra, qdec, kdec = tables
    h = intra.shape[0]
    width = h * HEAD_DIM
    n_chunks = n // RET_CHUNK
    blk_chunks = min(blk_chunks, n_chunks)
    assert n_chunks % blk_chunks == 0
    tb = blk_chunks * RET_CHUNK
    n_blk = n // tb
    pos = (lambda i: n_blk - 1 - i) if reverse else (lambda i: i)
    colblk = lambda c: (lambda bi, i: (bi, pos(i), c // GROUP_BLOCKS))
    tab = pl.BlockSpec((h, RET_CHUNK, HEAD_DIM), lambda bi, i: (0, 0, 0))
    state = pl.BlockSpec((None, h, HEAD_DIM, HEAD_DIM), lambda bi, i: (bi, 0, 0, 0))
    in_specs = [pl.BlockSpec(memory_space=pltpu.SMEM),
                pl.BlockSpec((None, tb, width), colblk(COL_C_Q)),
                pl.BlockSpec((None, tb, width), colblk(COL_C_K)),
                pl.BlockSpec((None, tb, width), colblk(COL_C_V)),
                tab, tab, tab, state]
    args = [cdec, p, p, p, intra, qdec, kdec, s0]
    final = final_inputs is not None
    if final:
        o_fwd, gn_g, gn_b = final_inputs
        in_specs += [pl.BlockSpec((None, tb, width), lambda bi, i: (bi, pos(i), 0)),
                     pl.BlockSpec((None, tb, width), colblk(COL_C_G)),
                     pl.BlockSpec((1, width), lambda bi, i: (0, 0)),
                     pl.BlockSpec((1, width), lambda bi, i: (0, 0))]
        args += [o_fwd, p, gn_g, gn_b]
    kern = functools.partial(_retention_kernel, reverse=reverse, final=final)
    return pl.pallas_call(
        kern,
        grid=(b, n_blk),
        in_specs=in_specs,
        out_specs=[pl.BlockSpec((None, tb, width), lambda bi, i: (bi, pos(i), 0)), state],
        out_shape=[jax.ShapeDtypeStruct((b, n, width), BF16 if final else F32),
                   jax.ShapeDtypeStruct((b, h, HEAD_DIM, HEAD_DIM), F32)],
        scratch_shapes=[pltpu.VMEM((h, HEAD_DIM, HEAD_DIM), F32)],
        compiler_params=_cparams("parallel", "arbitrary"),
        name="retention_bwd" if reverse else "retention_fwd",
    )(*args)


def _retention_tables(decay, reverse):
    lg = jax.nn.log_sigmoid(decay.astype(F32))[:, None, None]
    pos = jnp.arange(RET_CHUNK, dtype=F32)
    rel = pos[:, None] - pos[None, :]
    if reverse:
        rel = -rel
        q_pow, k_pow = RET_CHUNK - pos, pos
    else:
        q_pow, k_pow = pos + 1.0, RET_CHUNK - 1.0 - pos
    intra = jnp.where(rel >= 0, jnp.exp(jnp.maximum(rel, 0.0) * lg), 0.0)
    bc = lambda e: jnp.broadcast_to(jnp.exp(e[None, :, None] * lg), intra.shape)
    cdec = jnp.exp(RET_CHUNK * lg[:, 0, 0])
    return cdec, intra, bc(q_pow), bc(k_pow)


def bidirectional_retention(p, pc, decay_f, decay_b, gn_g, gn_b, with_ctx):
    b = p.shape[0]
    h = decay_f.shape[0]
    tf = _retention_tables(decay_f, False)
    tb = _retention_tables(decay_b, True)
    s0 = jnp.zeros((b, h, HEAD_DIM, HEAD_DIM), F32)
    gn = (gn_g.reshape(1, -1).astype(F32), gn_b.reshape(1, -1).astype(F32))
    oc_f, s_f = _retention_pass(pc, tf, s0, None, reverse=False)
    oc, s_b = _retention_pass(pc, tb, s0, (oc_f,) + gn, reverse=True)
    o_f, _ = _retention_pass(p, tf, s_f, None, reverse=False)
    o, _ = _retention_pass(p, tb, s_b, (o_f,) + gn, reverse=True)
    return o, (oc if with_ctx else None)


OUT_SUB_ROWS = 256


def _out_kernel(oa_ref, ob_ref, oc_ref, od_ref, w_ref, x_ref, g1_ref, ng_ref, sc_ref, sh_ref, wr_ref,
                xo_ref, h_ref, aff_ref):
    gw = oa_ref.shape[1]
    rows = min(OUT_SUB_ROWS, x_ref.shape[0])
    subs = [slice(sub * rows, (sub + 1) * rows) for sub in range(x_ref.shape[0] // rows)]
    accs = []
    for rs in subs:
        acc = jnp.dot(oa_ref[rs, :], w_ref[0 * gw:1 * gw, :], preferred_element_type=F32)
        acc += jnp.dot(ob_ref[rs, :], w_ref[1 * gw:2 * gw, :], preferred_element_type=F32)
        acc += jnp.dot(oc_ref[rs, :], w_ref[2 * gw:3 * gw, :], preferred_element_type=F32)
        acc += jnp.dot(od_ref[rs, :], w_ref[3 * gw:4 * gw, :], preferred_element_type=F32)
        accs.append(acc)
    for rs, acc in zip(subs, accs):
        x = x_ref[rs, :] + g1_ref[...] * acc
        xo_ref[rs, :] = x
        y = x * lax.rsqrt(jnp.mean(x * x, axis=-1, keepdims=True) + NORM_EPS) * ng_ref[...]
        h2 = (y * (1.0 + sc_ref[...]) + sh_ref[...]).astype(BF16)
        h_ref[rs, :] = h2
        logits = _nt_dot(wr_ref[...], h2)
        e = jnp.exp(logits - jnp.max(logits, axis=0, keepdims=True))
        aff_ref[:, rs] = e * (1.0 / jnp.sum(e, axis=0, keepdims=True))


def out_project(o_groups, w_out, x, g1, norm_g, scale, shift, w_router_t, *, tm=2 * OUT_SUB_ROWS):
    b, n, d = x.shape
    n_e = w_router_t.shape[0]
    gw = o_groups[0].shape[-1]
    tm = min(tm, n)
    tile = lambda w: pl.BlockSpec((None, tm, w), lambda bi, i: (bi, i, 0))
    vec = pl.BlockSpec((None, 1, d), lambda bi, i: (bi, 0, 0))
    return pl.pallas_call(
        _out_kernel,
        grid=(b, n // tm),
        in_specs=[tile(gw)] * 4 + [
            pl.BlockSpec((d, d), lambda bi, i: (0, 0), pipeline_mode=pl.Buffered(1)),
            tile(d), vec,
            pl.BlockSpec((1, d), lambda bi, i: (0, 0)),
            vec, vec,
            pl.BlockSpec((n_e, d), lambda bi, i: (0, 0)),
        ],
        out_specs=[tile(d), tile(d), pl.BlockSpec((None, n_e, tm), lambda bi, i: (bi, 0, i))],
        out_shape=[jax.ShapeDtypeStruct((b, n, d), F32),
                   jax.ShapeDtypeStruct((b, n, d), BF16),
                   jax.ShapeDtypeStruct((b, n_e, n), F32)],
        compiler_params=_cparams("parallel", "parallel"),
        name="out_project",
    )(*o_groups, w_out, x, g1, norm_g, scale, shift, w_router_t)


def _ffn_kernel(x_ref, gate_ref, wg_ref, wu_ref, wd_ref, o_ref):
    x = x_ref[...]
    a = jnp.dot(x, wg_ref[...].astype(BF16), preferred_element_type=F32)
    u = jnp.dot(x, wu_ref[...].astype(BF16), preferred_element_type=F32)
    hm = (a * (1.0 / (1.0 + jnp.exp(-a))) * u).astype(BF16)
    y = jnp.dot(hm, wd_ref[...].astype(BF16), preferred_element_type=F32) * gate_ref[...]
    o_ref[...] = y.astype(o_ref.dtype)


def expert_ffn(xin, gate, w_gate, w_up, w_down, *, tc=256):
    b, n_e, cap, d = xin.shape
    ff = w_gate.shape[-1]
    tc = min(tc, cap)
    return pl.pallas_call(
        _ffn_kernel,
        grid=(n_e, b, cap // tc),
        in_specs=[
            pl.BlockSpec((None, None, tc, d), lambda e, bi, i: (bi, e, i, 0)),
            pl.BlockSpec((None, None, tc, 1), lambda e, bi, i: (bi, e, i, 0)),
            pl.BlockSpec((None, d, ff), lambda e, bi, i: (e, 0, 0), pipeline_mode=pl.Buffered(1)),
            pl.BlockSpec((None, d, ff), lambda e, bi, i: (e, 0, 0), pipeline_mode=pl.Buffered(1)),
            pl.BlockSpec((None, ff, d), lambda e, bi, i: (e, 0, 0), pipeline_mode=pl.Buffered(1)),
        ],
        out_specs=pl.BlockSpec((None, None, tc, d), lambda e, bi, i: (bi, e, i, 0)),
        out_shape=jax.ShapeDtypeStruct((b, n_e, cap, d), BF16),
        compiler_params=_cparams("parallel", "parallel", "parallel"),
        name="expert_ffn",
    )(xin, gate, w_gate, w_up, w_down)


SCATTER_GROUP = 64


def _scatter_kernel(rlo_ref, nrd_ref, x_ref, g_ref, ng_ref, idx_ref, y_hbm, o_ref, ybuf, sem, acc_ref,
                    base_ref, *, sr, final):
    bi, i, nt = pl.program_id(0), pl.program_id(1), pl.num_programs(1)
    n_e, n_grp = idx_ref.shape[0], idx_ref.shape[1]
    tm = x_ref.shape[0]
    n_rounds = nrd_ref[bi * nt + i]

    def group(e, k, tile):
        return jnp.minimum(rlo_ref[(bi * n_e + e) * nt + tile] + k, n_grp - 1)

    def slot_tokens(e, k):
        in_range = rlo_ref[(bi * n_e + e) * nt + i] + k < n_grp
        return jnp.where(in_range, idx_ref[e, group(e, k, i)], -1)

    def copies(k, slot, tile):
        return [pltpu.make_async_copy(
            y_hbm.at[bi, e, pl.ds(pl.multiple_of(group(e, k, tile) * sr, sr), sr), :],
            ybuf.at[slot, pl.ds(e * sr, sr), :], sem.at[slot]) for e in range(n_e)]

    @pl.when(i == 0)
    def _():
        base_ref[0] = 0
        for c in copies(0, 0, i):
            c.start()

    base = base_ref[0]
    acc_ref[...] = jnp.zeros(acc_ref.shape, F32)
    per_row = HEAD_DIM // sr
    tok = i * tm + lax.broadcasted_iota(jnp.int32, (tm, HEAD_DIM), 0)

    def body(k, carry):
        slot = (base + k) % 2
        for c in copies(k, slot, i):
            c.wait()

        @pl.when(k + 1 < n_rounds)
        def _():
            for c in copies(k + 1, 1 - slot, i):
                c.start()

        @pl.when(jnp.logical_and(k + 1 == n_rounds, i + 1 < nt))
        def _():
            for c in copies(0, 1 - slot, i + 1):
                c.start()

        rows = [jnp.concatenate([slot_tokens(e + j, k) for j in range(per_row)], axis=1)
                for e in range(0, n_e, per_row)]
        onehot = jnp.concatenate([jnp.where(tok == r, 1.0, 0.0) for r in rows], axis=1)
        acc_ref[...] += jnp.dot(onehot.astype(BF16), ybuf[slot], preferred_element_type=F32)
        return carry

    lax.fori_loop(0, n_rounds, body, 0)
    base_ref[0] = (base + n_rounds) % 2
    x = x_ref[...] + g_ref[...] * acc_ref[...]
    if final:
        x = x * lax.rsqrt(jnp.mean(x * x, axis=-1, keepdims=True) + NORM_EPS) * ng_ref[...]
    o_ref[...] = x


def scatter_combine(x, y, idx, g2, final_g, *, tm=256):
    b, t, d = x.shape
    n_e, cap = idx.shape[1], idx.shape[2]
    tm = min(tm, t)
    sr = min(SCATTER_GROUP, cap)
    assert t % tm == 0 and cap % sr == 0 and HEAD_DIM % sr == 0 and n_e % (HEAD_DIM // sr) == 0
    nt, n_grp = t // tm, cap // sr
    edges = jnp.arange(nt + 1, dtype=idx.dtype) * tm
    pos = jnp.sum(idx[..., None] < edges, axis=2, dtype=jnp.int32)
    lo, hi = pos[..., :-1], pos[..., 1:]
    rlo = jnp.minimum(lo // sr, n_grp - 1)
    rhi = jnp.maximum(hi - 1, lo) // sr
    n_rounds = jnp.max(jnp.minimum(rhi, n_grp - 1) - rlo + 1, axis=1)
    final = final_g is not None
    ng = final_g if final else jnp.ones((1, d), F32)
    tile = pl.BlockSpec((None, tm, d), lambda bi, i, *_: (bi, i, 0))
    return pl.pallas_call(
        functools.partial(_scatter_kernel, sr=sr, final=final),
        grid_spec=pltpu.PrefetchScalarGridSpec(
            num_scalar_prefetch=2,
            grid=(b, nt),
            in_specs=[tile,
                      pl.BlockSpec((None, 1, d), lambda bi, i, *_: (bi, 0, 0)),
                      pl.BlockSpec((1, d), lambda bi, i, *_: (0, 0)),
                      pl.BlockSpec((None, n_e, n_grp, 1, sr), lambda bi, i, *_: (bi, 0, 0, 0, 0)),
                      pl.BlockSpec(memory_space=pl.ANY)],
            out_specs=tile,
            scratch_shapes=[pltpu.VMEM((2, n_e * sr, d), BF16),
                            pltpu.SemaphoreType.DMA((2,)),
                            pltpu.VMEM((tm, d), F32),
                            pltpu.SMEM((1,), jnp.int32)]),
        out_shape=jax.ShapeDtypeStruct((b, t, d), F32),
        compiler_params=_cparams("parallel", "arbitrary"),
        name="scatter_combine",
    )(rlo.reshape(-1), n_rounds.reshape(-1), x, g2, ng, idx.reshape(b, n_e, n_grp, 1, sr), y)


def expert_choice_ffn(x, h2, aff_t, g2, final_g, w_gate, w_up, w_down):
    b, t, d = h2.shape
    cap = EC_CAPACITY * t // N_EXPERTS
    gate, idx = lax.top_k(aff_t, cap)
    idx, gate = lax.sort_key_val(idx, gate, dimension=-1)
    xin = jax.vmap(lambda hb, ib: hb[ib])(h2, idx)
    y = expert_ffn(xin, gate[..., None], w_gate, w_up, w_down)
    return scatter_combine(x, y, idx, g2, final_g)


def _ctx_attn_kernel(scal_ref, q_ref, k_ref, v_ref, g_ref, o_ref, *, kind, post_scale):
    q, k, v = q_ref[...], k_ref[...], v_ref[...]
    if kind == "diff":
        qf = q.astype(F32)
        lane = lax.broadcasted_iota(jnp.int32, qf.shape, 1)

        def probs(keep):
            s = _nt_dot(jnp.where(keep, qf, 0.0).astype(BF16), k)
            e = jnp.exp2(s - jnp.max(s, axis=-1, keepdims=True))
            return e * (1.0 / jnp.sum(e, axis=-1, keepdims=True))

        w = probs(lane < DIFF_DIM) - scal_ref[0] * probs(lane >= DIFF_DIM)
        o = jnp.dot(w.astype(BF16), v, preferred_element_type=F32)
        o = o * lax.rsqrt(jnp.mean(o * o, axis=-1, keepdims=True) + NORM_EPS) * g_ref[...] * post_scale
    else:
        s = _nt_dot(q, k)
        m = jnp.max(s, axis=-1, keepdims=True)
        if kind == "sink":
            sink = scal_ref[pl.program_id(1)]
            m = jnp.maximum(m, sink)
        e = jnp.exp(s - m)
        l = jnp.sum(e, axis=-1, keepdims=True)
        if kind == "sink":
            l = l + jnp.exp(sink - m)
        o = jnp.dot((e * (1.0 / l)).astype(BF16), v, preferred_element_type=F32)
    o_ref[...] = o.astype(o_ref.dtype)


def _context_attention(pc, scal, norm_g, kind, post_scale, q_col, k_col, v_col, grp):
    b, lc, _ = pc.shape
    h = GROUP_BLOCKS
    col = lambda c0, div: (lambda bi, hi: (bi, 0, c0 + hi // div))
    return pl.pallas_call(
        functools.partial(_ctx_attn_kernel, kind=kind, post_scale=post_scale),
        grid=(b, h),
        in_specs=[
            pl.BlockSpec(memory_space=pltpu.SMEM),
            pl.BlockSpec((None, lc, HEAD_DIM), col(q_col, 1)),
            pl.BlockSpec((None, lc, HEAD_DIM), col(k_col, grp)),
            pl.BlockSpec((None, lc, HEAD_DIM), col(v_col, grp)),
            pl.BlockSpec((1, HEAD_DIM), lambda bi, hi: (0, 0)),
        ],
        out_specs=pl.BlockSpec((None, lc, HEAD_DIM), lambda bi, hi: (bi, 0, hi)),
        out_shape=jax.ShapeDtypeStruct((b, lc, h * HEAD_DIM), BF16),
        compiler_params=_cparams("parallel", "parallel"),
        name="context_attention_" + kind,
    )(scal, pc, pc, pc, norm_g)


def _context_mixers(pc, oc_ret, lam, lam_init, diff_norm_g, swa_sink):
    h = GROUP_BLOCKS
    zeros = jnp.zeros((h,), F32)
    ones = jnp.ones((1, HEAD_DIM), F32)
    o_a = _context_attention(pc, zeros, ones, "plain", 1.0, COL_A_Q, COL_A_K, COL_A_V, 1)
    o_b = _context_attention(pc, lam.reshape(1), diff_norm_g.reshape(1, HEAD_DIM).astype(F32), "diff",
                             1.0 - lam_init, COL_B_K - h, COL_B_K, COL_B_K + h, 1)
    o_d = _context_attention(pc, swa_sink.astype(F32), ones, "sink", 1.0, COL_D_Q, COL_D_K, COL_D_V, 2)
    return [o_a, o_b, oc_ret, o_d]


def _layer(x, xc, mod, layer_idx, with_ctx, final_g, rope_h, rope_d, rope_id,
           norm1_g, w_in, na_rpb, diff_lambda, diff_norm_g, ret_decay_fwd, ret_decay_bwd, ret_gn_g, ret_gn_b,
           swa_sink, w_out, norm2_g, w_router, w_gate, w_up, w_down):
    b, n, d = x.shape
    row = lambda v: v.reshape(1, -1).astype(F32)
    part = lambda r0, r1, k: mod[r0:r1, None, k * d:(k + 1) * d]
    sh1, sc1, g1, sh2, sc2, g2 = [part(0, b, k) for k in range(6)]
    ctx_rows = lambda k: jnp.broadcast_to(part(b, b + 1, k), (b, 1, d))
    sh1c, sc1c, g1c, sh2c, sc2c, g2c = [ctx_rows(k) for k in range(6)]

    w_in_b = w_in.astype(BF16)
    w_out_b = w_out.astype(BF16)
    w_router_t = w_router.T.astype(BF16)

    p, t = norm_project(x, row(norm1_g), sc1, sh1, w_in_b, rope_h, rope_d)
    pc, tc = norm_project(xc, row(norm1_g), sc1c, sh1c, w_in_b, rope_id, rope_id)

    lam_init = 0.8 - 0.6 * math.exp(-0.3 * layer_idx)
    lq1, lk1, lq2, lk2 = [diff_lambda[k].astype(F32) for k in range(4)]
    lam = jnp.exp(jnp.sum(lq1 * lk1)) - jnp.exp(jnp.sum(lq2 * lk2)) + lam_init

    o_a = neighbourhood_attention(p, pc, na_rpb)
    o_b = diff_attention(p, t, pc, tc, lam.reshape(1), row(diff_norm_g), 1.0 - lam_init)
    o_c, oc_ret = bidirectional_retention(p, pc, ret_decay_fwd, ret_decay_bwd, ret_gn_g, ret_gn_b, with_ctx)
    o_d = window_attention(p, pc, swa_sink.astype(F32))

    x, h2, aff_t = out_project([o_a, o_b, o_c, o_d], w_out_b, x, g1, row(norm2_g), sc2, sh2, w_router_t)
    x = expert_choice_ffn(x, h2, aff_t, g2, final_g, w_gate, w_up, w_down)

    if with_ctx:
        oc = _context_mixers(pc, oc_ret, lam, lam_init, diff_norm_g, swa_sink)
        xc, h2c, aff_c = out_project(oc, w_out_b, xc, g1c, row(norm2_g), sc2c, sh2c, w_router_t)
        xc = expert_choice_ffn(xc, h2c, aff_c, g2c, None, w_gate, w_up, w_down)
    return x, xc


def kernel(x, c, ctx, c_ctx, w_mod, b_mod, norm1_g, w_in, na_rpb, diff_lambda, diff_norm_g,
           ret_decay_fwd, ret_decay_bwd, ret_gn_g, ret_gn_b, swa_sink, w_out, norm2_g,
           w_router, w_gate, w_up, w_down, final_norm_g):
    b, n, d = x.shape
    depth = w_in.shape[0]
    lc = ctx.shape[1]
    cvec = jnp.zeros((8, d), F32).at[:b].set(c).at[b].set(c_ctx)
    mod = modulation(cvec, w_mod, b_mod.reshape(depth, 1, -1))
    rope_h = _rope_tables(n, HEAD_DIM)
    rope_d = _rope_tables(n, DIFF_DIM)
    rope_id = _identity_rope_tables(lc)
    xc = ctx
    for li in range(depth):
        last = li == depth - 1
        x, xc = _layer(x, xc, mod[li], li, not last, final_norm_g.reshape(1, -1) if last else None,
                       rope_h, rope_d, rope_id,
                       norm1_g[li], w_in[li], na_rpb[li], diff_lambda[li], diff_norm_g[li],
                       ret_decay_fwd[li], ret_decay_bwd[li], ret_gn_g[li], ret_gn_b[li], swa_sink[li],
                       w_out[li], norm2_g[li], w_router[li], w_gate[li], w_up[li], w_down[li])
    return x
```

```python
import functools
import math

import jax
import jax.numpy as jnp
from jax import lax
from jax.experimental import pallas as pl
from jax.experimental.pallas import tpu as pltpu

GRID_W = 64
HEAD_DIM = 128
DIFF_DIM = HEAD_DIM // 2
NA_WIN_R = 8
NA_WIN_C = 16
RET_CHUNK = 128
SWA_WINDOW = 128
N_EXPERTS = 16
EC_CAPACITY = 2
ROPE_BASE = 10000.0
NORM_EPS = 1e-6
GN_EPS = 1e-5
NEG_INF = -1e30
LOG2E = math.log2(math.e)

BF16 = jnp.bfloat16
F32 = jnp.float32

VMEM_LIMIT_BYTES = 48 * 1024 * 1024

GROUP_BLOCKS = 4
COL_A_Q, COL_A_K, COL_A_V = 0, 4, 8
COL_B_K = 16
COL_C_Q, COL_C_K, COL_C_V, COL_C_G = 24, 28, 32, 36
COL_D_Q, COL_D_K, COL_D_V = 40, 44, 46
PROJ_TN = GROUP_BLOCKS * HEAD_DIM


def _cparams(*sem):
    return pltpu.CompilerParams(dimension_semantics=sem, vmem_limit_bytes=VMEM_LIMIT_BYTES)


def _mod_kernel(c_ref, w_ref, b_ref, o_ref):
    c = c_ref[...]
    s = (c * (1.0 / (1.0 + jnp.exp(-c)))).astype(BF16)
    o_ref[...] = jnp.dot(s, w_ref[...].astype(BF16), preferred_element_type=F32) + b_ref[...]


def modulation(cvec, w_mod, b_mod, *, tn=1024):
    depth, d, n6 = w_mod.shape
    return pl.pallas_call(
        _mod_kernel,
        grid=(depth, n6 // tn),
        in_specs=[
            pl.BlockSpec((8, d), lambda l, j: (0, 0)),
            pl.BlockSpec((None, d, tn), lambda l, j: (l, 0, j)),
            pl.BlockSpec((None, 1, tn), lambda l, j: (l, 0, j)),
        ],
        out_specs=pl.BlockSpec((None, 8, tn), lambda l, j: (l, 0, j)),
        out_shape=jax.ShapeDtypeStruct((depth, 8, n6), F32),
        compiler_params=_cparams("parallel", "parallel"),
        name="modulation",
    )(cvec, w_mod, b_mod)


def _rope_tables(n_tok, dim):
    t = jnp.arange(n_tok)
    row = (t // GRID_W).astype(F32)[:, None]
    col = (t % GRID_W).astype(F32)[:, None]
    nf = dim // 4
    lane = jnp.arange(HEAD_DIM)
    quarter = (lane % dim) // nf
    inv = ROPE_BASE ** (-jnp.arange(nf, dtype=F32) / nf)
    ang = jnp.where(quarter[None, :] < 2, row, col) * inv[lane % nf][None, :]
    cos, sin = jnp.cos(ang), jnp.sin(ang)
    even = (quarter % 2 == 0)[None, :]
    return cos, jnp.where(even, -sin, 0.0), jnp.where(even, 0.0, sin)


def _identity_rope_tables(n_tok):
    z = jnp.zeros((n_tok, HEAD_DIM), F32)
    return jnp.ones((n_tok, HEAD_DIM), F32), z, z


def _proj_modes():
    att = HEAD_DIM ** -0.5
    plain = [(None, 1.0)] * GROUP_BLOCKS
    return [
        ([(None, att)] * 4, None), (plain, None), (plain, None),
        ([("d", LOG2E * DIFF_DIM ** -0.5)] * 4, 0), ([("d", 1.0)] * 4, None), (plain, 1),
        ([("h", 1.0)] * 4, None), ([("h", att)] * 4, None), (plain, None), (plain, None),
        ([("h", att)] * 4, None), ([("h", 1.0)] * 2 + [(None, 1.0)] * 2, None),
    ]


def _proj_kernel(x_ref, g_ref, sc_ref, sh_ref, w_ref, ch_ref, sah_ref, sbh_ref, cd_ref, sad_ref, sbd_ref,
                 o_ref, t_ref):
    x = x_ref[...]
    y = x * lax.rsqrt(jnp.mean(x * x, axis=-1, keepdims=True) + NORM_EPS) * g_ref[...]
    h = (y * (1.0 + sc_ref[...]) + sh_ref[...]).astype(BF16)

    def rope(a, kind):
        if kind is None:
            return a
        c, sa, sb, sh = ((ch_ref, sah_ref, sbh_ref, HEAD_DIM // 4) if kind == "h"
                         else (cd_ref, sad_ref, sbd_ref, DIFF_DIM // 4))
        return (a * c[...] + pltpu.roll(a, HEAD_DIM - sh, 1) * sa[...] + pltpu.roll(a, sh, 1) * sb[...])

    for j, (blocks, t_slot) in enumerate(_proj_modes()):
        cols = slice(j * PROJ_TN, (j + 1) * PROJ_TN)
        acc = jnp.dot(h, w_ref[:, cols], preferred_element_type=F32)
        outs = []
        for hb, (kind, scale) in enumerate(blocks):
            a = rope(acc[:, hb * HEAD_DIM:(hb + 1) * HEAD_DIM], kind)
            outs.append(a if scale == 1.0 else a * scale)
        full = jnp.concatenate(outs, axis=1)
        o_ref[:, cols] = full.astype(o_ref.dtype)
        if t_slot is not None:
            t_ref[t_slot] = full.T.astype(t_ref.dtype)


def norm_project(x, norm_g, scale, shift, w_in, rope_h, rope_d, *, tm=256):
    b, n, d = x.shape
    width = w_in.shape[1]
    assert width == len(_proj_modes()) * PROJ_TN
    tm = min(tm, n)
    tok = lambda bi, i: (i, 0)
    return pl.pallas_call(
        _proj_kernel,
        grid=(b, n // tm),
        in_specs=[
            pl.BlockSpec((None, tm, d), lambda bi, i: (bi, i, 0)),
            pl.BlockSpec((1, d), lambda bi, i: (0, 0)),
            pl.BlockSpec((None, 1, d), lambda bi, i: (bi, 0, 0)),
            pl.BlockSpec((None, 1, d), lambda bi, i: (bi, 0, 0)),
            pl.BlockSpec((d, width), lambda bi, i: (0, 0), pipeline_mode=pl.Buffered(1)),
        ] + [pl.BlockSpec((tm, HEAD_DIM), tok)] * 6,
        out_specs=[
            pl.BlockSpec((None, tm, width), lambda bi, i: (bi, i, 0)),
            pl.BlockSpec((None, 2, PROJ_TN, tm), lambda bi, i: (bi, 0, 0, i)),
        ],
        out_shape=[jax.ShapeDtypeStruct((b, n, width), BF16),
                   jax.ShapeDtypeStruct((b, 2, PROJ_TN, n), BF16)],
        compiler_params=_cparams("parallel", "parallel"),
        name="norm_project",
    )(x, norm_g, scale, shift, w_in, *rope_h, *rope_d)


def _softmax(s_loc, s_ctx, sink):
    m = jnp.maximum(jnp.max(s_loc, axis=-1, keepdims=True), jnp.max(s_ctx, axis=-1, keepdims=True))
    if sink is not None:
        m = jnp.maximum(m, sink)
    p_loc = jnp.exp(s_loc - m)
    p_ctx = jnp.exp(s_ctx - m)
    l = jnp.sum(p_loc, axis=-1, keepdims=True) + jnp.sum(p_ctx, axis=-1, keepdims=True)
    if sink is not None:
        l = l + jnp.exp(sink - m)
    return p_loc.astype(BF16), p_ctx.astype(BF16), 1.0 / l


def _local_attention(scores, values, vc_ref, sink, o_ref, tq):
    n_sub = o_ref.shape[0] // tq
    s = [scores(sub) for sub in range(n_sub)]
    p = [_softmax(s_loc, s_ctx, sink) for s_loc, s_ctx in s]
    for sub, (p_loc, p_ctx, inv_l) in enumerate(p):
        o = (jnp.dot(p_loc, values(sub), preferred_element_type=F32)
             + jnp.dot(p_ctx, vc_ref[...], preferred_element_type=F32))
        o_ref[sub * tq:(sub + 1) * tq, :] = (o * inv_l).astype(o_ref.dtype)


def _nt_dot(a, b):
    return lax.dot_general(a, b, (((1,), (1,)), ((), ())), preferred_element_type=F32)


LOCAL_SUB_BLOCKS = 4


def _na_kernel(q_ref, k_ref, v_ref, kc_ref, vc_ref, bias_ref, o_ref, *, tq, wk, n_blk):
    n = k_ref.shape[0]
    rows_q = tq // GRID_W
    n_sub = q_ref.shape[0] // tq

    def kstart(sub):
        blk = pl.program_id(2) * n_sub + sub
        start = jnp.clip(blk * rows_q - NA_WIN_R // 2, 0, (n - wk) // GRID_W) * GRID_W
        return blk, pl.multiple_of(start, GRID_W)

    def scores(sub):
        blk, start = kstart(sub)
        cls = jnp.where(blk == 0, 0, jnp.where(blk == n_blk - 1, 2, 1))
        q = q_ref[sub * tq:(sub + 1) * tq, :]
        return _nt_dot(q, k_ref[pl.ds(start, wk), :]) + bias_ref[cls], _nt_dot(q, kc_ref[...])

    _local_attention(scores, lambda sub: v_ref[pl.ds(kstart(sub)[1], wk), :], vc_ref, None, o_ref, tq)


def _na_bias(rpb, rows, rows_q, rows_k):
    n_blk = rows // rows_q
    kr = NA_WIN_R
    cols = jnp.arange(GRID_W)
    col_start = jnp.clip(cols - NA_WIN_C // 2, 0, GRID_W - NA_WIN_C)
    col_ok = (cols[None, :] >= col_start[:, None]) & (cols[None, :] < col_start[:, None] + NA_WIN_C)
    col_off = jnp.clip(cols[None, :] - cols[:, None] + NA_WIN_C - 1, 0, 2 * NA_WIN_C - 2)
    out = []
    for blk in (0, 1, n_blk - 1):
        r = blk * rows_q + jnp.arange(rows_q)
        kstart = min(max(blk * rows_q - kr // 2, 0), rows - rows_k)
        krow = kstart + jnp.arange(rows_k)
        start = jnp.clip(r - kr // 2, 0, rows - kr)
        row_ok = (krow[None, :] >= start[:, None]) & (krow[None, :] < start[:, None] + kr)
        row_off = jnp.clip(krow[None, :] - r[:, None] + NA_WIN_R - 1, 0, 2 * NA_WIN_R - 2)
        bias = jnp.einsum('abr,hrc,qkc->habqk', jax.nn.one_hot(row_off, 2 * NA_WIN_R - 1, dtype=F32),
                          rpb.astype(F32), jax.nn.one_hot(col_off, 2 * NA_WIN_C - 1, dtype=F32),
                          precision=lax.Precision.HIGHEST)
        ok = row_ok[:, :, None, None] & col_ok[None, None]
        bias = jnp.where(ok[None], bias, NEG_INF).transpose(0, 1, 3, 2, 4)
        out.append(bias.reshape(rpb.shape[0], rows_q * GRID_W, rows_k * GRID_W))
    return jnp.stack(out, axis=1)


def neighbourhood_attention(p, pc, rpb, *, rows_q=4):
    b, n, _ = p.shape
    lc = pc.shape[1]
    h = rpb.shape[0]
    rows = n // GRID_W
    rows_k = rows_q + NA_WIN_R - 1
    assert rows % rows_q == 0 and rows >= rows_k and rows_q >= NA_WIN_R // 2
    tq, wk = rows_q * GRID_W, rows_k * GRID_W
    n_blk = rows // rows_q
    sub = LOCAL_SUB_BLOCKS if n_blk % LOCAL_SUB_BLOCKS == 0 else 1
    bias = _na_bias(rpb, rows, rows_q, rows_k)
    kern = functools.partial(_na_kernel, tq=tq, wk=wk, n_blk=n_blk)
    return pl.pallas_call(
        kern,
        grid=(b, h, n_blk // sub),
        in_specs=[
            pl.BlockSpec((None, sub * tq, HEAD_DIM), lambda bi, hi, i: (bi, i, COL_A_Q + hi)),
            pl.BlockSpec((None, n, HEAD_DIM), lambda bi, hi, i: (bi, 0, COL_A_K + hi)),
            pl.BlockSpec((None, n, HEAD_DIM), lambda bi, hi, i: (bi, 0, COL_A_V + hi)),
            pl.BlockSpec((None, lc, HEAD_DIM), lambda bi, hi, i: (bi, 0, COL_A_K + hi)),
            pl.BlockSpec((None, lc, HEAD_DIM), lambda bi, hi, i: (bi, 0, COL_A_V + hi)),
            pl.BlockSpec((None, 3, tq, wk), lambda bi, hi, i: (hi, 0, 0, 0)),
        ],
        out_specs=pl.BlockSpec((None, sub * tq, HEAD_DIM), lambda bi, hi, i: (bi, i, hi)),
        out_shape=jax.ShapeDtypeStruct((b, n, h * HEAD_DIM), BF16),
        compiler_params=_cparams("parallel", "parallel", "arbitrary"),
        name="neighbourhood_attention",
    )(p, p, p, pc, pc, bias)


def _swa_kernel(sink_ref, q_ref, k_ref, v_ref, kc_ref, vc_ref, o_ref, *, tq, wk):
    n = k_ref.shape[0]

    def window(sub):
        q0 = pl.program_id(2) * q_ref.shape[0] + sub * tq
        return q0, pl.multiple_of(jnp.clip(q0 - SWA_WINDOW, 0, n - wk), SWA_WINDOW)

    def scores(sub):
        q0, start = window(sub)
        q = q_ref[sub * tq:(sub + 1) * tq, :]
        qpos = q0 + lax.broadcasted_iota(jnp.int32, (tq, wk), 0)
        kpos = start + lax.broadcasted_iota(jnp.int32, (tq, wk), 1)
        s_loc = jnp.where(jnp.abs(kpos - qpos) <= SWA_WINDOW, _nt_dot(q, k_ref[pl.ds(start, wk), :]), NEG_INF)
        return s_loc, _nt_dot(q, kc_ref[...])

    _local_attention(scores, lambda sub: v_ref[pl.ds(window(sub)[1], wk), :], vc_ref,
                     sink_ref[pl.program_id(1)], o_ref, tq)


def window_attention(p, pc, sink, *, tq=256):
    b, n, _ = p.shape
    lc = pc.shape[1]
    hq = sink.shape[0]
    grp = 2
    tq = min(tq, n)
    wk = min(tq + 2 * SWA_WINDOW, n)
    sub = LOCAL_SUB_BLOCKS if (n // tq) % LOCAL_SUB_BLOCKS == 0 else 1
    kern = functools.partial(_swa_kernel, tq=tq, wk=wk)
    return pl.pallas_call(
        kern,
        grid=(b, hq, n // (sub * tq)),
        in_specs=[
            pl.BlockSpec(memory_space=pltpu.SMEM),
            pl.BlockSpec((None, sub * tq, HEAD_DIM), lambda bi, hi, i: (bi, i, COL_D_Q + hi)),
            pl.BlockSpec((None, n, HEAD_DIM), lambda bi, hi, i: (bi, 0, COL_D_K + hi // grp)),
            pl.BlockSpec((None, n, HEAD_DIM), lambda bi, hi, i: (bi, 0, COL_D_V + hi // grp)),
            pl.BlockSpec((None, lc, HEAD_DIM), lambda bi, hi, i: (bi, 0, COL_D_K + hi // grp)),
            pl.BlockSpec((None, lc, HEAD_DIM), lambda bi, hi, i: (bi, 0, COL_D_V + hi // grp)),
        ],
        out_specs=pl.BlockSpec((None, sub * tq, HEAD_DIM), lambda bi, hi, i: (bi, i, hi)),
        out_shape=jax.ShapeDtypeStruct((b, n, hq * HEAD_DIM), BF16),
        compiler_params=_cparams("parallel", "parallel", "arbitrary"),
        name="window_attention",
    )(sink, p, p, p, pc, pc)


DIFF_SLOTS = 2
DIFF_UNROLL = 6
DIFF_AUG_ROWS = 16


def _diff_attn_kernel(lam_ref, qt_ref, k_ref, vt_ref, kc_ref, vct_ref, g_ref, o_ref,
                      qbd_ref, m_ref, acc_ref, s_scr, x_scr, *, tq, tk, post_scale):
    qt = qt_ref[...].astype(F32)
    row = lax.broadcasted_iota(jnp.int32, qt.shape, 0)
    zero = jnp.zeros_like(qt)
    qbd_ref[:, :tq] = jnp.where(row < DIFF_DIM, qt, zero).astype(BF16)
    qbd_ref[:, tq:] = jnp.where(row >= DIFF_DIM, qt, zero).astype(BF16)

    def aug(vt_tile):
        r = lax.broadcasted_iota(jnp.int32, (DIFF_AUG_ROWS, vt_tile.shape[1]), 0)
        return jnp.concatenate([vt_tile, jnp.where(r == 0, 1.0, 0.0).astype(BF16)], axis=0)

    s = jnp.dot(kc_ref[...], qbd_ref[...], preferred_element_type=F32)
    m0 = jnp.max(s, axis=0, keepdims=True)
    m_ref[...] = m0
    acc_ref[...] = jnp.dot(aug(vct_ref[...]), jnp.exp2(s - m0).astype(BF16), preferred_element_type=F32)

    def scores(t, slot):
        off = pl.multiple_of(t * tk, tk)
        s = jnp.dot(k_ref[pl.ds(off, tk), :], qbd_ref[...], preferred_element_type=F32)
        s_scr[slot] = s
        x_scr[slot] = jnp.max(s, axis=0, keepdims=True)

    def accumulate(t, slot):
        m_old = m_ref[...]
        m_new = jnp.maximum(m_old, x_scr[slot])
        alpha = jnp.exp2(m_old - m_new)
        p = jnp.exp2(s_scr[slot] - m_new).astype(BF16)
        m_ref[...] = m_new
        off = pl.multiple_of(t * tk, tk)
        pv = jnp.dot(aug(vt_ref[:, pl.ds(off, tk)]), p, preferred_element_type=F32)
        acc_ref[...] = alpha * acc_ref[...] + pv

    def stage(t, t_mod, do_scores=True):
        if do_scores:
            scores(t + 1, (t_mod + 1) % DIFF_SLOTS)
        accumulate(t, t_mod % DIFF_SLOTS)

    n_kt = k_ref.shape[0] // tk
    trips = (n_kt - 1) // DIFF_UNROLL
    scores(0, 0)

    def body(i, carry):
        for u in range(DIFF_UNROLL):
            stage(DIFF_UNROLL * i + u, u)
        return carry

    lax.fori_loop(0, trips, body, 0)
    for t in range(DIFF_UNROLL * trips, n_kt - 1):
        stage(t, t)
    stage(n_kt - 1, n_kt - 1, do_scores=False)

    lam = lam_ref[0]
    inv = 1.0 / acc_ref[HEAD_DIM:HEAD_DIM + 1, :]
    acc = acc_ref[:HEAD_DIM, :]
    o_t = acc[:, :tq] * inv[:, :tq] - lam * (acc[:, tq:] * inv[:, tq:])
    o = o_t.T
    ms = jnp.mean(o * o, axis=-1, keepdims=True)
    y = o * lax.rsqrt(ms + NORM_EPS) * g_ref[...]
    o_ref[...] = (y * post_scale).astype(o_ref.dtype)


def diff_attention(p, t, pc, tc, lam, norm_g, post_scale, *, tq=512, tk=512):
    b, n, _ = p.shape
    lc = pc.shape[1]
    h = GROUP_BLOCKS
    tq = min(tq, n)
    tk = min(tk, n)
    assert n % tq == 0 and n % tk == 0 and DIFF_UNROLL % DIFF_SLOTS == 0
    kern = functools.partial(_diff_attn_kernel, tq=tq, tk=tk, post_scale=post_scale)
    return pl.pallas_call(
        kern,
        grid=(b, h, n // tq),
        in_specs=[
            pl.BlockSpec(memory_space=pltpu.SMEM),
            pl.BlockSpec((None, None, HEAD_DIM, tq), lambda bi, hi, qi: (bi, 0, hi, qi)),
            pl.BlockSpec((None, n, HEAD_DIM), lambda bi, hi, qi: (bi, 0, COL_B_K + hi)),
            pl.BlockSpec((None, None, HEAD_DIM, n), lambda bi, hi, qi: (bi, 1, hi, 0)),
            pl.BlockSpec((None, lc, HEAD_DIM), lambda bi, hi, qi: (bi, 0, COL_B_K + hi)),
            pl.BlockSpec((None, None, HEAD_DIM, lc), lambda bi, hi, qi: (bi, 1, hi, 0)),
            pl.BlockSpec((1, HEAD_DIM), lambda bi, hi, qi: (0, 0)),
        ],
        out_specs=pl.BlockSpec((None, tq, HEAD_DIM), lambda bi, hi, qi: (bi, qi, hi)),
        out_shape=jax.ShapeDtypeStruct((b, n, h * HEAD_DIM), BF16),
        scratch_shapes=[
            pltpu.VMEM((HEAD_DIM, 2 * tq), BF16),
            pltpu.VMEM((1, 2 * tq), F32),
            pltpu.VMEM((HEAD_DIM + DIFF_AUG_ROWS, 2 * tq), F32),
            pltpu.VMEM((DIFF_SLOTS, tk, 2 * tq), F32),
            pltpu.VMEM((DIFF_SLOTS, 1, 2 * tq), F32),
        ],
        compiler_params=_cparams("parallel", "parallel", "arbitrary"),
        name="diff_attention",
    )(lam, t, p, t, pc, tc, norm_g)


def _retention_kernel(cdec_ref, q_ref, k_ref, v_ref, intra_ref, qdec_ref, kdec_ref, s0_ref, *rest,
                      reverse, final):
    if final:
        of_ref, gate_ref, gng_ref, gnb_ref, o_ref, sfin_ref, s_ref = rest
    else:
        o_ref, sfin_ref, s_ref = rest
    i = pl.program_id(1)
    n_heads = s_ref.shape[0]
    n_chunks = q_ref.shape[0] // RET_CHUNK

    @pl.when(i == 0)
    def _():
        s_ref[...] = s0_ref[...]

    order = range(n_chunks - 1, -1, -1) if reverse else range(n_chunks)
    units = [(c, h) for c in order for h in range(n_heads)]
    window = lambda c, h: (slice(c * RET_CHUNK, (c + 1) * RET_CHUNK), slice(h * HEAD_DIM, (h + 1) * HEAD_DIM))

    intra_o, kv, qdec = {}, {}, {}
    for c, h in units:
        rs, cs = window(c, h)
        qh, kh, vh = q_ref[rs, cs], k_ref[rs, cs], v_ref[rs, cs]
        qdec[c, h] = (qh.astype(F32) * qdec_ref[h]).astype(BF16)
        kd_t = (kh.astype(F32) * kdec_ref[h]).T.astype(BF16)
        a = (_nt_dot(qh, kh) * intra_ref[h]).astype(BF16)
        intra_o[c, h] = jnp.dot(a, vh, preferred_element_type=F32)
        kv[c, h] = jnp.dot(kd_t, vh, preferred_element_type=F32)

    for c, h in units:
        rs, cs = window(c, h)
        if True:
            s = s_ref[h]
            o = intra_o[c, h] + jnp.dot(qdec[c, h], s.astype(BF16), preferred_element_type=F32)
            s_ref[h] = cdec_ref[h] * s + kv[c, h]
            if final:
                o = o + of_ref[rs, cs]
                mu = jnp.mean(o, axis=-1, keepdims=True)
                var = jnp.mean(jnp.square(o - mu), axis=-1, keepdims=True)
                y = (o - mu) * lax.rsqrt(var + GN_EPS) * gng_ref[:, cs] + gnb_ref[:, cs]
                g = gate_ref[rs, cs].astype(F32)
                o = g * (1.0 / (1.0 + jnp.exp(-g))) * y
            o_ref[rs, cs] = o.astype(o_ref.dtype)

    @pl.when(i == pl.num_programs(1) - 1)
    def _():
        sfin_ref[...] = s_ref[...]


def _retention_pass(p, tables, s0, final_inputs, *, reverse, blk_chunks=4):
    b, n, _ = p.shape
    cdec, intra, qdec, kdec = tables
    h = intra.shape[0]
    width = h * HEAD_DIM
    n_chunks = n // RET_CHUNK
    blk_chunks = min(blk_chunks, n_chunks)
    assert n_chunks % blk_chunks == 0
    tb = blk_chunks * RET_CHUNK
    n_blk = n // tb
    pos = (lambda i: n_blk - 1 - i) if reverse else (lambda i: i)
    colblk = lambda c: (lambda bi, i: (bi, pos(i), c // GROUP_BLOCKS))
    tab = pl.BlockSpec((h, RET_CHUNK, HEAD_DIM), lambda bi, i: (0, 0, 0))
    state = pl.BlockSpec((None, h, HEAD_DIM, HEAD_DIM), lambda bi, i: (bi, 0, 0, 0))
    in_specs = [pl.BlockSpec(memory_space=pltpu.SMEM),
                pl.BlockSpec((None, tb, width), colblk(COL_C_Q)),
                pl.BlockSpec((None, tb, width), colblk(COL_C_K)),
                pl.BlockSpec((None, tb, width), colblk(COL_C_V)),
                tab, tab, tab, state]
    args = [cdec, p, p, p, intra, qdec, kdec, s0]
    final = final_inputs is not None
    if final:
        o_fwd, gn_g, gn_b = final_inputs
        in_specs += [pl.BlockSpec((None, tb, width), lambda bi, i: (bi, pos(i), 0)),
                     pl.BlockSpec((None, tb, width), colblk(COL_C_G)),
                     pl.BlockSpec((1, width), lambda bi, i: (0, 0)),
                     pl.BlockSpec((1, width), lambda bi, i: (0, 0))]
        args += [o_fwd, p, gn_g, gn_b]
    kern = functools.partial(_retention_kernel, reverse=reverse, final=final)
    return pl.pallas_call(
        kern,
        grid=(b, n_blk),
        in_specs=in_specs,
        out_specs=[pl.BlockSpec((None, tb, width), lambda bi, i: (bi, pos(i), 0)), state],
        out_shape=[jax.ShapeDtypeStruct((b, n, width), BF16 if final else F32),
                   jax.ShapeDtypeStruct((b, h, HEAD_DIM, HEAD_DIM), F32)],
        scratch_shapes=[pltpu.VMEM((h, HEAD_DIM, HEAD_DIM), F32)],
        compiler_params=_cparams("parallel", "arbitrary"),
        name="retention_bwd" if reverse else "retention_fwd",
    )(*args)


def _retention_tables(decay, reverse):
    lg = jax.nn.log_sigmoid(decay.astype(F32))[:, None, None]
    pos = jnp.arange(RET_CHUNK, dtype=F32)
    rel = pos[:, None] - pos[None, :]
    if reverse:
        rel = -rel
        q_pow, k_pow = RET_CHUNK - pos, pos
    else:
        q_pow, k_pow = pos + 1.0, RET_CHUNK - 1.0 - pos
    intra = jnp.where(rel >= 0, jnp.exp(jnp.maximum(rel, 0.0) * lg), 0.0)
    bc = lambda e: jnp.broadcast_to(jnp.exp(e[None, :, None] * lg), intra.shape)
    cdec = jnp.exp(RET_CHUNK * lg[:, 0, 0])
    return cdec, intra, bc(q_pow), bc(k_pow)


def bidirectional_retention(p, pc, decay_f, decay_b, gn_g, gn_b, with_ctx):
    b = p.shape[0]
    h = decay_f.shape[0]
    tf = _retention_tables(decay_f, False)
    tb = _retention_tables(decay_b, True)
    s0 = jnp.zeros((b, h, HEAD_DIM, HEAD_DIM), F32)
    gn = (gn_g.reshape(1, -1).astype(F32), gn_b.reshape(1, -1).astype(F32))
    oc_f, s_f = _retention_pass(pc, tf, s0, None, reverse=False)
    oc, s_b = _retention_pass(pc, tb, s0, (oc_f,) + gn, reverse=True)
    o_f, _ = _retention_pass(p, tf, s_f, None, reverse=False)
    o, _ = _retention_pass(p, tb, s_b, (o_f,) + gn, reverse=True)
    return o, (oc if with_ctx else None)


OUT_SUB_ROWS = 256


def _out_kernel(oa_ref, ob_ref, oc_ref, od_ref, w_ref, x_ref, g1_ref, ng_ref, sc_ref, sh_ref, wr_ref,
                xo_ref, h_ref, aff_ref):
    gw = oa_ref.shape[1]
    rows = min(OUT_SUB_ROWS, x_ref.shape[0])
    subs = [slice(sub * rows, (sub + 1) * rows) for sub in range(x_ref.shape[0] // rows)]
    accs = []
    for rs in subs:
        acc = jnp.dot(oa_ref[rs, :], w_ref[0 * gw:1 * gw, :], preferred_element_type=F32)
        acc += jnp.dot(ob_ref[rs, :], w_ref[1 * gw:2 * gw, :], preferred_element_type=F32)
        acc += jnp.dot(oc_ref[rs, :], w_ref[2 * gw:3 * gw, :], preferred_element_type=F32)
        acc += jnp.dot(od_ref[rs, :], w_ref[3 * gw:4 * gw, :], preferred_element_type=F32)
        accs.append(acc)
    for rs, acc in zip(subs, accs):
        x = x_ref[rs, :] + g1_ref[...] * acc
        xo_ref[rs, :] = x
        y = x * lax.rsqrt(jnp.mean(x * x, axis=-1, keepdims=True) + NORM_EPS) * ng_ref[...]
        h2 = (y * (1.0 + sc_ref[...]) + sh_ref[...]).astype(BF16)
        h_ref[rs, :] = h2
        logits = _nt_dot(wr_ref[...], h2)
        e = jnp.exp(logits - jnp.max(logits, axis=0, keepdims=True))
        aff_ref[:, rs] = e * (1.0 / jnp.sum(e, axis=0, keepdims=True))


def out_project(o_groups, w_out, x, g1, norm_g, scale, shift, w_router_t, *, tm=2 * OUT_SUB_ROWS):
    b, n, d = x.shape
    n_e = w_router_t.shape[0]
    gw = o_groups[0].shape[-1]
    tm = min(tm, n)
    tile = lambda w: pl.BlockSpec((None, tm, w), lambda bi, i: (bi, i, 0))
    vec = pl.BlockSpec((None, 1, d), lambda bi, i: (bi, 0, 0))
    return pl.pallas_call(
        _out_kernel,
        grid=(b, n // tm),
        in_specs=[tile(gw)] * 4 + [
            pl.BlockSpec((d, d), lambda bi, i: (0, 0), pipeline_mode=pl.Buffered(1)),
            tile(d), vec,
            pl.BlockSpec((1, d), lambda bi, i: (0, 0)),
            vec, vec,
            pl.BlockSpec((n_e, d), lambda bi, i: (0, 0)),
        ],
        out_specs=[tile(d), tile(d), pl.BlockSpec((None, n_e, tm), lambda bi, i: (bi, 0, i))],
        out_shape=[jax.ShapeDtypeStruct((b, n, d), F32),
                   jax.ShapeDtypeStruct((b, n, d), BF16),
                   jax.ShapeDtypeStruct((b, n_e, n), F32)],
        compiler_params=_cparams("parallel", "parallel"),
        name="out_project",
    )(*o_groups, w_out, x, g1, norm_g, scale, shift, w_router_t)


def _ffn_kernel(x_ref, gate_ref, wg_ref, wu_ref, wd_ref, o_ref):
    x = x_ref[...]
    a = jnp.dot(x, wg_ref[...].astype(BF16), preferred_element_type=F32)
    u = jnp.dot(x, wu_ref[...].astype(BF16), preferred_element_type=F32)
    hm = (a * (1.0 / (1.0 + jnp.exp(-a))) * u).astype(BF16)
    y = jnp.dot(hm, wd_ref[...].astype(BF16), preferred_element_type=F32) * gate_ref[...]
    o_ref[...] = y.astype(o_ref.dtype)


def expert_ffn(xin, gate, w_gate, w_up, w_down, layer, *, tc=256):
    b, n_e, cap, d = xin.shape
    ff = w_gate.shape[-1]
    tc = min(tc, cap)
    weight = lambda rows, cols: pl.BlockSpec((None, None, rows, cols), lambda e, bi, i: (layer, e, 0, 0),
                                             pipeline_mode=pl.Buffered(1))
    return pl.pallas_call(
        _ffn_kernel,
        grid=(n_e, b, cap // tc),
        in_specs=[
            pl.BlockSpec((None, None, tc, d), lambda e, bi, i: (bi, e, i, 0)),
            pl.BlockSpec((None, None, tc, 1), lambda e, bi, i: (bi, e, i, 0)),
            weight(d, ff), weight(d, ff), weight(ff, d),
        ],
        out_specs=pl.BlockSpec((None, None, tc, d), lambda e, bi, i: (bi, e, i, 0)),
        out_shape=jax.ShapeDtypeStruct((b, n_e, cap, d), BF16),
        compiler_params=_cparams("parallel", "parallel", "parallel"),
        name="expert_ffn",
    )(xin, gate, w_gate, w_up, w_down)


SCATTER_GROUP = 64


def _scatter_kernel(rlo_ref, nrd_ref, x_ref, g_ref, ng_ref, idx_ref, y_hbm, o_ref, ybuf, sem, acc_ref,
                    base_ref, *, sr, final):
    bi, i, nt = pl.program_id(0), pl.program_id(1), pl.num_programs(1)
    n_e, n_grp = idx_ref.shape[0], idx_ref.shape[1]
    tm = x_ref.shape[0]
    n_rounds = nrd_ref[bi * nt + i]

    def group(e, k, tile):
        return jnp.minimum(rlo_ref[(bi * n_e + e) * nt + tile] + k, n_grp - 1)

    def slot_tokens(e, k):
        in_range = rlo_ref[(bi * n_e + e) * nt + i] + k < n_grp
        return jnp.where(in_range, idx_ref[e, group(e, k, i)], -1)

    def copies(k, slot, tile):
        return [pltpu.make_async_copy(
            y_hbm.at[bi, e, pl.ds(pl.multiple_of(group(e, k, tile) * sr, sr), sr), :],
            ybuf.at[slot, pl.ds(e * sr, sr), :], sem.at[slot]) for e in range(n_e)]

    @pl.when(i == 0)
    def _():
        base_ref[0] = 0
        for c in copies(0, 0, i):
            c.start()

    base = base_ref[0]
    acc_ref[...] = jnp.zeros(acc_ref.shape, F32)
    per_row = HEAD_DIM // sr
    tok = i * tm + lax.broadcasted_iota(jnp.int32, (tm, HEAD_DIM), 0)

    def body(k, carry):
        slot = (base + k) % 2
        for c in copies(k, slot, i):
            c.wait()

        @pl.when(k + 1 < n_rounds)
        def _():
            for c in copies(k + 1, 1 - slot, i):
                c.start()

        @pl.when(jnp.logical_and(k + 1 == n_rounds, i + 1 < nt))
        def _():
            for c in copies(0, 1 - slot, i + 1):
                c.start()

        rows = [jnp.concatenate([slot_tokens(e + j, k) for j in range(per_row)], axis=1)
                for e in range(0, n_e, per_row)]
        onehot = jnp.concatenate([jnp.where(tok == r, 1.0, 0.0) for r in rows], axis=1)
        acc_ref[...] += jnp.dot(onehot.astype(BF16), ybuf[slot], preferred_element_type=F32)
        return carry

    lax.fori_loop(0, n_rounds, body, 0)
    base_ref[0] = (base + n_rounds) % 2
    x = x_ref[...] + g_ref[...] * acc_ref[...]
    if final:
        x = x * lax.rsqrt(jnp.mean(x * x, axis=-1, keepdims=True) + NORM_EPS) * ng_ref[...]
    o_ref[...] = x


def scatter_combine(x, y, idx, g2, final_g, *, tm=256):
    b, t, d = x.shape
    n_e, cap = idx.shape[1], idx.shape[2]
    tm = min(tm, t)
    sr = min(SCATTER_GROUP, cap)
    assert t % tm == 0 and cap % sr == 0 and HEAD_DIM % sr == 0 and n_e % (HEAD_DIM // sr) == 0
    nt, n_grp = t // tm, cap // sr
    edges = jnp.arange(nt + 1, dtype=idx.dtype) * tm
    pos = jnp.sum(idx[..., None] < edges, axis=2, dtype=jnp.int32)
    lo, hi = pos[..., :-1], pos[..., 1:]
    rlo = jnp.minimum(lo // sr, n_grp - 1)
    rhi = jnp.maximum(hi - 1, lo) // sr
    n_rounds = jnp.max(jnp.minimum(rhi, n_grp - 1) - rlo + 1, axis=1)
    final = final_g is not None
    ng = final_g if final else jnp.ones((1, d), F32)
    tile = pl.BlockSpec((None, tm, d), lambda bi, i, *_: (bi, i, 0))
    return pl.pallas_call(
        functools.partial(_scatter_kernel, sr=sr, final=final),
        grid_spec=pltpu.PrefetchScalarGridSpec(
            num_scalar_prefetch=2,
            grid=(b, nt),
            in_specs=[tile,
                      pl.BlockSpec((None, 1, d), lambda bi, i, *_: (bi, 0, 0)),
                      pl.BlockSpec((1, d), lambda bi, i, *_: (0, 0)),
                      pl.BlockSpec((None, n_e, n_grp, 1, sr), lambda bi, i, *_: (bi, 0, 0, 0, 0)),
                      pl.BlockSpec(memory_space=pl.ANY)],
            out_specs=tile,
            scratch_shapes=[pltpu.VMEM((2, n_e * sr, d), BF16),
                            pltpu.SemaphoreType.DMA((2,)),
                            pltpu.VMEM((tm, d), F32),
                            pltpu.SMEM((1,), jnp.int32)]),
        out_shape=jax.ShapeDtypeStruct((b, t, d), F32),
        compiler_params=_cparams("parallel", "arbitrary"),
        name="scatter_combine",
    )(rlo.reshape(-1), n_rounds.reshape(-1), x, g2, ng, idx.reshape(b, n_e, n_grp, 1, sr), y)


def expert_choice_ffn(x, h2, aff_t, g2, final_g, w_gate, w_up, w_down, layer):
    b, t, d = h2.shape
    cap = EC_CAPACITY * t // N_EXPERTS
    gate, idx = lax.top_k(aff_t, cap)
    idx, gate = lax.sort_key_val(idx, gate, dimension=-1)
    xin = jax.vmap(lambda hb, ib: hb[ib])(h2, idx)
    y = expert_ffn(xin, gate[..., None], w_gate, w_up, w_down, layer)
    return scatter_combine(x, y, idx, g2, final_g)


def _ctx_attn_kernel(scal_ref, q_ref, k_ref, v_ref, g_ref, o_ref, *, kind, post_scale):
    q, k, v = q_ref[...], k_ref[...], v_ref[...]
    if kind == "diff":
        qf = q.astype(F32)
        lane = lax.broadcasted_iota(jnp.int32, qf.shape, 1)

        def probs(keep):
            s = _nt_dot(jnp.where(keep, qf, 0.0).astype(BF16), k)
            e = jnp.exp2(s - jnp.max(s, axis=-1, keepdims=True))
            return e * (1.0 / jnp.sum(e, axis=-1, keepdims=True))

        w = probs(lane < DIFF_DIM) - scal_ref[0] * probs(lane >= DIFF_DIM)
        o = jnp.dot(w.astype(BF16), v, preferred_element_type=F32)
        o = o * lax.rsqrt(jnp.mean(o * o, axis=-1, keepdims=True) + NORM_EPS) * g_ref[...] * post_scale
    else:
        s = _nt_dot(q, k)
        m = jnp.max(s, axis=-1, keepdims=True)
        if kind == "sink":
            sink = scal_ref[pl.program_id(1)]
            m = jnp.maximum(m, sink)
        e = jnp.exp(s - m)
        l = jnp.sum(e, axis=-1, keepdims=True)
        if kind == "sink":
            l = l + jnp.exp(sink - m)
        o = jnp.dot((e * (1.0 / l)).astype(BF16), v, preferred_element_type=F32)
    o_ref[...] = o.astype(o_ref.dtype)


def _context_attention(pc, scal, norm_g, kind, post_scale, q_col, k_col, v_col, grp):
    b, lc, _ = pc.shape
    h = GROUP_BLOCKS
    col = lambda c0, div: (lambda bi, hi: (bi, 0, c0 + hi // div))
    return pl.pallas_call(
        functools.partial(_ctx_attn_kernel, kind=kind, post_scale=post_scale),
        grid=(b, h),
        in_specs=[
            pl.BlockSpec(memory_space=pltpu.SMEM),
            pl.BlockSpec((None, lc, HEAD_DIM), col(q_col, 1)),
            pl.BlockSpec((None, lc, HEAD_DIM), col(k_col, grp)),
            pl.BlockSpec((None, lc, HEAD_DIM), col(v_col, grp)),
            pl.BlockSpec((1, HEAD_DIM), lambda bi, hi: (0, 0)),
        ],
        out_specs=pl.BlockSpec((None, lc, HEAD_DIM), lambda bi, hi: (bi, 0, hi)),
        out_shape=jax.ShapeDtypeStruct((b, lc, h * HEAD_DIM), BF16),
        compiler_params=_cparams("parallel", "parallel"),
        name="context_attention_" + kind,
    )(scal, pc, pc, pc, norm_g)


def _context_mixers(pc, oc_ret, lam, lam_init, diff_norm_g, swa_sink):
    h = GROUP_BLOCKS
    zeros = jnp.zeros((h,), F32)
    ones = jnp.ones((1, HEAD_DIM), F32)
    o_a = _context_attention(pc, zeros, ones, "plain", 1.0, COL_A_Q, COL_A_K, COL_A_V, 1)
    o_b = _context_attention(pc, lam.reshape(1), diff_norm_g.reshape(1, HEAD_DIM).astype(F32), "diff",
                             1.0 - lam_init, COL_B_K - h, COL_B_K, COL_B_K + h, 1)
    o_d = _context_attention(pc, swa_sink.astype(F32), ones, "sink", 1.0, COL_D_Q, COL_D_K, COL_D_V, 2)
    return [o_a, o_b, oc_ret, o_d]


def _layer(x, xc, mod, layer_idx, with_ctx, final_g, rope_h, rope_d, rope_id,
           norm1_g, w_in, na_rpb, diff_lambda, diff_norm_g, ret_decay_fwd, ret_decay_bwd, ret_gn_g, ret_gn_b,
           swa_sink, w_out, norm2_g, w_router, w_gate, w_up, w_down):
    b, n, d = x.shape
    row = lambda v: v.reshape(1, -1).astype(F32)
    part = lambda r0, r1, k: mod[r0:r1, None, k * d:(k + 1) * d]
    sh1, sc1, g1, sh2, sc2, g2 = [part(0, b, k) for k in range(6)]
    ctx_rows = lambda k: jnp.broadcast_to(part(b, b + 1, k), (b, 1, d))
    sh1c, sc1c, g1c, sh2c, sc2c, g2c = [ctx_rows(k) for k in range(6)]

    w_in_b = w_in.astype(BF16)
    w_out_b = w_out.astype(BF16)
    w_router_t = w_router.T.astype(BF16)

    p, t = norm_project(x, row(norm1_g), sc1, sh1, w_in_b, rope_h, rope_d)
    pc, tc = norm_project(xc, row(norm1_g), sc1c, sh1c, w_in_b, rope_id, rope_id)

    lam_init = 0.8 - 0.6 * math.exp(-0.3 * layer_idx)
    lq1, lk1, lq2, lk2 = [diff_lambda[k].astype(F32) for k in range(4)]
    lam = jnp.exp(jnp.sum(lq1 * lk1)) - jnp.exp(jnp.sum(lq2 * lk2)) + lam_init

    o_a = neighbourhood_attention(p, pc, na_rpb)
    o_b = diff_attention(p, t, pc, tc, lam.reshape(1), row(diff_norm_g), 1.0 - lam_init)
    o_c, oc_ret = bidirectional_retention(p, pc, ret_decay_fwd, ret_decay_bwd, ret_gn_g, ret_gn_b, with_ctx)
    o_d = window_attention(p, pc, swa_sink.astype(F32))

    x, h2, aff_t = out_project([o_a, o_b, o_c, o_d], w_out_b, x, g1, row(norm2_g), sc2, sh2, w_router_t)
    x = expert_choice_ffn(x, h2, aff_t, g2, final_g, w_gate, w_up, w_down, layer_idx)

    if with_ctx:
        oc = _context_mixers(pc, oc_ret, lam, lam_init, diff_norm_g, swa_sink)
        xc, h2c, aff_c = out_project(oc, w_out_b, xc, g1c, row(norm2_g), sc2c, sh2c, w_router_t)
        xc = expert_choice_ffn(xc, h2c, aff_c, g2c, None, w_gate, w_up, w_down, layer_idx)
    return x, xc


def kernel(x, c, ctx, c_ctx, w_mod, b_mod, norm1_g, w_in, na_rpb, diff_lambda, diff_norm_g,
           ret_decay_fwd, ret_decay_bwd, ret_gn_g, ret_gn_b, swa_sink, w_out, norm2_g,
           w_router, w_gate, w_up, w_down, final_norm_g):
    b, n, d = x.shape
    depth = w_in.shape[0]
    lc = ctx.shape[1]
    cvec = jnp.zeros((8, d), F32).at[:b].set(c).at[b].set(c_ctx)
    mod = modulation(cvec, w_mod, b_mod.reshape(depth, 1, -1))
    rope_h = _rope_tables(n, HEAD_DIM)
    rope_d = _rope_tables(n, DIFF_DIM)
    rope_id = _identity_rope_tables(lc)
    xc = ctx
    for li in range(depth):
        last = li == depth - 1
        x, xc = _layer(x, xc, mod[li], li, not last, final_norm_g.reshape(1, -1) if last else None,
                       rope_h, rope_d, rope_id,
                       norm1_g[li], w_in[li], na_rpb[li], diff_lambda[li], diff_norm_g[li],
                       ret_decay_fwd[li], ret_decay_bwd[li], ret_gn_g[li], ret_gn_b[li], swa_sink[li],
                       w_out[li], norm2_g[li], w_router[li], w_gate, w_up, w_down)
    return x
```

```python
import functools
import math

import jax
import jax.numpy as jnp
from jax import lax
from jax.experimental import pallas as pl
from jax.experimental.pallas import tpu as pltpu

GRID_W = 64
HEAD_DIM = 128
DIFF_DIM = HEAD_DIM // 2
NA_WIN_R = 8
NA_WIN_C = 16
RET_CHUNK = 128
SWA_WINDOW = 128
N_EXPERTS = 16
EC_CAPACITY = 2
ROPE_BASE = 10000.0
NORM_EPS = 1e-6
GN_EPS = 1e-5
NEG_INF = -1e30
LOG2E = math.log2(math.e)

BF16 = jnp.bfloat16
F32 = jnp.float32

VMEM_LIMIT_BYTES = 48 * 1024 * 1024

GROUP_BLOCKS = 4
COL_A_Q, COL_A_K, COL_A_V = 0, 4, 8
COL_B_K = 16
COL_C_Q, COL_C_K, COL_C_V, COL_C_G = 24, 28, 32, 36
COL_D_Q, COL_D_K, COL_D_V = 40, 44, 46
PROJ_TN = GROUP_BLOCKS * HEAD_DIM


def _cparams(*sem):
    return pltpu.CompilerParams(dimension_semantics=sem, vmem_limit_bytes=VMEM_LIMIT_BYTES)


def _mod_kernel(c_ref, w_ref, b_ref, o_ref):
    c = c_ref[...]
    s = (c * (1.0 / (1.0 + jnp.exp(-c)))).astype(BF16)
    o_ref[...] = jnp.dot(s, w_ref[...].astype(BF16), preferred_element_type=F32) + b_ref[...]


def modulation(cvec, w_mod, b_mod, *, tn=1024):
    depth, d, n6 = w_mod.shape
    return pl.pallas_call(
        _mod_kernel,
        grid=(depth, n6 // tn),
        in_specs=[
            pl.BlockSpec((8, d), lambda l, j: (0, 0)),
            pl.BlockSpec((None, d, tn), lambda l, j: (l, 0, j)),
            pl.BlockSpec((None, 1, tn), lambda l, j: (l, 0, j)),
        ],
        out_specs=pl.BlockSpec((None, 8, tn), lambda l, j: (l, 0, j)),
        out_shape=jax.ShapeDtypeStruct((depth, 8, n6), F32),
        compiler_params=_cparams("parallel", "parallel"),
        name="modulation",
    )(cvec, w_mod, b_mod)


def _rope_tables(n_tok, dim):
    t = jnp.arange(n_tok)
    row = (t // GRID_W).astype(F32)[:, None]
    col = (t % GRID_W).astype(F32)[:, None]
    nf = dim // 4
    lane = jnp.arange(HEAD_DIM)
    quarter = (lane % dim) // nf
    inv = ROPE_BASE ** (-jnp.arange(nf, dtype=F32) / nf)
    ang = jnp.where(quarter[None, :] < 2, row, col) * inv[lane % nf][None, :]
    cos, sin = jnp.cos(ang), jnp.sin(ang)
    even = (quarter % 2 == 0)[None, :]
    return cos, jnp.where(even, -sin, 0.0), jnp.where(even, 0.0, sin)


def _identity_rope_tables(n_tok):
    z = jnp.zeros((n_tok, HEAD_DIM), F32)
    return jnp.ones((n_tok, HEAD_DIM), F32), z, z


def _proj_modes():
    att = HEAD_DIM ** -0.5
    plain = [(None, 1.0)] * GROUP_BLOCKS
    return [
        ([(None, att)] * 4, None), (plain, None), (plain, None),
        ([("d", LOG2E * DIFF_DIM ** -0.5)] * 4, 0), ([("d", 1.0)] * 4, None), (plain, 1),
        ([("h", 1.0)] * 4, None), ([("h", att)] * 4, None), (plain, None), (plain, None),
        ([("h", att)] * 4, None), ([("h", 1.0)] * 2 + [(None, 1.0)] * 2, None),
    ]


def _proj_kernel(x_ref, g_ref, sc_ref, sh_ref, w_ref, ch_ref, sah_ref, sbh_ref, cd_ref, sad_ref, sbd_ref,
                 o_ref, t_ref):
    x = x_ref[...]
    y = x * lax.rsqrt(jnp.mean(x * x, axis=-1, keepdims=True) + NORM_EPS) * g_ref[...]
    h = (y * (1.0 + sc_ref[...]) + sh_ref[...]).astype(BF16)

    def rope(a, kind):
        if kind is None:
            return a
        c, sa, sb, sh = ((ch_ref, sah_ref, sbh_ref, HEAD_DIM // 4) if kind == "h"
                         else (cd_ref, sad_ref, sbd_ref, DIFF_DIM // 4))
        return (a * c[...] + pltpu.roll(a, HEAD_DIM - sh, 1) * sa[...] + pltpu.roll(a, sh, 1) * sb[...])

    for j, (blocks, t_slot) in enumerate(_proj_modes()):
        cols = slice(j * PROJ_TN, (j + 1) * PROJ_TN)
        acc = jnp.dot(h, w_ref[:, cols], preferred_element_type=F32)
        outs = []
        for hb, (kind, scale) in enumerate(blocks):
            a = rope(acc[:, hb * HEAD_DIM:(hb + 1) * HEAD_DIM], kind)
            outs.append(a if scale == 1.0 else a * scale)
        full = jnp.concatenate(outs, axis=1)
        o_ref[:, cols] = full.astype(o_ref.dtype)
        if t_slot is not None:
            t_ref[t_slot] = full.T.astype(t_ref.dtype)


def norm_project(x, norm_g, scale, shift, w_in, rope_h, rope_d, *, tm=256):
    b, n, d = x.shape
    width = w_in.shape[1]
    assert width == len(_proj_modes()) * PROJ_TN
    tm = min(tm, n)
    tok = lambda bi, i: (i, 0)
    return pl.pallas_call(
        _proj_kernel,
        grid=(b, n // tm),
        in_specs=[
            pl.BlockSpec((None, tm, d), lambda bi, i: (bi, i, 0)),
            pl.BlockSpec((1, d), lambda bi, i: (0, 0)),
            pl.BlockSpec((None, 1, d), lambda bi, i: (bi, 0, 0)),
            pl.BlockSpec((None, 1, d), lambda bi, i: (bi, 0, 0)),
            pl.BlockSpec((d, width), lambda bi, i: (0, 0), pipeline_mode=pl.Buffered(1)),
        ] + [pl.BlockSpec((tm, HEAD_DIM), tok)] * 6,
        out_specs=[
            pl.BlockSpec((None, tm, width), lambda bi, i: (bi, i, 0)),
            pl.BlockSpec((None, 2, PROJ_TN, tm), lambda bi, i: (bi, 0, 0, i)),
        ],
        out_shape=[jax.ShapeDtypeStruct((b, n, width), BF16),
                   jax.ShapeDtypeStruct((b, 2, PROJ_TN, n), BF16)],
        compiler_params=_cparams("parallel", "parallel"),
        name="norm_project",
    )(x, norm_g, scale, shift, w_in, *rope_h, *rope_d)


def _softmax(s_loc, s_ctx, sink):
    m = jnp.maximum(jnp.max(s_loc, axis=-1, keepdims=True), jnp.max(s_ctx, axis=-1, keepdims=True))
    if sink is not None:
        m = jnp.maximum(m, sink)
    p_loc = jnp.exp(s_loc - m)
    p_ctx = jnp.exp(s_ctx - m)
    l = jnp.sum(p_loc, axis=-1, keepdims=True) + jnp.sum(p_ctx, axis=-1, keepdims=True)
    if sink is not None:
        l = l + jnp.exp(sink - m)
    return p_loc.astype(BF16), p_ctx.astype(BF16), 1.0 / l


def _local_attention(scores, values, vc_ref, sink, o_ref, tq):
    n_sub = o_ref.shape[0] // tq
    s = [scores(sub) for sub in range(n_sub)]
    p = [_softmax(s_loc, s_ctx, sink) for s_loc, s_ctx in s]
    for sub, (p_loc, p_ctx, inv_l) in enumerate(p):
        o = (jnp.dot(p_loc, values(sub), preferred_element_type=F32)
             + jnp.dot(p_ctx, vc_ref[...], preferred_element_type=F32))
        o_ref[sub * tq:(sub + 1) * tq, :] = (o * inv_l).astype(o_ref.dtype)


def _nt_dot(a, b):
    return lax.dot_general(a, b, (((1,), (1,)), ((), ())), preferred_element_type=F32)


LOCAL_SUB_BLOCKS = 4


def _na_kernel(q_ref, k_ref, v_ref, kc_ref, vc_ref, bias_ref, o_ref, *, tq, wk, n_blk):
    n = k_ref.shape[0]
    rows_q = tq // GRID_W
    n_sub = q_ref.shape[0] // tq

    def kstart(sub):
        blk = pl.program_id(2) * n_sub + sub
        start = jnp.clip(blk * rows_q - NA_WIN_R // 2, 0, (n - wk) // GRID_W) * GRID_W
        return blk, pl.multiple_of(start, GRID_W)

    def scores(sub):
        blk, start = kstart(sub)
        cls = jnp.where(blk == 0, 0, jnp.where(blk == n_blk - 1, 2, 1))
        q = q_ref[sub * tq:(sub + 1) * tq, :]
        return _nt_dot(q, k_ref[pl.ds(start, wk), :]) + bias_ref[cls], _nt_dot(q, kc_ref[...])

    _local_attention(scores, lambda sub: v_ref[pl.ds(kstart(sub)[1], wk), :], vc_ref, None, o_ref, tq)


def _na_bias(rpb, rows, rows_q, rows_k):
    n_blk = rows // rows_q
    kr = NA_WIN_R
    cols = jnp.arange(GRID_W)
    col_start = jnp.clip(cols - NA_WIN_C // 2, 0, GRID_W - NA_WIN_C)
    col_ok = (cols[None, :] >= col_start[:, None]) & (cols[None, :] < col_start[:, None] + NA_WIN_C)
    col_off = jnp.clip(cols[None, :] - cols[:, None] + NA_WIN_C - 1, 0, 2 * NA_WIN_C - 2)
    out = []
    for blk in (0, 1, n_blk - 1):
        r = blk * rows_q + jnp.arange(rows_q)
        kstart = min(max(blk * rows_q - kr // 2, 0), rows - rows_k)
        krow = kstart + jnp.arange(rows_k)
        start = jnp.clip(r - kr // 2, 0, rows - kr)
        row_ok = (krow[None, :] >= start[:, None]) & (krow[None, :] < start[:, None] + kr)
        row_off = jnp.clip(krow[None, :] - r[:, None] + NA_WIN_R - 1, 0, 2 * NA_WIN_R - 2)
        bias = jnp.einsum('abr,hrc,qkc->habqk', jax.nn.one_hot(row_off, 2 * NA_WIN_R - 1, dtype=F32),
                          rpb.astype(F32), jax.nn.one_hot(col_off, 2 * NA_WIN_C - 1, dtype=F32),
                          precision=lax.Precision.HIGHEST)
        ok = row_ok[:, :, None, None] & col_ok[None, None]
        bias = jnp.where(ok[None], bias, NEG_INF).transpose(0, 1, 3, 2, 4)
        out.append(bias.reshape(rpb.shape[0], rows_q * GRID_W, rows_k * GRID_W))
    return jnp.stack(out, axis=1)


def neighbourhood_attention(p, pc, rpb, *, rows_q=4):
    b, n, _ = p.shape
    lc = pc.shape[1]
    h = rpb.shape[0]
    rows = n // GRID_W
    rows_k = rows_q + NA_WIN_R - 1
    assert rows % rows_q == 0 and rows >= rows_k and rows_q >= NA_WIN_R // 2
    tq, wk = rows_q * GRID_W, rows_k * GRID_W
    n_blk = rows // rows_q
    sub = LOCAL_SUB_BLOCKS if n_blk % LOCAL_SUB_BLOCKS == 0 else 1
    bias = _na_bias(rpb, rows, rows_q, rows_k)
    kern = functools.partial(_na_kernel, tq=tq, wk=wk, n_blk=n_blk)
    return pl.pallas_call(
        kern,
        grid=(b, h, n_blk // sub),
        in_specs=[
            pl.BlockSpec((None, sub * tq, HEAD_DIM), lambda bi, hi, i: (bi, i, COL_A_Q + hi)),
            pl.BlockSpec((None, n, HEAD_DIM), lambda bi, hi, i: (bi, 0, COL_A_K + hi)),
            pl.BlockSpec((None, n, HEAD_DIM), lambda bi, hi, i: (bi, 0, COL_A_V + hi)),
            pl.BlockSpec((None, lc, HEAD_DIM), lambda bi, hi, i: (bi, 0, COL_A_K + hi)),
            pl.BlockSpec((None, lc, HEAD_DIM), lambda bi, hi, i: (bi, 0, COL_A_V + hi)),
            pl.BlockSpec((None, 3, tq, wk), lambda bi, hi, i: (hi, 0, 0, 0)),
        ],
        out_specs=pl.BlockSpec((None, sub * tq, HEAD_DIM), lambda bi, hi, i: (bi, i, hi)),
        out_shape=jax.ShapeDtypeStruct((b, n, h * HEAD_DIM), BF16),
        compiler_params=_cparams("parallel", "parallel", "arbitrary"),
        name="neighbourhood_attention",
    )(p, p, p, pc, pc, bias)


def _swa_kernel(sink_ref, q_ref, k_ref, v_ref, kc_ref, vc_ref, o_ref, *, tq, wk):
    n = k_ref.shape[0]

    def window(sub):
        q0 = pl.program_id(2) * q_ref.shape[0] + sub * tq
        return q0, pl.multiple_of(jnp.clip(q0 - SWA_WINDOW, 0, n - wk), SWA_WINDOW)

    def scores(sub):
        q0, start = window(sub)
        q = q_ref[sub * tq:(sub + 1) * tq, :]
        qpos = q0 + lax.broadcasted_iota(jnp.int32, (tq, wk), 0)
        kpos = start + lax.broadcasted_iota(jnp.int32, (tq, wk), 1)
        s_loc = jnp.where(jnp.abs(kpos - qpos) <= SWA_WINDOW, _nt_dot(q, k_ref[pl.ds(start, wk), :]), NEG_INF)
        return s_loc, _nt_dot(q, kc_ref[...])

    _local_attention(scores, lambda sub: v_ref[pl.ds(window(sub)[1], wk), :], vc_ref,
                     sink_ref[pl.program_id(1)], o_ref, tq)


def window_attention(p, pc, sink, *, tq=256):
    b, n, _ = p.shape
    lc = pc.shape[1]
    hq = sink.shape[0]
    grp = 2
    tq = min(tq, n)
    wk = min(tq + 2 * SWA_WINDOW, n)
    sub = LOCAL_SUB_BLOCKS if (n // tq) % LOCAL_SUB_BLOCKS == 0 else 1
    kern = functools.partial(_swa_kernel, tq=tq, wk=wk)
    return pl.pallas_call(
        kern,
        grid=(b, hq, n // (sub * tq)),
        in_specs=[
            pl.BlockSpec(memory_space=pltpu.SMEM),
            pl.BlockSpec((None, sub * tq, HEAD_DIM), lambda bi, hi, i: (bi, i, COL_D_Q + hi)),
            pl.BlockSpec((None, n, HEAD_DIM), lambda bi, hi, i: (bi, 0, COL_D_K + hi // grp)),
            pl.BlockSpec((None, n, HEAD_DIM), lambda bi, hi, i: (bi, 0, COL_D_V + hi // grp)),
            pl.BlockSpec((None, lc, HEAD_DIM), lambda bi, hi, i: (bi, 0, COL_D_K + hi // grp)),
            pl.BlockSpec((None, lc, HEAD_DIM), lambda bi, hi, i: (bi, 0, COL_D_V + hi // grp)),
        ],
        out_specs=pl.BlockSpec((None, sub * tq, HEAD_DIM), lambda bi, hi, i: (bi, i, hi)),
        out_shape=jax.ShapeDtypeStruct((b, n, hq * HEAD_DIM), BF16),
        compiler_params=_cparams("parallel", "parallel", "arbitrary"),
        name="window_attention",
    )(sink, p, p, p, pc, pc)


DIFF_SLOTS = 2
DIFF_UNROLL = 6
DIFF_AUG_ROWS = 16


def _diff_attn_kernel(lam_ref, qt_ref, k_ref, vt_ref, kc_ref, vct_ref, g_ref, o_ref,
                      qbd_ref, m_ref, acc_ref, s_scr, x_scr, *, tq, tk, post_scale):
    qt = qt_ref[...].astype(F32)
    row = lax.broadcasted_iota(jnp.int32, qt.shape, 0)
    zero = jnp.zeros_like(qt)
    qbd_ref[:, :tq] = jnp.where(row < DIFF_DIM, qt, zero).astype(BF16)
    qbd_ref[:, tq:] = jnp.where(row >= DIFF_DIM, qt, zero).astype(BF16)

    def aug(vt_tile):
        r = lax.broadcasted_iota(jnp.int32, (DIFF_AUG_ROWS, vt_tile.shape[1]), 0)
        return jnp.concatenate([vt_tile, jnp.where(r == 0, 1.0, 0.0).astype(BF16)], axis=0)

    s = jnp.dot(kc_ref[...], qbd_ref[...], preferred_element_type=F32)
    m0 = jnp.max(s, axis=0, keepdims=True)
    m_ref[...] = m0
    acc_ref[...] = jnp.dot(aug(vct_ref[...]), jnp.exp2(s - m0).astype(BF16), preferred_element_type=F32)

    def scores(t, slot):
        off = pl.multiple_of(t * tk, tk)
        s = jnp.dot(k_ref[pl.ds(off, tk), :], qbd_ref[...], preferred_element_type=F32)
        s_scr[slot] = s
        x_scr[slot] = jnp.max(s, axis=0, keepdims=True)

    def accumulate(t, slot):
        m_old = m_ref[...]
        m_new = jnp.maximum(m_old, x_scr[slot])
        alpha = jnp.exp2(m_old - m_new)
        p = jnp.exp2(s_scr[slot] - m_new).astype(BF16)
        m_ref[...] = m_new
        off = pl.multiple_of(t * tk, tk)
        pv = jnp.dot(aug(vt_ref[:, pl.ds(off, tk)]), p, preferred_element_type=F32)
        acc_ref[...] = alpha * acc_ref[...] + pv

    def stage(t, t_mod, do_scores=True):
        if do_scores:
            scores(t + 1, (t_mod + 1) % DIFF_SLOTS)
        accumulate(t, t_mod % DIFF_SLOTS)

    n_kt = k_ref.shape[0] // tk
    trips = (n_kt - 1) // DIFF_UNROLL
    scores(0, 0)

    def body(i, carry):
        for u in range(DIFF_UNROLL):
            stage(DIFF_UNROLL * i + u, u)
        return carry

    lax.fori_loop(0, trips, body, 0)
    for t in range(DIFF_UNROLL * trips, n_kt - 1):
        stage(t, t)
    stage(n_kt - 1, n_kt - 1, do_scores=False)

    lam = lam_ref[0]
    inv = 1.0 / acc_ref[HEAD_DIM:HEAD_DIM + 1, :]
    acc = acc_ref[:HEAD_DIM, :]
    o_t = acc[:, :tq] * inv[:, :tq] - lam * (acc[:, tq:] * inv[:, tq:])
    o = o_t.T
    ms = jnp.mean(o * o, axis=-1, keepdims=True)
    y = o * lax.rsqrt(ms + NORM_EPS) * g_ref[...]
    o_ref[...] = (y * post_scale).astype(o_ref.dtype)


def diff_attention(p, t, pc, tc, lam, norm_g, post_scale, *, tq=512, tk=512):
    b, n, _ = p.shape
    lc = pc.shape[1]
    h = GROUP_BLOCKS
    tq = min(tq, n)
    tk = min(tk, n)
    assert n % tq == 0 and n % tk == 0 and DIFF_UNROLL % DIFF_SLOTS == 0
    kern = functools.partial(_diff_attn_kernel, tq=tq, tk=tk, post_scale=post_scale)
    return pl.pallas_call(
        kern,
        grid=(b, h, n // tq),
        in_specs=[
            pl.BlockSpec(memory_space=pltpu.SMEM),
            pl.BlockSpec((None, None, HEAD_DIM, tq), lambda bi, hi, qi: (bi, 0, hi, qi)),
            pl.BlockSpec((None, n, HEAD_DIM), lambda bi, hi, qi: (bi, 0, COL_B_K + hi)),
            pl.BlockSpec((None, None, HEAD_DIM, n), lambda bi, hi, qi: (bi, 1, hi, 0)),
            pl.BlockSpec((None, lc, HEAD_DIM), lambda bi, hi, qi: (bi, 0, COL_B_K + hi)),
            pl.BlockSpec((None, None, HEAD_DIM, lc), lambda bi, hi, qi: (bi, 1, hi, 0)),
            pl.BlockSpec((1, HEAD_DIM), lambda bi, hi, qi: (0, 0)),
        ],
        out_specs=pl.BlockSpec((None, tq, HEAD_DIM), lambda bi, hi, qi: (bi, qi, hi)),
        out_shape=jax.ShapeDtypeStruct((b, n, h * HEAD_DIM), BF16),
        scratch_shapes=[
            pltpu.VMEM((HEAD_DIM, 2 * tq), BF16),
            pltpu.VMEM((1, 2 * tq), F32),
            pltpu.VMEM((HEAD_DIM + DIFF_AUG_ROWS, 2 * tq), F32),
            pltpu.VMEM((DIFF_SLOTS, tk, 2 * tq), F32),
            pltpu.VMEM((DIFF_SLOTS, 1, 2 * tq), F32),
        ],
        compiler_params=_cparams("parallel", "parallel", "arbitrary"),
        name="diff_attention",
    )(lam, t, p, t, pc, tc, norm_g)


def _retention_kernel(cdec_ref, q_ref, k_ref, v_ref, intra_ref, qdec_ref, kdec_ref, s0_ref, *rest,
                      reverse, final):
    if final:
        of_ref, gate_ref, gng_ref, gnb_ref, o_ref, sfin_ref, s_ref = rest
    else:
        o_ref, sfin_ref, s_ref = rest
    i = pl.program_id(1)
    n_heads = s_ref.shape[0]
    n_chunks = q_ref.shape[0] // RET_CHUNK

    @pl.when(i == 0)
    def _():
        s_ref[...] = s0_ref[...]

    order = range(n_chunks - 1, -1, -1) if reverse else range(n_chunks)
    units = [(c, h) for c in order for h in range(n_heads)]
    window = lambda c, h: (slice(c * RET_CHUNK, (c + 1) * RET_CHUNK), slice(h * HEAD_DIM, (h + 1) * HEAD_DIM))

    intra_o, kv, qdec = {}, {}, {}
    for c, h in units:
        rs, cs = window(c, h)
        qh, kh, vh = q_ref[rs, cs], k_ref[rs, cs], v_ref[rs, cs]
        qdec[c, h] = (qh.astype(F32) * qdec_ref[h]).astype(BF16)
        kd_t = (kh.astype(F32) * kdec_ref[h]).T.astype(BF16)
        a = (_nt_dot(qh, kh) * intra_ref[h]).astype(BF16)
        intra_o[c, h] = jnp.dot(a, vh, preferred_element_type=F32)
        kv[c, h] = jnp.dot(kd_t, vh, preferred_element_type=F32)

    for c, h in units:
        rs, cs = window(c, h)
        if True:
            s = s_ref[h]
            o = intra_o[c, h] + jnp.dot(qdec[c, h], s.astype(BF16), preferred_element_type=F32)
            s_ref[h] = cdec_ref[h] * s + kv[c, h]
            if final:
                o = o + of_ref[rs, cs]
                mu = jnp.mean(o, axis=-1, keepdims=True)
                var = jnp.mean(jnp.square(o - mu), axis=-1, keepdims=True)
                y = (o - mu) * lax.rsqrt(var + GN_EPS) * gng_ref[:, cs] + gnb_ref[:, cs]
                g = gate_ref[rs, cs].astype(F32)
                o = g * (1.0 / (1.0 + jnp.exp(-g))) * y
            o_ref[rs, cs] = o.astype(o_ref.dtype)

    @pl.when(i == pl.num_programs(1) - 1)
    def _():
        sfin_ref[...] = s_ref[...]


def _retention_pass(p, tables, s0, final_inputs, *, reverse, blk_chunks=4):
    b, n, _ = p.shape
    cdec, intra, qdec, kdec = tables
    h = intra.shape[0]
    width = h * HEAD_DIM
    n_chunks = n // RET_CHUNK
    blk_chunks = min(blk_chunks, n_chunks)
    assert n_chunks % blk_chunks == 0
    tb = blk_chunks * RET_CHUNK
    n_blk = n // tb
    pos = (lambda i: n_blk - 1 - i) if reverse else (lambda i: i)
    colblk = lambda c: (lambda bi, i: (bi, pos(i), c // GROUP_BLOCKS))
    tab = pl.BlockSpec((h, RET_CHUNK, HEAD_DIM), lambda bi, i: (0, 0, 0))
    state = pl.BlockSpec((None, h, HEAD_DIM, HEAD_DIM), lambda bi, i: (bi, 0, 0, 0))
    in_specs = [pl.BlockSpec(memory_space=pltpu.SMEM),
                pl.BlockSpec((None, tb, width), colblk(COL_C_Q)),
                pl.BlockSpec((None, tb, width), colblk(COL_C_K)),
                pl.BlockSpec((None, tb, width), colblk(COL_C_V)),
                tab, tab, tab, state]
    args = [cdec, p, p, p, intra, qdec, kdec, s0]
    final = final_inputs is not None
    if final:
        o_fwd, gn_g, gn_b = final_inputs
        in_specs += [pl.BlockSpec((None, tb, width), lambda bi, i: (bi, pos(i), 0)),
                     pl.BlockSpec((None, tb, width), colblk(COL_C_G)),
                     pl.BlockSpec((1, width), lambda bi, i: (0, 0)),
                     pl.BlockSpec((1, width), lambda bi, i: (0, 0))]
        args += [o_fwd, p, gn_g, gn_b]
    kern = functools.partial(_retention_kernel, reverse=reverse, final=final)
    return pl.pallas_call(
        kern,
        grid=(b, n_blk),
        in_specs=in_specs,
        out_specs=[pl.BlockSpec((None, tb, width), lambda bi, i: (bi, pos(i), 0)), state],
        out_shape=[jax.ShapeDtypeStruct((b, n, width), BF16 if final else F32),
                   jax.ShapeDtypeStruct((b, h, HEAD_DIM, HEAD_DIM), F32)],
        scratch_shapes=[pltpu.VMEM((h, HEAD_DIM, HEAD_DIM), F32)],
        compiler_params=_cparams("parallel", "arbitrary"),
        name="retention_bwd" if reverse else "retention_fwd",
    )(*args)


def _retention_tables(decay, reverse):
    lg = jax.nn.log_sigmoid(decay.astype(F32))[:, None, None]
    pos = jnp.arange(RET_CHUNK, dtype=F32)
    rel = pos[:, None] - pos[None, :]
    if reverse:
        rel = -rel
        q_pow, k_pow = RET_CHUNK - pos, pos
    else:
        q_pow, k_pow = pos + 1.0, RET_CHUNK - 1.0 - pos
    intra = jnp.where(rel >= 0, jnp.exp(jnp.maximum(rel, 0.0) * lg), 0.0)
    bc = lambda e: jnp.broadcast_to(jnp.exp(e[None, :, None] * lg), intra.shape)
    cdec = jnp.exp(RET_CHUNK * lg[:, 0, 0])
    return cdec, intra, bc(q_pow), bc(k_pow)


def bidirectional_retention(p, pc, decay_f, decay_b, gn_g, gn_b, with_ctx):
    b = p.shape[0]
    h = decay_f.shape[0]
    tf = _retention_tables(decay_f, False)
    tb = _retention_tables(decay_b, True)
    s0 = jnp.zeros((b, h, HEAD_DIM, HEAD_DIM), F32)
    gn = (gn_g.reshape(1, -1).astype(F32), gn_b.reshape(1, -1).astype(F32))
    oc_f, s_f = _retention_pass(pc, tf, s0, None, reverse=False)
    oc, s_b = _retention_pass(pc, tb, s0, (oc_f,) + gn, reverse=True)
    o_f, _ = _retention_pass(p, tf, s_f, None, reverse=False)
    o, _ = _retention_pass(p, tb, s_b, (o_f,) + gn, reverse=True)
    return o, (oc if with_ctx else None)


OUT_SUB_ROWS = 256


def _out_kernel(oa_ref, ob_ref, oc_ref, od_ref, w_ref, x_ref, g1_ref, ng_ref, sc_ref, sh_ref, wr_ref,
                xo_ref, h_ref, aff_ref):
    gw = oa_ref.shape[1]
    rows = min(OUT_SUB_ROWS, x_ref.shape[0])
    subs = [slice(sub * rows, (sub + 1) * rows) for sub in range(x_ref.shape[0] // rows)]
    accs = []
    for rs in subs:
        acc = jnp.dot(oa_ref[rs, :], w_ref[0 * gw:1 * gw, :], preferred_element_type=F32)
        acc += jnp.dot(ob_ref[rs, :], w_ref[1 * gw:2 * gw, :], preferred_element_type=F32)
        acc += jnp.dot(oc_ref[rs, :], w_ref[2 * gw:3 * gw, :], preferred_element_type=F32)
        acc += jnp.dot(od_ref[rs, :], w_ref[3 * gw:4 * gw, :], preferred_element_type=F32)
        accs.append(acc)
    for rs, acc in zip(subs, accs):
        x = x_ref[rs, :] + g1_ref[...] * acc
        xo_ref[rs, :] = x
        y = x * lax.rsqrt(jnp.mean(x * x, axis=-1, keepdims=True) + NORM_EPS) * ng_ref[...]
        h2 = (y * (1.0 + sc_ref[...]) + sh_ref[...]).astype(BF16)
        h_ref[rs, :] = h2
        logits = _nt_dot(wr_ref[...], h2)
        e = jnp.exp(logits - jnp.max(logits, axis=0, keepdims=True))
        aff_ref[:, rs] = e * (1.0 / jnp.sum(e, axis=0, keepdims=True))


def out_project(o_groups, w_out, x, g1, norm_g, scale, shift, w_router_t, *, tm=2 * OUT_SUB_ROWS):
    b, n, d = x.shape
    n_e = w_router_t.shape[0]
    gw = o_groups[0].shape[-1]
    tm = min(tm, n)
    tile = lambda w: pl.BlockSpec((None, tm, w), lambda bi, i: (bi, i, 0))
    vec = pl.BlockSpec((None, 1, d), lambda bi, i: (bi, 0, 0))
    return pl.pallas_call(
        _out_kernel,
        grid=(b, n // tm),
        in_specs=[tile(gw)] * 4 + [
            pl.BlockSpec((d, d), lambda bi, i: (0, 0), pipeline_mode=pl.Buffered(1)),
            tile(d), vec,
            pl.BlockSpec((1, d), lambda bi, i: (0, 0)),
            vec, vec,
            pl.BlockSpec((n_e, d), lambda bi, i: (0, 0)),
        ],
        out_specs=[tile(d), tile(d), pl.BlockSpec((None, n_e, tm), lambda bi, i: (bi, 0, i))],
        out_shape=[jax.ShapeDtypeStruct((b, n, d), F32),
                   jax.ShapeDtypeStruct((b, n, d), BF16),
                   jax.ShapeDtypeStruct((b, n_e, n), F32)],
        compiler_params=_cparams("parallel", "parallel"),
        name="out_project",
    )(*o_groups, w_out, x, g1, norm_g, scale, shift, w_router_t)


def _ffn_kernel(*refs, n_lat, with_ctx):
    if with_ctx:
        x_ref, gate_ref, xc_ref, gc_ref, wg_ref, wu_ref, wd_ref, o_ref, oc_ref = refs
    else:
        x_ref, gate_ref, wg_ref, wu_ref, wd_ref, o_ref = refs

    def ffn(x_ref, gate_ref, o_ref):
        x = x_ref[...]
        a = jnp.dot(x, wg_ref[...].astype(BF16), preferred_element_type=F32)
        u = jnp.dot(x, wu_ref[...].astype(BF16), preferred_element_type=F32)
        hm = (a * (1.0 / (1.0 + jnp.exp(-a))) * u).astype(BF16)
        y = jnp.dot(hm, wd_ref[...].astype(BF16), preferred_element_type=F32) * gate_ref[...]
        o_ref[...] = y.astype(o_ref.dtype)

    if not with_ctx:
        ffn(x_ref, gate_ref, o_ref)
        return

    @pl.when(pl.program_id(2) < n_lat)
    def _():
        ffn(x_ref, gate_ref, o_ref)

    @pl.when(pl.program_id(2) == n_lat)
    def _():
        ffn(xc_ref, gc_ref, oc_ref)


def expert_ffn(xin, gate, ctx_rows, w_gate, w_up, w_down, layer, *, tc=256):
    b, n_e, cap, d = xin.shape
    ff = w_gate.shape[-1]
    tc = min(tc, cap)
    n_lat = cap // tc
    with_ctx = ctx_rows is not None
    weight = lambda rows, cols: pl.BlockSpec((None, None, rows, cols), lambda e, bi, i: (layer, e, 0, 0),
                                             pipeline_mode=pl.Buffered(1))
    lat = lambda w: pl.BlockSpec((None, None, tc, w), lambda e, bi, i: (bi, e, jnp.minimum(i, n_lat - 1), 0))
    in_specs, args = [lat(d), lat(1)], [xin, gate]
    out_specs, out_shape = [lat(d)], [jax.ShapeDtypeStruct((b, n_e, cap, d), BF16)]
    if with_ctx:
        cap_c = ctx_rows[0].shape[2]
        whole = lambda w: pl.BlockSpec((None, None, cap_c, w), lambda e, bi, i: (bi, e, 0, 0))
        in_specs += [whole(d), whole(1)]
        args += list(ctx_rows)
        out_specs.append(whole(d))
        out_shape.append(jax.ShapeDtypeStruct((b, n_e, cap_c, d), BF16))
    out = pl.pallas_call(
        functools.partial(_ffn_kernel, n_lat=n_lat, with_ctx=with_ctx),
        grid=(n_e, b, n_lat + int(with_ctx)),
        in_specs=in_specs + [weight(d, ff), weight(d, ff), weight(ff, d)],
        out_specs=out_specs,
        out_shape=out_shape,
        compiler_params=_cparams("parallel", "parallel", "arbitrary"),
        name="expert_ffn",
    )(*args, w_gate, w_up, w_down)
    return (out[0], out[1]) if with_ctx else (out[0], None)


SCATTER_GROUP = 64


def _scatter_kernel(rlo_ref, nrd_ref, x_ref, g_ref, ng_ref, idx_ref, y_hbm, o_ref, ybuf, sem, acc_ref,
                    base_ref, *, sr, final):
    bi, i, nt = pl.program_id(0), pl.program_id(1), pl.num_programs(1)
    n_e, n_grp = idx_ref.shape[0], idx_ref.shape[1]
    tm = x_ref.shape[0]
    n_rounds = nrd_ref[bi * nt + i]

    def group(e, k, tile):
        return jnp.minimum(rlo_ref[(bi * n_e + e) * nt + tile] + k, n_grp - 1)

    def slot_tokens(e, k):
        in_range = rlo_ref[(bi * n_e + e) * nt + i] + k < n_grp
        return jnp.where(in_range, idx_ref[e, group(e, k, i)], -1)

    def copies(k, slot, tile):
        return [pltpu.make_async_copy(
            y_hbm.at[bi, e, pl.ds(pl.multiple_of(group(e, k, tile) * sr, sr), sr), :],
            ybuf.at[slot, pl.ds(e * sr, sr), :], sem.at[slot]) for e in range(n_e)]

    @pl.when(i == 0)
    def _():
        base_ref[0] = 0
        for c in copies(0, 0, i):
            c.start()

    base = base_ref[0]
    acc_ref[...] = jnp.zeros(acc_ref.shape, F32)
    per_row = HEAD_DIM // sr
    tok = i * tm + lax.broadcasted_iota(jnp.int32, (tm, HEAD_DIM), 0)

    def body(k, carry):
        slot = (base + k) % 2
        for c in copies(k, slot, i):
            c.wait()

        @pl.when(k + 1 < n_rounds)
        def _():
            for c in copies(k + 1, 1 - slot, i):
                c.start()

        @pl.when(jnp.logical_and(k + 1 == n_rounds, i + 1 < nt))
        def _():
            for c in copies(0, 1 - slot, i + 1):
                c.start()

        rows = [jnp.concatenate([slot_tokens(e + j, k) for j in range(per_row)], axis=1)
                for e in range(0, n_e, per_row)]
        onehot = jnp.concatenate([jnp.where(tok == r, 1.0, 0.0) for r in rows], axis=1)
        acc_ref[...] += jnp.dot(onehot.astype(BF16), ybuf[slot], preferred_element_type=F32)
        return carry

    lax.fori_loop(0, n_rounds, body, 0)
    base_ref[0] = (base + n_rounds) % 2
    x = x_ref[...] + g_ref[...] * acc_ref[...]
    if final:
        x = x * lax.rsqrt(jnp.mean(x * x, axis=-1, keepdims=True) + NORM_EPS) * ng_ref[...]
    o_ref[...] = x


def scatter_combine(x, y, idx, g2, final_g, *, tm=256):
    b, t, d = x.shape
    n_e, cap = idx.shape[1], idx.shape[2]
    tm = min(tm, t)
    sr = min(SCATTER_GROUP, cap)
    assert t % tm == 0 and cap % sr == 0 and HEAD_DIM % sr == 0 and n_e % (HEAD_DIM // sr) == 0
    nt, n_grp = t // tm, cap // sr
    edges = jnp.arange(nt + 1, dtype=idx.dtype) * tm
    pos = jnp.sum(idx[..., None] < edges, axis=2, dtype=jnp.int32)
    lo, hi = pos[..., :-1], pos[..., 1:]
    rlo = jnp.minimum(lo // sr, n_grp - 1)
    rhi = jnp.maximum(hi - 1, lo) // sr
    n_rounds = jnp.max(jnp.minimum(rhi, n_grp - 1) - rlo + 1, axis=1)
    final = final_g is not None
    ng = final_g if final else jnp.ones((1, d), F32)
    tile = pl.BlockSpec((None, tm, d), lambda bi, i, *_: (bi, i, 0))
    return pl.pallas_call(
        functools.partial(_scatter_kernel, sr=sr, final=final),
        grid_spec=pltpu.PrefetchScalarGridSpec(
            num_scalar_prefetch=2,
            grid=(b, nt),
            in_specs=[tile,
                      pl.BlockSpec((None, 1, d), lambda bi, i, *_: (bi, 0, 0)),
                      pl.BlockSpec((1, d), lambda bi, i, *_: (0, 0)),
                      pl.BlockSpec((None, n_e, n_grp, 1, sr), lambda bi, i, *_: (bi, 0, 0, 0, 0)),
                      pl.BlockSpec(memory_space=pl.ANY)],
            out_specs=tile,
            scratch_shapes=[pltpu.VMEM((2, n_e * sr, d), BF16),
                            pltpu.SemaphoreType.DMA((2,)),
                            pltpu.VMEM((tm, d), F32),
                            pltpu.SMEM((1,), jnp.int32)]),
        out_shape=jax.ShapeDtypeStruct((b, t, d), F32),
        compiler_params=_cparams("parallel", "arbitrary"),
        name="scatter_combine",
    )(rlo.reshape(-1), n_rounds.reshape(-1), x, g2, ng, idx.reshape(b, n_e, n_grp, 1, sr), y)


def expert_choice_route(h2, aff_t):
    t = h2.shape[1]
    cap = EC_CAPACITY * t // N_EXPERTS
    gate, idx = lax.top_k(aff_t, cap)
    idx, gate = lax.sort_key_val(idx, gate, dimension=-1)
    xin = jax.vmap(lambda hb, ib: hb[ib])(h2, idx)
    return xin, gate[..., None], idx


def _ctx_attn_kernel(scal_ref, q_ref, k_ref, v_ref, g_ref, o_ref, *, kind, post_scale):
    q, k, v = q_ref[...], k_ref[...], v_ref[...]
    if kind == "diff":
        qf = q.astype(F32)
        lane = lax.broadcasted_iota(jnp.int32, qf.shape, 1)

        def probs(keep):
            s = _nt_dot(jnp.where(keep, qf, 0.0).astype(BF16), k)
            e = jnp.exp2(s - jnp.max(s, axis=-1, keepdims=True))
            return e * (1.0 / jnp.sum(e, axis=-1, keepdims=True))

        w = probs(lane < DIFF_DIM) - scal_ref[0] * probs(lane >= DIFF_DIM)
        o = jnp.dot(w.astype(BF16), v, preferred_element_type=F32)
        o = o * lax.rsqrt(jnp.mean(o * o, axis=-1, keepdims=True) + NORM_EPS) * g_ref[...] * post_scale
    else:
        s = _nt_dot(q, k)
        m = jnp.max(s, axis=-1, keepdims=True)
        if kind == "sink":
            sink = scal_ref[pl.program_id(1)]
            m = jnp.maximum(m, sink)
        e = jnp.exp(s - m)
        l = jnp.sum(e, axis=-1, keepdims=True)
        if kind == "sink":
            l = l + jnp.exp(sink - m)
        o = jnp.dot((e * (1.0 / l)).astype(BF16), v, preferred_element_type=F32)
    o_ref[...] = o.astype(o_ref.dtype)


def _context_attention(pc, scal, norm_g, kind, post_scale, q_col, k_col, v_col, grp):
    b, lc, _ = pc.shape
    h = GROUP_BLOCKS
    col = lambda c0, div: (lambda bi, hi: (bi, 0, c0 + hi // div))
    return pl.pallas_call(
        functools.partial(_ctx_attn_kernel, kind=kind, post_scale=post_scale),
        grid=(b, h),
        in_specs=[
            pl.BlockSpec(memory_space=pltpu.SMEM),
            pl.BlockSpec((None, lc, HEAD_DIM), col(q_col, 1)),
            pl.BlockSpec((None, lc, HEAD_DIM), col(k_col, grp)),
            pl.BlockSpec((None, lc, HEAD_DIM), col(v_col, grp)),
            pl.BlockSpec((1, HEAD_DIM), lambda bi, hi: (0, 0)),
        ],
        out_specs=pl.BlockSpec((None, lc, HEAD_DIM), lambda bi, hi: (bi, 0, hi)),
        out_shape=jax.ShapeDtypeStruct((b, lc, h * HEAD_DIM), BF16),
        compiler_params=_cparams("parallel", "parallel"),
        name="context_attention_" + kind,
    )(scal, pc, pc, pc, norm_g)


def _context_mixers(pc, oc_ret, lam, lam_init, diff_norm_g, swa_sink):
    h = GROUP_BLOCKS
    zeros = jnp.zeros((h,), F32)
    ones = jnp.ones((1, HEAD_DIM), F32)
    o_a = _context_attention(pc, zeros, ones, "plain", 1.0, COL_A_Q, COL_A_K, COL_A_V, 1)
    o_b = _context_attention(pc, lam.reshape(1), diff_norm_g.reshape(1, HEAD_DIM).astype(F32), "diff",
                             1.0 - lam_init, COL_B_K - h, COL_B_K, COL_B_K + h, 1)
    o_d = _context_attention(pc, swa_sink.astype(F32), ones, "sink", 1.0, COL_D_Q, COL_D_K, COL_D_V, 2)
    return [o_a, o_b, oc_ret, o_d]


def _layer(x, xc, mod, layer_idx, with_ctx, final_g, rope_h, rope_d, rope_id,
           norm1_g, w_in, na_rpb, diff_lambda, diff_norm_g, ret_decay_fwd, ret_decay_bwd, ret_gn_g, ret_gn_b,
           swa_sink, w_out, norm2_g, w_router, w_gate, w_up, w_down):
    b, n, d = x.shape
    row = lambda v: v.reshape(1, -1).astype(F32)
    part = lambda r0, r1, k: mod[r0:r1, None, k * d:(k + 1) * d]
    sh1, sc1, g1, sh2, sc2, g2 = [part(0, b, k) for k in range(6)]
    ctx_rows = lambda k: jnp.broadcast_to(part(b, b + 1, k), (b, 1, d))
    sh1c, sc1c, g1c, sh2c, sc2c, g2c = [ctx_rows(k) for k in range(6)]

    w_in_b = w_in.astype(BF16)
    w_out_b = w_out.astype(BF16)
    w_router_t = w_router.T.astype(BF16)

    p, t = norm_project(x, row(norm1_g), sc1, sh1, w_in_b, rope_h, rope_d)
    pc, tc = norm_project(xc, row(norm1_g), sc1c, sh1c, w_in_b, rope_id, rope_id)

    lam_init = 0.8 - 0.6 * math.exp(-0.3 * layer_idx)
    lq1, lk1, lq2, lk2 = [diff_lambda[k].astype(F32) for k in range(4)]
    lam = jnp.exp(jnp.sum(lq1 * lk1)) - jnp.exp(jnp.sum(lq2 * lk2)) + lam_init

    o_a = neighbourhood_attention(p, pc, na_rpb)
    o_b = diff_attention(p, t, pc, tc, lam.reshape(1), row(diff_norm_g), 1.0 - lam_init)
    o_c, oc_ret = bidirectional_retention(p, pc, ret_decay_fwd, ret_decay_bwd, ret_gn_g, ret_gn_b, with_ctx)
    o_d = window_attention(p, pc, swa_sink.astype(F32))

    x, h2, aff_t = out_project([o_a, o_b, o_c, o_d], w_out_b, x, g1, row(norm2_g), sc2, sh2, w_router_t)
    xin, gate, idx = expert_choice_route(h2, aff_t)
    ctx_rows = idx_c = None
    if with_ctx:
        oc = _context_mixers(pc, oc_ret, lam, lam_init, diff_norm_g, swa_sink)
        xc, h2c, aff_c = out_project(oc, w_out_b, xc, g1c, row(norm2_g), sc2c, sh2c, w_router_t)
        xin_c, gate_c, idx_c = expert_choice_route(h2c, aff_c)
        ctx_rows = (xin_c, gate_c)
    y, y_c = expert_ffn(xin, gate, ctx_rows, w_gate, w_up, w_down, layer_idx)
    x = scatter_combine(x, y, idx, g2, final_g)
    if with_ctx:
        xc = scatter_combine(xc, y_c, idx_c, g2c, None)
    return x, xc


def kernel(x, c, ctx, c_ctx, w_mod, b_mod, norm1_g, w_in, na_rpb, diff_lambda, diff_norm_g,
           ret_decay_fwd, ret_decay_bwd, ret_gn_g, ret_gn_b, swa_sink, w_out, norm2_g,
           w_router, w_gate, w_up, w_down, final_norm_g):
    b, n, d = x.shape
    depth = w_in.shape[0]
    lc = ctx.shape[1]
    cvec = jnp.zeros((8, d), F32).at[:b].set(c).at[b].set(c_ctx)
    mod = modulation(cvec, w_mod, b_mod.reshape(depth, 1, -1))
    rope_h = _rope_tables(n, HEAD_DIM)
    rope_d = _rope_tables(n, DIFF_DIM)
    rope_id = _identity_rope_tables(lc)
    xc = ctx
    for li in range(depth):
        last = li == depth - 1
        x, xc = _layer(x, xc, mod[li], li, not last, final_norm_g.reshape(1, -1) if last else None,
                       rope_h, rope_d, rope_id,
                       norm1_g[li], w_in[li], na_rpb[li], diff_lambda[li], diff_norm_g[li],
                       ret_decay_fwd[li], ret_decay_bwd[li], ret_gn_g[li], ret_gn_b[li], swa_sink[li],
                       w_out[li], norm2_g[li], w_router[li], w_gate, w_up, w_down)
    return x
```

```python
import functools
import math

import jax
import jax.numpy as jnp
from jax import lax
from jax.experimental import pallas as pl
from jax.experimental.pallas import tpu as pltpu

GRID_W = 64
HEAD_DIM = 128
DIFF_DIM = HEAD_DIM // 2
NA_WIN_R = 8
NA_WIN_C = 16
RET_CHUNK = 128
SWA_WINDOW = 128
N_EXPERTS = 16
EC_CAPACITY = 2
ROPE_BASE = 10000.0
NORM_EPS = 1e-6
GN_EPS = 1e-5
NEG_INF = -1e30
LOG2E = math.log2(math.e)

BF16 = jnp.bfloat16
F32 = jnp.float32

VMEM_LIMIT_BYTES = 48 * 1024 * 1024

GROUP_BLOCKS = 4
COL_A_Q, COL_A_K, COL_A_V = 0, 4, 8
COL_B_K = 16
COL_C_Q, COL_C_K, COL_C_V, COL_C_G = 24, 28, 32, 36
COL_D_Q, COL_D_K, COL_D_V = 40, 44, 46
PROJ_TN = GROUP_BLOCKS * HEAD_DIM


def _cparams(*sem):
    return pltpu.CompilerParams(dimension_semantics=sem, vmem_limit_bytes=VMEM_LIMIT_BYTES)


def _mod_kernel(c_ref, w_ref, b_ref, o_ref):
    c = c_ref[...]
    s = (c * (1.0 / (1.0 + jnp.exp(-c)))).astype(BF16)
    o_ref[...] = jnp.dot(s, w_ref[...].astype(BF16), preferred_element_type=F32) + b_ref[...]


def modulation(cvec, w_mod, b_mod, *, tn=1024):
    depth, d, n6 = w_mod.shape
    return pl.pallas_call(
        _mod_kernel,
        grid=(depth, n6 // tn),
        in_specs=[
            pl.BlockSpec((8, d), lambda l, j: (0, 0)),
            pl.BlockSpec((None, d, tn), lambda l, j: (l, 0, j)),
            pl.BlockSpec((None, 1, tn), lambda l, j: (l, 0, j)),
        ],
        out_specs=pl.BlockSpec((None, 8, tn), lambda l, j: (l, 0, j)),
        out_shape=jax.ShapeDtypeStruct((depth, 8, n6), F32),
        compiler_params=_cparams("parallel", "parallel"),
        name="modulation",
    )(cvec, w_mod, b_mod)


def _rope_tables(n_tok, dim):
    t = jnp.arange(n_tok)
    row = (t // GRID_W).astype(F32)[:, None]
    col = (t % GRID_W).astype(F32)[:, None]
    nf = dim // 4
    lane = jnp.arange(HEAD_DIM)
    quarter = (lane % dim) // nf
    inv = ROPE_BASE ** (-jnp.arange(nf, dtype=F32) / nf)
    ang = jnp.where(quarter[None, :] < 2, row, col) * inv[lane % nf][None, :]
    cos, sin = jnp.cos(ang), jnp.sin(ang)
    even = (quarter % 2 == 0)[None, :]
    return cos, jnp.where(even, -sin, 0.0), jnp.where(even, 0.0, sin)


def _identity_rope_tables(n_tok):
    z = jnp.zeros((n_tok, HEAD_DIM), F32)
    return jnp.ones((n_tok, HEAD_DIM), F32), z, z


def _proj_modes():
    att = HEAD_DIM ** -0.5
    plain = [(None, 1.0)] * GROUP_BLOCKS
    return [
        ([(None, att)] * 4, None), (plain, None), (plain, None),
        ([("d", LOG2E * DIFF_DIM ** -0.5)] * 4, 0), ([("d", 1.0)] * 4, None), (plain, 1),
        ([("h", 1.0)] * 4, None), ([("h", att)] * 4, None), (plain, None), (plain, None),
        ([("h", att)] * 4, None), ([("h", 1.0)] * 2 + [(None, 1.0)] * 2, None),
    ]


def _proj_kernel(x_ref, g_ref, sc_ref, sh_ref, w_ref, ch_ref, sah_ref, sbh_ref, cd_ref, sad_ref, sbd_ref,
                 o_ref, t_ref):
    x = x_ref[...]
    y = x * lax.rsqrt(jnp.mean(x * x, axis=-1, keepdims=True) + NORM_EPS) * g_ref[...]
    h = (y * (1.0 + sc_ref[...]) + sh_ref[...]).astype(BF16)

    def rope(a, kind):
        if kind is None:
            return a
        c, sa, sb, sh = ((ch_ref, sah_ref, sbh_ref, HEAD_DIM // 4) if kind == "h"
                         else (cd_ref, sad_ref, sbd_ref, DIFF_DIM // 4))
        return (a * c[...] + pltpu.roll(a, HEAD_DIM - sh, 1) * sa[...] + pltpu.roll(a, sh, 1) * sb[...])

    for j, (blocks, t_slot) in enumerate(_proj_modes()):
        cols = slice(j * PROJ_TN, (j + 1) * PROJ_TN)
        acc = jnp.dot(h, w_ref[:, cols], preferred_element_type=F32)
        outs = []
        for hb, (kind, scale) in enumerate(blocks):
            a = rope(acc[:, hb * HEAD_DIM:(hb + 1) * HEAD_DIM], kind)
            outs.append(a if scale == 1.0 else a * scale)
        full = jnp.concatenate(outs, axis=1)
        o_ref[:, cols] = full.astype(o_ref.dtype)
        if t_slot is not None:
            t_ref[t_slot] = full.T.astype(t_ref.dtype)


def norm_project(x, norm_g, scale, shift, w_in, rope_h, rope_d, *, tm=256):
    b, n, d = x.shape
    width = w_in.shape[1]
    assert width == len(_proj_modes()) * PROJ_TN
    tm = min(tm, n)
    tok = lambda bi, i: (i, 0)
    return pl.pallas_call(
        _proj_kernel,
        grid=(b, n // tm),
        in_specs=[
            pl.BlockSpec((None, tm, d), lambda bi, i: (bi, i, 0)),
            pl.BlockSpec((1, d), lambda bi, i: (0, 0)),
            pl.BlockSpec((None, 1, d), lambda bi, i: (bi, 0, 0)),
            pl.BlockSpec((None, 1, d), lambda bi, i: (bi, 0, 0)),
            pl.BlockSpec((d, width), lambda bi, i: (0, 0), pipeline_mode=pl.Buffered(1)),
        ] + [pl.BlockSpec((tm, HEAD_DIM), tok)] * 6,
        out_specs=[
            pl.BlockSpec((None, tm, width), lambda bi, i: (bi, i, 0)),
            pl.BlockSpec((None, 2, PROJ_TN, tm), lambda bi, i: (bi, 0, 0, i)),
        ],
        out_shape=[jax.ShapeDtypeStruct((b, n, width), BF16),
                   jax.ShapeDtypeStruct((b, 2, PROJ_TN, n), BF16)],
        compiler_params=_cparams("parallel", "parallel"),
        name="norm_project",
    )(x, norm_g, scale, shift, w_in, *rope_h, *rope_d)


def _softmax(s_loc, s_ctx, sink):
    m = jnp.maximum(jnp.max(s_loc, axis=-1, keepdims=True), jnp.max(s_ctx, axis=-1, keepdims=True))
    if sink is not None:
        m = jnp.maximum(m, sink)
    p_loc = jnp.exp(s_loc - m)
    p_ctx = jnp.exp(s_ctx - m)
    l = jnp.sum(p_loc, axis=-1, keepdims=True) + jnp.sum(p_ctx, axis=-1, keepdims=True)
    if sink is not None:
        l = l + jnp.exp(sink - m)
    return p_loc.astype(BF16), p_ctx.astype(BF16), 1.0 / l


def _local_attention(scores, values, vc_ref, sink, o_ref, tq):
    n_sub = o_ref.shape[0] // tq
    s = [scores(sub) for sub in range(n_sub)]
    p = [_softmax(s_loc, s_ctx, sink) for s_loc, s_ctx in s]
    for sub, (p_loc, p_ctx, inv_l) in enumerate(p):
        o = (jnp.dot(p_loc, values(sub), preferred_element_type=F32)
             + jnp.dot(p_ctx, vc_ref[...], preferred_element_type=F32))
        o_ref[sub * tq:(sub + 1) * tq, :] = (o * inv_l).astype(o_ref.dtype)


def _nt_dot(a, b):
    return lax.dot_general(a, b, (((1,), (1,)), ((), ())), preferred_element_type=F32)


LOCAL_SUB_BLOCKS = 4


def _na_kernel(q_ref, k_ref, v_ref, kc_ref, vc_ref, bias_ref, o_ref, *, tq, wk, n_blk):
    n = k_ref.shape[0]
    rows_q = tq // GRID_W
    n_sub = q_ref.shape[0] // tq

    def kstart(sub):
        blk = pl.program_id(2) * n_sub + sub
        start = jnp.clip(blk * rows_q - NA_WIN_R // 2, 0, (n - wk) // GRID_W) * GRID_W
        return blk, pl.multiple_of(start, GRID_W)

    def scores(sub):
        blk, start = kstart(sub)
        cls = jnp.where(blk == 0, 0, jnp.where(blk == n_blk - 1, 2, 1))
        q = q_ref[sub * tq:(sub + 1) * tq, :]
        return _nt_dot(q, k_ref[pl.ds(start, wk), :]) + bias_ref[cls], _nt_dot(q, kc_ref[...])

    _local_attention(scores, lambda sub: v_ref[pl.ds(kstart(sub)[1], wk), :], vc_ref, None, o_ref, tq)


def _na_bias(rpb, rows, rows_q, rows_k):
    n_blk = rows // rows_q
    kr = NA_WIN_R
    cols = jnp.arange(GRID_W)
    col_start = jnp.clip(cols - NA_WIN_C // 2, 0, GRID_W - NA_WIN_C)
    col_ok = (cols[None, :] >= col_start[:, None]) & (cols[None, :] < col_start[:, None] + NA_WIN_C)
    col_off = jnp.clip(cols[None, :] - cols[:, None] + NA_WIN_C - 1, 0, 2 * NA_WIN_C - 2)
    out = []
    for blk in (0, 1, n_blk - 1):
        r = blk * rows_q + jnp.arange(rows_q)
        kstart = min(max(blk * rows_q - kr // 2, 0), rows - rows_k)
        krow = kstart + jnp.arange(rows_k)
        start = jnp.clip(r - kr // 2, 0, rows - kr)
        row_ok = (krow[None, :] >= start[:, None]) & (krow[None, :] < start[:, None] + kr)
        row_off = jnp.clip(krow[None, :] - r[:, None] + NA_WIN_R - 1, 0, 2 * NA_WIN_R - 2)
        bias = jnp.einsum('abr,hrc,qkc->habqk', jax.nn.one_hot(row_off, 2 * NA_WIN_R - 1, dtype=F32),
                          rpb.astype(F32), jax.nn.one_hot(col_off, 2 * NA_WIN_C - 1, dtype=F32),
                          precision=lax.Precision.HIGHEST)
        ok = row_ok[:, :, None, None] & col_ok[None, None]
        bias = jnp.where(ok[None], bias, NEG_INF).transpose(0, 1, 3, 2, 4)
        out.append(bias.reshape(rpb.shape[0], rows_q * GRID_W, rows_k * GRID_W))
    return jnp.stack(out, axis=1)


def neighbourhood_attention(p, pc, rpb, *, rows_q=4):
    b, n, _ = p.shape
    lc = pc.shape[1]
    h = rpb.shape[0]
    rows = n // GRID_W
    rows_k = rows_q + NA_WIN_R - 1
    assert rows % rows_q == 0 and rows >= rows_k and rows_q >= NA_WIN_R // 2
    tq, wk = rows_q * GRID_W, rows_k * GRID_W
    n_blk = rows // rows_q
    sub = LOCAL_SUB_BLOCKS if n_blk % LOCAL_SUB_BLOCKS == 0 else 1
    bias = _na_bias(rpb, rows, rows_q, rows_k)
    kern = functools.partial(_na_kernel, tq=tq, wk=wk, n_blk=n_blk)
    return pl.pallas_call(
        kern,
        grid=(b, h, n_blk // sub),
        in_specs=[
            pl.BlockSpec((None, sub * tq, HEAD_DIM), lambda bi, hi, i: (bi, i, COL_A_Q + hi)),
            pl.BlockSpec((None, n, HEAD_DIM), lambda bi, hi, i: (bi, 0, COL_A_K + hi)),
            pl.BlockSpec((None, n, HEAD_DIM), lambda bi, hi, i: (bi, 0, COL_A_V + hi)),
            pl.BlockSpec((None, lc, HEAD_DIM), lambda bi, hi, i: (bi, 0, COL_A_K + hi)),
            pl.BlockSpec((None, lc, HEAD_DIM), lambda bi, hi, i: (bi, 0, COL_A_V + hi)),
            pl.BlockSpec((None, 3, tq, wk), lambda bi, hi, i: (hi, 0, 0, 0)),
        ],
        out_specs=pl.BlockSpec((None, sub * tq, HEAD_DIM), lambda bi, hi, i: (bi, i, hi)),
        out_shape=jax.ShapeDtypeStruct((b, n, h * HEAD_DIM), BF16),
        compiler_params=_cparams("parallel", "parallel", "arbitrary"),
        name="neighbourhood_attention",
    )(p, p, p, pc, pc, bias)


def _swa_kernel(sink_ref, q_ref, k_ref, v_ref, kc_ref, vc_ref, o_ref, *, tq, wk):
    n = k_ref.shape[0]

    def window(sub):
        q0 = pl.program_id(2) * q_ref.shape[0] + sub * tq
        return q0, pl.multiple_of(jnp.clip(q0 - SWA_WINDOW, 0, n - wk), SWA_WINDOW)

    def scores(sub):
        q0, start = window(sub)
        q = q_ref[sub * tq:(sub + 1) * tq, :]
        qpos = q0 + lax.broadcasted_iota(jnp.int32, (tq, wk), 0)
        kpos = start + lax.broadcasted_iota(jnp.int32, (tq, wk), 1)
        s_loc = jnp.where(jnp.abs(kpos - qpos) <= SWA_WINDOW, _nt_dot(q, k_ref[pl.ds(start, wk), :]), NEG_INF)
        return s_loc, _nt_dot(q, kc_ref[...])

    _local_attention(scores, lambda sub: v_ref[pl.ds(window(sub)[1], wk), :], vc_ref,
                     sink_ref[pl.program_id(1)], o_ref, tq)


def window_attention(p, pc, sink, *, tq=256):
    b, n, _ = p.shape
    lc = pc.shape[1]
    hq = sink.shape[0]
    grp = hq // (COL_D_V - COL_D_K)
    tq = min(tq, n)
    wk = min(tq + 2 * SWA_WINDOW, n)
    sub = LOCAL_SUB_BLOCKS if (n // tq) % LOCAL_SUB_BLOCKS == 0 else 1
    kern = functools.partial(_swa_kernel, tq=tq, wk=wk)
    return pl.pallas_call(
        kern,
        grid=(b, hq, n // (sub * tq)),
        in_specs=[
            pl.BlockSpec(memory_space=pltpu.SMEM),
            pl.BlockSpec((None, sub * tq, HEAD_DIM), lambda bi, hi, i: (bi, i, COL_D_Q + hi)),
            pl.BlockSpec((None, n, HEAD_DIM), lambda bi, hi, i: (bi, 0, COL_D_K + hi // grp)),
            pl.BlockSpec((None, n, HEAD_DIM), lambda bi, hi, i: (bi, 0, COL_D_V + hi // grp)),
            pl.BlockSpec((None, lc, HEAD_DIM), lambda bi, hi, i: (bi, 0, COL_D_K + hi // grp)),
            pl.BlockSpec((None, lc, HEAD_DIM), lambda bi, hi, i: (bi, 0, COL_D_V + hi // grp)),
        ],
        out_specs=pl.BlockSpec((None, sub * tq, HEAD_DIM), lambda bi, hi, i: (bi, i, hi)),
        out_shape=jax.ShapeDtypeStruct((b, n, hq * HEAD_DIM), BF16),
        compiler_params=_cparams("parallel", "parallel", "arbitrary"),
        name="window_attention",
    )(sink, p, p, p, pc, pc)


DIFF_SLOTS = 2
DIFF_UNROLL = 6
DIFF_AUG_ROWS = 16


def _diff_attn_kernel(lam_ref, qt_ref, k_ref, vt_ref, kc_ref, vct_ref, g_ref, o_ref,
                      qbd_ref, m_ref, acc_ref, s_scr, x_scr, *, tq, tk, post_scale):
    qt = qt_ref[...].astype(F32)
    row = lax.broadcasted_iota(jnp.int32, qt.shape, 0)
    zero = jnp.zeros_like(qt)
    qbd_ref[:, :tq] = jnp.where(row < DIFF_DIM, qt, zero).astype(BF16)
    qbd_ref[:, tq:] = jnp.where(row >= DIFF_DIM, qt, zero).astype(BF16)

    def aug(vt_tile):
        r = lax.broadcasted_iota(jnp.int32, (DIFF_AUG_ROWS, vt_tile.shape[1]), 0)
        return jnp.concatenate([vt_tile, jnp.where(r == 0, 1.0, 0.0).astype(BF16)], axis=0)

    s = jnp.dot(kc_ref[...], qbd_ref[...], preferred_element_type=F32)
    m0 = jnp.max(s, axis=0, keepdims=True)
    m_ref[...] = m0
    acc_ref[...] = jnp.dot(aug(vct_ref[...]), jnp.exp2(s - m0).astype(BF16), preferred_element_type=F32)

    def scores(t, slot):
        off = pl.multiple_of(t * tk, tk)
        s = jnp.dot(k_ref[pl.ds(off, tk), :], qbd_ref[...], preferred_element_type=F32)
        s_scr[slot] = s
        x_scr[slot] = jnp.max(s, axis=0, keepdims=True)

    def accumulate(t, slot):
        m_old = m_ref[...]
        m_new = jnp.maximum(m_old, x_scr[slot])
        alpha = jnp.exp2(m_old - m_new)
        p = jnp.exp2(s_scr[slot] - m_new).astype(BF16)
        m_ref[...] = m_new
        off = pl.multiple_of(t * tk, tk)
        pv = jnp.dot(aug(vt_ref[:, pl.ds(off, tk)]), p, preferred_element_type=F32)
        acc_ref[...] = alpha * acc_ref[...] + pv

    def stage(t, t_mod, do_scores=True):
        if do_scores:
            scores(t + 1, (t_mod + 1) % DIFF_SLOTS)
        accumulate(t, t_mod % DIFF_SLOTS)

    n_kt = k_ref.shape[0] // tk
    trips = (n_kt - 1) // DIFF_UNROLL
    scores(0, 0)

    def body(i, carry):
        for u in range(DIFF_UNROLL):
            stage(DIFF_UNROLL * i + u, u)
        return carry

    lax.fori_loop(0, trips, body, 0)
    for t in range(DIFF_UNROLL * trips, n_kt - 1):
        stage(t, t)
    stage(n_kt - 1, n_kt - 1, do_scores=False)

    lam = lam_ref[0]
    inv = 1.0 / acc_ref[HEAD_DIM:HEAD_DIM + 1, :]
    acc = acc_ref[:HEAD_DIM, :]
    o_t = acc[:, :tq] * inv[:, :tq] - lam * (acc[:, tq:] * inv[:, tq:])
    o = o_t.T
    ms = jnp.mean(o * o, axis=-1, keepdims=True)
    y = o * lax.rsqrt(ms + NORM_EPS) * g_ref[...]
    o_ref[...] = (y * post_scale).astype(o_ref.dtype)


def diff_attention(p, t, pc, tc, lam, norm_g, post_scale, *, tq=512, tk=512):
    b, n, _ = p.shape
    lc = pc.shape[1]
    h = GROUP_BLOCKS
    tq = min(tq, n)
    tk = min(tk, n)
    assert n % tq == 0 and n % tk == 0 and DIFF_UNROLL % DIFF_SLOTS == 0
    kern = functools.partial(_diff_attn_kernel, tq=tq, tk=tk, post_scale=post_scale)
    return pl.pallas_call(
        kern,
        grid=(b, h, n // tq),
        in_specs=[
            pl.BlockSpec(memory_space=pltpu.SMEM),
            pl.BlockSpec((None, None, HEAD_DIM, tq), lambda bi, hi, qi: (bi, 0, hi, qi)),
            pl.BlockSpec((None, n, HEAD_DIM), lambda bi, hi, qi: (bi, 0, COL_B_K + hi)),
            pl.BlockSpec((None, None, HEAD_DIM, n), lambda bi, hi, qi: (bi, 1, hi, 0)),
            pl.BlockSpec((None, lc, HEAD_DIM), lambda bi, hi, qi: (bi, 0, COL_B_K + hi)),
            pl.BlockSpec((None, None, HEAD_DIM, lc), lambda bi, hi, qi: (bi, 1, hi, 0)),
            pl.BlockSpec((1, HEAD_DIM), lambda bi, hi, qi: (0, 0)),
        ],
        out_specs=pl.BlockSpec((None, tq, HEAD_DIM), lambda bi, hi, qi: (bi, qi, hi)),
        out_shape=jax.ShapeDtypeStruct((b, n, h * HEAD_DIM), BF16),
        scratch_shapes=[
            pltpu.VMEM((HEAD_DIM, 2 * tq), BF16),
            pltpu.VMEM((1, 2 * tq), F32),
            pltpu.VMEM((HEAD_DIM + DIFF_AUG_ROWS, 2 * tq), F32),
            pltpu.VMEM((DIFF_SLOTS, tk, 2 * tq), F32),
            pltpu.VMEM((DIFF_SLOTS, 1, 2 * tq), F32),
        ],
        compiler_params=_cparams("parallel", "parallel", "arbitrary"),
        name="diff_attention",
    )(lam, t, p, t, pc, tc, norm_g)


def _retention_kernel(cdec_ref, q_ref, k_ref, v_ref, intra_ref, qdec_ref, kdec_ref, s0_ref, *rest,
                      reverse, final):
    if final:
        of_ref, gate_ref, gng_ref, gnb_ref, o_ref, sfin_ref, s_ref = rest
    else:
        o_ref, sfin_ref, s_ref = rest
    i = pl.program_id(1)
    n_heads = s_ref.shape[0]
    n_chunks = q_ref.shape[0] // RET_CHUNK

    @pl.when(i == 0)
    def _():
        s_ref[...] = s0_ref[...]

    order = range(n_chunks - 1, -1, -1) if reverse else range(n_chunks)
    units = [(c, h) for c in order for h in range(n_heads)]
    window = lambda c, h: (slice(c * RET_CHUNK, (c + 1) * RET_CHUNK), slice(h * HEAD_DIM, (h + 1) * HEAD_DIM))

    intra_o, kv, qdec = {}, {}, {}
    for c, h in units:
        rs, cs = window(c, h)
        qh, kh, vh = q_ref[rs, cs], k_ref[rs, cs], v_ref[rs, cs]
        qdec[c, h] = (qh.astype(F32) * qdec_ref[h]).astype(BF16)
        kd_t = (kh.astype(F32) * kdec_ref[h]).T.astype(BF16)
        a = (_nt_dot(qh, kh) * intra_ref[h]).astype(BF16)
        intra_o[c, h] = jnp.dot(a, vh, preferred_element_type=F32)
        kv[c, h] = jnp.dot(kd_t, vh, preferred_element_type=F32)

    for c, h in units:
        rs, cs = window(c, h)
        s = s_ref[h]
        o = intra_o[c, h] + jnp.dot(qdec[c, h], s.astype(BF16), preferred_element_type=F32)
        s_ref[h] = cdec_ref[h] * s + kv[c, h]
        if final:
            o = o + of_ref[rs, cs]
            mu = jnp.mean(o, axis=-1, keepdims=True)
            var = jnp.mean(jnp.square(o - mu), axis=-1, keepdims=True)
            y = (o - mu) * lax.rsqrt(var + GN_EPS) * gng_ref[:, cs] + gnb_ref[:, cs]
            g = gate_ref[rs, cs].astype(F32)
            o = g * (1.0 / (1.0 + jnp.exp(-g))) * y
        o_ref[rs, cs] = o.astype(o_ref.dtype)

    @pl.when(i == pl.num_programs(1) - 1)
    def _():
        sfin_ref[...] = s_ref[...]


def _retention_pass(p, tables, s0, final_inputs, *, reverse, blk_chunks=4):
    b, n, _ = p.shape
    cdec, intra, qdec, kdec = tables
    h = intra.shape[0]
    width = h * HEAD_DIM
    n_chunks = n // RET_CHUNK
    blk_chunks = min(blk_chunks, n_chunks)
    assert n_chunks % blk_chunks == 0
    tb = blk_chunks * RET_CHUNK
    n_blk = n // tb
    pos = (lambda i: n_blk - 1 - i) if reverse else (lambda i: i)
    colblk = lambda c: (lambda bi, i: (bi, pos(i), c // GROUP_BLOCKS))
    tab = pl.BlockSpec((h, RET_CHUNK, HEAD_DIM), lambda bi, i: (0, 0, 0))
    state = pl.BlockSpec((None, h, HEAD_DIM, HEAD_DIM), lambda bi, i: (bi, 0, 0, 0))
    in_specs = [pl.BlockSpec(memory_space=pltpu.SMEM),
                pl.BlockSpec((None, tb, width), colblk(COL_C_Q)),
                pl.BlockSpec((None, tb, width), colblk(COL_C_K)),
                pl.BlockSpec((None, tb, width), colblk(COL_C_V)),
                tab, tab, tab, state]
    args = [cdec, p, p, p, intra, qdec, kdec, s0]
    final = final_inputs is not None
    if final:
        o_fwd, gn_g, gn_b = final_inputs
        in_specs += [pl.BlockSpec((None, tb, width), lambda bi, i: (bi, pos(i), 0)),
                     pl.BlockSpec((None, tb, width), colblk(COL_C_G)),
                     pl.BlockSpec((1, width), lambda bi, i: (0, 0)),
                     pl.BlockSpec((1, width), lambda bi, i: (0, 0))]
        args += [o_fwd, p, gn_g, gn_b]
    kern = functools.partial(_retention_kernel, reverse=reverse, final=final)
    return pl.pallas_call(
        kern,
        grid=(b, n_blk),
        in_specs=in_specs,
        out_specs=[pl.BlockSpec((None, tb, width), lambda bi, i: (bi, pos(i), 0)), state],
        out_shape=[jax.ShapeDtypeStruct((b, n, width), BF16 if final else F32),
                   jax.ShapeDtypeStruct((b, h, HEAD_DIM, HEAD_DIM), F32)],
        scratch_shapes=[pltpu.VMEM((h, HEAD_DIM, HEAD_DIM), F32)],
        compiler_params=_cparams("parallel", "arbitrary"),
        name="retention_bwd" if reverse else "retention_fwd",
    )(*args)


def _retention_tables(decay, reverse):
    lg = jax.nn.log_sigmoid(decay.astype(F32))[:, None, None]
    pos = jnp.arange(RET_CHUNK, dtype=F32)
    rel = pos[:, None] - pos[None, :]
    if reverse:
        rel = -rel
        q_pow, k_pow = RET_CHUNK - pos, pos
    else:
        q_pow, k_pow = pos + 1.0, RET_CHUNK - 1.0 - pos
    intra = jnp.where(rel >= 0, jnp.exp(jnp.maximum(rel, 0.0) * lg), 0.0)
    bc = lambda e: jnp.broadcast_to(jnp.exp(e[None, :, None] * lg), intra.shape)
    cdec = jnp.exp(RET_CHUNK * lg[:, 0, 0])
    return cdec, intra, bc(q_pow), bc(k_pow)


def bidirectional_retention(p, pc, decay_f, decay_b, gn_g, gn_b, with_ctx):
    b = p.shape[0]
    h = decay_f.shape[0]
    tf = _retention_tables(decay_f, False)
    tb = _retention_tables(decay_b, True)
    s0 = jnp.zeros((b, h, HEAD_DIM, HEAD_DIM), F32)
    gn = (gn_g.reshape(1, -1).astype(F32), gn_b.reshape(1, -1).astype(F32))
    oc_f, s_f = _retention_pass(pc, tf, s0, None, reverse=False)
    oc, s_b = _retention_pass(pc, tb, s0, (oc_f,) + gn, reverse=True)
    o_f, _ = _retention_pass(p, tf, s_f, None, reverse=False)
    o, _ = _retention_pass(p, tb, s_b, (o_f,) + gn, reverse=True)
    return o, (oc if with_ctx else None)


OUT_SUB_ROWS = 256


def _out_kernel(oa_ref, ob_ref, oc_ref, od_ref, w_ref, x_ref, g1_ref, ng_ref, sc_ref, sh_ref, wr_ref,
                xo_ref, h_ref, aff_ref):
    gw = oa_ref.shape[1]
    rows = min(OUT_SUB_ROWS, x_ref.shape[0])
    subs = [slice(sub * rows, (sub + 1) * rows) for sub in range(x_ref.shape[0] // rows)]
    accs = []
    for rs in subs:
        acc = jnp.dot(oa_ref[rs, :], w_ref[0 * gw:1 * gw, :], preferred_element_type=F32)
        acc += jnp.dot(ob_ref[rs, :], w_ref[1 * gw:2 * gw, :], preferred_element_type=F32)
        acc += jnp.dot(oc_ref[rs, :], w_ref[2 * gw:3 * gw, :], preferred_element_type=F32)
        acc += jnp.dot(od_ref[rs, :], w_ref[3 * gw:4 * gw, :], preferred_element_type=F32)
        accs.append(acc)
    for rs, acc in zip(subs, accs):
        x = x_ref[rs, :] + g1_ref[...] * acc
        xo_ref[rs, :] = x
        y = x * lax.rsqrt(jnp.mean(x * x, axis=-1, keepdims=True) + NORM_EPS) * ng_ref[...]
        h2 = (y * (1.0 + sc_ref[...]) + sh_ref[...]).astype(BF16)
        h_ref[rs, :] = h2
        logits = _nt_dot(wr_ref[...], h2)
        e = jnp.exp(logits - jnp.max(logits, axis=0, keepdims=True))
        aff_ref[:, rs] = e * (1.0 / jnp.sum(e, axis=0, keepdims=True))


def out_project(o_groups, w_out, x, g1, norm_g, scale, shift, w_router_t, *, tm=2 * OUT_SUB_ROWS):
    b, n, d = x.shape
    n_e = w_router_t.shape[0]
    gw = o_groups[0].shape[-1]
    tm = min(tm, n)
    tile = lambda w: pl.BlockSpec((None, tm, w), lambda bi, i: (bi, i, 0))
    vec = pl.BlockSpec((None, 1, d), lambda bi, i: (bi, 0, 0))
    return pl.pallas_call(
        _out_kernel,
        grid=(b, n // tm),
        in_specs=[tile(gw)] * 4 + [
            pl.BlockSpec((d, d), lambda bi, i: (0, 0), pipeline_mode=pl.Buffered(1)),
            tile(d), vec,
            pl.BlockSpec((1, d), lambda bi, i: (0, 0)),
            vec, vec,
            pl.BlockSpec((n_e, d), lambda bi, i: (0, 0)),
        ],
        out_specs=[tile(d), tile(d), pl.BlockSpec((None, n_e, tm), lambda bi, i: (bi, 0, i))],
        out_shape=[jax.ShapeDtypeStruct((b, n, d), F32),
                   jax.ShapeDtypeStruct((b, n, d), BF16),
                   jax.ShapeDtypeStruct((b, n_e, n), F32)],
        compiler_params=_cparams("parallel", "parallel"),
        name="out_project",
    )(*o_groups, w_out, x, g1, norm_g, scale, shift, w_router_t)


def _ffn_kernel(*refs, n_lat, with_ctx):
    if with_ctx:
        x_ref, gate_ref, xc_ref, gc_ref, wg_ref, wu_ref, wd_ref, o_ref, oc_ref = refs
    else:
        x_ref, gate_ref, wg_ref, wu_ref, wd_ref, o_ref = refs

    def ffn(x_ref, gate_ref, o_ref):
        x = x_ref[...]
        a = jnp.dot(x, wg_ref[...].astype(BF16), preferred_element_type=F32)
        u = jnp.dot(x, wu_ref[...].astype(BF16), preferred_element_type=F32)
        hm = (a * (1.0 / (1.0 + jnp.exp(-a))) * u).astype(BF16)
        y = jnp.dot(hm, wd_ref[...].astype(BF16), preferred_element_type=F32) * gate_ref[...]
        o_ref[...] = y.astype(o_ref.dtype)

    if not with_ctx:
        ffn(x_ref, gate_ref, o_ref)
        return

    @pl.when(pl.program_id(2) < n_lat)
    def _():
        ffn(x_ref, gate_ref, o_ref)

    @pl.when(pl.program_id(2) == n_lat)
    def _():
        ffn(xc_ref, gc_ref, oc_ref)


def expert_ffn(xin, gate, ctx_rows, w_gate, w_up, w_down, layer, *, tc=256):
    b, n_e, cap, d = xin.shape
    ff = w_gate.shape[-1]
    tc = min(tc, cap)
    n_lat = cap // tc
    with_ctx = ctx_rows is not None
    weight = lambda rows, cols: pl.BlockSpec((None, None, rows, cols), lambda e, bi, i: (layer, e, 0, 0),
                                             pipeline_mode=pl.Buffered(1))
    lat = lambda w: pl.BlockSpec((None, None, tc, w), lambda e, bi, i: (bi, e, jnp.minimum(i, n_lat - 1), 0))
    in_specs, args = [lat(d), lat(1)], [xin, gate]
    out_specs, out_shape = [lat(d)], [jax.ShapeDtypeStruct((b, n_e, cap, d), BF16)]
    if with_ctx:
        cap_c = ctx_rows[0].shape[2]
        whole = lambda w: pl.BlockSpec((None, None, cap_c, w), lambda e, bi, i: (bi, e, 0, 0))
        in_specs += [whole(d), whole(1)]
        args += list(ctx_rows)
        out_specs.append(whole(d))
        out_shape.append(jax.ShapeDtypeStruct((b, n_e, cap_c, d), BF16))
    out = pl.pallas_call(
        functools.partial(_ffn_kernel, n_lat=n_lat, with_ctx=with_ctx),
        grid=(n_e, b, n_lat + int(with_ctx)),
        in_specs=in_specs + [weight(d, ff), weight(d, ff), weight(ff, d)],
        out_specs=out_specs,
        out_shape=out_shape,
        compiler_params=_cparams("parallel", "parallel", "arbitrary"),
        name="expert_ffn",
    )(*args, w_gate, w_up, w_down)
    return (out[0], out[1]) if with_ctx else (out[0], None)


SCATTER_GROUP = 64


def _scatter_kernel(rlo_ref, nrd_ref, x_ref, g_ref, ng_ref, idx_ref, y_hbm, o_ref, ybuf, sem, acc_ref,
                    base_ref, *, sr, final):
    bi, i, nt = pl.program_id(0), pl.program_id(1), pl.num_programs(1)
    n_e, n_grp = idx_ref.shape[0], idx_ref.shape[1]
    tm = x_ref.shape[0]
    n_rounds = nrd_ref[bi * nt + i]

    def group(e, k, tile):
        return jnp.minimum(rlo_ref[(bi * n_e + e) * nt + tile] + k, n_grp - 1)

    def slot_tokens(e, k):
        in_range = rlo_ref[(bi * n_e + e) * nt + i] + k < n_grp
        return jnp.where(in_range, idx_ref[e, group(e, k, i)], -1)

    def copies(k, slot, tile):
        return [pltpu.make_async_copy(
            y_hbm.at[bi, e, pl.ds(pl.multiple_of(group(e, k, tile) * sr, sr), sr), :],
            ybuf.at[slot, pl.ds(e * sr, sr), :], sem.at[slot]) for e in range(n_e)]

    @pl.when(i == 0)
    def _():
        base_ref[0] = 0
        for c in copies(0, 0, i):
            c.start()

    base = base_ref[0]
    acc_ref[...] = jnp.zeros(acc_ref.shape, F32)
    per_row = HEAD_DIM // sr
    tok = i * tm + lax.broadcasted_iota(jnp.int32, (tm, HEAD_DIM), 0)

    def body(k, carry):
        slot = (base + k) % 2
        for c in copies(k, slot, i):
            c.wait()

        @pl.when(k + 1 < n_rounds)
        def _():
            for c in copies(k + 1, 1 - slot, i):
                c.start()

        @pl.when(jnp.logical_and(k + 1 == n_rounds, i + 1 < nt))
        def _():
            for c in copies(0, 1 - slot, i + 1):
                c.start()

        rows = [jnp.concatenate([slot_tokens(e + j, k) for j in range(per_row)], axis=1)
                for e in range(0, n_e, per_row)]
        onehot = jnp.concatenate([jnp.where(tok == r, 1.0, 0.0) for r in rows], axis=1)
        acc_ref[...] += jnp.dot(onehot.astype(BF16), ybuf[slot], preferred_element_type=F32)
        return carry

    lax.fori_loop(0, n_rounds, body, 0)
    base_ref[0] = (base + n_rounds) % 2
    x = x_ref[...] + g_ref[...] * acc_ref[...]
    if final:
        x = x * lax.rsqrt(jnp.mean(x * x, axis=-1, keepdims=True) + NORM_EPS) * ng_ref[...]
    o_ref[...] = x


def scatter_combine(x, y, idx, g2, final_g, *, tm=256):
    b, t, d = x.shape
    n_e, cap = idx.shape[1], idx.shape[2]
    tm = min(tm, t)
    sr = min(SCATTER_GROUP, cap)
    assert t % tm == 0 and cap % sr == 0 and HEAD_DIM % sr == 0 and n_e % (HEAD_DIM // sr) == 0
    nt, n_grp = t // tm, cap // sr
    edges = jnp.arange(nt + 1, dtype=idx.dtype) * tm
    pos = jnp.sum(idx[..., None] < edges, axis=2, dtype=jnp.int32)
    lo, hi = pos[..., :-1], pos[..., 1:]
    rlo = jnp.minimum(lo // sr, n_grp - 1)
    rhi = jnp.maximum(hi - 1, lo) // sr
    n_rounds = jnp.max(jnp.minimum(rhi, n_grp - 1) - rlo + 1, axis=1)
    final = final_g is not None
    ng = final_g if final else jnp.ones((1, d), F32)
    tile = pl.BlockSpec((None, tm, d), lambda bi, i, *_: (bi, i, 0))
    return pl.pallas_call(
        functools.partial(_scatter_kernel, sr=sr, final=final),
        grid_spec=pltpu.PrefetchScalarGridSpec(
            num_scalar_prefetch=2,
            grid=(b, nt),
            in_specs=[tile,
                      pl.BlockSpec((None, 1, d), lambda bi, i, *_: (bi, 0, 0)),
                      pl.BlockSpec((1, d), lambda bi, i, *_: (0, 0)),
                      pl.BlockSpec((None, n_e, n_grp, 1, sr), lambda bi, i, *_: (bi, 0, 0, 0, 0)),
                      pl.BlockSpec(memory_space=pl.ANY)],
            out_specs=tile,
            scratch_shapes=[pltpu.VMEM((2, n_e * sr, d), BF16),
                            pltpu.SemaphoreType.DMA((2,)),
                            pltpu.VMEM((tm, d), F32),
                            pltpu.SMEM((1,), jnp.int32)]),
        out_shape=jax.ShapeDtypeStruct((b, t, d), F32),
        compiler_params=_cparams("parallel", "arbitrary"),
        name="scatter_combine",
    )(rlo.reshape(-1), n_rounds.reshape(-1), x, g2, ng, idx.reshape(b, n_e, n_grp, 1, sr), y)


def expert_choice_route(h2, aff_t):
    t = h2.shape[1]
    cap = EC_CAPACITY * t // N_EXPERTS
    gate, idx = lax.top_k(aff_t, cap)
    idx, gate = lax.sort_key_val(idx, gate, dimension=-1)
    xin = jax.vmap(lambda hb, ib: hb[ib])(h2, idx)
    return xin, gate[..., None], idx


def _ctx_attn_kernel(scal_ref, q_ref, k_ref, v_ref, g_ref, o_ref, *, kind, post_scale):
    q, k, v = q_ref[...], k_ref[...], v_ref[...]
    if kind == "diff":
        qf = q.astype(F32)
        lane = lax.broadcasted_iota(jnp.int32, qf.shape, 1)

        def probs(keep):
            s = _nt_dot(jnp.where(keep, qf, 0.0).astype(BF16), k)
            e = jnp.exp2(s - jnp.max(s, axis=-1, keepdims=True))
            return e * (1.0 / jnp.sum(e, axis=-1, keepdims=True))

        w = probs(lane < DIFF_DIM) - scal_ref[0] * probs(lane >= DIFF_DIM)
        o = jnp.dot(w.astype(BF16), v, preferred_element_type=F32)
        o = o * lax.rsqrt(jnp.mean(o * o, axis=-1, keepdims=True) + NORM_EPS) * g_ref[...] * post_scale
    else:
        s = _nt_dot(q, k)
        m = jnp.max(s, axis=-1, keepdims=True)
        if kind == "sink":
            sink = scal_ref[pl.program_id(1)]
            m = jnp.maximum(m, sink)
        e = jnp.exp(s - m)
        l = jnp.sum(e, axis=-1, keepdims=True)
        if kind == "sink":
            l = l + jnp.exp(sink - m)
        o = jnp.dot((e * (1.0 / l)).astype(BF16), v, preferred_element_type=F32)
    o_ref[...] = o.astype(o_ref.dtype)


def _context_attention(pc, scal, norm_g, kind, post_scale, q_col, k_col, v_col, grp):
    b, lc, _ = pc.shape
    h = GROUP_BLOCKS
    col = lambda c0, div: (lambda bi, hi: (bi, 0, c0 + hi // div))
    return pl.pallas_call(
        functools.partial(_ctx_attn_kernel, kind=kind, post_scale=post_scale),
        grid=(b, h),
        in_specs=[
            pl.BlockSpec(memory_space=pltpu.SMEM),
            pl.BlockSpec((None, lc, HEAD_DIM), col(q_col, 1)),
            pl.BlockSpec((None, lc, HEAD_DIM), col(k_col, grp)),
            pl.BlockSpec((None, lc, HEAD_DIM), col(v_col, grp)),
            pl.BlockSpec((1, HEAD_DIM), lambda bi, hi: (0, 0)),
        ],
        out_specs=pl.BlockSpec((None, lc, HEAD_DIM), lambda bi, hi: (bi, 0, hi)),
        out_shape=jax.ShapeDtypeStruct((b, lc, h * HEAD_DIM), BF16),
        compiler_params=_cparams("parallel", "parallel"),
        name="context_attention_" + kind,
    )(scal, pc, pc, pc, norm_g)


def _context_mixers(pc, oc_ret, lam, lam_init, diff_norm_g, swa_sink):
    h = GROUP_BLOCKS
    zeros = jnp.zeros((h,), F32)
    ones = jnp.ones((1, HEAD_DIM), F32)
    o_a = _context_attention(pc, zeros, ones, "plain", 1.0, COL_A_Q, COL_A_K, COL_A_V, 1)
    o_b = _context_attention(pc, lam.reshape(1), diff_norm_g.reshape(1, HEAD_DIM).astype(F32), "diff",
                             1.0 - lam_init, COL_B_K - h, COL_B_K, COL_B_K + h, 1)
    o_d = _context_attention(pc, swa_sink.astype(F32), ones, "sink", 1.0, COL_D_Q, COL_D_K, COL_D_V, 2)
    return [o_a, o_b, oc_ret, o_d]


def _layer(x, xc, mod, layer_idx, with_ctx, final_g, rope_h, rope_d, rope_id,
           norm1_g, w_in, na_rpb, diff_lambda, diff_norm_g, ret_decay_fwd, ret_decay_bwd, ret_gn_g, ret_gn_b,
           swa_sink, w_out, norm2_g, w_router, w_gate, w_up, w_down):
    b, n, d = x.shape
    row = lambda v: v.reshape(1, -1).astype(F32)
    part = lambda r0, r1, k: mod[r0:r1, None, k * d:(k + 1) * d]
    sh1, sc1, g1, sh2, sc2, g2 = [part(0, b, k) for k in range(6)]
    ctx_rows = lambda k: jnp.broadcast_to(part(b, b + 1, k), (b, 1, d))
    sh1c, sc1c, g1c, sh2c, sc2c, g2c = [ctx_rows(k) for k in range(6)]

    w_in_b = w_in.astype(BF16)
    w_out_b = w_out.astype(BF16)
    w_router_t = w_router.T.astype(BF16)

    p, t = norm_project(x, row(norm1_g), sc1, sh1, w_in_b, rope_h, rope_d)
    pc, tc = norm_project(xc, row(norm1_g), sc1c, sh1c, w_in_b, rope_id, rope_id)

    lam_init = 0.8 - 0.6 * math.exp(-0.3 * layer_idx)
    lq1, lk1, lq2, lk2 = [diff_lambda[k].astype(F32) for k in range(4)]
    lam = jnp.exp(jnp.sum(lq1 * lk1)) - jnp.exp(jnp.sum(lq2 * lk2)) + lam_init

    o_a = neighbourhood_attention(p, pc, na_rpb)
    o_b = diff_attention(p, t, pc, tc, lam.reshape(1), row(diff_norm_g), 1.0 - lam_init)
    o_c, oc_ret = bidirectional_retention(p, pc, ret_decay_fwd, ret_decay_bwd, ret_gn_g, ret_gn_b, with_ctx)
    o_d = window_attention(p, pc, swa_sink.astype(F32))

    x, h2, aff_t = out_project([o_a, o_b, o_c, o_d], w_out_b, x, g1, row(norm2_g), sc2, sh2, w_router_t)
    xin, gate, idx = expert_choice_route(h2, aff_t)
    ctx_rows = idx_c = None
    if with_ctx:
        oc = _context_mixers(pc, oc_ret, lam, lam_init, diff_norm_g, swa_sink)
        xc, h2c, aff_c = out_project(oc, w_out_b, xc, g1c, row(norm2_g), sc2c, sh2c, w_router_t)
        xin_c, gate_c, idx_c = expert_choice_route(h2c, aff_c)
        ctx_rows = (xin_c, gate_c)
    y, y_c = expert_ffn(xin, gate, ctx_rows, w_gate, w_up, w_down, layer_idx)
    x = scatter_combine(x, y, idx, g2, final_g)
    if with_ctx:
        xc = scatter_combine(xc, y_c, idx_c, g2c, None)
    return x, xc


def kernel(x, c, ctx, c_ctx, w_mod, b_mod, norm1_g, w_in, na_rpb, diff_lambda, diff_norm_g,
           ret_decay_fwd, ret_decay_bwd, ret_gn_g, ret_gn_b, swa_sink, w_out, norm2_g,
           w_router, w_gate, w_up, w_down, final_norm_g):
    b, n, d = x.shape
    depth = w_in.shape[0]
    lc = ctx.shape[1]
    cvec = jnp.zeros((8, d), F32).at[:b].set(c).at[b].set(c_ctx)
    mod = modulation(cvec, w_mod, b_mod.reshape(depth, 1, -1))
    rope_h = _rope_tables(n, HEAD_DIM)
    rope_d = _rope_tables(n, DIFF_DIM)
    rope_id = _identity_rope_tables(lc)
    xc = ctx
    for li in range(depth):
        last = li == depth - 1
        x, xc = _layer(x, xc, mod[li], li, not last, final_norm_g.reshape(1, -1) if last else None,
                       rope_h, rope_d, rope_id,
                       norm1_g[li], w_in[li], na_rpb[li], diff_lambda[li], diff_norm_g[li],
                       ret_decay_fwd[li], ret_decay_bwd[li], ret_gn_g[li], ret_gn_b[li], swa_sink[li],
                       w_out[li], norm2_g[li], w_router[li], w_gate, w_up, w_down)
    return x
```

```python
import functools
import math

import jax
import jax.numpy as jnp
from jax import lax
from jax.experimental import pallas as pl
from jax.experimental.pallas import tpu as pltpu

GRID_W = 64
HEAD_DIM = 128
DIFF_DIM = HEAD_DIM // 2
NA_WIN_R = 8
NA_WIN_C = 16
RET_CHUNK = 128
SWA_WINDOW = 128
N_EXPERTS = 16
EC_CAPACITY = 2
ROPE_BASE = 10000.0
NORM_EPS = 1e-6
GN_EPS = 1e-5
NEG_INF = -1e30
LOG2E = math.log2(math.e)

BF16 = jnp.bfloat16
F32 = jnp.float32

VMEM_LIMIT_BYTES = 48 * 1024 * 1024

GROUP_BLOCKS = 4
COL_A_Q, COL_A_K, COL_A_V = 0, 4, 8
COL_B_K = 16
COL_C_Q, COL_C_K, COL_C_V, COL_C_G = 24, 28, 32, 36
COL_D_Q, COL_D_K, COL_D_V = 40, 44, 46
PROJ_TN = GROUP_BLOCKS * HEAD_DIM


def _cparams(*sem):
    return pltpu.CompilerParams(dimension_semantics=sem, vmem_limit_bytes=VMEM_LIMIT_BYTES)


def _mod_kernel(c_ref, w_ref, b_ref, o_ref):
    c = c_ref[...]
    s = (c * (1.0 / (1.0 + jnp.exp(-c)))).astype(BF16)
    o_ref[...] = jnp.dot(s, w_ref[...].astype(BF16), preferred_element_type=F32) + b_ref[...]


def modulation(cvec, w_mod, b_mod, *, tn=1024):
    depth, d, n6 = w_mod.shape
    return pl.pallas_call(
        _mod_kernel,
        grid=(depth, n6 // tn),
        in_specs=[
            pl.BlockSpec((8, d), lambda l, j: (0, 0)),
            pl.BlockSpec((None, d, tn), lambda l, j: (l, 0, j)),
            pl.BlockSpec((None, 1, tn), lambda l, j: (l, 0, j)),
        ],
        out_specs=pl.BlockSpec((None, 8, tn), lambda l, j: (l, 0, j)),
        out_shape=jax.ShapeDtypeStruct((depth, 8, n6), F32),
        compiler_params=_cparams("parallel", "parallel"),
        name="modulation",
    )(cvec, w_mod, b_mod)


def _rope_tables(n_tok, dim):
    t = jnp.arange(n_tok)
    row = (t // GRID_W).astype(F32)[:, None]
    col = (t % GRID_W).astype(F32)[:, None]
    nf = dim // 4
    lane = jnp.arange(HEAD_DIM)
    quarter = (lane % dim) // nf
    inv = ROPE_BASE ** (-jnp.arange(nf, dtype=F32) / nf)
    ang = jnp.where(quarter[None, :] < 2, row, col) * inv[lane % nf][None, :]
    cos, sin = jnp.cos(ang), jnp.sin(ang)
    even = (quarter % 2 == 0)[None, :]
    return cos, jnp.where(even, -sin, 0.0), jnp.where(even, 0.0, sin)


def _identity_rope_tables(n_tok):
    z = jnp.zeros((n_tok, HEAD_DIM), F32)
    return jnp.ones((n_tok, HEAD_DIM), F32), z, z


def _proj_modes():
    att = HEAD_DIM ** -0.5
    plain = [(None, 1.0)] * GROUP_BLOCKS
    return [
        ([(None, att)] * 4, None), (plain, None), (plain, None),
        ([("d", LOG2E * DIFF_DIM ** -0.5)] * 4, 0), ([("d", 1.0)] * 4, None), (plain, 1),
        ([("h", 1.0)] * 4, None), ([("h", att)] * 4, None), (plain, None), (plain, None),
        ([("h", att)] * 4, None), ([("h", 1.0)] * 2 + [(None, 1.0)] * 2, None),
    ]


def _proj_kernel(x_ref, g_ref, sc_ref, sh_ref, w_ref, ch_ref, sah_ref, sbh_ref, cd_ref, sad_ref, sbd_ref,
                 o_ref, t_ref):
    x = x_ref[...]
    y = x * lax.rsqrt(jnp.mean(x * x, axis=-1, keepdims=True) + NORM_EPS) * g_ref[...]
    h = (y * (1.0 + sc_ref[...]) + sh_ref[...]).astype(BF16)

    def rope(a, kind):
        if kind is None:
            return a
        c, sa, sb, sh = ((ch_ref, sah_ref, sbh_ref, HEAD_DIM // 4) if kind == "h"
                         else (cd_ref, sad_ref, sbd_ref, DIFF_DIM // 4))
        return (a * c[...] + pltpu.roll(a, HEAD_DIM - sh, 1) * sa[...] + pltpu.roll(a, sh, 1) * sb[...])

    for j, (blocks, t_slot) in enumerate(_proj_modes()):
        cols = slice(j * PROJ_TN, (j + 1) * PROJ_TN)
        acc = jnp.dot(h, w_ref[:, cols], preferred_element_type=F32)
        outs = []
        for hb, (kind, scale) in enumerate(blocks):
            a = rope(acc[:, hb * HEAD_DIM:(hb + 1) * HEAD_DIM], kind)
            outs.append(a if scale == 1.0 else a * scale)
        full = jnp.concatenate(outs, axis=1)
        o_ref[:, cols] = full.astype(o_ref.dtype)
        if t_slot is not None:
            t_ref[t_slot] = full.T.astype(t_ref.dtype)


def norm_project(x, norm_g, scale, shift, w_in, rope_h, rope_d, *, tm=256):
    b, n, d = x.shape
    width = w_in.shape[1]
    assert width == len(_proj_modes()) * PROJ_TN
    tm = min(tm, n)
    tok = lambda bi, i: (i, 0)
    return pl.pallas_call(
        _proj_kernel,
        grid=(b, n // tm),
        in_specs=[
            pl.BlockSpec((None, tm, d), lambda bi, i: (bi, i, 0)),
            pl.BlockSpec((1, d), lambda bi, i: (0, 0)),
            pl.BlockSpec((None, 1, d), lambda bi, i: (bi, 0, 0)),
            pl.BlockSpec((None, 1, d), lambda bi, i: (bi, 0, 0)),
            pl.BlockSpec((d, width), lambda bi, i: (0, 0), pipeline_mode=pl.Buffered(1)),
        ] + [pl.BlockSpec((tm, HEAD_DIM), tok)] * 6,
        out_specs=[
            pl.BlockSpec((None, tm, width), lambda bi, i: (bi, i, 0)),
            pl.BlockSpec((None, 2, PROJ_TN, tm), lambda bi, i: (bi, 0, 0, i)),
        ],
        out_shape=[jax.ShapeDtypeStruct((b, n, width), BF16),
                   jax.ShapeDtypeStruct((b, 2, PROJ_TN, n), BF16)],
        compiler_params=_cparams("parallel", "parallel"),
        name="norm_project",
    )(x, norm_g, scale, shift, w_in, *rope_h, *rope_d)


def _softmax(s_loc, s_ctx, sink):
    m = jnp.maximum(jnp.max(s_loc, axis=-1, keepdims=True), jnp.max(s_ctx, axis=-1, keepdims=True))
    if sink is not None:
        m = jnp.maximum(m, sink)
    p_loc = jnp.exp(s_loc - m)
    p_ctx = jnp.exp(s_ctx - m)
    l = jnp.sum(p_loc, axis=-1, keepdims=True) + jnp.sum(p_ctx, axis=-1, keepdims=True)
    if sink is not None:
        l = l + jnp.exp(sink - m)
    return p_loc.astype(BF16), p_ctx.astype(BF16), 1.0 / l


def _local_attention(scores, values, vc_ref, sink, o_ref, tq):
    n_sub = o_ref.shape[0] // tq
    s = [scores(sub) for sub in range(n_sub)]
    p = [_softmax(s_loc, s_ctx, sink) for s_loc, s_ctx in s]
    for sub, (p_loc, p_ctx, inv_l) in enumerate(p):
        o = (jnp.dot(p_loc, values(sub), preferred_element_type=F32)
             + jnp.dot(p_ctx, vc_ref[...], preferred_element_type=F32))
        o_ref[sub * tq:(sub + 1) * tq, :] = (o * inv_l).astype(o_ref.dtype)


def _nt_dot(a, b):
    return lax.dot_general(a, b, (((1,), (1,)), ((), ())), preferred_element_type=F32)


LOCAL_SUB_BLOCKS = 4


def _na_kernel(q_ref, k_ref, v_ref, kc_ref, vc_ref, bias_ref, o_ref, *, tq, wk, n_blk):
    n = k_ref.shape[0]
    rows_q = tq // GRID_W
    n_sub = q_ref.shape[0] // tq

    def kstart(sub):
        blk = pl.program_id(2) * n_sub + sub
        start = jnp.clip(blk * rows_q - NA_WIN_R // 2, 0, (n - wk) // GRID_W) * GRID_W
        return blk, pl.multiple_of(start, GRID_W)

    def scores(sub):
        blk, start = kstart(sub)
        cls = jnp.where(blk == 0, 0, jnp.where(blk == n_blk - 1, 2, 1))
        q = q_ref[sub * tq:(sub + 1) * tq, :]
        return _nt_dot(q, k_ref[pl.ds(start, wk), :]) + bias_ref[cls], _nt_dot(q, kc_ref[...])

    _local_attention(scores, lambda sub: v_ref[pl.ds(kstart(sub)[1], wk), :], vc_ref, None, o_ref, tq)


def _na_bias(rpb, rows, rows_q, rows_k):
    n_blk = rows // rows_q
    kr = NA_WIN_R
    cols = jnp.arange(GRID_W)
    col_start = jnp.clip(cols - NA_WIN_C // 2, 0, GRID_W - NA_WIN_C)
    col_ok = (cols[None, :] >= col_start[:, None]) & (cols[None, :] < col_start[:, None] + NA_WIN_C)
    col_off = jnp.clip(cols[None, :] - cols[:, None] + NA_WIN_C - 1, 0, 2 * NA_WIN_C - 2)
    out = []
    for blk in (0, 1, n_blk - 1):
        r = blk * rows_q + jnp.arange(rows_q)
        kstart = min(max(blk * rows_q - kr // 2, 0), rows - rows_k)
        krow = kstart + jnp.arange(rows_k)
        start = jnp.clip(r - kr // 2, 0, rows - kr)
        row_ok = (krow[None, :] >= start[:, None]) & (krow[None, :] < start[:, None] + kr)
        row_off = jnp.clip(krow[None, :] - r[:, None] + NA_WIN_R - 1, 0, 2 * NA_WIN_R - 2)
        bias = jnp.einsum('abr,hrc,qkc->habqk', jax.nn.one_hot(row_off, 2 * NA_WIN_R - 1, dtype=F32),
                          rpb.astype(F32), jax.nn.one_hot(col_off, 2 * NA_WIN_C - 1, dtype=F32),
                          precision=lax.Precision.HIGHEST)
        ok = row_ok[:, :, None, None] & col_ok[None, None]
        bias = jnp.where(ok[None], bias, NEG_INF).transpose(0, 1, 3, 2, 4)
        out.append(bias.reshape(rpb.shape[0], rows_q * GRID_W, rows_k * GRID_W))
    return jnp.stack(out, axis=1)


def neighbourhood_attention(p, pc, rpb, *, rows_q=4):
    b, n, _ = p.shape
    lc = pc.shape[1]
    h = rpb.shape[0]
    rows = n // GRID_W
    rows_k = rows_q + NA_WIN_R - 1
    assert rows % rows_q == 0 and rows >= rows_k and rows_q >= NA_WIN_R // 2
    tq, wk = rows_q * GRID_W, rows_k * GRID_W
    n_blk = rows // rows_q
    sub = LOCAL_SUB_BLOCKS if n_blk % LOCAL_SUB_BLOCKS == 0 else 1
    bias = _na_bias(rpb, rows, rows_q, rows_k)
    kern = functools.partial(_na_kernel, tq=tq, wk=wk, n_blk=n_blk)
    return pl.pallas_call(
        kern,
        grid=(b, h, n_blk // sub),
        in_specs=[
            pl.BlockSpec((None, sub * tq, HEAD_DIM), lambda bi, hi, i: (bi, i, COL_A_Q + hi)),
            pl.BlockSpec((None, n, HEAD_DIM), lambda bi, hi, i: (bi, 0, COL_A_K + hi)),
            pl.BlockSpec((None, n, HEAD_DIM), lambda bi, hi, i: (bi, 0, COL_A_V + hi)),
            pl.BlockSpec((None, lc, HEAD_DIM), lambda bi, hi, i: (bi, 0, COL_A_K + hi)),
            pl.BlockSpec((None, lc, HEAD_DIM), lambda bi, hi, i: (bi, 0, COL_A_V + hi)),
            pl.BlockSpec((None, 3, tq, wk), lambda bi, hi, i: (hi, 0, 0, 0)),
        ],
        out_specs=pl.BlockSpec((None, sub * tq, HEAD_DIM), lambda bi, hi, i: (bi, i, hi)),
        out_shape=jax.ShapeDtypeStruct((b, n, h * HEAD_DIM), BF16),
        compiler_params=_cparams("parallel", "parallel", "arbitrary"),
        name="neighbourhood_attention",
    )(p, p, p, pc, pc, bias)


def _swa_kernel(sink_ref, q_ref, k_ref, v_ref, kc_ref, vc_ref, o_ref, *, tq, wk):
    n = k_ref.shape[0]

    def window(sub):
        q0 = pl.program_id(2) * q_ref.shape[0] + sub * tq
        return q0, pl.multiple_of(jnp.clip(q0 - SWA_WINDOW, 0, n - wk), SWA_WINDOW)

    def scores(sub):
        q0, start = window(sub)
        q = q_ref[sub * tq:(sub + 1) * tq, :]
        qpos = q0 + lax.broadcasted_iota(jnp.int32, (tq, wk), 0)
        kpos = start + lax.broadcasted_iota(jnp.int32, (tq, wk), 1)
        s_loc = jnp.where(jnp.abs(kpos - qpos) <= SWA_WINDOW, _nt_dot(q, k_ref[pl.ds(start, wk), :]), NEG_INF)
        return s_loc, _nt_dot(q, kc_ref[...])

    _local_attention(scores, lambda sub: v_ref[pl.ds(window(sub)[1], wk), :], vc_ref,
                     sink_ref[pl.program_id(1)], o_ref, tq)


def window_attention(p, pc, sink, *, tq=256):
    b, n, _ = p.shape
    lc = pc.shape[1]
    hq = sink.shape[0]
    grp = hq // (COL_D_V - COL_D_K)
    tq = min(tq, n)
    wk = min(tq + 2 * SWA_WINDOW, n)
    sub = LOCAL_SUB_BLOCKS if (n // tq) % LOCAL_SUB_BLOCKS == 0 else 1
    kern = functools.partial(_swa_kernel, tq=tq, wk=wk)
    return pl.pallas_call(
        kern,
        grid=(b, hq, n // (sub * tq)),
        in_specs=[
            pl.BlockSpec(memory_space=pltpu.SMEM),
            pl.BlockSpec((None, sub * tq, HEAD_DIM), lambda bi, hi, i: (bi, i, COL_D_Q + hi)),
            pl.BlockSpec((None, n, HEAD_DIM), lambda bi, hi, i: (bi, 0, COL_D_K + hi // grp)),
            pl.BlockSpec((None, n, HEAD_DIM), lambda bi, hi, i: (bi, 0, COL_D_V + hi // grp)),
            pl.BlockSpec((None, lc, HEAD_DIM), lambda bi, hi, i: (bi, 0, COL_D_K + hi // grp)),
            pl.BlockSpec((None, lc, HEAD_DIM), lambda bi, hi, i: (bi, 0, COL_D_V + hi // grp)),
        ],
        out_specs=pl.BlockSpec((None, sub * tq, HEAD_DIM), lambda bi, hi, i: (bi, i, hi)),
        out_shape=jax.ShapeDtypeStruct((b, n, hq * HEAD_DIM), BF16),
        compiler_params=_cparams("parallel", "parallel", "arbitrary"),
        name="window_attention",
    )(sink, p, p, p, pc, pc)


DIFF_SLOTS = 2
DIFF_UNROLL = 10
DIFF_AUG_ROWS = 16


def _diff_attn_kernel(lam_ref, qt_ref, k_ref, vt_ref, kc_ref, vct_ref, g_ref, o_ref,
                      qbd_ref, m_ref, acc_ref, s_scr, x_scr, *, tq, tk, post_scale):
    qt = qt_ref[...].astype(F32)
    row = lax.broadcasted_iota(jnp.int32, qt.shape, 0)
    zero = jnp.zeros_like(qt)
    qbd_ref[:, :tq] = jnp.where(row < DIFF_DIM, qt, zero).astype(BF16)
    qbd_ref[:, tq:] = jnp.where(row >= DIFF_DIM, qt, zero).astype(BF16)

    def aug(vt_tile):
        r = lax.broadcasted_iota(jnp.int32, (DIFF_AUG_ROWS, vt_tile.shape[1]), 0)
        return jnp.concatenate([vt_tile, jnp.where(r == 0, 1.0, 0.0).astype(BF16)], axis=0)

    s = jnp.dot(kc_ref[...], qbd_ref[...], preferred_element_type=F32)
    m0 = jnp.max(s, axis=0, keepdims=True)
    m_ref[...] = m0
    acc_ref[...] = jnp.dot(aug(vct_ref[...]), jnp.exp2(s - m0).astype(BF16), preferred_element_type=F32)

    def scores(t, slot):
        off = pl.multiple_of(t * tk, tk)
        s = jnp.dot(k_ref[pl.ds(off, tk), :], qbd_ref[...], preferred_element_type=F32)
        s_scr[slot] = s
        x_scr[slot] = jnp.max(s, axis=0, keepdims=True)

    def accumulate(t, slot):
        m_old = m_ref[...]
        m_new = jnp.maximum(m_old, x_scr[slot])
        alpha = jnp.exp2(m_old - m_new)
        p = jnp.exp2(s_scr[slot] - m_new).astype(BF16)
        m_ref[...] = m_new
        off = pl.multiple_of(t * tk, tk)
        pv = jnp.dot(aug(vt_ref[:, pl.ds(off, tk)]), p, preferred_element_type=F32)
        acc_ref[...] = alpha * acc_ref[...] + pv

    def stage(t, t_mod, do_scores=True):
        if do_scores:
            scores(t + 1, (t_mod + 1) % DIFF_SLOTS)
        accumulate(t, t_mod % DIFF_SLOTS)

    n_kt = k_ref.shape[0] // tk
    trips = (n_kt - 1) // DIFF_UNROLL
    scores(0, 0)

    def body(i, carry):
        for u in range(DIFF_UNROLL):
            stage(DIFF_UNROLL * i + u, u)
        return carry

    lax.fori_loop(0, trips, body, 0)
    for t in range(DIFF_UNROLL * trips, n_kt - 1):
        stage(t, t)
    stage(n_kt - 1, n_kt - 1, do_scores=False)

    lam = lam_ref[0]
    inv = 1.0 / acc_ref[HEAD_DIM:HEAD_DIM + 1, :]
    acc = acc_ref[:HEAD_DIM, :]
    o_t = acc[:, :tq] * inv[:, :tq] - lam * (acc[:, tq:] * inv[:, tq:])
    o = o_t.T
    ms = jnp.mean(o * o, axis=-1, keepdims=True)
    y = o * lax.rsqrt(ms + NORM_EPS) * g_ref[...]
    o_ref[...] = (y * post_scale).astype(o_ref.dtype)


def diff_attention(p, t, pc, tc, lam, norm_g, post_scale, *, tq=512, tk=512):
    b, n, _ = p.shape
    lc = pc.shape[1]
    h = GROUP_BLOCKS
    tq = min(tq, n)
    tk = min(tk, n)
    assert n % tq == 0 and n % tk == 0 and DIFF_UNROLL % DIFF_SLOTS == 0
    kern = functools.partial(_diff_attn_kernel, tq=tq, tk=tk, post_scale=post_scale)
    return pl.pallas_call(
        kern,
        grid=(b, h, n // tq),
        in_specs=[
            pl.BlockSpec(memory_space=pltpu.SMEM),
            pl.BlockSpec((None, None, HEAD_DIM, tq), lambda bi, hi, qi: (bi, 0, hi, qi)),
            pl.BlockSpec((None, n, HEAD_DIM), lambda bi, hi, qi: (bi, 0, COL_B_K + hi)),
            pl.BlockSpec((None, None, HEAD_DIM, n), lambda bi, hi, qi: (bi, 1, hi, 0)),
            pl.BlockSpec((None, lc, HEAD_DIM), lambda bi, hi, qi: (bi, 0, COL_B_K + hi)),
            pl.BlockSpec((None, None, HEAD_DIM, lc), lambda bi, hi, qi: (bi, 1, hi, 0)),
            pl.BlockSpec((1, HEAD_DIM), lambda bi, hi, qi: (0, 0)),
        ],
        out_specs=pl.BlockSpec((None, tq, HEAD_DIM), lambda bi, hi, qi: (bi, qi, hi)),
        out_shape=jax.ShapeDtypeStruct((b, n, h * HEAD_DIM), BF16),
        scratch_shapes=[
            pltpu.VMEM((HEAD_DIM, 2 * tq), BF16),
            pltpu.VMEM((1, 2 * tq), F32),
            pltpu.VMEM((HEAD_DIM + DIFF_AUG_ROWS, 2 * tq), F32),
            pltpu.VMEM((DIFF_SLOTS, tk, 2 * tq), F32),
            pltpu.VMEM((DIFF_SLOTS, 1, 2 * tq), F32),
        ],
        compiler_params=_cparams("parallel", "parallel", "arbitrary"),
        name="diff_attention",
    )(lam, t, p, t, pc, tc, norm_g)


def _retention_kernel(cdec_ref, q_ref, k_ref, v_ref, intra_ref, qdec_ref, kdec_ref, s0_ref, *rest,
                      reverse, final):
    if final:
        of_ref, gate_ref, gng_ref, gnb_ref, o_ref, sfin_ref, s_ref = rest
    else:
        o_ref, sfin_ref, s_ref = rest
    i = pl.program_id(1)
    n_heads = s_ref.shape[0]
    n_chunks = q_ref.shape[0] // RET_CHUNK

    @pl.when(i == 0)
    def _():
        s_ref[...] = s0_ref[...]

    order = range(n_chunks - 1, -1, -1) if reverse else range(n_chunks)
    units = [(c, h) for c in order for h in range(n_heads)]
    window = lambda c, h: (slice(c * RET_CHUNK, (c + 1) * RET_CHUNK), slice(h * HEAD_DIM, (h + 1) * HEAD_DIM))

    intra_o, kv, qdec = {}, {}, {}
    for c, h in units:
        rs, cs = window(c, h)
        qh, kh, vh = q_ref[rs, cs], k_ref[rs, cs], v_ref[rs, cs]
        qdec[c, h] = (qh.astype(F32) * qdec_ref[h]).astype(BF16)
        kd_t = (kh.astype(F32) * kdec_ref[h]).T.astype(BF16)
        a = (_nt_dot(qh, kh) * intra_ref[h]).astype(BF16)
        intra_o[c, h] = jnp.dot(a, vh, preferred_element_type=F32)
        kv[c, h] = jnp.dot(kd_t, vh, preferred_element_type=F32)

    for c, h in units:
        rs, cs = window(c, h)
        s = s_ref[h]
        o = intra_o[c, h] + jnp.dot(qdec[c, h], s.astype(BF16), preferred_element_type=F32)
        s_ref[h] = cdec_ref[h] * s + kv[c, h]
        if final:
            o = o + of_ref[rs, cs]
            mu = jnp.mean(o, axis=-1, keepdims=True)
            var = jnp.mean(jnp.square(o - mu), axis=-1, keepdims=True)
            y = (o - mu) * lax.rsqrt(var + GN_EPS) * gng_ref[:, cs] + gnb_ref[:, cs]
            g = gate_ref[rs, cs].astype(F32)
            o = g * (1.0 / (1.0 + jnp.exp(-g))) * y
        o_ref[rs, cs] = o.astype(o_ref.dtype)

    @pl.when(i == pl.num_programs(1) - 1)
    def _():
        sfin_ref[...] = s_ref[...]


def _retention_pass(p, tables, s0, final_inputs, *, reverse, blk_chunks=4):
    b, n, _ = p.shape
    cdec, intra, qdec, kdec = tables
    h = intra.shape[0]
    width = h * HEAD_DIM
    n_chunks = n // RET_CHUNK
    blk_chunks = min(blk_chunks, n_chunks)
    assert n_chunks % blk_chunks == 0
    tb = blk_chunks * RET_CHUNK
    n_blk = n // tb
    pos = (lambda i: n_blk - 1 - i) if reverse else (lambda i: i)
    colblk = lambda c: (lambda bi, i: (bi, pos(i), c // GROUP_BLOCKS))
    tab = pl.BlockSpec((h, RET_CHUNK, HEAD_DIM), lambda bi, i: (0, 0, 0))
    state = pl.BlockSpec((None, h, HEAD_DIM, HEAD_DIM), lambda bi, i: (bi, 0, 0, 0))
    in_specs = [pl.BlockSpec(memory_space=pltpu.SMEM),
                pl.BlockSpec((None, tb, width), colblk(COL_C_Q)),
                pl.BlockSpec((None, tb, width), colblk(COL_C_K)),
                pl.BlockSpec((None, tb, width), colblk(COL_C_V)),
                tab, tab, tab, state]
    args = [cdec, p, p, p, intra, qdec, kdec, s0]
    final = final_inputs is not None
    if final:
        o_fwd, gn_g, gn_b = final_inputs
        in_specs += [pl.BlockSpec((None, tb, width), lambda bi, i: (bi, pos(i), 0)),
                     pl.BlockSpec((None, tb, width), colblk(COL_C_G)),
                     pl.BlockSpec((1, width), lambda bi, i: (0, 0)),
                     pl.BlockSpec((1, width), lambda bi, i: (0, 0))]
        args += [o_fwd, p, gn_g, gn_b]
    kern = functools.partial(_retention_kernel, reverse=reverse, final=final)
    return pl.pallas_call(
        kern,
        grid=(b, n_blk),
        in_specs=in_specs,
        out_specs=[pl.BlockSpec((None, tb, width), lambda bi, i: (bi, pos(i), 0)), state],
        out_shape=[jax.ShapeDtypeStruct((b, n, width), BF16 if final else F32),
                   jax.ShapeDtypeStruct((b, h, HEAD_DIM, HEAD_DIM), F32)],
        scratch_shapes=[pltpu.VMEM((h, HEAD_DIM, HEAD_DIM), F32)],
        compiler_params=_cparams("parallel", "arbitrary"),
        name="retention_bwd" if reverse else "retention_fwd",
    )(*args)


def _retention_tables(decay, reverse):
    lg = jax.nn.log_sigmoid(decay.astype(F32))[:, None, None]
    pos = jnp.arange(RET_CHUNK, dtype=F32)
    rel = pos[:, None] - pos[None, :]
    if reverse:
        rel = -rel
        q_pow, k_pow = RET_CHUNK - pos, pos
    else:
        q_pow, k_pow = pos + 1.0, RET_CHUNK - 1.0 - pos
    intra = jnp.where(rel >= 0, jnp.exp(jnp.maximum(rel, 0.0) * lg), 0.0)
    bc = lambda e: jnp.broadcast_to(jnp.exp(e[None, :, None] * lg), intra.shape)
    cdec = jnp.exp(RET_CHUNK * lg[:, 0, 0])
    return cdec, intra, bc(q_pow), bc(k_pow)


def bidirectional_retention(p, pc, decay_f, decay_b, gn_g, gn_b, with_ctx):
    b = p.shape[0]
    h = decay_f.shape[0]
    tf = _retention_tables(decay_f, False)
    tb = _retention_tables(decay_b, True)
    s0 = jnp.zeros((b, h, HEAD_DIM, HEAD_DIM), F32)
    gn = (gn_g.reshape(1, -1).astype(F32), gn_b.reshape(1, -1).astype(F32))
    oc_f, s_f = _retention_pass(pc, tf, s0, None, reverse=False)
    oc, s_b = _retention_pass(pc, tb, s0, (oc_f,) + gn, reverse=True)
    o_f, _ = _retention_pass(p, tf, s_f, None, reverse=False)
    o, _ = _retention_pass(p, tb, s_b, (o_f,) + gn, reverse=True)
    return o, (oc if with_ctx else None)


OUT_SUB_ROWS = 256


def _out_kernel(oa_ref, ob_ref, oc_ref, od_ref, w_ref, x_ref, g1_ref, ng_ref, sc_ref, sh_ref, wr_ref,
                xo_ref, h_ref, aff_ref):
    gw = oa_ref.shape[1]
    rows = min(OUT_SUB_ROWS, x_ref.shape[0])
    subs = [slice(sub * rows, (sub + 1) * rows) for sub in range(x_ref.shape[0] // rows)]
    accs = []
    for rs in subs:
        acc = jnp.dot(oa_ref[rs, :], w_ref[0 * gw:1 * gw, :], preferred_element_type=F32)
        acc += jnp.dot(ob_ref[rs, :], w_ref[1 * gw:2 * gw, :], preferred_element_type=F32)
        acc += jnp.dot(oc_ref[rs, :], w_ref[2 * gw:3 * gw, :], preferred_element_type=F32)
        acc += jnp.dot(od_ref[rs, :], w_ref[3 * gw:4 * gw, :], preferred_element_type=F32)
        accs.append(acc)
    for rs, acc in zip(subs, accs):
        x = x_ref[rs, :] + g1_ref[...] * acc
        xo_ref[rs, :] = x
        y = x * lax.rsqrt(jnp.mean(x * x, axis=-1, keepdims=True) + NORM_EPS) * ng_ref[...]
        h2 = (y * (1.0 + sc_ref[...]) + sh_ref[...]).astype(BF16)
        h_ref[rs, :] = h2
        logits = _nt_dot(wr_ref[...], h2)
        e = jnp.exp(logits - jnp.max(logits, axis=0, keepdims=True))
        aff_ref[:, rs] = e * (1.0 / jnp.sum(e, axis=0, keepdims=True))


def out_project(o_groups, w_out, x, g1, norm_g, scale, shift, w_router_t, *, tm=2 * OUT_SUB_ROWS):
    b, n, d = x.shape
    n_e = w_router_t.shape[0]
    gw = o_groups[0].shape[-1]
    tm = min(tm, n)
    tile = lambda w: pl.BlockSpec((None, tm, w), lambda bi, i: (bi, i, 0))
    vec = pl.BlockSpec((None, 1, d), lambda bi, i: (bi, 0, 0))
    return pl.pallas_call(
        _out_kernel,
        grid=(b, n // tm),
        in_specs=[tile(gw)] * 4 + [
            pl.BlockSpec((d, d), lambda bi, i: (0, 0), pipeline_mode=pl.Buffered(1)),
            tile(d), vec,
            pl.BlockSpec((1, d), lambda bi, i: (0, 0)),
            vec, vec,
            pl.BlockSpec((n_e, d), lambda bi, i: (0, 0)),
        ],
        out_specs=[tile(d), tile(d), pl.BlockSpec((None, n_e, tm), lambda bi, i: (bi, 0, i))],
        out_shape=[jax.ShapeDtypeStruct((b, n, d), F32),
                   jax.ShapeDtypeStruct((b, n, d), BF16),
                   jax.ShapeDtypeStruct((b, n_e, n), F32)],
        compiler_params=_cparams("parallel", "parallel"),
        name="out_project",
    )(*o_groups, w_out, x, g1, norm_g, scale, shift, w_router_t)


def _ffn_kernel(*refs, n_lat, with_ctx):
    if with_ctx:
        x_ref, gate_ref, xc_ref, gc_ref, wg_ref, wu_ref, wd_ref, o_ref, oc_ref = refs
    else:
        x_ref, gate_ref, wg_ref, wu_ref, wd_ref, o_ref = refs

    def ffn(x_ref, gate_ref, o_ref):
        x = x_ref[...]
        a = jnp.dot(x, wg_ref[...].astype(BF16), preferred_element_type=F32)
        u = jnp.dot(x, wu_ref[...].astype(BF16), preferred_element_type=F32)
        hm = (a * (1.0 / (1.0 + jnp.exp(-a))) * u).astype(BF16)
        y = jnp.dot(hm, wd_ref[...].astype(BF16), preferred_element_type=F32) * gate_ref[...]
        o_ref[...] = y.astype(o_ref.dtype)

    if not with_ctx:
        ffn(x_ref, gate_ref, o_ref)
        return

    @pl.when(pl.program_id(2) < n_lat)
    def _():
        ffn(x_ref, gate_ref, o_ref)

    @pl.when(pl.program_id(2) == n_lat)
    def _():
        ffn(xc_ref, gc_ref, oc_ref)


def expert_ffn(xin, gate, ctx_rows, w_gate, w_up, w_down, layer, *, tc=256):
    b, n_e, cap, d = xin.shape
    ff = w_gate.shape[-1]
    tc = min(tc, cap)
    n_lat = cap // tc
    with_ctx = ctx_rows is not None
    weight = lambda rows, cols: pl.BlockSpec((None, None, rows, cols), lambda e, bi, i: (layer, e, 0, 0),
                                             pipeline_mode=pl.Buffered(1))
    lat = lambda w: pl.BlockSpec((None, None, tc, w), lambda e, bi, i: (bi, e, jnp.minimum(i, n_lat - 1), 0))
    in_specs, args = [lat(d), lat(1)], [xin, gate]
    out_specs, out_shape = [lat(d)], [jax.ShapeDtypeStruct((b, n_e, cap, d), BF16)]
    if with_ctx:
        cap_c = ctx_rows[0].shape[2]
        whole = lambda w: pl.BlockSpec((None, None, cap_c, w), lambda e, bi, i: (bi, e, 0, 0))
        in_specs += [whole(d), whole(1)]
        args += list(ctx_rows)
        out_specs.append(whole(d))
        out_shape.append(jax.ShapeDtypeStruct((b, n_e, cap_c, d), BF16))
    out = pl.pallas_call(
        functools.partial(_ffn_kernel, n_lat=n_lat, with_ctx=with_ctx),
        grid=(n_e, b, n_lat + int(with_ctx)),
        in_specs=in_specs + [weight(d, ff), weight(d, ff), weight(ff, d)],
        out_specs=out_specs,
        out_shape=out_shape,
        compiler_params=_cparams("parallel", "parallel", "arbitrary"),
        name="expert_ffn",
    )(*args, w_gate, w_up, w_down)
    return (out[0], out[1]) if with_ctx else (out[0], None)


SCATTER_GROUP = 64


def _scatter_kernel(rlo_ref, nrd_ref, x_ref, g_ref, ng_ref, idx_ref, y_hbm, o_ref, ybuf, sem, acc_ref,
                    base_ref, *, sr, final):
    bi, i, nt = pl.program_id(0), pl.program_id(1), pl.num_programs(1)
    n_e, n_grp = idx_ref.shape[0], idx_ref.shape[1]
    tm = x_ref.shape[0]
    n_rounds = nrd_ref[bi * nt + i]

    def group(e, k, tile):
        return jnp.minimum(rlo_ref[(bi * n_e + e) * nt + tile] + k, n_grp - 1)

    def slot_tokens(e, k):
        in_range = rlo_ref[(bi * n_e + e) * nt + i] + k < n_grp
        return jnp.where(in_range, idx_ref[e, group(e, k, i)], -1)

    def copies(k, slot, tile):
        return [pltpu.make_async_copy(
            y_hbm.at[bi, e, pl.ds(pl.multiple_of(group(e, k, tile) * sr, sr), sr), :],
            ybuf.at[slot, pl.ds(e * sr, sr), :], sem.at[slot]) for e in range(n_e)]

    @pl.when(i == 0)
    def _():
        base_ref[0] = 0
        for c in copies(0, 0, i):
            c.start()

    base = base_ref[0]
    acc_ref[...] = jnp.zeros(acc_ref.shape, F32)
    per_row = HEAD_DIM // sr
    tok = i * tm + lax.broadcasted_iota(jnp.int32, (tm, HEAD_DIM), 0)

    def body(k, carry):
        slot = (base + k) % 2
        for c in copies(k, slot, i):
            c.wait()

        @pl.when(k + 1 < n_rounds)
        def _():
            for c in copies(k + 1, 1 - slot, i):
                c.start()

        @pl.when(jnp.logical_and(k + 1 == n_rounds, i + 1 < nt))
        def _():
            for c in copies(0, 1 - slot, i + 1):
                c.start()

        rows = [jnp.concatenate([slot_tokens(e + j, k) for j in range(per_row)], axis=1)
                for e in range(0, n_e, per_row)]
        onehot = jnp.concatenate([jnp.where(tok == r, 1.0, 0.0) for r in rows], axis=1)
        acc_ref[...] += jnp.dot(onehot.astype(BF16), ybuf[slot], preferred_element_type=F32)
        return carry

    lax.fori_loop(0, n_rounds, body, 0)
    base_ref[0] = (base + n_rounds) % 2
    x = x_ref[...] + g_ref[...] * acc_ref[...]
    if final:
        x = x * lax.rsqrt(jnp.mean(x * x, axis=-1, keepdims=True) + NORM_EPS) * ng_ref[...]
    o_ref[...] = x


def scatter_combine(x, y, idx, g2, final_g, *, tm=256):
    b, t, d = x.shape
    n_e, cap = idx.shape[1], idx.shape[2]
    tm = min(tm, t)
    sr = min(SCATTER_GROUP, cap)
    assert t % tm == 0 and cap % sr == 0 and HEAD_DIM % sr == 0 and n_e % (HEAD_DIM // sr) == 0
    nt, n_grp = t // tm, cap // sr
    edges = jnp.arange(nt + 1, dtype=idx.dtype) * tm
    pos = jnp.sum(idx[..., None] < edges, axis=2, dtype=jnp.int32)
    lo, hi = pos[..., :-1], pos[..., 1:]
    rlo = jnp.minimum(lo // sr, n_grp - 1)
    rhi = jnp.maximum(hi - 1, lo) // sr
    n_rounds = jnp.max(jnp.minimum(rhi, n_grp - 1) - rlo + 1, axis=1)
    final = final_g is not None
    ng = final_g if final else jnp.ones((1, d), F32)
    tile = pl.BlockSpec((None, tm, d), lambda bi, i, *_: (bi, i, 0))
    return pl.pallas_call(
        functools.partial(_scatter_kernel, sr=sr, final=final),
        grid_spec=pltpu.PrefetchScalarGridSpec(
            num_scalar_prefetch=2,
            grid=(b, nt),
            in_specs=[tile,
                      pl.BlockSpec((None, 1, d), lambda bi, i, *_: (bi, 0, 0)),
                      pl.BlockSpec((1, d), lambda bi, i, *_: (0, 0)),
                      pl.BlockSpec((None, n_e, n_grp, 1, sr), lambda bi, i, *_: (bi, 0, 0, 0, 0)),
                      pl.BlockSpec(memory_space=pl.ANY)],
            out_specs=tile,
            scratch_shapes=[pltpu.VMEM((2, n_e * sr, d), BF16),
                            pltpu.SemaphoreType.DMA((2,)),
                            pltpu.VMEM((tm, d), F32),
                            pltpu.SMEM((1,), jnp.int32)]),
        out_shape=jax.ShapeDtypeStruct((b, t, d), F32),
        compiler_params=_cparams("parallel", "arbitrary"),
        name="scatter_combine",
    )(rlo.reshape(-1), n_rounds.reshape(-1), x, g2, ng, idx.reshape(b, n_e, n_grp, 1, sr), y)


def expert_choice_route(h2, aff_t):
    t = h2.shape[1]
    cap = EC_CAPACITY * t // N_EXPERTS
    gate, idx = lax.top_k(aff_t, cap)
    idx, gate = lax.sort_key_val(idx, gate, dimension=-1)
    xin = jax.vmap(lambda hb, ib: hb[ib])(h2, idx)
    return xin, gate[..., None], idx


def _ctx_attn_kernel(scal_ref, q_ref, k_ref, v_ref, g_ref, o_ref, *, kind, post_scale):
    q, k, v = q_ref[...], k_ref[...], v_ref[...]
    if kind == "diff":
        qf = q.astype(F32)
        lane = lax.broadcasted_iota(jnp.int32, qf.shape, 1)

        def probs(keep):
            s = _nt_dot(jnp.where(keep, qf, 0.0).astype(BF16), k)
            e = jnp.exp2(s - jnp.max(s, axis=-1, keepdims=True))
            return e * (1.0 / jnp.sum(e, axis=-1, keepdims=True))

        w = probs(lane < DIFF_DIM) - scal_ref[0] * probs(lane >= DIFF_DIM)
        o = jnp.dot(w.astype(BF16), v, preferred_element_type=F32)
        o = o * lax.rsqrt(jnp.mean(o * o, axis=-1, keepdims=True) + NORM_EPS) * g_ref[...] * post_scale
    else:
        s = _nt_dot(q, k)
        m = jnp.max(s, axis=-1, keepdims=True)
        if kind == "sink":
            sink = scal_ref[pl.program_id(1)]
            m = jnp.maximum(m, sink)
        e = jnp.exp(s - m)
        l = jnp.sum(e, axis=-1, keepdims=True)
        if kind == "sink":
            l = l + jnp.exp(sink - m)
        o = jnp.dot((e * (1.0 / l)).astype(BF16), v, preferred_element_type=F32)
    o_ref[...] = o.astype(o_ref.dtype)


def _context_attention(pc, scal, norm_g, kind, post_scale, q_col, k_col, v_col, grp):
    b, lc, _ = pc.shape
    h = GROUP_BLOCKS
    col = lambda c0, div: (lambda bi, hi: (bi, 0, c0 + hi // div))
    return pl.pallas_call(
        functools.partial(_ctx_attn_kernel, kind=kind, post_scale=post_scale),
        grid=(b, h),
        in_specs=[
            pl.BlockSpec(memory_space=pltpu.SMEM),
            pl.BlockSpec((None, lc, HEAD_DIM), col(q_col, 1)),
            pl.BlockSpec((None, lc, HEAD_DIM), col(k_col, grp)),
            pl.BlockSpec((None, lc, HEAD_DIM), col(v_col, grp)),
            pl.BlockSpec((1, HEAD_DIM), lambda bi, hi: (0, 0)),
        ],
        out_specs=pl.BlockSpec((None, lc, HEAD_DIM), lambda bi, hi: (bi, 0, hi)),
        out_shape=jax.ShapeDtypeStruct((b, lc, h * HEAD_DIM), BF16),
        compiler_params=_cparams("parallel", "parallel"),
        name="context_attention_" + kind,
    )(scal, pc, pc, pc, norm_g)


def _context_mixers(pc, oc_ret, lam, lam_init, diff_norm_g, swa_sink):
    h = GROUP_BLOCKS
    zeros = jnp.zeros((h,), F32)
    ones = jnp.ones((1, HEAD_DIM), F32)
    o_a = _context_attention(pc, zeros, ones, "plain", 1.0, COL_A_Q, COL_A_K, COL_A_V, 1)
    o_b = _context_attention(pc, lam.reshape(1), diff_norm_g.reshape(1, HEAD_DIM).astype(F32), "diff",
                             1.0 - lam_init, COL_B_K - h, COL_B_K, COL_B_K + h, 1)
    o_d = _context_attention(pc, swa_sink.astype(F32), ones, "sink", 1.0, COL_D_Q, COL_D_K, COL_D_V, 2)
    return [o_a, o_b, oc_ret, o_d]


def _layer(x, xc, mod, layer_idx, with_ctx, final_g, rope_h, rope_d, rope_id,
           norm1_g, w_in, na_rpb, diff_lambda, diff_norm_g, ret_decay_fwd, ret_decay_bwd, ret_gn_g, ret_gn_b,
           swa_sink, w_out, norm2_g, w_router, w_gate, w_up, w_down):
    b, n, d = x.shape
    row = lambda v: v.reshape(1, -1).astype(F32)
    part = lambda r0, r1, k: mod[r0:r1, None, k * d:(k + 1) * d]
    sh1, sc1, g1, sh2, sc2, g2 = [part(0, b, k) for k in range(6)]
    ctx_rows = lambda k: jnp.broadcast_to(part(b, b + 1, k), (b, 1, d))
    sh1c, sc1c, g1c, sh2c, sc2c, g2c = [ctx_rows(k) for k in range(6)]

    w_in_b = w_in.astype(BF16)
    w_out_b = w_out.astype(BF16)
    w_router_t = w_router.T.astype(BF16)

    p, t = norm_project(x, row(norm1_g), sc1, sh1, w_in_b, rope_h, rope_d)
    pc, tc = norm_project(xc, row(norm1_g), sc1c, sh1c, w_in_b, rope_id, rope_id)

    lam_init = 0.8 - 0.6 * math.exp(-0.3 * layer_idx)
    lq1, lk1, lq2, lk2 = [diff_lambda[k].astype(F32) for k in range(4)]
    lam = jnp.exp(jnp.sum(lq1 * lk1)) - jnp.exp(jnp.sum(lq2 * lk2)) + lam_init

    o_a = neighbourhood_attention(p, pc, na_rpb)
    o_b = diff_attention(p, t, pc, tc, lam.reshape(1), row(diff_norm_g), 1.0 - lam_init)
    o_c, oc_ret = bidirectional_retention(p, pc, ret_decay_fwd, ret_decay_bwd, ret_gn_g, ret_gn_b, with_ctx)
    o_d = window_attention(p, pc, swa_sink.astype(F32))

    x, h2, aff_t = out_project([o_a, o_b, o_c, o_d], w_out_b, x, g1, row(norm2_g), sc2, sh2, w_router_t)
    xin, gate, idx = expert_choice_route(h2, aff_t)
    ctx_rows = idx_c = None
    if with_ctx:
        oc = _context_mixers(pc, oc_ret, lam, lam_init, diff_norm_g, swa_sink)
        xc, h2c, aff_c = out_project(oc, w_out_b, xc, g1c, row(norm2_g), sc2c, sh2c, w_router_t)
        xin_c, gate_c, idx_c = expert_choice_route(h2c, aff_c)
        ctx_rows = (xin_c, gate_c)
    y, y_c = expert_ffn(xin, gate, ctx_rows, w_gate, w_up, w_down, layer_idx)
    x = scatter_combine(x, y, idx, g2, final_g)
    if with_ctx:
        xc = scatter_combine(xc, y_c, idx_c, g2c, None)
    return x, xc


def kernel(x, c, ctx, c_ctx, w_mod, b_mod, norm1_g, w_in, na_rpb, diff_lambda, diff_norm_g,
           ret_decay_fwd, ret_decay_bwd, ret_gn_g, ret_gn_b, swa_sink, w_out, norm2_g,
           w_router, w_gate, w_up, w_down, final_norm_g):
    b, n, d = x.shape
    depth = w_in.shape[0]
    lc = ctx.shape[1]
    cvec = jnp.zeros((8, d), F32).at[:b].set(c).at[b].set(c_ctx)
    mod = modulation(cvec, w_mod, b_mod.reshape(depth, 1, -1))
    rope_h = _rope_tables(n, HEAD_DIM)
    rope_d = _rope_tables(n, DIFF_DIM)
    rope_id = _identity_rope_tables(lc)
    xc = ctx
    for li in range(depth):
        last = li == depth - 1
        x, xc = _layer(x, xc, mod[li], li, not last, final_norm_g.reshape(1, -1) if last else None,
                       rope_h, rope_d, rope_id,
                       norm1_g[li], w_in[li], na_rpb[li], diff_lambda[li], diff_norm_g[li],
                       ret_decay_fwd[li], ret_decay_bwd[li], ret_gn_g[li], ret_gn_b[li], swa_sink[li],
                       w_out[li], norm2_g[li], w_router[li], w_gate, w_up, w_down)
    return x
```
